```python
import math
import jax
import jax.numpy as jnp
from jax import lax
import numpy as np

D_MODEL = 2048
BATCH = 4
SEQ = 2048
DEPTH = 1
DEC_BATCH = 128
DEC_SEQ = 4
PAST_LEN = 16384
PAGE_SIZE = 128

DN_HEADS = 8
DN_DK = 128
DN_DV = 128
DN_CONV = 4
DN_CHUNK = 64
SG_GROUPS = 8
SG_CH = 128
SG_CHUNK = 128
N_EXPERTS = 32
TOP_K = 4
D_EXPERT = D_MODEL
SWIGLU_LIMIT = 7.0
SWIGLU_ALPHA = 1.702
MOE_BLOCK = 128
QK_W = DN_HEADS * DN_DK
V_W = DN_HEADS * DN_DV
QKV_W = 2 * QK_W + V_W
SG_W = SG_GROUPS * SG_CH
IN_SPLIT = (QKV_W, V_W, DN_HEADS, DN_HEADS, SG_W, SG_W, D_MODEL, D_MODEL)
IN_W = QKV_W + V_W + 2 * DN_HEADS + 2 * SG_W + 2 * D_MODEL
DEEP_ALPHA = (2 * DEPTH) ** 0.25
DEEP_BETA = (8 * DEPTH) ** -0.25
LN_EPS = 1e-5
NORM_EPS = 1e-6

kernel_name = "hybrid_gdn_sgu_moe_step"


def layer_norm(x, g, b):
    xf = x.astype(jnp.float32)
    mu = jnp.mean(xf, axis=-1, keepdims=True)
    var = jnp.mean(jnp.square(xf - mu), axis=-1, keepdims=True)
    y = (xf - mu) * lax.rsqrt(var + LN_EPS) * g.astype(jnp.float32) + b.astype(jnp.float32)
    return y.astype(x.dtype)


def l2norm(t):
    return t * lax.rsqrt(jnp.sum(t * t, axis=-1, keepdims=True) + NORM_EPS)


def causal_short_conv(x, buf, w):
    L = x.shape[1]
    xp = jnp.concatenate([buf.astype(x.dtype), x], axis=1)
    y = sum(xp[:, j:j + L] * w[j] for j in range(DN_CONV))
    return jax.nn.silu(y), xp[:, xp.shape[1] - (DN_CONV - 1):]


def gated_delta_chunked(q, k, v, g, beta, s0):
    B, L, H, _ = q.shape
    C = min(DN_CHUNK, L)
    pad = (-L) % C
    if pad:
        pw = ((0, 0), (0, pad), (0, 0), (0, 0))
        q, k, v = jnp.pad(q, pw), jnp.pad(k, pw), jnp.pad(v, pw)
        g, beta = jnp.pad(g, pw[:3]), jnp.pad(beta, pw[:3])
    N = (L + pad) // C

    def to_blocks(t):
        return jnp.moveaxis(t.reshape(B, N, C, H, -1), 3, 1)

    q, k, v = to_blocks(q), to_blocks(k), to_blocks(v)
    g = jnp.moveaxis(g.reshape(B, N, C, H), 3, 1)
    beta = jnp.moveaxis(beta.reshape(B, N, C, H), 3, 1)
    gc = jnp.cumsum(g, axis=-1)
    causal = jnp.tril(jnp.ones((C, C), bool))
    strict = jnp.tril(jnp.ones((C, C), bool), -1)
    diff = gc[..., :, None] - gc[..., None, :]
    decay = jnp.where(causal, jnp.exp(jnp.where(causal, diff, 0.0)), 0.0)
    kb = k * beta[..., None]
    a_mat = jnp.where(strict, jnp.einsum('bhnid,bhnjd->bhnij', kb, k) * decay, 0.0)
    t_mat = a_mat + jnp.eye(C, dtype=a_mat.dtype)
    u = lax.linalg.triangular_solve(t_mat, v * beta[..., None], left_side=True, lower=True, unit_diagonal=True)
    w = lax.linalg.triangular_solve(t_mat, kb * jnp.exp(gc)[..., None], left_side=True, lower=True, unit_diagonal=True)
    attn = jnp.einsum('bhnid,bhnjd->bhnij', q, k) * decay
    qg = q * jnp.exp(gc)[..., None]
    kd = k * jnp.exp(gc[..., -1:] - gc)[..., None]
    gl = jnp.exp(gc[..., -1])
    xs = tuple(jnp.moveaxis(t, 2, 0) for t in (u, w, attn, qg, kd, gl))

    def step(S, blk):
        u_n, w_n, attn_n, qg_n, kd_n, gl_n = blk
        v_new = u_n - jnp.einsum('bhcd,bhde->bhce', w_n, S)
        o_n = jnp.einsum('bhcd,bhde->bhce', qg_n, S) + jnp.einsum('bhij,bhje->bhie', attn_n, v_new)
        S = S * gl_n[..., None, None] + jnp.einsum('bhcd,bhce->bhde', kd_n, v_new)
        return S, o_n

    S, o = lax.scan(step, s0, xs)
    o = jnp.moveaxis(jnp.moveaxis(o, 0, 2), 1, 3).reshape(B, N * C, H, -1)[:, :L]
    return o, S


def gated_delta_branch(qkv, a, bt, z, conv_buf, s0, w_conv, a_log, dt_bias, o_norm_g):
    B, L, _ = qkv.shape
    qkv_c, conv_new = causal_short_conv(qkv, conv_buf, w_conv)
    qkv_c = qkv_c.astype(jnp.float32)
    q, k, v = jnp.split(qkv_c, [QK_W, 2 * QK_W], axis=-1)
    q = l2norm(q.reshape(B, L, DN_HEADS, DN_DK)) * (DN_DK ** -0.5)
    k = l2norm(k.reshape(B, L, DN_HEADS, DN_DK))
    v = v.reshape(B, L, DN_HEADS, DN_DV)
    g = -jnp.exp(a_log.astype(jnp.float32)) * jax.nn.softplus(a.astype(jnp.float32) + dt_bias.astype(jnp.float32))
    beta = jax.nn.sigmoid(bt.astype(jnp.float32))
    o, s_new = gated_delta_chunked(q, k, v, g, beta, s0.astype(jnp.float32))
    zf = z.reshape(B, L, DN_HEADS, DN_DV).astype(jnp.float32)
    o = o * lax.rsqrt(jnp.mean(o * o, axis=-1, keepdims=True) + NORM_EPS) * o_norm_g.astype(jnp.float32) * jax.nn.silu(zf)
    return o.reshape(B, L, V_W).astype(qkv.dtype), conv_new, s_new.astype(s0.dtype)


def chunk_sgu_branch(u, v, sg_ln_g, sg_ln_b, w_s, b_s):
    B, L, _ = u.shape
    u = jax.nn.gelu(u)
    vn = layer_norm(jax.nn.gelu(v), sg_ln_g, sg_ln_b)
    pad = (-L) % SG_CHUNK
    vp = jnp.pad(vn, ((0, 0), (0, pad), (0, 0)))
    N = (L + pad) // SG_CHUNK
    vp = vp.reshape(B, N, SG_CHUNK, SG_GROUPS, SG_CH)
    mask = jnp.tril(jnp.ones((SG_CHUNK, SG_CHUNK), bool))
    ws = jnp.where(mask, w_s, 0.0).astype(vp.dtype)
    mixed = jnp.einsum('gts,bnsgc->bntgc', ws, vp) + b_s.T[None, None, :, :, None]
    mixed = mixed.reshape(B, N * SG_CHUNK, SG_W)[:, :L].astype(u.dtype)
    return u * mixed, vn


def moe_ffn(h, router_w, router_b, w_gu, b_gu, w_dn, b_dn):
    T, D = h.shape
    logits = h.astype(jnp.float32) @ router_w.astype(jnp.float32) + router_b.astype(jnp.float32)
    top_v, top_i = lax.top_k(logits, TOP_K)
    gates = jax.nn.softmax(top_v, axis=-1)
    A = T * TOP_K
    e_flat = top_i.reshape(A)
    tok_flat = jnp.arange(A, dtype=jnp.int32) // TOP_K
    order = jnp.argsort(e_flat)
    e_sorted = e_flat[order]
    tok_sorted = tok_flat[order]
    gate_sorted = gates.reshape(A)[order]
    counts = jnp.bincount(e_flat, length=N_EXPERTS)
    padded = (counts + MOE_BLOCK - 1) // MOE_BLOCK * MOE_BLOCK
    grp_start = jnp.cumsum(counts) - counts
    pad_end = jnp.cumsum(padded)
    pad_start = pad_end - padded
    dest = pad_start[e_sorted] + jnp.arange(A, dtype=jnp.int32) - grp_start[e_sorted]
    n_blocks = -(-A // MOE_BLOCK) + N_EXPERTS
    slot_tok = jnp.full((n_blocks * MOE_BLOCK,), T, jnp.int32).at[dest].set(tok_sorted)
    blk_e = jnp.minimum(jnp.searchsorted(pad_end, jnp.arange(n_blocks) * MOE_BLOCK, side='right'), N_EXPERTS - 1)
    h_pad = jnp.concatenate([h, jnp.zeros((1, D), h.dtype)], axis=0)
    xb = h_pad[slot_tok].reshape(n_blocks, MOE_BLOCK, D)

    def expert_block(args):
        xe, e = args
        gu = xe @ w_gu[e] + b_gu[e]
        gate, up = jnp.split(gu, 2, axis=-1)
        gate = jnp.minimum(gate, SWIGLU_LIMIT)
        up = jnp.clip(up, -SWIGLU_LIMIT, SWIGLU_LIMIT)
        act = (up + 1.0) * gate * jax.nn.sigmoid(SWIGLU_ALPHA * gate)
        return act @ w_dn[e] + b_dn[e]

    yb = lax.map(expert_block, (xb, blk_e)).reshape(n_blocks * MOE_BLOCK, D)
    contrib = yb[dest] * gate_sorted[:, None].astype(h.dtype)
    return jnp.zeros_like(h).at[tok_sorted].add(contrib)


def decoder_layer(x, c, conv_buf, s0, p):
    B, L, D = x.shape
    mod = jax.nn.silu(c) @ p['w_ada'] + p['b_ada']
    sh1, sc1, gt1, sh2, sc2, gt2 = jnp.split(mod[:, None, :], 6, axis=-1)
    h = x * (1.0 + sc1) + sh1
    proj = h @ p['w_in']
    offsets = [int(o) for o in np.cumsum(IN_SPLIT)[:-1]]
    qkv, z, a, bt, u, v, ga, gb = jnp.split(proj, offsets, axis=-1)
    o_a, conv_new, s_new = gated_delta_branch(qkv, a, bt, z, conv_buf, s0, p['w_conv'], p['a_log'], p['dt_bias'], p['o_norm_g'])
    o_b, v_rows = chunk_sgu_branch(u, v, p['sg_ln_g'], p['sg_ln_b'], p['w_s'], p['b_s'])
    merged = jax.nn.sigmoid(ga) * (o_a @ p['p_a']) + jax.nn.sigmoid(gb) * (o_b @ p['p_b'])
    x = layer_norm(DEEP_ALPHA * x + gt1 * (merged @ p['w_out']), p['ln1_g'], p['ln1_b'])
    h = x * (1.0 + sc2) + sh2
    y = moe_ffn(h.reshape(B * L, D), p['router_w'], p['router_b'], p['w_gu'], p['b_gu'], p['w_dn'], p['b_dn']).reshape(B, L, D)
    x = layer_norm(DEEP_ALPHA * x + gt2 * y, p['ln2_g'], p['ln2_b'])
    return x, conv_new, s_new, v_rows


def setup_inputs(seed: int = 0) -> dict:
    key = jax.random.key(seed)
    ks = jax.random.split(key, 40)
    f32 = jnp.float32

    def nrm(k, shape, scale):
        return jax.random.normal(k, shape, f32) * scale

    D = D_MODEL
    col_scale = jnp.ones((IN_W,), f32).at[2 * QK_W:QKV_W].set(DEEP_BETA)
    dt = jnp.exp(jax.random.uniform(ks[9], (DEPTH, DN_HEADS), f32, math.log(1e-3), math.log(1e-1)))
    return {
        'x_prompt': nrm(ks[0], (BATCH, SEQ, D), 1.0),
        'x_sample': nrm(ks[1], (DEC_BATCH, DEC_SEQ, D), 1.0),
        'state_conv_qkv': nrm(ks[2], (DEPTH, DEC_BATCH, DN_CONV - 1, QKV_W), 1.0),
        'state_delta': nrm(ks[3], (DEPTH, DEC_BATCH, DN_HEADS, DN_DK, DN_DV), DN_DK ** -0.5),
        'c_prompt': nrm(ks[4], (BATCH, D), 1.0),
        'c_sample': nrm(ks[5], (DEC_BATCH, D), 1.0),
        'w_ada': nrm(ks[6], (DEPTH, D, 6 * D), 0.5 * D ** -0.5),
        'b_ada': nrm(ks[7], (DEPTH, 6 * D), 0.02),
        'w_in': nrm(ks[8], (DEPTH, D, IN_W), D ** -0.5) * col_scale,
        'w_conv': nrm(ks[10], (DEPTH, DN_CONV, QKV_W), DN_CONV ** -0.5),
        'a_log': jnp.log(jax.random.uniform(ks[11], (DEPTH, DN_HEADS), f32, 1.0, 16.0)),
        'dt_bias': dt + jnp.log(-jnp.expm1(-dt)),
        'o_norm_g': 1.0 + nrm(ks[12], (DEPTH, DN_DV), 0.02),
        'sg_ln_g': 1.0 + nrm(ks[13], (DEPTH, SG_W), 0.02),
        'sg_ln_b': nrm(ks[14], (DEPTH, SG_W), 0.02),
        'w_s': nrm(ks[15], (DEPTH, SG_GROUPS, SG_CHUNK, SG_CHUNK), SG_CHUNK ** -0.5),
        'b_s': 1.0 + nrm(ks[16], (DEPTH, SG_GROUPS, SG_CHUNK), 0.02),
        'p_a': nrm(ks[17], (DEPTH, V_W, D), DEEP_BETA * V_W ** -0.5),
        'p_b': nrm(ks[18], (DEPTH, SG_W, D), DEEP_BETA * SG_W ** -0.5),
        'w_out': nrm(ks[19], (DEPTH, D, D), DEEP_BETA * D ** -0.5),
        'ln1_g': 1.0 + nrm(ks[20], (DEPTH, D), 0.02),
        'ln1_b': nrm(ks[21], (DEPTH, D), 0.02),
        'router_w': nrm(ks[22], (DEPTH, D, N_EXPERTS), D ** -0.5),
        'router_b': nrm(ks[23], (DEPTH, N_EXPERTS), 0.01),
        'w_gu': nrm(ks[24], (DEPTH, N_EXPERTS, D, 2 * D_EXPERT), D ** -0.5),
        'b_gu': nrm(ks[25], (DEPTH, N_EXPERTS, 2 * D_EXPERT), 0.01),
        'w_dn': nrm(ks[26], (DEPTH, N_EXPERTS, D_EXPERT, D), DEEP_BETA * D_EXPERT ** -0.5),
        'b_dn': nrm(ks[27], (DEPTH, N_EXPERTS, D), 0.01),
        'ln2_g': 1.0 + nrm(ks[28], (DEPTH, D), 0.02),
        'ln2_b': nrm(ks[29], (DEPTH, D), 0.02),
    }


def reference(x_prompt, x_sample, state_conv_qkv, state_delta, c_prompt, c_sample, w_ada, b_ada, w_in, w_conv, a_log, dt_bias, o_norm_g, sg_ln_g, sg_ln_b, w_s, b_s, p_a, p_b, w_out, ln1_g, ln1_b, router_w, router_b, w_gu, b_gu, w_dn, b_dn, ln2_g, ln2_b):
    xp, xs = x_prompt, x_sample
    convs_p, deltas_p, convs_s, deltas_s, vrows_s = [], [], [], [], []
    for l in range(DEPTH):
        p = {
            'w_ada': w_ada[l], 'b_ada': b_ada[l], 'w_in': w_in[l], 'w_conv': w_conv[l],
            'a_log': a_log[l], 'dt_bias': dt_bias[l], 'o_norm_g': o_norm_g[l],
            'sg_ln_g': sg_ln_g[l], 'sg_ln_b': sg_ln_b[l], 'w_s': w_s[l], 'b_s': b_s[l],
            'p_a': p_a[l], 'p_b': p_b[l], 'w_out': w_out[l], 'ln1_g': ln1_g[l], 'ln1_b': ln1_b[l],
            'router_w': router_w[l], 'router_b': router_b[l], 'w_gu': w_gu[l], 'b_gu': b_gu[l],
            'w_dn': w_dn[l], 'b_dn': b_dn[l], 'ln2_g': ln2_g[l], 'ln2_b': ln2_b[l],
        }
        bp = xp.shape[0]
        conv0 = jnp.zeros((bp, DN_CONV - 1, QKV_W), xp.dtype)
        s0 = jnp.zeros((bp, DN_HEADS, DN_DK, DN_DV), xp.dtype)
        xp, conv_p, delta_p, _ = decoder_layer(xp, c_prompt, conv0, s0, p)
        xs, conv_s, delta_s, v_rows = decoder_layer(xs, c_sample, state_conv_qkv[l], state_delta[l], p)
        convs_p.append(conv_p)
        deltas_p.append(delta_p)
        convs_s.append(conv_s)
        deltas_s.append(delta_s)
        vrows_s.append(v_rows)
    return (xp, xs, jnp.stack(convs_p), jnp.stack(deltas_p), jnp.stack(convs_s), jnp.stack(deltas_s), jnp.stack(vrows_s))
```

```python
import functools

import jax
import jax.numpy as jnp
from jax import lax
from jax.experimental import pallas as pl
from jax.experimental.pallas import tpu as pltpu

F32 = jnp.float32
BF16 = jnp.bfloat16
HIGHEST = lax.Precision.HIGHEST

DN_HEADS = 8
DN_DK = 128
DN_DV = 128
DN_CONV = 4
DN_CHUNK = 64
SG_GROUPS = 8
SG_CH = 128
SG_CHUNK = 128
TOP_K = 4
SWIGLU_LIMIT = 7.0
SWIGLU_ALPHA = 1.702
LN_EPS = 1e-5
NORM_EPS = 1e-6
QK_W = DN_HEADS * DN_DK
V_W = DN_HEADS * DN_DV
QKV_W = 2 * QK_W + V_W
SG_W = SG_GROUPS * SG_CH

LANES = 128
SUBLANES = 8
VMEM_LIMIT = 56 * 1024 * 1024

PROJ_TM = 512
PROJ_TN = 1152
MERGE_TM = 256
MOE_TM = 256
MOE_TF = 512
MOE_SUBS = 6
COMB_TQ = 128
SAMPLE_BT = 4
SAMPLE_TP = 8


def _sigmoid(x):
    return 1.0 / (1.0 + jnp.exp(-x))


def _silu(x):
    return x * _sigmoid(x)


def _dot(a, b):
    return jnp.dot(a.astype(BF16), b.astype(BF16), preferred_element_type=F32)


def _dot_nt(a, b):
    return lax.dot_general(a.astype(BF16), b.astype(BF16), (((1,), (1,)), ((), ())),
                           preferred_element_type=F32)


def _dot_hi(a, b):
    return jnp.dot(a, b, precision=HIGHEST, preferred_element_type=F32)


def _layer_norm(x, g, b):
    mu = jnp.mean(x, axis=-1, keepdims=True)
    xc = x - mu
    var = jnp.mean(xc * xc, axis=-1, keepdims=True)
    return xc * lax.rsqrt(var + LN_EPS) * g + b


def _params(*sem):
    return pltpu.CompilerParams(dimension_semantics=sem, vmem_limit_bytes=VMEM_LIMIT)


def _adaln_kernel(c_ref, w_ref, b_ref, o_ref):
    o_ref[...] = _dot(_silu(c_ref[...]), w_ref[...]) + b_ref[...]


def _adaln(c, w, b):
    rows, d = c.shape
    n = w.shape[1]
    tn = 1024
    return pl.pallas_call(
        _adaln_kernel,
        grid=(n // tn,),
        in_specs=[pl.BlockSpec((rows, d), lambda j: (0, 0)),
                  pl.BlockSpec((d, tn), lambda j: (0, j)),
                  pl.BlockSpec((1, tn), lambda j: (0, j))],
        out_specs=pl.BlockSpec((rows, tn), lambda j: (0, j)),
        out_shape=jax.ShapeDtypeStruct((rows, n), F32),
        compiler_params=_params("arbitrary"),
        name="adaln",
    )(c, w, b.reshape(1, n))


def _inproj_kernel(x_ref, sc_ref, sh_ref, w_ref, o_ref, h_scr):
    @pl.when(pl.program_id(1) == 0)
    def _():
        h_scr[...] = (x_ref[...] * (1.0 + sc_ref[...]) + sh_ref[...]).astype(BF16)

    o_ref[...] = jnp.dot(h_scr[...], w_ref[...], preferred_element_type=F32)


def _mod_spec(mod, tm, rows_per_seq):
    r = mod.shape[1]
    d = mod.shape[2]
    if r == 1:
        assert rows_per_seq % tm == 0, (rows_per_seq, tm)
        tiles = rows_per_seq // tm
        return pl.BlockSpec((None, 1, d), lambda i, *_: (i // tiles, 0, 0))
    return pl.BlockSpec((None, tm, d), lambda i, *_: (i, 0, 0))


def _inproj(x, sc, sh, w, rows_per_seq):
    rows, d = x.shape
    nw = w.shape[1]
    tm = min(PROJ_TM, rows)
    return pl.pallas_call(
        _inproj_kernel,
        grid=(rows // tm, nw // PROJ_TN),
        in_specs=[pl.BlockSpec((tm, d), lambda i, j: (i, 0)),
                  _mod_spec(sc, tm, rows_per_seq),
                  _mod_spec(sh, tm, rows_per_seq),
                  pl.BlockSpec((d, PROJ_TN), lambda i, j: (0, j))],
        out_specs=pl.BlockSpec((tm, PROJ_TN), lambda i, j: (i, j)),
        out_shape=jax.ShapeDtypeStruct((rows, nw), F32),
        scratch_shapes=[pltpu.VMEM((tm, d), BF16)],
        compiler_params=_params("arbitrary", "arbitrary"),
        name="inproj",
    )(x, sc, sh, w)


def _softplus(x):
    return jnp.maximum(x, 0.0) + jnp.log1p(jnp.exp(-jnp.abs(x)))


def _unit_lower_inverse(a, rows, cols):
    c = a.shape[0]
    eye = (rows == cols).astype(F32)
    same = (rows // 16) == (cols // 16)
    n = jnp.where(same, a, 0.0)
    b = a - n
    n2 = _dot_hi(n, n)
    n4 = _dot_hi(n2, n2)
    n8 = _dot_hi(n4, n4)
    r = eye - n
    r = r + _dot_hi(r, n2)
    r = r + _dot_hi(r, n4)
    dinv = r + _dot_hi(r, n8)
    m = _dot_hi(dinv, b)
    m2 = _dot_hi(m, m)
    x = dinv + _dot_hi(m2, dinv)
    del c
    return x - _dot_hi(m, x)


def _gdn_prompt_kernel(qkv_ref, z_ref, ab_ref, hist_ref, s0_ref, wconv_ref, alog_ref, dt_ref, og_ref,
                       o_ref, conv_ref, s_ref, xp_scr):
    n = pl.program_id(1)
    c = DN_CHUNK
    pad = SUBLANES

    @pl.when(n == 0)
    def _():
        xp_scr[pl.ds(0, pad), :] = jnp.zeros((pad, QKV_W), F32)
        xp_scr[pl.ds(pad - (DN_CONV - 1), DN_CONV - 1), :] = hist_ref[...]
        s_ref[...] = s0_ref[...]

    x = qkv_ref[...]
    xp_scr[pl.ds(pad, c), :] = x
    y = jnp.zeros((c, QKV_W), F32)
    for j in range(DN_CONV):
        y = y + xp_scr[pl.ds(pad - (DN_CONV - 1) + j, c), :] * wconv_ref[pl.ds(j, 1), :]
    y = _silu(y)
    tail = xp_scr[pl.ds(c + pad - (DN_CONV - 1), DN_CONV - 1), :]
    conv_ref[...] = tail
    xp_scr[pl.ds(pad - (DN_CONV - 1), DN_CONV - 1), :] = tail

    ab = ab_ref[...]
    g = -jnp.exp(alog_ref[...]) * _softplus(ab + dt_ref[...])
    beta_all = _sigmoid(ab)
    rows = lax.broadcasted_iota(jnp.int32, (c, c), 0)
    cols = lax.broadcasted_iota(jnp.int32, (c, c), 1)
    causal = rows >= cols
    strict = rows > cols
    gc = _dot_hi(causal.astype(F32), g)
    gc_t = gc.T
    gc_last = gc[c - 1:c, :]
    z = z_ref[...]
    og = og_ref[...]

    for h in range(DN_HEADS):
        qh = y[:, h * DN_DK:(h + 1) * DN_DK]
        kh = y[:, QK_W + h * DN_DK:QK_W + (h + 1) * DN_DK]
        vh = y[:, 2 * QK_W + h * DN_DV:2 * QK_W + (h + 1) * DN_DV]
        qh = qh * lax.rsqrt(jnp.sum(qh * qh, axis=-1, keepdims=True) + NORM_EPS) * (DN_DK ** -0.5)
        kh = kh * lax.rsqrt(jnp.sum(kh * kh, axis=-1, keepdims=True) + NORM_EPS)
        beta = beta_all[:, DN_HEADS + h:DN_HEADS + h + 1]
        gcol = gc[:, h:h + 1]
        grow = gc_t[h:h + 1, :]
        diff = gcol - grow
        decay = jnp.where(causal, jnp.exp(jnp.where(causal, diff, 0.0)), 0.0)
        egc = jnp.exp(gcol)
        kb = kh * beta
        a_mat = jnp.where(strict, _dot_nt(kb, kh) * decay, 0.0)
        tinv = _unit_lower_inverse(a_mat, rows, cols)
        u = _dot_hi(tinv, vh * beta)
        w = _dot_hi(tinv, kb * egc)
        attn = _dot_nt(qh, kh) * decay
        qg = qh * egc
        glast = gc_last[:, h:h + 1]
        kd = kh * jnp.exp(glast - gcol)
        s = s_ref[h]
        v_new = u - _dot(w, s)
        o = _dot(qg, s) + _dot(attn, v_new)
        s_ref[h] = s * jnp.exp(glast) + _dot(kd.T, v_new)
        zh = z[:, h * DN_DV:(h + 1) * DN_DV]
        o = o * lax.rsqrt(jnp.mean(o * o, axis=-1, keepdims=True) + NORM_EPS) * og * _silu(zh)
        o_ref[:, h * DN_DV:(h + 1) * DN_DV] = o


def _gdn_prompt(proj, hist, s0, wconv, alog, dtb, og, batch, seq, col_z, col_ab):
    c = DN_CHUNK
    nchunk = seq // c
    rows = batch * seq
    row = lambda b, n: b * nchunk + n
    return pl.pallas_call(
        _gdn_prompt_kernel,
        grid=(batch, nchunk),
        in_specs=[pl.BlockSpec((c, QKV_W), lambda b, n: (row(b, n), 0)),
                  pl.BlockSpec((c, V_W), lambda b, n: (row(b, n), col_z // V_W)),
                  pl.BlockSpec((c, LANES), lambda b, n: (row(b, n), col_ab // LANES)),
                  pl.BlockSpec((None, DN_CONV - 1, QKV_W), lambda b, n: (b, 0, 0)),
                  pl.BlockSpec((None, DN_HEADS, DN_DK, DN_DV), lambda b, n: (b, 0, 0, 0)),
                  pl.BlockSpec((DN_CONV, QKV_W), lambda b, n: (0, 0)),
                  pl.BlockSpec((1, LANES), lambda b, n: (0, 0)),
                  pl.BlockSpec((1, LANES), lambda b, n: (0, 0)),
                  pl.BlockSpec((1, DN_DV), lambda b, n: (0, 0))],
        out_specs=[pl.BlockSpec((c, V_W), lambda b, n: (row(b, n), 0)),
                   pl.BlockSpec((None, DN_CONV - 1, QKV_W), lambda b, n: (b, 0, 0)),
                   pl.BlockSpec((None, DN_HEADS, DN_DK, DN_DV), lambda b, n: (b, 0, 0, 0))],
        out_shape=[jax.ShapeDtypeStruct((rows, V_W), F32),
                   jax.ShapeDtypeStruct((batch, DN_CONV - 1, QKV_W), F32),
                   jax.ShapeDtypeStruct((batch, DN_HEADS, DN_DK, DN_DV), F32)],
        scratch_shapes=[pltpu.VMEM((c + SUBLANES, QKV_W), F32)],
        compiler_params=_params("arbitrary", "arbitrary"),
        name="gdn_prompt",
    )(proj, proj, proj, hist, s0, wconv, alog, dtb, og)


def _gdn_sample_kernel(qkv_ref, z_ref, ab_ref, hist_ref, s0_ref, wconv_ref, alog_ref, dt_ref, og_ref,
                       o_ref, conv_ref, s_ref, xp_scr, ab_scr, wq_scr, r_scr, kdt_scr, vn_scr, gl_scr):
    bt, tp, lt = SAMPLE_BT, SAMPLE_TP, DN_CONV
    nblk = DN_HEADS * bt
    nrow = nblk * tp
    hist_rows = DN_CONV - 1

    xp_scr[...] = jnp.zeros(xp_scr.shape, F32)
    ab_scr[...] = jnp.zeros(ab_scr.shape, F32)
    ys = []
    for b in range(bt):
        xp_scr[b, pl.ds(SUBLANES - hist_rows, hist_rows), :] = hist_ref[b]
        xp_scr[b, pl.ds(SUBLANES, lt), :] = qkv_ref[pl.ds(b * lt, lt), :]
        yb = jnp.zeros((tp, QKV_W), F32)
        for j in range(DN_CONV):
            yb = yb + xp_scr[b, pl.ds(SUBLANES - hist_rows + j, tp), :] * wconv_ref[pl.ds(j, 1), :]
        ys.append(_silu(yb))
        conv_ref[b] = xp_scr[b, pl.ds(SUBLANES + lt - hist_rows, hist_rows), :]
        ab_scr[b, pl.ds(0, lt), :] = ab_ref[pl.ds(b * lt, lt), :]
    y = jnp.concatenate(ys, axis=0)
    ab = jnp.concatenate([ab_scr[b] for b in range(bt)], axis=0)
    tok = lax.broadcasted_iota(jnp.int32, (bt * tp, 1), 0) % tp
    real = tok < lt
    g_all = jnp.where(real, -jnp.exp(alog_ref[...]) * _softplus(ab + dt_ref[...]), 0.0)
    beta_all = jnp.where(real, _sigmoid(ab), 0.0)

    def heads_to_rows(t, off, width):
        return jnp.concatenate([t[:, off + h * width:off + (h + 1) * width] for h in range(DN_HEADS)], axis=0)

    realr = jnp.concatenate([real] * DN_HEADS, axis=0)
    q = jnp.where(realr, heads_to_rows(y, 0, DN_DK), 0.0)
    k = jnp.where(realr, heads_to_rows(y, QK_W, DN_DK), 0.0)
    v = jnp.where(realr, heads_to_rows(y, 2 * QK_W, DN_DV), 0.0)
    gcol = jnp.concatenate([g_all[:, h:h + 1] for h in range(DN_HEADS)], axis=0)
    beta = jnp.concatenate([beta_all[:, DN_HEADS + h:DN_HEADS + h + 1] for h in range(DN_HEADS)], axis=0)
    q = q * lax.rsqrt(jnp.sum(q * q, axis=-1, keepdims=True) + NORM_EPS) * (DN_DK ** -0.5)
    k = k * lax.rsqrt(jnp.sum(k * k, axis=-1, keepdims=True) + NORM_EPS)

    rows = lax.broadcasted_iota(jnp.int32, (nrow, nrow), 0)
    cols = lax.broadcasted_iota(jnp.int32, (nrow, nrow), 1)
    same = (rows // tp) == (cols // tp)
    causal = same & (rows >= cols)
    strict = same & (rows > cols)
    gfull = jnp.broadcast_to(gcol, (nrow, LANES))
    gc = _dot_hi(causal.astype(F32), gfull)
    gtot = _dot_hi(same.astype(F32), gfull)
    gcc = gc[:, 0:1]
    grow = gc.T[0:1, :]
    decay = jnp.where(causal, jnp.exp(jnp.where(causal, gcc - grow, 0.0)), 0.0)
    egc = jnp.exp(gcc)
    kb = k * beta
    a_mat = jnp.where(strict, _dot_nt(kb, k) * decay, 0.0)
    a2 = _dot_hi(a_mat, a_mat)
    eye = (rows == cols).astype(F32)
    tinv = eye - a_mat + a2 - _dot_hi(a_mat, a2)
    u = _dot_hi(tinv, v * beta)
    w = _dot_hi(tinv, kb * egc)
    attn = _dot_nt(q, k) * decay
    qg = q * egc
    for i in range(nblk):
        wq_scr[pl.ds(2 * tp * i, tp), :] = w[i * tp:(i + 1) * tp, :]
        wq_scr[pl.ds(2 * tp * i + tp, tp), :] = qg[i * tp:(i + 1) * tp, :]
    kdt_scr[...] = (k * jnp.exp(gtot[:, 0:1] - gcc)).T
    gl_scr[...] = jnp.exp(gtot)

    def read_state(i, carry):
        r0 = pl.multiple_of(i * 2 * tp, 2 * tp)
        r_scr[pl.ds(r0, 2 * tp), :] = _dot(wq_scr[pl.ds(r0, 2 * tp), :], s0_ref[i % bt, i // bt])
        return carry

    lax.fori_loop(0, nblk, read_state, 0)
    ws = jnp.concatenate([r_scr[pl.ds(2 * tp * i, tp), :] for i in range(nblk)], axis=0)
    qs = jnp.concatenate([r_scr[pl.ds(2 * tp * i + tp, tp), :] for i in range(nblk)], axis=0)
    v_new = u - ws
    vn_scr[...] = v_new
    o = qs + _dot(attn, v_new)
    blockid = lax.broadcasted_iota(jnp.int32, (nrow, 1), 0) // tp

    def write_state(i, carry):
        r0 = pl.multiple_of(i * tp, tp)
        upd = _dot(kdt_scr[...], jnp.where(blockid == i, vn_scr[...], 0.0))
        s_ref[i % bt, i // bt] = s0_ref[i % bt, i // bt] * gl_scr[pl.ds(r0, 1), :] + upd
        return carry

    lax.fori_loop(0, nblk, write_state, 0)

    og = og_ref[...]
    o = o * lax.rsqrt(jnp.mean(o * o, axis=-1, keepdims=True) + NORM_EPS) * og
    for b in range(bt):
        for h in range(DN_HEADS):
            zbh = z_ref[pl.ds(b * lt, lt), h * DN_DV:(h + 1) * DN_DV]
            blk = o[(h * bt + b) * tp:(h * bt + b) * tp + lt, :]
            o_ref[pl.ds(b * lt, lt), h * DN_DV:(h + 1) * DN_DV] = blk * _silu(zbh)


def _gdn_sample(proj, hist, s0, wconv, alog, dtb, og, batch, col_z, col_ab):
    bt, tp, lt = SAMPLE_BT, SAMPLE_TP, DN_CONV
    rows = batch * lt
    nrow = DN_HEADS * bt * tp
    blk = bt * lt
    return pl.pallas_call(
        _gdn_sample_kernel,
        grid=(batch // bt,),
        in_specs=[pl.BlockSpec((blk, QKV_W), lambda i: (i, 0)),
                  pl.BlockSpec((blk, V_W), lambda i: (i, col_z // V_W)),
                  pl.BlockSpec((blk, LANES), lambda i: (i, col_ab // LANES)),
                  pl.BlockSpec((bt, DN_CONV - 1, QKV_W), lambda i: (i, 0, 0)),
                  pl.BlockSpec((bt, DN_HEADS, DN_DK, DN_DV), lambda i: (i, 0, 0, 0)),
                  pl.BlockSpec((DN_CONV, QKV_W), lambda i: (0, 0)),
                  pl.BlockSpec((1, LANES), lambda i: (0, 0)),
                  pl.BlockSpec((1, LANES), lambda i: (0, 0)),
                  pl.BlockSpec((1, DN_DV), lambda i: (0, 0))],
        out_specs=[pl.BlockSpec((blk, V_W), lambda i: (i, 0)),
                   pl.BlockSpec((bt, DN_CONV - 1, QKV_W), lambda i: (i, 0, 0)),
                   pl.BlockSpec((bt, DN_HEADS, DN_DK, DN_DV), lambda i: (i, 0, 0, 0))],
        out_shape=[jax.ShapeDtypeStruct((rows, V_W), F32),
                   jax.ShapeDtypeStruct((batch, DN_CONV - 1, QKV_W), F32),
                   jax.ShapeDtypeStruct((batch, DN_HEADS, DN_DK, DN_DV), F32)],
        scratch_shapes=[pltpu.VMEM((bt, SUBLANES + tp, QKV_W), F32),
                        pltpu.VMEM((bt, tp, LANES), F32),
                        pltpu.VMEM((2 * nrow, DN_DK), F32),
                        pltpu.VMEM((2 * nrow, DN_DV), F32),
                        pltpu.VMEM((DN_DK, nrow), F32),
                        pltpu.VMEM((nrow, DN_DV), F32),
                        pltpu.VMEM((nrow, LANES), F32)],
        compiler_params=_params("arbitrary"),
        name="gdn_sample",
    )(proj, proj, proj, hist, s0, wconv, alog, dtb, og)


def _sgu_kernel(u_ref, v_ref, g_ref, b_ref, ws_ref, bias_ref, o_ref, vn_ref):
    u = jax.nn.gelu(u_ref[...])
    vn = _layer_norm(jax.nn.gelu(v_ref[...]), g_ref[...], b_ref[...])
    vn_ref[...] = vn
    for g in range(SG_GROUPS):
        sl = slice(g * SG_CH, (g + 1) * SG_CH)
        mixed = _dot(ws_ref[g], vn[:, sl]) + bias_ref[:, sl]
        o_ref[:, sl] = u[:, sl] * mixed


def _sgu(proj, ln_g, ln_b, ws, bias, col_u, col_v):
    rows = proj.shape[0]
    t = ws.shape[1]
    return pl.pallas_call(
        _sgu_kernel,
        grid=(rows // t,),
        in_specs=[pl.BlockSpec((t, SG_W), lambda i: (i, col_u // SG_W)),
                  pl.BlockSpec((t, SG_W), lambda i: (i, col_v // SG_W)),
                  pl.BlockSpec((1, SG_W), lambda i: (0, 0)),
                  pl.BlockSpec((1, SG_W), lambda i: (0, 0)),
                  pl.BlockSpec((SG_GROUPS, t, t), lambda i: (0, 0, 0)),
                  pl.BlockSpec((t, SG_W), lambda i: (0, 0))],
        out_specs=[pl.BlockSpec((t, SG_W), lambda i: (i, 0)),
                   pl.BlockSpec((t, SG_W), lambda i: (i, 0))],
        out_shape=[jax.ShapeDtypeStruct((rows, SG_W), F32),
                   jax.ShapeDtypeStruct((rows, SG_W), F32)],
        compiler_params=_params("arbitrary"),
        name="sgu",
    )(proj, proj, ln_g, ln_b, ws, bias)


def _merge_kernel(alpha, n_experts, oa_ref, ob_ref, ga_ref, gb_ref, x_ref, gt_ref, sc_ref, sh_ref,
                  pa_ref, pb_ref, wo_ref, lg_ref, lb_ref, rw_ref, rb_ref,
                  x1_ref, hp_ref, ti_ref, tg_ref):
    merged = (_sigmoid(ga_ref[...]) * _dot(oa_ref[...], pa_ref[...])
              + _sigmoid(gb_ref[...]) * _dot(ob_ref[...], pb_ref[...]))
    y = _dot(merged, wo_ref[...])
    x1 = _layer_norm(alpha * x_ref[...] + gt_ref[...] * y, lg_ref[...], lb_ref[...])
    x1_ref[...] = x1
    h2 = x1 * (1.0 + sc_ref[...]) + sh_ref[...]
    half = h2.shape[1] // 2
    lo = pltpu.bitcast(h2[:, :half].astype(BF16).astype(F32), jnp.uint32) >> 16
    hi = pltpu.bitcast(h2[:, half:].astype(BF16).astype(F32), jnp.uint32) & jnp.uint32(0xFFFF0000)
    hp_ref[...] = lo | hi
    logits = _dot_hi(h2, rw_ref[...]) + rb_ref[...]
    lane = lax.broadcasted_iota(jnp.int32, logits.shape, 1)
    logits = jnp.where(lane < n_experts, logits, -jnp.inf)
    ti = jnp.zeros(logits.shape, jnp.int32)
    tv = jnp.zeros(logits.shape, F32)
    top = None
    for kk in range(TOP_K):
        m = jnp.max(logits, axis=-1, keepdims=True)
        idx = jnp.min(jnp.where(logits == m, lane.astype(F32), float(LANES)), axis=-1,
                      keepdims=True).astype(jnp.int32)
        if kk == 0:
            top = m
        ti = jnp.where(lane == kk, idx, ti)
        tv = jnp.where(lane == kk, jnp.exp(m - top), tv)
        logits = jnp.where(lane == idx, -jnp.inf, logits)
    ti_ref[...] = ti
    tg_ref[...] = tv / jnp.sum(tv, axis=-1, keepdims=True)


def _merge(alpha, n_experts, oa, ob, proj, x, gt, sc, sh, pa, pb, wo, lg, lb, rw, rb,
           rows_per_seq, col_ga, col_gb):
    rows, d = x.shape
    tm = min(MERGE_TM, rows)
    const = lambda shape: pl.BlockSpec(shape, lambda i: (0,) * len(shape), pipeline_mode=pl.Buffered(1))
    return pl.pallas_call(
        functools.partial(_merge_kernel, alpha, n_experts),
        grid=(rows // tm,),
        in_specs=[pl.BlockSpec((tm, V_W), lambda i: (i, 0)),
                  pl.BlockSpec((tm, SG_W), lambda i: (i, 0)),
                  pl.BlockSpec((tm, d), lambda i: (i, col_ga // d)),
                  pl.BlockSpec((tm, d), lambda i: (i, col_gb // d)),
                  pl.BlockSpec((tm, d), lambda i: (i, 0)),
                  _mod_spec(gt, tm, rows_per_seq),
                  _mod_spec(sc, tm, rows_per_seq),
                  _mod_spec(sh, tm, rows_per_seq),
                  const((V_W, d)), const((SG_W, d)), const((d, d)),
                  const((1, d)), const((1, d)), const((d, LANES)), const((1, LANES))],
        out_specs=[pl.BlockSpec((tm, d), lambda i: (i, 0)),
                   pl.BlockSpec((tm, d // 2), lambda i: (i, 0)),
                   pl.BlockSpec((tm, LANES), lambda i: (i, 0)),
                   pl.BlockSpec((tm, LANES), lambda i: (i, 0))],
        out_shape=[jax.ShapeDtypeStruct((rows, d), F32),
                   jax.ShapeDtypeStruct((rows, d // 2), jnp.uint32),
                   jax.ShapeDtypeStruct((rows, LANES), jnp.int32),
                   jax.ShapeDtypeStruct((rows, LANES), F32)],
        compiler_params=_params("arbitrary"),
        name="merge",
    )(oa, ob, proj, proj, x, gt, sc, sh, pa, pb, wo, lg, lb, rw, rb)


def _dispatch_kernel(tok_ref, h_hbm, xb_hbm, sem):
    i = pl.program_id(0)
    tm = tok_ref.shape[-1]

    def copy(r):
        return pltpu.make_async_copy(h_hbm.at[pl.ds(tok_ref[0, r], 1)],
                                     xb_hbm.at[pl.ds(i * tm + r, 1)], sem)

    def start(r, carry):
        copy(r).start()
        return carry

    def wait(r, carry):
        copy(r).wait()
        return carry

    lax.fori_loop(0, tm, start, 0)
    lax.fori_loop(0, tm, wait, 0)


def _dispatch(slot_tok, hp, tm):
    nslot = slot_tok.shape[0]
    nblk = nslot // tm
    return pl.pallas_call(
        _dispatch_kernel,
        grid=(nblk,),
        in_specs=[pl.BlockSpec((None, 1, tm), lambda i: (i, 0, 0), memory_space=pltpu.SMEM),
                  pl.BlockSpec(memory_space=pl.ANY)],
        out_specs=pl.BlockSpec(memory_space=pl.ANY),
        out_shape=jax.ShapeDtypeStruct((nslot, hp.shape[1]), hp.dtype),
        scratch_shapes=[pltpu.SemaphoreType.DMA(())],
        compiler_params=pltpu.CompilerParams(dimension_semantics=("arbitrary",), has_side_effects=True),
        name="dispatch",
    )(slot_tok.reshape(nblk, 1, tm), hp)


def _moe_kernel(nj, e_ref, j_ref, xb_ref, ob_ref, r_ref, flag_ref,
                x_ref, wg_ref, wu_ref, wd_ref, bg_ref, bu_ref, bd_ref, o_ref,
                wg_scr, wu_scr, wd_scr, acc_scr):
    s = pl.program_id(0)
    flags = flag_ref[s]
    j = j_ref[s]
    r = r_ref[s]

    @pl.when((flags & 2) != 0)
    def _():
        wg_scr[...] = wg_ref[...].astype(BF16)
        wu_scr[...] = wu_ref[...].astype(BF16)
        wd_scr[...] = wd_ref[...].astype(BF16)

    @pl.when(flags == 0)
    def _():
        o_ref[...] = jnp.zeros(o_ref.shape, F32)

    @pl.when((flags & 1) != 0)
    def _():
        w = x_ref[...]
        lo = pltpu.bitcast(w << 16, F32)
        hi = pltpu.bitcast(w & jnp.uint32(0xFFFF0000), F32)
        x = jnp.concatenate([lo, hi], axis=1).astype(BF16)
        gate = jnp.dot(x, wg_scr[...], preferred_element_type=F32) + bg_ref[...]
        up = jnp.dot(x, wu_scr[...], preferred_element_type=F32) + bu_ref[...]
        gate = jnp.minimum(gate, SWIGLU_LIMIT)
        up = jnp.clip(up, -SWIGLU_LIMIT, SWIGLU_LIMIT)
        act = (up + 1.0) * gate * _sigmoid(SWIGLU_ALPHA * gate)
        y = jnp.dot(act.astype(BF16), wd_scr[...], preferred_element_type=F32)

        @pl.when(j == 0)
        def _():
            acc_scr[r] = y

        @pl.when(jnp.logical_and(j > 0, j < nj - 1))
        def _():
            acc_scr[r] = acc_scr[r] + y

        @pl.when(j == nj - 1)
        def _():
            o_ref[...] = acc_scr[r] + y + bd_ref[...]


def _moe(items, xb, w_gu, b_gu, w_dn, b_dn, tm, tf, subs):
    item_e, item_j, item_xb, item_ob, item_r, item_flag = items
    n_items = item_e.shape[0]
    nslot, dh = xb.shape
    d = 2 * dh
    n_exp, _, f2 = w_gu.shape
    f = f2 // 2
    nj = f // tf
    assert nj >= 2
    grid_spec = pltpu.PrefetchScalarGridSpec(
        num_scalar_prefetch=6,
        grid=(n_items,),
        in_specs=[pl.BlockSpec((tm, dh), lambda s, e, j, xbk, obk, r, fl: (xbk[s], 0)),
                  pl.BlockSpec((None, d, tf), lambda s, e, j, xbk, obk, r, fl: (e[s], 0, j[s])),
                  pl.BlockSpec((None, d, tf), lambda s, e, j, xbk, obk, r, fl: (e[s], 0, nj + j[s])),
                  pl.BlockSpec((None, tf, d), lambda s, e, j, xbk, obk, r, fl: (e[s], j[s], 0)),
                  pl.BlockSpec((None, 1, tf), lambda s, e, j, xbk, obk, r, fl: (e[s], 0, j[s])),
                  pl.BlockSpec((None, 1, tf), lambda s, e, j, xbk, obk, r, fl: (e[s], 0, nj + j[s])),
                  pl.BlockSpec((None, 1, d), lambda s, e, j, xbk, obk, r, fl: (e[s], 0, 0))],
        out_specs=pl.BlockSpec((tm, d), lambda s, e, j, xbk, obk, r, fl: (obk[s], 0)),
        scratch_shapes=[pltpu.VMEM((d, tf), BF16), pltpu.VMEM((d, tf), BF16), pltpu.VMEM((tf, d), BF16),
                        pltpu.VMEM((subs, tm, d), F32)],
    )
    return pl.pallas_call(
        functools.partial(_moe_kernel, nj),
        grid_spec=grid_spec,
        out_shape=jax.ShapeDtypeStruct((nslot, d), F32),
        compiler_params=_params("arbitrary"),
        name="moe",
    )(item_e, item_j, item_xb, item_ob, item_r, item_flag,
      xb, w_gu, w_gu, w_dn, b_gu.reshape(n_exp, 1, f2), b_gu.reshape(n_exp, 1, f2), b_dn.reshape(n_exp, 1, d))


def _combine_kernel(alpha, dest_ref, yb_hbm, tg_ref, x1_ref, gt_ref, lg_ref, lb_ref, o_ref, buf, sem):
    tq = x1_ref.shape[0]

    def copy(t, kk):
        return pltpu.make_async_copy(yb_hbm.at[pl.ds(dest_ref[0, t * TOP_K + kk], 1)],
                                     buf.at[kk, pl.ds(t, 1)], sem)

    def start(t, carry):
        for kk in range(TOP_K):
            copy(t, kk).start()
        return carry

    def wait(t, carry):
        for kk in range(TOP_K):
            copy(t, kk).wait()
        return carry

    lax.fori_loop(0, tq, start, 0)
    lax.fori_loop(0, tq, wait, 0)
    tg = tg_ref[...]
    y = jnp.zeros(x1_ref.shape, F32)
    for kk in range(TOP_K):
        y = y + buf[kk] * tg[:, kk:kk + 1]
    o_ref[...] = _layer_norm(alpha * x1_ref[...] + gt_ref[...] * y, lg_ref[...], lb_ref[...])


def _combine(alpha, dest, yb, tg, x1, gt, lg, lb, rows_per_seq):
    rows, d = x1.shape
    tq = min(COMB_TQ, rows)
    nt = rows // tq
    return pl.pallas_call(
        functools.partial(_combine_kernel, alpha),
        grid=(nt,),
        in_specs=[pl.BlockSpec((None, 1, tq * TOP_K), lambda i: (i, 0, 0), memory_space=pltpu.SMEM),
                  pl.BlockSpec(memory_space=pl.ANY),
                  pl.BlockSpec((tq, LANES), lambda i: (i, 0)),
                  pl.BlockSpec((tq, d), lambda i: (i, 0)),
                  _mod_spec(gt, tq, rows_per_seq),
                  pl.BlockSpec((1, d), lambda i: (0, 0)),
                  pl.BlockSpec((1, d), lambda i: (0, 0))],
        out_specs=pl.BlockSpec((tq, d), lambda i: (i, 0)),
        out_shape=jax.ShapeDtypeStruct((rows, d), F32),
        scratch_shapes=[pltpu.VMEM((TOP_K, tq, d), F32), pltpu.SemaphoreType.DMA(())],
        compiler_params=_params("arbitrary"),
        name="combine",
    )(dest.reshape(nt, 1, tq * TOP_K), yb, tg, x1, gt, lg, lb)


def _routing(ti, n_experts, tm, nj, subs):
    t = ti.shape[0]
    n_assign = t * TOP_K
    nb_max = n_assign // tm + n_experts
    onehot = (ti[:, :, None] == jnp.arange(n_experts, dtype=jnp.int32)[None, None, :]).astype(jnp.int32)
    per_tok = jnp.sum(onehot, axis=1)
    cum = jnp.cumsum(per_tok, axis=0)
    counts = cum[-1]
    rank = jnp.take_along_axis(cum, ti, axis=1) - 1
    nblk = (counts + tm - 1) // tm
    blk_end = jnp.cumsum(nblk)
    blk_start = blk_end - nblk
    dest = blk_start[ti] * tm + rank
    tok = jnp.broadcast_to(jnp.arange(t, dtype=jnp.int32)[:, None], (t, TOP_K))
    slot_tok = jnp.zeros((nb_max * tm,), jnp.int32).at[dest.reshape(-1)].set(tok.reshape(-1))
    total_blk = blk_end[-1]
    blocks = jnp.arange(nb_max, dtype=jnp.int32)
    blk_e = jnp.minimum(jnp.searchsorted(blk_end, blocks, side='right'), n_experts - 1).astype(jnp.int32)
    r_in_e = blocks - blk_start[blk_e]
    g0 = blk_start[blk_e] + (r_in_e // subs) * subs
    nsub = jnp.minimum(subs, nblk[blk_e] - (r_in_e // subs) * subs)
    p = jnp.arange(nb_max * nj, dtype=jnp.int32)
    bp = p // nj
    valid = bp < total_blk
    last = jnp.maximum(total_blk - 1, 0)
    bq = jnp.where(valid, bp, last)
    gq, nq, eq = g0[bq], jnp.maximum(nsub[bq], 1), blk_e[bq]
    local = p - nj * gq
    jq = jnp.where(valid, local // nq, nj - 1)
    rq = jnp.where(valid, local % nq, nq - 1)
    item_xb = gq + rq
    item_ob = jnp.where(valid, jnp.where(jq == nj - 1, gq + rq, gq), bp)
    flags = valid.astype(jnp.int32) + 2 * (valid & (rq == 0)).astype(jnp.int32)
    items = (eq.astype(jnp.int32), jq.astype(jnp.int32), item_xb.astype(jnp.int32),
             item_ob.astype(jnp.int32), rq.astype(jnp.int32), flags)
    return dest.astype(jnp.int32), slot_tok, items


def _rearranged_w_in(w_in, d):
    o_qkv, o_z = 0, QKV_W
    o_a = o_z + V_W
    o_b = o_a + DN_HEADS
    o_u = o_b + DN_HEADS
    o_v = o_u + SG_W
    o_ga = o_v + SG_W
    o_gb = o_ga + d
    ab = jnp.pad(w_in[:, o_a:o_u], ((0, 0), (0, LANES - 2 * DN_HEADS)))
    w = jnp.concatenate([w_in[:, o_qkv:o_a], w_in[:, o_u:], ab], axis=1).astype(BF16)
    cols = dict(z=QKV_W, u=QKV_W + V_W, v=QKV_W + V_W + SG_W, ga=QKV_W + V_W + 2 * SG_W,
                gb=QKV_W + V_W + 2 * SG_W + d, ab=QKV_W + V_W + 2 * SG_W + 2 * d)
    return w, cols


def _expand(m, reps):
    return jnp.repeat(m, reps, axis=0)


def kernel(x_prompt, x_sample, state_conv_qkv, state_delta, c_prompt, c_sample, w_ada, b_ada, w_in, w_conv, a_log, dt_bias, o_norm_g, sg_ln_g, sg_ln_b, w_s, b_s, p_a, p_b, w_out, ln1_g, ln1_b, router_w, router_b, w_gu, b_gu, w_dn, b_dn, ln2_g, ln2_b):
    depth = w_ada.shape[0]
    alpha = float((2 * depth) ** 0.25)
    bp, seq, d = x_prompt.shape
    bs, lt, _ = x_sample.shape
    n_experts = router_w.shape[2]
    rows_p, rows_s = bp * seq, bs * lt
    xp = x_prompt.reshape(rows_p, d)
    xs = x_sample.reshape(rows_s, d)
    c_all = jnp.concatenate([c_prompt, c_sample], axis=0)
    outs = dict(conv_p=[], delta_p=[], conv_s=[], delta_s=[], vrows=[])

    for l in range(depth):
        mod = _adaln(c_all, w_ada[l], b_ada[l])
        mods = jnp.split(mod, 6, axis=1)
        mp = [m[:bp].reshape(bp, 1, d) for m in mods]

        def per_row(m, tm):
            return jnp.repeat(m[bp:], lt, axis=0).reshape(rows_s // tm, tm, d)

        w_r, col = _rearranged_w_in(w_in[l], d)
        tm_ps = min(PROJ_TM, rows_s)
        proj_p = _inproj(xp, mp[1], mp[0], w_r, seq)
        proj_s = _inproj(xs, per_row(mods[1], tm_ps), per_row(mods[0], tm_ps), w_r, lt)

        alog = jnp.pad(a_log[l].reshape(1, DN_HEADS), ((0, 0), (0, LANES - DN_HEADS)))
        dtb = jnp.pad(dt_bias[l].reshape(1, DN_HEADS), ((0, 0), (0, LANES - DN_HEADS)))
        og = o_norm_g[l].reshape(1, DN_DV)
        conv0 = jnp.zeros((bp, DN_CONV - 1, QKV_W), F32)
        s0 = jnp.zeros((bp, DN_HEADS, DN_DK, DN_DV), F32)
        oa_p, conv_p, delta_p = _gdn_prompt(proj_p, conv0, s0, w_conv[l], alog, dtb, og, bp, seq,
                                            col['z'], col['ab'])
        oa_s, conv_s, delta_s = _gdn_sample(proj_s, state_conv_qkv[l], state_delta[l], w_conv[l], alog, dtb, og,
                                            bs, col['z'], col['ab'])

        lng = sg_ln_g[l].reshape(1, SG_W)
        lnb = sg_ln_b[l].reshape(1, SG_W)
        ws_p = jnp.tril(w_s[l]).astype(BF16)
        bias_p = jnp.repeat(b_s[l].T, SG_CH, axis=1)
        ob_p, _ = _sgu(proj_p, lng, lnb, ws_p, bias_p, col['u'], col['v'])
        ts = min(SG_CHUNK, rows_s)
        eye = jnp.eye(ts // lt, dtype=F32)
        ws_s = jnp.stack([jnp.kron(eye, jnp.tril(w_s[l, g, :lt, :lt])) for g in range(SG_GROUPS)]).astype(BF16)
        bias_s = jnp.tile(jnp.repeat(b_s[l, :, :lt].T, SG_CH, axis=1), (ts // lt, 1))
        ob_s, vn_s = _sgu(proj_s, lng, lnb, ws_s, bias_s, col['u'], col['v'])

        pa, pb, wo = p_a[l].astype(BF16), p_b[l].astype(BF16), w_out[l].astype(BF16)
        lg1, lb1 = ln1_g[l].reshape(1, d), ln1_b[l].reshape(1, d)
        rw = jnp.pad(router_w[l], ((0, 0), (0, LANES - n_experts)))
        rb = jnp.pad(router_b[l].reshape(1, n_experts), ((0, 0), (0, LANES - n_experts)))
        tm_ms = min(MERGE_TM, rows_s)
        x1_p, hp_p, ti_p, tg_p = _merge(alpha, n_experts, oa_p, ob_p, proj_p, xp, mp[2], mp[4], mp[3],
                                        pa, pb, wo, lg1, lb1, rw, rb, seq, col['ga'], col['gb'])
        x1_s, hp_s, ti_s, tg_s = _merge(alpha, n_experts, oa_s, ob_s, proj_s, xs,
                                        per_row(mods[2], tm_ms), per_row(mods[4], tm_ms), per_row(mods[3], tm_ms),
                                        pa, pb, wo, lg1, lb1, rw, rb, lt, col['ga'], col['gb'])

        hp = jnp.concatenate([hp_p, hp_s], axis=0)
        ti = jnp.concatenate([ti_p[:, :TOP_K], ti_s[:, :TOP_K]], axis=0)
        nj = w_dn.shape[2] // MOE_TF
        dest, slot_tok, items = _routing(ti, n_experts, MOE_TM, nj, MOE_SUBS)
        xb = _dispatch(slot_tok, hp, MOE_TM)
        yb = _moe(items, xb, w_gu[l], b_gu[l], w_dn[l], b_dn[l], MOE_TM, MOE_TF, MOE_SUBS)
        lg2, lb2 = ln2_g[l].reshape(1, d), ln2_b[l].reshape(1, d)
        tq_s = min(COMB_TQ, rows_s)
        xp = _combine(alpha, dest[:rows_p], yb, tg_p, x1_p, mp[5], lg2, lb2, seq)
        xs = _combine(alpha, dest[rows_p:], yb, tg_s, x1_s, per_row(mods[5], tq_s), lg2, lb2, lt)

        outs['conv_p'].append(conv_p)
        outs['delta_p'].append(delta_p)
        outs['conv_s'].append(conv_s)
        outs['delta_s'].append(delta_s)
        outs['vrows'].append(vn_s.reshape(bs, lt, SG_W))

    return (xp.reshape(bp, seq, d), xs.reshape(bs, lt, d),
            jnp.stack(outs['conv_p']), jnp.stack(outs['delta_p']),
            jnp.stack(outs['conv_s']), jnp.stack(outs['delta_s']), jnp.stack(outs['vrows']))
```

```python
import functools
import math

import jax
import jax.numpy as jnp
from jax import lax
from jax.experimental import pallas as pl
from jax.experimental.pallas import tpu as pltpu

F32 = jnp.float32
BF16 = jnp.bfloat16
HIGHEST = lax.Precision.HIGHEST

DN_HEADS = 8
DN_DK = 128
DN_DV = 128
DN_CONV = 4
DN_CHUNK = 64
SG_GROUPS = 8
SG_CH = 128
SG_CHUNK = 128
TOP_K = 4
SWIGLU_LIMIT = 7.0
SWIGLU_ALPHA = 1.702
LN_EPS = 1e-5
NORM_EPS = 1e-6
QK_W = DN_HEADS * DN_DK
V_W = DN_HEADS * DN_DV
QKV_W = 2 * QK_W + V_W
SG_W = SG_GROUPS * SG_CH

LANES = 128
SUBLANES = 8
VMEM_LIMIT = 56 * 1024 * 1024

PROJ_TM = 512
PROJ_TN = 1152
MERGE_TM = 256
MOE_TM = 256
MOE_TF = 512
MOE_SUBS = 6
COMB_TQ = 128
SAMPLE_BT = 4
SAMPLE_TP = 8
GDN_HG = 4


def _sigmoid(x):
    return 1.0 / (1.0 + jnp.exp(-x))


def _silu(x):
    return x * _sigmoid(x)


def _dot(a, b):
    return jnp.dot(a.astype(BF16), b.astype(BF16), preferred_element_type=F32)


def _dot_nt(a, b):
    return lax.dot_general(a.astype(BF16), b.astype(BF16), (((1,), (1,)), ((), ())),
                           preferred_element_type=F32)


def _dot_hi(a, b):
    return jnp.dot(a, b, precision=HIGHEST, preferred_element_type=F32)


def _layer_norm(x, g, b):
    mu = jnp.mean(x, axis=-1, keepdims=True)
    xc = x - mu
    var = jnp.mean(xc * xc, axis=-1, keepdims=True)
    return xc * lax.rsqrt(var + LN_EPS) * g + b


def _params(*sem):
    return pltpu.CompilerParams(dimension_semantics=sem, vmem_limit_bytes=VMEM_LIMIT)


class _Rows:
    def __init__(self, tm, rows_p, rows_s, seq):
        assert rows_p % tm == 0 and rows_s % tm == 0 and seq % tm == 0, (tm, rows_p, rows_s, seq)
        self.tm, self.n_p, self.n_s = tm, rows_p // tm, rows_s // tm
        self.tiles_per_seq = seq // tm
        self.bp = rows_p // seq

    @property
    def n(self):
        return self.n_p + self.n_s

    def prompt(self, width, col=0):
        return pl.BlockSpec((self.tm, width), lambda i, *_: (jnp.minimum(i, self.n_p - 1), col))

    def sample(self, width, col=0, single=True):
        mode = dict(pipeline_mode=pl.Buffered(1)) if single else {}
        return pl.BlockSpec((self.tm, width), lambda i, *_: (jnp.maximum(i - self.n_p, 0), col), **mode)

    def joint(self, width, col=0):
        return pl.BlockSpec((self.tm, width), lambda i, *_: (i, col))

    def seq_vec(self, d):
        return pl.BlockSpec((None, 1, d), lambda i, *_: (jnp.minimum(i // self.tiles_per_seq, self.bp - 1), 0, 0))

    def row_vec(self, d):
        return pl.BlockSpec((None, self.tm, d), lambda i, *_: (jnp.maximum(i - self.n_p, 0), 0, 0),
                            pipeline_mode=pl.Buffered(1))


def _pick(is_sample, prompt_ref, sample_ref):
    return jnp.where(is_sample, sample_ref[...], prompt_ref[...])


def _adaln_kernel(c_ref, w_ref, b_ref, o_ref):
    o_ref[...] = _dot(_silu(c_ref[...]), w_ref[...]) + b_ref[...]


def _adaln(c, w, b):
    rows, d = c.shape
    n = w.shape[1]
    tn = 1024
    return pl.pallas_call(
        _adaln_kernel,
        grid=(n // tn,),
        in_specs=[pl.BlockSpec((rows, d), lambda j: (0, 0)),
                  pl.BlockSpec((d, tn), lambda j: (0, j)),
                  pl.BlockSpec((1, tn), lambda j: (0, j))],
        out_specs=pl.BlockSpec((rows, tn), lambda j: (0, j)),
        out_shape=jax.ShapeDtypeStruct((rows, n), F32),
        compiler_params=_params("arbitrary"),
        name="adaln",
    )(c, w, b.reshape(1, n))


def _inproj_kernel(n_p, xp_ref, xs_ref, scp_ref, scs_ref, shp_ref, shs_ref, w_ref, o_ref, h_scr):
    @pl.when(pl.program_id(1) == 0)
    def _():
        is_s = pl.program_id(0) >= n_p
        x = _pick(is_s, xp_ref, xs_ref)
        h_scr[...] = (x * (1.0 + _pick(is_s, scp_ref, scs_ref)) + _pick(is_s, shp_ref, shs_ref)).astype(BF16)

    o_ref[...] = jnp.dot(h_scr[...], w_ref[...], preferred_element_type=F32)


def _inproj(rt, xp, xs, sc, sh, w):
    d = xp.shape[1]
    nw = w.shape[1]
    tm = rt.tm
    return pl.pallas_call(
        functools.partial(_inproj_kernel, rt.n_p),
        grid=(rt.n, nw // PROJ_TN),
        in_specs=[rt.prompt(d), rt.sample(d), rt.seq_vec(d), rt.row_vec(d), rt.seq_vec(d), rt.row_vec(d),
                  pl.BlockSpec((d, PROJ_TN), lambda i, j: (0, j))],
        out_specs=pl.BlockSpec((tm, PROJ_TN), lambda i, j: (i, j)),
        out_shape=jax.ShapeDtypeStruct((rt.n * tm, nw), F32),
        scratch_shapes=[pltpu.VMEM((tm, d), BF16)],
        compiler_params=_params("arbitrary", "arbitrary"),
        name="inproj",
    )(xp, xs, sc[0], sc[1], sh[0], sh[1], w)


def _softplus(x):
    return jnp.maximum(x, 0.0) + jnp.log1p(jnp.exp(-jnp.abs(x)))


def _split(a):
    hi = a.astype(BF16)
    lo = (a - hi.astype(F32)).astype(BF16)
    return hi, lo


def _dot3(a, b):
    lhs = jnp.concatenate([a[0], a[1], a[0]], axis=1)
    rhs = jnp.concatenate([b[0], b[0], b[1]], axis=0)
    return jnp.dot(lhs, rhs, preferred_element_type=F32)


def _map(f, *lists):
    return [f(*args) for args in zip(*lists)]


def _unit_lower_inverses(mats, rows, cols):
    eye = (rows == cols).astype(F32)
    same = (rows // 16) == (cols // 16)
    n = [jnp.where(same, a, 0.0) for a in mats]
    bs = _map(lambda a, x: _split(a - x), mats, n)
    ns = _map(_split, n)
    n2s = _map(lambda x: _split(_dot3(x, x)), ns)
    n4s = _map(lambda x: _split(_dot3(x, x)), n2s)
    n8s = _map(lambda x: _split(_dot3(x, x)), n4s)
    r = [eye - x for x in n]
    r = _map(lambda x, p: x + _dot3(_split(x), p), r, n2s)
    r = _map(lambda x, p: x + _dot3(_split(x), p), r, n4s)
    dinv = _map(lambda x, p: x + _dot3(_split(x), p), r, n8s)
    ds = _map(_split, dinv)
    ms = _map(lambda x, b: _split(_dot3(x, b)), ds, bs)
    m2s = _map(lambda x: _split(_dot3(x, x)), ms)
    xs = _map(lambda x, p, q: x + _dot3(p, q), dinv, m2s, ds)
    return _map(lambda x, p: x - _dot3(p, _split(x)), xs, ms)


def _gdn_prompt_kernel(qkv_ref, z_ref, ab_ref, hist_ref, s0_ref, wconv_ref, alog_ref, dt_ref, og_ref,
                       o_ref, conv_ref, s_ref, xp_scr):
    n = pl.program_id(1)
    c = DN_CHUNK
    pad = SUBLANES
    nr = GDN_HG * c
    ngrp = DN_HEADS // GDN_HG

    @pl.when(n == 0)
    def _():
        xp_scr[pl.ds(0, pad), :] = jnp.zeros((pad, QKV_W), F32)
        xp_scr[pl.ds(pad - (DN_CONV - 1), DN_CONV - 1), :] = hist_ref[...]
        s_ref[...] = s0_ref[...]

    x = qkv_ref[...]
    xp_scr[pl.ds(pad, c), :] = x
    y = jnp.zeros((c, QKV_W), F32)
    for j in range(DN_CONV):
        y = y + xp_scr[pl.ds(pad - (DN_CONV - 1) + j, c), :] * wconv_ref[pl.ds(j, 1), :]
    y = _silu(y)
    tail = xp_scr[pl.ds(c + pad - (DN_CONV - 1), DN_CONV - 1), :]
    conv_ref[...] = tail
    xp_scr[pl.ds(pad - (DN_CONV - 1), DN_CONV - 1), :] = tail

    ab = ab_ref[...]
    g = -jnp.exp(alog_ref[...]) * _softplus(ab + dt_ref[...])
    beta_all = _sigmoid(ab)
    r64 = lax.broadcasted_iota(jnp.int32, (c, c), 0)
    c64 = lax.broadcasted_iota(jnp.int32, (c, c), 1)
    gc = _dot_hi((r64 >= c64).astype(F32), g)
    z = z_ref[...]
    og = og_ref[...]

    rows = lax.broadcasted_iota(jnp.int32, (nr, nr), 0)
    cols = lax.broadcasted_iota(jnp.int32, (nr, nr), 1)
    same = (rows // c) == (cols // c)
    causal = same & (rows >= cols)
    strict = same & (rows > cols)
    rowhead = lax.broadcasted_iota(jnp.int32, (nr, 1), 0) // c
    groups = [range(grp * GDN_HG, (grp + 1) * GDN_HG) for grp in range(ngrp)]

    def stack(heads, off, width):
        return jnp.concatenate([y[:, off + h * width:off + (h + 1) * width] for h in heads], axis=0)

    def l2n(t):
        return t * lax.rsqrt(jnp.sum(t * t, axis=-1, keepdims=True) + NORM_EPS)

    q = [l2n(stack(hs, 0, DN_DK)) * (DN_DK ** -0.5) for hs in groups]
    k = [l2n(stack(hs, QK_W, DN_DK)) for hs in groups]
    v = [stack(hs, 2 * QK_W, DN_DV) for hs in groups]
    beta = [jnp.concatenate([beta_all[:, DN_HEADS + h:DN_HEADS + h + 1] for h in hs], axis=0) for hs in groups]
    gcf = [jnp.concatenate([jnp.broadcast_to(gc[:, h:h + 1], (c, LANES)) for h in hs], axis=0) for hs in groups]
    gcc = [t[:, 0:1] for t in gcf]
    grow = [t.T[0:1, :] for t in gcf]
    glast = [[gc[c - 1:c, h:h + 1] for h in hs] for hs in groups]
    gtot = [jnp.concatenate([jnp.broadcast_to(t, (c, 1)) for t in gl], axis=0) for gl in glast]
    decay = _map(lambda a, b: jnp.where(causal, jnp.exp(jnp.where(causal, a - b, 0.0)), 0.0), gcc, grow)
    egc = _map(jnp.exp, gcc)
    kb = _map(lambda a, b: a * b, k, beta)
    a_mat = _map(lambda a, b, dd: jnp.where(strict, _dot_nt(a, b) * dd, 0.0), kb, k, decay)
    ts = _map(_split, _unit_lower_inverses(a_mat, rows, cols))
    u = _map(lambda t, a, b: _dot3(t, _split(a * b)), ts, v, beta)
    w = _map(lambda t, a, b: _dot3(t, _split(a * b)), ts, kb, egc)
    attn = _map(lambda a, b, dd: _dot_nt(a, b) * dd, q, k, decay)
    qg = _map(lambda a, b: a * b, q, egc)
    kd_t = _map(lambda a, b, cc: (a * jnp.exp(b - cc)).T, k, gtot, gcc)

    for gi, hs in enumerate(groups):
        v_news, qss = [], []
        for hl, h in enumerate(hs):
            sl = slice(hl * c, (hl + 1) * c)
            rs = _dot(jnp.concatenate([w[gi][sl], qg[gi][sl]], axis=0), s_ref[h])
            v_news.append(u[gi][sl] - rs[:c])
            qss.append(rs[c:])
        v_new = jnp.concatenate(v_news, axis=0)
        o = jnp.concatenate(qss, axis=0) + _dot(attn[gi], v_new)
        for hl, h in enumerate(hs):
            upd = _dot(kd_t[gi], jnp.where(rowhead == hl, v_new, 0.0))
            s_ref[h] = s_ref[h] * jnp.exp(glast[gi][hl]) + upd
        o = o * lax.rsqrt(jnp.mean(o * o, axis=-1, keepdims=True) + NORM_EPS) * og
        for hl, h in enumerate(hs):
            o_ref[:, h * DN_DV:(h + 1) * DN_DV] = o[hl * c:(hl + 1) * c] * _silu(z[:, h * DN_DV:(h + 1) * DN_DV])


def _gdn_prompt(proj, hist, s0, wconv, alog, dtb, og, batch, seq, col_z, col_ab):
    c = DN_CHUNK
    nchunk = seq // c
    rows = batch * seq
    row = lambda b, n: b * nchunk + n
    return pl.pallas_call(
        _gdn_prompt_kernel,
        grid=(batch, nchunk),
        in_specs=[pl.BlockSpec((c, QKV_W), lambda b, n: (row(b, n), 0)),
                  pl.BlockSpec((c, V_W), lambda b, n: (row(b, n), col_z // V_W)),
                  pl.BlockSpec((c, LANES), lambda b, n: (row(b, n), col_ab // LANES)),
                  pl.BlockSpec((None, DN_CONV - 1, QKV_W), lambda b, n: (b, 0, 0)),
                  pl.BlockSpec((None, DN_HEADS, DN_DK, DN_DV), lambda b, n: (b, 0, 0, 0)),
                  pl.BlockSpec((DN_CONV, QKV_W), lambda b, n: (0, 0)),
                  pl.BlockSpec((1, LANES), lambda b, n: (0, 0)),
                  pl.BlockSpec((1, LANES), lambda b, n: (0, 0)),
                  pl.BlockSpec((1, DN_DV), lambda b, n: (0, 0))],
        out_specs=[pl.BlockSpec((c, V_W), lambda b, n: (row(b, n), 0)),
                   pl.BlockSpec((None, DN_CONV - 1, QKV_W), lambda b, n: (b, 0, 0)),
                   pl.BlockSpec((None, DN_HEADS, DN_DK, DN_DV), lambda b, n: (b, 0, 0, 0))],
        out_shape=[jax.ShapeDtypeStruct((rows, V_W), F32),
                   jax.ShapeDtypeStruct((batch, DN_CONV - 1, QKV_W), F32),
                   jax.ShapeDtypeStruct((batch, DN_HEADS, DN_DK, DN_DV), F32)],
        scratch_shapes=[pltpu.VMEM((c + SUBLANES, QKV_W), F32)],
        compiler_params=_params("arbitrary", "arbitrary"),
        name="gdn_prompt",
    )(proj, proj, proj, hist, s0, wconv, alog, dtb, og)


def _gdn_sample_kernel(qkv_ref, z_ref, ab_ref, hist_ref, s0_ref, wconv_ref, alog_ref, dt_ref, og_ref,
                       o_ref, conv_ref, s_ref, xp_scr, ab_scr, wq_scr, r_scr, kdt_scr, vn_scr, gl_scr):
    bt, tp, lt = SAMPLE_BT, SAMPLE_TP, DN_CONV
    nblk = DN_HEADS * bt
    nrow = nblk * tp
    hist_rows = DN_CONV - 1

    xp_scr[...] = jnp.zeros(xp_scr.shape, F32)
    ab_scr[...] = jnp.zeros(ab_scr.shape, F32)
    ys = []
    for b in range(bt):
        xp_scr[b, pl.ds(SUBLANES - hist_rows, hist_rows), :] = hist_ref[b]
        xp_scr[b, pl.ds(SUBLANES, lt), :] = qkv_ref[pl.ds(b * lt, lt), :]
        yb = jnp.zeros((tp, QKV_W), F32)
        for j in range(DN_CONV):
            yb = yb + xp_scr[b, pl.ds(SUBLANES - hist_rows + j, tp), :] * wconv_ref[pl.ds(j, 1), :]
        ys.append(_silu(yb))
        conv_ref[b] = xp_scr[b, pl.ds(SUBLANES + lt - hist_rows, hist_rows), :]
        ab_scr[b, pl.ds(0, lt), :] = ab_ref[pl.ds(b * lt, lt), :]
    y = jnp.concatenate(ys, axis=0)
    ab = jnp.concatenate([ab_scr[b] for b in range(bt)], axis=0)
    tok = lax.broadcasted_iota(jnp.int32, (bt * tp, 1), 0) % tp
    real = tok < lt
    g_all = jnp.where(real, -jnp.exp(alog_ref[...]) * _softplus(ab + dt_ref[...]), 0.0)
    beta_all = jnp.where(real, _sigmoid(ab), 0.0)

    def heads_to_rows(t, off, width):
        return jnp.concatenate([t[:, off + h * width:off + (h + 1) * width] for h in range(DN_HEADS)], axis=0)

    realr = jnp.concatenate([real] * DN_HEADS, axis=0)
    q = jnp.where(realr, heads_to_rows(y, 0, DN_DK), 0.0)
    k = jnp.where(realr, heads_to_rows(y, QK_W, DN_DK), 0.0)
    v = jnp.where(realr, heads_to_rows(y, 2 * QK_W, DN_DV), 0.0)
    gcol = jnp.concatenate([g_all[:, h:h + 1] for h in range(DN_HEADS)], axis=0)
    beta = jnp.concatenate([beta_all[:, DN_HEADS + h:DN_HEADS + h + 1] for h in range(DN_HEADS)], axis=0)
    q = q * lax.rsqrt(jnp.sum(q * q, axis=-1, keepdims=True) + NORM_EPS) * (DN_DK ** -0.5)
    k = k * lax.rsqrt(jnp.sum(k * k, axis=-1, keepdims=True) + NORM_EPS)

    rows = lax.broadcasted_iota(jnp.int32, (nrow, nrow), 0)
    cols = lax.broadcasted_iota(jnp.int32, (nrow, nrow), 1)
    same = (rows // tp) == (cols // tp)
    causal = same & (rows >= cols)
    strict = same & (rows > cols)
    gfull = jnp.broadcast_to(gcol, (nrow, LANES))
    gc = _dot_hi(causal.astype(F32), gfull)
    gtot = _dot_hi(same.astype(F32), gfull)
    gcc = gc[:, 0:1]
    grow = gc.T[0:1, :]
    decay = jnp.where(causal, jnp.exp(jnp.where(causal, gcc - grow, 0.0)), 0.0)
    egc = jnp.exp(gcc)
    kb = k * beta
    a_mat = jnp.where(strict, _dot_nt(kb, k) * decay, 0.0)
    a_s = _split(a_mat)
    a2 = _dot3(a_s, a_s)
    eye = (rows == cols).astype(F32)
    ts = _split(eye - a_mat + a2 - _dot3(a_s, _split(a2)))
    u = _dot3(ts, _split(v * beta))
    w = _dot3(ts, _split(kb * egc))
    attn = _dot_nt(q, k) * decay
    qg = q * egc
    for i in range(nblk):
        wq_scr[pl.ds(2 * tp * i, tp), :] = w[i * tp:(i + 1) * tp, :]
        wq_scr[pl.ds(2 * tp * i + tp, tp), :] = qg[i * tp:(i + 1) * tp, :]
    kdt_scr[...] = (k * jnp.exp(gtot[:, 0:1] - gcc)).T
    gl_scr[...] = jnp.exp(gtot)

    def read_state(i, carry):
        r0 = pl.multiple_of(i * 2 * tp, 2 * tp)
        r_scr[pl.ds(r0, 2 * tp), :] = _dot(wq_scr[pl.ds(r0, 2 * tp), :], s0_ref[i % bt, i // bt])
        return carry

    lax.fori_loop(0, nblk, read_state, 0, unroll=4)
    ws = jnp.concatenate([r_scr[pl.ds(2 * tp * i, tp), :] for i in range(nblk)], axis=0)
    qs = jnp.concatenate([r_scr[pl.ds(2 * tp * i + tp, tp), :] for i in range(nblk)], axis=0)
    v_new = u - ws
    vn_scr[...] = v_new
    o = qs + _dot(attn, v_new)
    blockid = lax.broadcasted_iota(jnp.int32, (nrow, 1), 0) // tp

    def write_state(i, carry):
        r0 = pl.multiple_of(i * tp, tp)
        upd = _dot(kdt_scr[...], jnp.where(blockid == i, vn_scr[...], 0.0))
        s_ref[i % bt, i // bt] = s0_ref[i % bt, i // bt] * gl_scr[pl.ds(r0, 1), :] + upd
        return carry

    lax.fori_loop(0, nblk, write_state, 0, unroll=4)

    og = og_ref[...]
    o = o * lax.rsqrt(jnp.mean(o * o, axis=-1, keepdims=True) + NORM_EPS) * og
    for b in range(bt):
        for h in range(DN_HEADS):
            zbh = z_ref[pl.ds(b * lt, lt), h * DN_DV:(h + 1) * DN_DV]
            blk = o[(h * bt + b) * tp:(h * bt + b) * tp + lt, :]
            o_ref[pl.ds(b * lt, lt), h * DN_DV:(h + 1) * DN_DV] = blk * _silu(zbh)


def _gdn_sample(proj, row0, hist, s0, wconv, alog, dtb, og, batch, col_z, col_ab):
    bt, tp, lt = SAMPLE_BT, SAMPLE_TP, DN_CONV
    rows = batch * lt
    nrow = DN_HEADS * bt * tp
    blk = bt * lt
    assert row0 % blk == 0 and batch % bt == 0
    b0 = row0 // blk
    return pl.pallas_call(
        _gdn_sample_kernel,
        grid=(batch // bt,),
        in_specs=[pl.BlockSpec((blk, QKV_W), lambda i: (b0 + i, 0)),
                  pl.BlockSpec((blk, V_W), lambda i: (b0 + i, col_z // V_W)),
                  pl.BlockSpec((blk, LANES), lambda i: (b0 + i, col_ab // LANES)),
                  pl.BlockSpec((bt, DN_CONV - 1, QKV_W), lambda i: (i, 0, 0)),
                  pl.BlockSpec((bt, DN_HEADS, DN_DK, DN_DV), lambda i: (i, 0, 0, 0)),
                  pl.BlockSpec((DN_CONV, QKV_W), lambda i: (0, 0)),
                  pl.BlockSpec((1, LANES), lambda i: (0, 0)),
                  pl.BlockSpec((1, LANES), lambda i: (0, 0)),
                  pl.BlockSpec((1, DN_DV), lambda i: (0, 0))],
        out_specs=[pl.BlockSpec((blk, V_W), lambda i: (i, 0)),
                   pl.BlockSpec((bt, DN_CONV - 1, QKV_W), lambda i: (i, 0, 0)),
                   pl.BlockSpec((bt, DN_HEADS, DN_DK, DN_DV), lambda i: (i, 0, 0, 0))],
        out_shape=[jax.ShapeDtypeStruct((rows, V_W), F32),
                   jax.ShapeDtypeStruct((batch, DN_CONV - 1, QKV_W), F32),
                   jax.ShapeDtypeStruct((batch, DN_HEADS, DN_DK, DN_DV), F32)],
        scratch_shapes=[pltpu.VMEM((bt, SUBLANES + tp, QKV_W), F32),
                        pltpu.VMEM((bt, tp, LANES), F32),
                        pltpu.VMEM((2 * nrow, DN_DK), F32),
                        pltpu.VMEM((2 * nrow, DN_DV), F32),
                        pltpu.VMEM((DN_DK, nrow), F32),
                        pltpu.VMEM((nrow, DN_DV), F32),
                        pltpu.VMEM((nrow, LANES), F32)],
        compiler_params=_params("arbitrary"),
        name="gdn_sample",
    )(proj, proj, proj, hist, s0, wconv, alog, dtb, og)


def _sgu_kernel(n_p, u_ref, v_ref, g_ref, b_ref, wsp_ref, wss_ref, bp_ref, bs_ref, o_ref, vn_ref):
    is_s = pl.program_id(0) >= n_p
    u = jax.nn.gelu(u_ref[...])
    vn = _layer_norm(jax.nn.gelu(v_ref[...]), g_ref[...], b_ref[...])
    vn_ref[...] = vn
    bias = _pick(is_s, bp_ref, bs_ref)
    for g in range(SG_GROUPS):
        sl = slice(g * SG_CH, (g + 1) * SG_CH)
        ws = jnp.where(is_s, wss_ref[g], wsp_ref[g])
        mixed = _dot(ws, vn[:, sl]) + bias[:, sl]
        o_ref[:, sl] = u[:, sl] * mixed


def _sgu(rt, proj, ln_g, ln_b, ws_p, ws_s, bias_p, bias_s, col_u, col_v):
    t = rt.tm
    const = lambda shape: pl.BlockSpec(shape, lambda i: (0,) * len(shape))
    return pl.pallas_call(
        functools.partial(_sgu_kernel, rt.n_p),
        grid=(rt.n,),
        in_specs=[rt.joint(SG_W, col_u // SG_W), rt.joint(SG_W, col_v // SG_W),
                  const((1, SG_W)), const((1, SG_W)),
                  const((SG_GROUPS, t, t)), const((SG_GROUPS, t, t)), const((t, SG_W)), const((t, SG_W))],
        out_specs=[rt.joint(SG_W), rt.sample(SG_W, single=False)],
        out_shape=[jax.ShapeDtypeStruct((rt.n * t, SG_W), F32),
                   jax.ShapeDtypeStruct((rt.n_s * t, SG_W), F32)],
        compiler_params=_params("arbitrary"),
        name="sgu",
    )(proj, proj, ln_g, ln_b, ws_p, ws_s, bias_p, bias_s)


def _merge_kernel(alpha, n_experts, n_p, oap_ref, oas_ref, ob_ref, ga_ref, gb_ref, xp_ref, xs_ref,
                  gtp_ref, gts_ref, scp_ref, scs_ref, shp_ref, shs_ref,
                  pa_ref, pb_ref, wo_ref, lg_ref, lb_ref, rw_ref, rb_ref,
                  x1_ref, h2_ref, ti_ref, tg_ref):
    is_s = pl.program_id(0) >= n_p
    oa = _pick(is_s, oap_ref, oas_ref)
    merged = (_sigmoid(ga_ref[...]) * _dot(oa, pa_ref[...])
              + _sigmoid(gb_ref[...]) * _dot(ob_ref[...], pb_ref[...]))
    y = _dot(merged, wo_ref[...])
    x = _pick(is_s, xp_ref, xs_ref)
    x1 = _layer_norm(alpha * x + _pick(is_s, gtp_ref, gts_ref) * y, lg_ref[...], lb_ref[...])
    x1_ref[...] = x1
    h2 = x1 * (1.0 + _pick(is_s, scp_ref, scs_ref)) + _pick(is_s, shp_ref, shs_ref)
    h2_ref[...] = h2
    logits = _dot_hi(h2, rw_ref[...]) + rb_ref[...]
    lane = lax.broadcasted_iota(jnp.int32, logits.shape, 1)
    logits = jnp.where(lane < n_experts, logits, -jnp.inf)
    ti = jnp.zeros(logits.shape, jnp.int32)
    tv = jnp.zeros(logits.shape, F32)
    top = None
    for kk in range(TOP_K):
        m = jnp.max(logits, axis=-1, keepdims=True)
        idx = jnp.min(jnp.where(logits == m, lane.astype(F32), float(LANES)), axis=-1,
                      keepdims=True).astype(jnp.int32)
        if kk == 0:
            top = m
        ti = jnp.where(lane == kk, idx, ti)
        tv = jnp.where(lane == kk, jnp.exp(m - top), tv)
        logits = jnp.where(lane == idx, -jnp.inf, logits)
    ti_ref[...] = ti
    tg_ref[...] = tv / jnp.sum(tv, axis=-1, keepdims=True)


def _merge(alpha, n_experts, rt, oa_p, oa_s, ob, proj, xp, xs, gt, sc, sh, pa, pb, wo, lg, lb, rw, rb,
           col_ga, col_gb):
    d = xp.shape[1]
    tm = rt.tm
    rows = rt.n * tm
    const = lambda shape: pl.BlockSpec(shape, lambda i: (0,) * len(shape), pipeline_mode=pl.Buffered(1))
    return pl.pallas_call(
        functools.partial(_merge_kernel, alpha, n_experts, rt.n_p),
        grid=(rt.n,),
        in_specs=[rt.prompt(V_W), rt.sample(V_W), rt.joint(SG_W),
                  rt.joint(d, col_ga // d), rt.joint(d, col_gb // d),
                  rt.prompt(d), rt.sample(d),
                  rt.seq_vec(d), rt.row_vec(d), rt.seq_vec(d), rt.row_vec(d), rt.seq_vec(d), rt.row_vec(d),
                  const((V_W, d)), const((SG_W, d)), const((d, d)),
                  const((1, d)), const((1, d)), const((d, LANES)), const((1, LANES))],
        out_specs=[rt.joint(d), rt.joint(d), rt.joint(LANES), rt.joint(LANES)],
        out_shape=[jax.ShapeDtypeStruct((rows, d), F32),
                   jax.ShapeDtypeStruct((rows, d), F32),
                   jax.ShapeDtypeStruct((rows, LANES), jnp.int32),
                   jax.ShapeDtypeStruct((rows, LANES), F32)],
        compiler_params=_params("arbitrary"),
        name="merge",
    )(oa_p, oa_s, ob, proj, proj, xp, xs, gt[0], gt[1], sc[0], sc[1], sh[0], sh[1],
      pa, pb, wo, lg, lb, rw, rb)


def _dispatch_kernel(nblk_ref, tok_ref, h_hbm, o_ref, buf, sem):
    i = pl.program_id(0)
    tm = o_ref.shape[0]

    @pl.when(i < nblk_ref[0])
    def _():
        def start(r, carry):
            pltpu.make_async_copy(h_hbm.at[pl.ds(tok_ref[0, r], 1)], buf.at[pl.ds(r, 1)], sem).start()
            return carry

        lax.fori_loop(0, tm, start, 0, unroll=8)
        pltpu.make_async_copy(h_hbm.at[pl.ds(0, tm)], buf, sem).wait()
        o_ref[...] = buf[...].astype(BF16)

    @pl.when(i >= nblk_ref[0])
    def _():
        o_ref[...] = jnp.zeros(o_ref.shape, o_ref.dtype)


def _dispatch(nblk_used, slot_tok, h2, tm):
    nslot = slot_tok.shape[0]
    nblk = nslot // tm
    d = h2.shape[1]
    grid_spec = pltpu.PrefetchScalarGridSpec(
        num_scalar_prefetch=1,
        grid=(nblk,),
        in_specs=[pl.BlockSpec((None, 1, tm), lambda i, nb: (i, 0, 0), memory_space=pltpu.SMEM),
                  pl.BlockSpec(memory_space=pl.ANY)],
        out_specs=pl.BlockSpec((tm, d), lambda i, nb: (i, 0)),
        scratch_shapes=[pltpu.VMEM((tm, d), F32), pltpu.SemaphoreType.DMA(())],
    )
    return pl.pallas_call(
        _dispatch_kernel,
        grid_spec=grid_spec,
        out_shape=jax.ShapeDtypeStruct((nslot, d), BF16),
        compiler_params=_params("arbitrary"),
        name="dispatch",
    )(nblk_used.reshape(1), slot_tok.reshape(nblk, 1, tm), h2)


def _moe_kernel(nj, e_ref, j_ref, xb_ref, ob_ref, r_ref, flag_ref,
                x_ref, wg_ref, wu_ref, wd_ref, bg_ref, bu_ref, bd_ref, o_ref,
                wg_scr, wu_scr, wd_scr, acc_scr):
    s = pl.program_id(0)
    flags = flag_ref[s]
    j = j_ref[s]
    r = r_ref[s]

    @pl.when((flags & 2) != 0)
    def _():
        wg_scr[...] = wg_ref[...].astype(BF16)
        wu_scr[...] = wu_ref[...].astype(BF16)
        wd_scr[...] = wd_ref[...].astype(BF16)

    @pl.when(flags == 0)
    def _():
        o_ref[...] = jnp.zeros(o_ref.shape, F32)

    @pl.when((flags & 1) != 0)
    def _():
        x = x_ref[...]
        gate = jnp.dot(x, wg_scr[...], preferred_element_type=F32) + bg_ref[...]
        up = jnp.dot(x, wu_scr[...], preferred_element_type=F32) + bu_ref[...]
        gate = jnp.minimum(gate, SWIGLU_LIMIT)
        up = jnp.clip(up, -SWIGLU_LIMIT, SWIGLU_LIMIT)
        act = (up + 1.0) * gate * _sigmoid(SWIGLU_ALPHA * gate)
        y = jnp.dot(act.astype(BF16), wd_scr[...], preferred_element_type=F32)

        @pl.when(j == 0)
        def _():
            acc_scr[r] = y

        @pl.when(jnp.logical_and(j > 0, j < nj - 1))
        def _():
            acc_scr[r] = acc_scr[r] + y

        @pl.when(j == nj - 1)
        def _():
            o_ref[...] = acc_scr[r] + y + bd_ref[...]


def _moe(items, xb, w_gu, b_gu, w_dn, b_dn, tm, tf, subs):
    item_e, item_j, item_xb, item_ob, item_r, item_flag = items
    n_items = item_e.shape[0]
    nslot, d = xb.shape
    n_exp, _, f2 = w_gu.shape
    f = f2 // 2
    nj = f // tf
    assert nj >= 2
    grid_spec = pltpu.PrefetchScalarGridSpec(
        num_scalar_prefetch=6,
        grid=(n_items,),
        in_specs=[pl.BlockSpec((tm, d), lambda s, e, j, xbk, obk, r, fl: (xbk[s], 0)),
                  pl.BlockSpec((None, d, tf), lambda s, e, j, xbk, obk, r, fl: (e[s], 0, j[s])),
                  pl.BlockSpec((None, d, tf), lambda s, e, j, xbk, obk, r, fl: (e[s], 0, nj + j[s])),
                  pl.BlockSpec((None, tf, d), lambda s, e, j, xbk, obk, r, fl: (e[s], j[s], 0)),
                  pl.BlockSpec((None, 1, tf), lambda s, e, j, xbk, obk, r, fl: (e[s], 0, j[s])),
                  pl.BlockSpec((None, 1, tf), lambda s, e, j, xbk, obk, r, fl: (e[s], 0, nj + j[s])),
                  pl.BlockSpec((None, 1, d), lambda s, e, j, xbk, obk, r, fl: (e[s], 0, 0))],
        out_specs=pl.BlockSpec((tm, d), lambda s, e, j, xbk, obk, r, fl: (obk[s], 0)),
        scratch_shapes=[pltpu.VMEM((d, tf), BF16), pltpu.VMEM((d, tf), BF16), pltpu.VMEM((tf, d), BF16),
                        pltpu.VMEM((subs, tm, d), F32)],
    )
    return pl.pallas_call(
        functools.partial(_moe_kernel, nj),
        grid_spec=grid_spec,
        out_shape=jax.ShapeDtypeStruct((nslot, d), F32),
        compiler_params=_params("arbitrary"),
        name="moe",
    )(item_e, item_j, item_xb, item_ob, item_r, item_flag,
      xb, w_gu, w_gu, w_dn, b_gu.reshape(n_exp, 1, f2), b_gu.reshape(n_exp, 1, f2), b_dn.reshape(n_exp, 1, d))


def _combine_kernel(alpha, n_p, dest_ref, yb_hbm, tg_ref, x1_ref, gtp_ref, gts_ref, lg_ref, lb_ref,
                    op_ref, os_ref, buf, sem):
    tq = x1_ref.shape[0]

    def start(t, carry):
        for kk in range(TOP_K):
            pltpu.make_async_copy(yb_hbm.at[pl.ds(dest_ref[0, t * TOP_K + kk], 1)],
                                  buf.at[kk, pl.ds(t, 1)], sem).start()
        return carry

    lax.fori_loop(0, tq, start, 0, unroll=4)
    for kk in range(TOP_K):
        pltpu.make_async_copy(yb_hbm.at[pl.ds(0, tq)], buf.at[kk], sem).wait()
    tg = tg_ref[...]
    y = jnp.zeros(x1_ref.shape, F32)
    for kk in range(TOP_K):
        y = y + buf[kk] * tg[:, kk:kk + 1]
    is_s = pl.program_id(0) >= n_p
    out = _layer_norm(alpha * x1_ref[...] + _pick(is_s, gtp_ref, gts_ref) * y, lg_ref[...], lb_ref[...])

    @pl.when(jnp.logical_not(is_s))
    def _():
        op_ref[...] = out

    @pl.when(is_s)
    def _():
        os_ref[...] = out


def _combine(alpha, rt, dest, yb, tg, x1, gt, lg, lb):
    d = x1.shape[1]
    tq = rt.tm
    return pl.pallas_call(
        functools.partial(_combine_kernel, alpha, rt.n_p),
        grid=(rt.n,),
        in_specs=[pl.BlockSpec((None, 1, tq * TOP_K), lambda i: (i, 0, 0), memory_space=pltpu.SMEM),
                  pl.BlockSpec(memory_space=pl.ANY),
                  rt.joint(LANES), rt.joint(d), rt.seq_vec(d), rt.row_vec(d),
                  pl.BlockSpec((1, d), lambda i: (0, 0)),
                  pl.BlockSpec((1, d), lambda i: (0, 0))],
        out_specs=[rt.prompt(d), rt.sample(d, single=False)],
        out_shape=[jax.ShapeDtypeStruct((rt.n_p * tq, d), F32), jax.ShapeDtypeStruct((rt.n_s * tq, d), F32)],
        scratch_shapes=[pltpu.VMEM((TOP_K, tq, d), F32), pltpu.SemaphoreType.DMA(())],
        compiler_params=_params("arbitrary"),
        name="combine",
    )(dest.reshape(rt.n, 1, tq * TOP_K), yb, tg, x1, gt[0], gt[1], lg, lb)


def _routing(ti, n_experts, tm, nj, subs):
    t = ti.shape[0]
    n_assign = t * TOP_K
    nb_max = n_assign // tm + n_experts
    onehot = (ti[:, :, None] == jnp.arange(n_experts, dtype=jnp.int32)[None, None, :]).astype(jnp.int32)
    per_tok = jnp.sum(onehot, axis=1)
    cum = jnp.cumsum(per_tok, axis=0)
    counts = cum[-1]
    rank = jnp.take_along_axis(cum, ti, axis=1) - 1
    nblk = (counts + tm - 1) // tm
    blk_end = jnp.cumsum(nblk)
    blk_start = blk_end - nblk
    dest = blk_start[ti] * tm + rank
    tok = jnp.broadcast_to(jnp.arange(t, dtype=jnp.int32)[:, None], (t, TOP_K))
    slot_tok = jnp.zeros((nb_max * tm,), jnp.int32).at[dest.reshape(-1)].set(tok.reshape(-1))
    total_blk = blk_end[-1]
    blocks = jnp.arange(nb_max, dtype=jnp.int32)
    blk_e = jnp.minimum(jnp.sum((blk_end[None, :] <= blocks[:, None]).astype(jnp.int32), axis=1), n_experts - 1)
    r_in_e = blocks - blk_start[blk_e]
    g0 = blk_start[blk_e] + (r_in_e // subs) * subs
    nsub = jnp.minimum(subs, nblk[blk_e] - (r_in_e // subs) * subs)
    p = jnp.arange(nb_max * nj, dtype=jnp.int32)
    bp = p // nj
    valid = bp < total_blk
    last = jnp.maximum(total_blk - 1, 0)
    bq = jnp.where(valid, bp, last)
    gq, nq, eq = g0[bq], jnp.maximum(nsub[bq], 1), blk_e[bq]
    local = p - nj * gq
    jq = jnp.where(valid, local // nq, nj - 1)
    rq = jnp.where(valid, local % nq, nq - 1)
    item_xb = gq + rq
    item_ob = jnp.where(valid, jnp.where(jq == nj - 1, gq + rq, gq), bp)
    flags = valid.astype(jnp.int32) + 2 * (valid & (rq == 0)).astype(jnp.int32)
    items = (eq.astype(jnp.int32), jq.astype(jnp.int32), item_xb.astype(jnp.int32),
             item_ob.astype(jnp.int32), rq.astype(jnp.int32), flags)
    return dest.astype(jnp.int32), slot_tok, total_blk.astype(jnp.int32), items


def _rearranged_w_in(w_in, d):
    o_a = QKV_W + V_W
    o_u = o_a + 2 * DN_HEADS
    ab = jnp.pad(w_in[:, o_a:o_u], ((0, 0), (0, LANES - 2 * DN_HEADS)))
    w = jnp.concatenate([w_in[:, :o_a], w_in[:, o_u:], ab], axis=1).astype(BF16)
    cols = dict(z=QKV_W, u=QKV_W + V_W, v=QKV_W + V_W + SG_W, ga=QKV_W + V_W + 2 * SG_W,
                gb=QKV_W + V_W + 2 * SG_W + d, ab=QKV_W + V_W + 2 * SG_W + 2 * d)
    return w, cols


def kernel(x_prompt, x_sample, state_conv_qkv, state_delta, c_prompt, c_sample, w_ada, b_ada, w_in, w_conv, a_log, dt_bias, o_norm_g, sg_ln_g, sg_ln_b, w_s, b_s, p_a, p_b, w_out, ln1_g, ln1_b, router_w, router_b, w_gu, b_gu, w_dn, b_dn, ln2_g, ln2_b):
    depth = w_ada.shape[0]
    alpha = float((2 * depth) ** 0.25)
    bp, seq, d = x_prompt.shape
    bs, lt, _ = x_sample.shape
    assert lt == DN_CONV and seq % SG_CHUNK == 0
    n_experts = router_w.shape[2]
    rows_p, rows_s = bp * seq, bs * lt
    xp = x_prompt.reshape(rows_p, d)
    xs = x_sample.reshape(rows_s, d)
    c_all = jnp.concatenate([c_prompt, c_sample], axis=0)
    tile = lambda cap: _Rows(math.gcd(math.gcd(cap, rows_s), seq), rows_p, rows_s, seq)
    rt_proj, rt_sgu, rt_merge, rt_comb = tile(PROJ_TM), tile(SG_CHUNK), tile(MERGE_TM), tile(COMB_TQ)
    outs = dict(conv_p=[], delta_p=[], conv_s=[], delta_s=[], vrows=[])

    for l in range(depth):
        mod = _adaln(c_all, w_ada[l], b_ada[l])
        mods = jnp.split(mod, 6, axis=1)
        rows_of = [jnp.repeat(m[bp:], lt, axis=0) for m in mods]

        def vec(i, rt):
            return mods[i][:bp].reshape(bp, 1, d), rows_of[i].reshape(rt.n_s, rt.tm, d)

        w_r, col = _rearranged_w_in(w_in[l], d)
        proj = _inproj(rt_proj, xp, xs, vec(1, rt_proj), vec(0, rt_proj), w_r)

        alog = jnp.pad(a_log[l].reshape(1, DN_HEADS), ((0, 0), (0, LANES - DN_HEADS)))
        dtb = jnp.pad(dt_bias[l].reshape(1, DN_HEADS), ((0, 0), (0, LANES - DN_HEADS)))
        og = o_norm_g[l].reshape(1, DN_DV)
        conv0 = jnp.zeros((bp, DN_CONV - 1, QKV_W), F32)
        s0 = jnp.zeros((bp, DN_HEADS, DN_DK, DN_DV), F32)
        oa_p, conv_p, delta_p = _gdn_prompt(proj, conv0, s0, w_conv[l], alog, dtb, og, bp, seq,
                                            col['z'], col['ab'])
        oa_s, conv_s, delta_s = _gdn_sample(proj, rows_p, state_conv_qkv[l], state_delta[l], w_conv[l],
                                            alog, dtb, og, bs, col['z'], col['ab'])

        ts = rt_sgu.tm
        assert ts == SG_CHUNK
        ws_p = jnp.tril(w_s[l][:, :ts, :ts]).astype(BF16)
        bias_p = jnp.repeat(b_s[l].T[:ts], SG_CH, axis=1)
        eye = jnp.eye(ts // lt, dtype=F32)
        ws_s = jnp.stack([jnp.kron(eye, jnp.tril(w_s[l, g, :lt, :lt])) for g in range(SG_GROUPS)]).astype(BF16)
        bias_s = jnp.tile(jnp.repeat(b_s[l, :, :lt].T, SG_CH, axis=1), (ts // lt, 1))
        ob, vn_s = _sgu(rt_sgu, proj, sg_ln_g[l].reshape(1, SG_W), sg_ln_b[l].reshape(1, SG_W),
                        ws_p, ws_s, bias_p, bias_s, col['u'], col['v'])

        pa, pb, wo = p_a[l].astype(BF16), p_b[l].astype(BF16), w_out[l].astype(BF16)
        rw = jnp.pad(router_w[l], ((0, 0), (0, LANES - n_experts)))
        rb = jnp.pad(router_b[l].reshape(1, n_experts), ((0, 0), (0, LANES - n_experts)))
        x1, h2, ti, tg = _merge(alpha, n_experts, rt_merge, oa_p, oa_s, ob, proj, xp, xs,
                                vec(2, rt_merge), vec(4, rt_merge), vec(3, rt_merge),
                                pa, pb, wo, ln1_g[l].reshape(1, d), ln1_b[l].reshape(1, d), rw, rb,
                                col['ga'], col['gb'])

        nj = w_dn.shape[2] // MOE_TF
        dest, slot_tok, nblk_used, items = _routing(ti[:, :TOP_K], n_experts, MOE_TM, nj, MOE_SUBS)
        xb = _dispatch(nblk_used, slot_tok, h2, MOE_TM)
        yb = _moe(items, xb, w_gu[l], b_gu[l], w_dn[l], b_dn[l], MOE_TM, MOE_TF, MOE_SUBS)
        xp, xs = _combine(alpha, rt_comb, dest, yb, tg, x1, vec(5, rt_comb),
                          ln2_g[l].reshape(1, d), ln2_b[l].reshape(1, d))

        outs['conv_p'].append(conv_p)
        outs['delta_p'].append(delta_p)
        outs['conv_s'].append(conv_s)
        outs['delta_s'].append(delta_s)
        outs['vrows'].append(vn_s.reshape(bs, lt, SG_W))

    return (xp.reshape(bp, seq, d), xs.reshape(bs, lt, d),
            jnp.stack(outs['conv_p']), jnp.stack(outs['delta_p']),
            jnp.stack(outs['conv_s']), jnp.stack(outs['delta_s']), jnp.stack(outs['vrows']))
```

```python
import functools
import math

import jax
import jax.numpy as jnp
from jax import lax
from jax.experimental import pallas as pl
from jax.experimental.pallas import tpu as pltpu

F32 = jnp.float32
BF16 = jnp.bfloat16
HIGHEST = lax.Precision.HIGHEST

DN_HEADS = 8
DN_DK = 128
DN_DV = 128
DN_CONV = 4
DN_CHUNK = 64
SG_GROUPS = 8
SG_CH = 128
SG_CHUNK = 128
TOP_K = 4
SWIGLU_LIMIT = 7.0
SWIGLU_ALPHA = 1.702
LN_EPS = 1e-5
NORM_EPS = 1e-6
QK_W = DN_HEADS * DN_DK
V_W = DN_HEADS * DN_DV
QKV_W = 2 * QK_W + V_W
SG_W = SG_GROUPS * SG_CH

LANES = 128
SUBLANES = 8
VMEM_LIMIT = 56 * 1024 * 1024

PROJ_TM = 512
PROJ_TN = 1152
MERGE_TM = 256
MOE_TM = 256
MOE_TF = 512
MOE_SUBS = 6
COMB_TQ = 128
SAMPLE_BT = 4
SAMPLE_TP = 8
GDN_HG = 4


def _sigmoid(x):
    return 1.0 / (1.0 + jnp.exp(-x))


def _silu(x):
    return x * _sigmoid(x)


def _dot(a, b):
    return jnp.dot(a.astype(BF16), b.astype(BF16), preferred_element_type=F32)


def _dot_nt(a, b):
    return lax.dot_general(a.astype(BF16), b.astype(BF16), (((1,), (1,)), ((), ())),
                           preferred_element_type=F32)


def _dot_hi(a, b):
    return jnp.dot(a, b, precision=HIGHEST, preferred_element_type=F32)


def _layer_norm(x, g, b):
    mu = jnp.mean(x, axis=-1, keepdims=True)
    xc = x - mu
    var = jnp.mean(xc * xc, axis=-1, keepdims=True)
    return xc * lax.rsqrt(var + LN_EPS) * g + b


def _params(*sem):
    return pltpu.CompilerParams(dimension_semantics=sem, vmem_limit_bytes=VMEM_LIMIT)


class _Rows:
    def __init__(self, tm, rows_p, rows_s, seq):
        assert rows_p % tm == 0 and rows_s % tm == 0 and seq % tm == 0, (tm, rows_p, rows_s, seq)
        self.tm, self.n_p, self.n_s = tm, rows_p // tm, rows_s // tm
        self.tiles_per_seq = seq // tm
        self.bp = rows_p // seq

    @property
    def n(self):
        return self.n_p + self.n_s

    def prompt(self, width, col=0):
        return pl.BlockSpec((self.tm, width), lambda i, *_: (jnp.minimum(i, self.n_p - 1), col))

    def sample(self, width, col=0, single=True):
        mode = dict(pipeline_mode=pl.Buffered(1)) if single else {}
        return pl.BlockSpec((self.tm, width), lambda i, *_: (jnp.maximum(i - self.n_p, 0), col), **mode)

    def joint(self, width, col=0):
        return pl.BlockSpec((self.tm, width), lambda i, *_: (i, col))

    def seq_vec(self, d):
        return pl.BlockSpec((None, 1, d), lambda i, *_: (jnp.minimum(i // self.tiles_per_seq, self.bp - 1), 0, 0))

    def row_vec(self, d):
        return pl.BlockSpec((None, self.tm, d), lambda i, *_: (jnp.maximum(i - self.n_p, 0), 0, 0),
                            pipeline_mode=pl.Buffered(1))


def _pick(is_sample, prompt_ref, sample_ref):
    return jnp.where(is_sample, sample_ref[...], prompt_ref[...])


def _adaln_kernel(c_ref, w_ref, b_ref, o_ref):
    o_ref[...] = _dot(_silu(c_ref[...]), w_ref[...]) + b_ref[...]


def _adaln(c, w, b):
    rows, d = c.shape
    n = w.shape[1]
    tn = 1024
    return pl.pallas_call(
        _adaln_kernel,
        grid=(n // tn,),
        in_specs=[pl.BlockSpec((rows, d), lambda j: (0, 0)),
                  pl.BlockSpec((d, tn), lambda j: (0, j)),
                  pl.BlockSpec((1, tn), lambda j: (0, j))],
        out_specs=pl.BlockSpec((rows, tn), lambda j: (0, j)),
        out_shape=jax.ShapeDtypeStruct((rows, n), F32),
        compiler_params=_params("arbitrary"),
        name="adaln",
    )(c, w, b.reshape(1, n))


def _inproj_kernel(n_p, xp_ref, xs_ref, scp_ref, scs_ref, shp_ref, shs_ref, w_ref, o_ref, h_scr):
    @pl.when(pl.program_id(1) == 0)
    def _():
        is_s = pl.program_id(0) >= n_p
        x = _pick(is_s, xp_ref, xs_ref)
        h_scr[...] = (x * (1.0 + _pick(is_s, scp_ref, scs_ref)) + _pick(is_s, shp_ref, shs_ref)).astype(BF16)

    o_ref[...] = jnp.dot(h_scr[...], w_ref[...], preferred_element_type=F32)


def _inproj(rt, xp, xs, sc, sh, w):
    d = xp.shape[1]
    nw = w.shape[1]
    tm = rt.tm
    return pl.pallas_call(
        functools.partial(_inproj_kernel, rt.n_p),
        grid=(rt.n, nw // PROJ_TN),
        in_specs=[rt.prompt(d), rt.sample(d), rt.seq_vec(d), rt.row_vec(d), rt.seq_vec(d), rt.row_vec(d),
                  pl.BlockSpec((d, PROJ_TN), lambda i, j: (0, j))],
        out_specs=pl.BlockSpec((tm, PROJ_TN), lambda i, j: (i, j)),
        out_shape=jax.ShapeDtypeStruct((rt.n * tm, nw), F32),
        scratch_shapes=[pltpu.VMEM((tm, d), BF16)],
        compiler_params=_params("arbitrary", "arbitrary"),
        name="inproj",
    )(xp, xs, sc[0], sc[1], sh[0], sh[1], w)


def _softplus(x):
    return jnp.maximum(x, 0.0) + jnp.log1p(jnp.exp(-jnp.abs(x)))


def _split(a):
    hi = a.astype(BF16)
    lo = (a - hi.astype(F32)).astype(BF16)
    return hi, lo


def _dot3(a, b):
    lhs = jnp.concatenate([a[0], a[1], a[0]], axis=1)
    rhs = jnp.concatenate([b[0], b[0], b[1]], axis=0)
    return jnp.dot(lhs, rhs, preferred_element_type=F32)


def _map(f, *lists):
    return [f(*args) for args in zip(*lists)]


def _unit_lower_inverses(mats, rows, cols):
    eye = (rows == cols).astype(F32)
    same = (rows // 16) == (cols // 16)
    n = [jnp.where(same, a, 0.0) for a in mats]
    bs = _map(lambda a, x: _split(a - x), mats, n)
    ns = _map(_split, n)
    n2s = _map(lambda x: _split(_dot3(x, x)), ns)
    n4 = _map(lambda x: _dot(x[0], x[0]), n2s)
    n8 = _map(lambda x: _dot(x, x), n4)
    r = [eye - x for x in n]
    r = _map(lambda x, p: x + _dot3(_split(x), p), r, n2s)
    r = _map(lambda x, p: x + _dot(x, p), r, n4)
    dinv = _map(lambda x, p: x + _dot(x, p), r, n8)
    ds = _map(_split, dinv)
    ms = _map(lambda x, b: _split(_dot3(x, b)), ds, bs)
    m2s = _map(lambda x: _split(_dot3(x, x)), ms)
    xs = _map(lambda x, p, q: x + _dot3(p, q), dinv, m2s, ds)
    return _map(lambda x, p: x - _dot3(p, _split(x)), xs, ms)


def _gdn_prompt_kernel(qkv_ref, z_ref, ab_ref, hist_ref, s0_ref, wconv_ref, alog_ref, dt_ref, og_ref,
                       o_ref, conv_ref, s_ref, xp_scr):
    n = pl.program_id(1)
    c = DN_CHUNK
    pad = SUBLANES
    nr = GDN_HG * c
    ngrp = DN_HEADS // GDN_HG

    @pl.when(n == 0)
    def _():
        xp_scr[pl.ds(0, pad), :] = jnp.zeros((pad, QKV_W), F32)
        xp_scr[pl.ds(pad - (DN_CONV - 1), DN_CONV - 1), :] = hist_ref[...]
        s_ref[...] = s0_ref[...]

    x = qkv_ref[...]
    xp_scr[pl.ds(pad, c), :] = x
    y = jnp.zeros((c, QKV_W), F32)
    for j in range(DN_CONV):
        y = y + xp_scr[pl.ds(pad - (DN_CONV - 1) + j, c), :] * wconv_ref[pl.ds(j, 1), :]
    y = _silu(y)
    tail = xp_scr[pl.ds(c + pad - (DN_CONV - 1), DN_CONV - 1), :]
    conv_ref[...] = tail
    xp_scr[pl.ds(pad - (DN_CONV - 1), DN_CONV - 1), :] = tail

    ab = ab_ref[...]
    g = -jnp.exp(alog_ref[...]) * _softplus(ab + dt_ref[...])
    beta_all = _sigmoid(ab)
    r64 = lax.broadcasted_iota(jnp.int32, (c, c), 0)
    c64 = lax.broadcasted_iota(jnp.int32, (c, c), 1)
    gc = _dot_hi((r64 >= c64).astype(F32), g)
    z = z_ref[...]
    og = og_ref[...]

    rows = lax.broadcasted_iota(jnp.int32, (nr, nr), 0)
    cols = lax.broadcasted_iota(jnp.int32, (nr, nr), 1)
    same = (rows // c) == (cols // c)
    causal = same & (rows >= cols)
    strict = same & (rows > cols)
    rowhead = lax.broadcasted_iota(jnp.int32, (nr, 1), 0) // c
    groups = [range(grp * GDN_HG, (grp + 1) * GDN_HG) for grp in range(ngrp)]

    def stack(heads, off, width):
        return jnp.concatenate([y[:, off + h * width:off + (h + 1) * width] for h in heads], axis=0)

    def l2n(t):
        return t * lax.rsqrt(jnp.sum(t * t, axis=-1, keepdims=True) + NORM_EPS)

    q = [l2n(stack(hs, 0, DN_DK)) * (DN_DK ** -0.5) for hs in groups]
    k = [l2n(stack(hs, QK_W, DN_DK)) for hs in groups]
    v = [stack(hs, 2 * QK_W, DN_DV) for hs in groups]
    beta = [jnp.concatenate([beta_all[:, DN_HEADS + h:DN_HEADS + h + 1] for h in hs], axis=0) for hs in groups]
    gcf = [jnp.concatenate([jnp.broadcast_to(gc[:, h:h + 1], (c, LANES)) for h in hs], axis=0) for hs in groups]
    gcc = [t[:, 0:1] for t in gcf]
    grow = [t.T[0:1, :] for t in gcf]
    glast = [[gc[c - 1:c, h:h + 1] for h in hs] for hs in groups]
    gtot = [jnp.concatenate([jnp.broadcast_to(t, (c, 1)) for t in gl], axis=0) for gl in glast]
    decay = _map(lambda a, b: jnp.where(causal, jnp.exp(jnp.where(causal, a - b, 0.0)), 0.0), gcc, grow)
    egc = _map(jnp.exp, gcc)
    kb = _map(lambda a, b: a * b, k, beta)
    a_mat = _map(lambda a, b, dd: jnp.where(strict, _dot_nt(a, b) * dd, 0.0), kb, k, decay)
    ts = _map(_split, _unit_lower_inverses(a_mat, rows, cols))
    u = _map(lambda t, a, b: _dot3(t, _split(a * b)), ts, v, beta)
    w = _map(lambda t, a, b: _dot3(t, _split(a * b)), ts, kb, egc)
    attn = _map(lambda a, b, dd: _dot_nt(a, b) * dd, q, k, decay)
    qg = _map(lambda a, b: a * b, q, egc)
    kd_t = _map(lambda a, b, cc: (a * jnp.exp(b - cc)).T, k, gtot, gcc)

    for gi, hs in enumerate(groups):
        v_news, qss = [], []
        for hl, h in enumerate(hs):
            sl = slice(hl * c, (hl + 1) * c)
            rs = _dot(jnp.concatenate([w[gi][sl], qg[gi][sl]], axis=0), s_ref[h])
            v_news.append(u[gi][sl] - rs[:c])
            qss.append(rs[c:])
        v_new = jnp.concatenate(v_news, axis=0)
        o = jnp.concatenate(qss, axis=0) + _dot(attn[gi], v_new)
        for hl, h in enumerate(hs):
            upd = _dot(kd_t[gi], jnp.where(rowhead == hl, v_new, 0.0))
            s_ref[h] = s_ref[h] * jnp.exp(glast[gi][hl]) + upd
        o = o * lax.rsqrt(jnp.mean(o * o, axis=-1, keepdims=True) + NORM_EPS) * og
        for hl, h in enumerate(hs):
            o_ref[:, h * DN_DV:(h + 1) * DN_DV] = o[hl * c:(hl + 1) * c] * _silu(z[:, h * DN_DV:(h + 1) * DN_DV])


def _gdn_prompt(proj, hist, s0, wconv, alog, dtb, og, batch, seq, col_z, col_ab):
    c = DN_CHUNK
    nchunk = seq // c
    rows = batch * seq
    row = lambda b, n: b * nchunk + n
    return pl.pallas_call(
        _gdn_prompt_kernel,
        grid=(batch, nchunk),
        in_specs=[pl.BlockSpec((c, QKV_W), lambda b, n: (row(b, n), 0)),
                  pl.BlockSpec((c, V_W), lambda b, n: (row(b, n), col_z // V_W)),
                  pl.BlockSpec((c, LANES), lambda b, n: (row(b, n), col_ab // LANES)),
                  pl.BlockSpec((None, DN_CONV - 1, QKV_W), lambda b, n: (b, 0, 0)),
                  pl.BlockSpec((None, DN_HEADS, DN_DK, DN_DV), lambda b, n: (b, 0, 0, 0)),
                  pl.BlockSpec((DN_CONV, QKV_W), lambda b, n: (0, 0)),
                  pl.BlockSpec((1, LANES), lambda b, n: (0, 0)),
                  pl.BlockSpec((1, LANES), lambda b, n: (0, 0)),
                  pl.BlockSpec((1, DN_DV), lambda b, n: (0, 0))],
        out_specs=[pl.BlockSpec((c, V_W), lambda b, n: (row(b, n), 0)),
                   pl.BlockSpec((None, DN_CONV - 1, QKV_W), lambda b, n: (b, 0, 0)),
                   pl.BlockSpec((None, DN_HEADS, DN_DK, DN_DV), lambda b, n: (b, 0, 0, 0))],
        out_shape=[jax.ShapeDtypeStruct((rows, V_W), F32),
                   jax.ShapeDtypeStruct((batch, DN_CONV - 1, QKV_W), F32),
                   jax.ShapeDtypeStruct((batch, DN_HEADS, DN_DK, DN_DV), F32)],
        scratch_shapes=[pltpu.VMEM((c + SUBLANES, QKV_W), F32)],
        compiler_params=_params("arbitrary", "arbitrary"),
        name="gdn_prompt",
    )(proj, proj, proj, hist, s0, wconv, alog, dtb, og)


def _gdn_sample_kernel(qkv_ref, z_ref, ab_ref, hist_ref, s0_ref, wconv_ref, alog_ref, dt_ref, og_ref,
                       o_ref, conv_ref, s_ref, xp_scr, ab_scr, wq_scr, r_scr, kdt_scr, vn_scr, gl_scr):
    bt, tp, lt = SAMPLE_BT, SAMPLE_TP, DN_CONV
    nblk = DN_HEADS * bt
    nrow = nblk * tp
    hist_rows = DN_CONV - 1

    xp_scr[...] = jnp.zeros(xp_scr.shape, F32)
    ab_scr[...] = jnp.zeros(ab_scr.shape, F32)
    ys = []
    for b in range(bt):
        xp_scr[b, pl.ds(SUBLANES - hist_rows, hist_rows), :] = hist_ref[b]
        xp_scr[b, pl.ds(SUBLANES, lt), :] = qkv_ref[pl.ds(b * lt, lt), :]
        yb = jnp.zeros((tp, QKV_W), F32)
        for j in range(DN_CONV):
            yb = yb + xp_scr[b, pl.ds(SUBLANES - hist_rows + j, tp), :] * wconv_ref[pl.ds(j, 1), :]
        ys.append(_silu(yb))
        conv_ref[b] = xp_scr[b, pl.ds(SUBLANES + lt - hist_rows, hist_rows), :]
        ab_scr[b, pl.ds(0, lt), :] = ab_ref[pl.ds(b * lt, lt), :]
    y = jnp.concatenate(ys, axis=0)
    ab = jnp.concatenate([ab_scr[b] for b in range(bt)], axis=0)
    tok = lax.broadcasted_iota(jnp.int32, (bt * tp, 1), 0) % tp
    real = tok < lt
    g_all = jnp.where(real, -jnp.exp(alog_ref[...]) * _softplus(ab + dt_ref[...]), 0.0)
    beta_all = jnp.where(real, _sigmoid(ab), 0.0)

    def heads_to_rows(t, off, width):
        return jnp.concatenate([t[:, off + h * width:off + (h + 1) * width] for h in range(DN_HEADS)], axis=0)

    realr = jnp.concatenate([real] * DN_HEADS, axis=0)
    q = jnp.where(realr, heads_to_rows(y, 0, DN_DK), 0.0)
    k = jnp.where(realr, heads_to_rows(y, QK_W, DN_DK), 0.0)
    v = jnp.where(realr, heads_to_rows(y, 2 * QK_W, DN_DV), 0.0)
    gcol = jnp.concatenate([g_all[:, h:h + 1] for h in range(DN_HEADS)], axis=0)
    beta = jnp.concatenate([beta_all[:, DN_HEADS + h:DN_HEADS + h + 1] for h in range(DN_HEADS)], axis=0)
    q = q * lax.rsqrt(jnp.sum(q * q, axis=-1, keepdims=True) + NORM_EPS) * (DN_DK ** -0.5)
    k = k * lax.rsqrt(jnp.sum(k * k, axis=-1, keepdims=True) + NORM_EPS)

    rows = lax.broadcasted_iota(jnp.int32, (nrow, nrow), 0)
    cols = lax.broadcasted_iota(jnp.int32, (nrow, nrow), 1)
    same = (rows // tp) == (cols // tp)
    causal = same & (rows >= cols)
    strict = same & (rows > cols)
    gfull = jnp.broadcast_to(gcol, (nrow, LANES))
    gc = _dot_hi(causal.astype(F32), gfull)
    gtot = _dot_hi(same.astype(F32), gfull)
    gcc = gc[:, 0:1]
    grow = gc.T[0:1, :]
    decay = jnp.where(causal, jnp.exp(jnp.where(causal, gcc - grow, 0.0)), 0.0)
    egc = jnp.exp(gcc)
    kb = k * beta
    a_mat = jnp.where(strict, _dot_nt(kb, k) * decay, 0.0)
    a_s = _split(a_mat)
    a2 = _dot3(a_s, a_s)
    eye = (rows == cols).astype(F32)
    ts = _split(eye - a_mat + a2 - _dot3(a_s, _split(a2)))
    u = _dot3(ts, _split(v * beta))
    w = _dot3(ts, _split(kb * egc))
    attn = _dot_nt(q, k) * decay
    qg = q * egc
    for i in range(nblk):
        wq_scr[pl.ds(2 * tp * i, tp), :] = w[i * tp:(i + 1) * tp, :]
        wq_scr[pl.ds(2 * tp * i + tp, tp), :] = qg[i * tp:(i + 1) * tp, :]
    kdt_scr[...] = (k * jnp.exp(gtot[:, 0:1] - gcc)).T
    gl_scr[...] = jnp.exp(gtot)

    def read_state(i, carry):
        r0 = pl.multiple_of(i * 2 * tp, 2 * tp)
        r_scr[pl.ds(r0, 2 * tp), :] = _dot(wq_scr[pl.ds(r0, 2 * tp), :], s0_ref[i % bt, i // bt])
        return carry

    lax.fori_loop(0, nblk, read_state, 0, unroll=4)
    ws = jnp.concatenate([r_scr[pl.ds(2 * tp * i, tp), :] for i in range(nblk)], axis=0)
    qs = jnp.concatenate([r_scr[pl.ds(2 * tp * i + tp, tp), :] for i in range(nblk)], axis=0)
    v_new = u - ws
    vn_scr[...] = v_new
    o = qs + _dot(attn, v_new)
    blockid = lax.broadcasted_iota(jnp.int32, (nrow, 1), 0) // tp

    def write_state(i, carry):
        r0 = pl.multiple_of(i * tp, tp)
        upd = _dot(kdt_scr[...], jnp.where(blockid == i, vn_scr[...], 0.0))
        s_ref[i % bt, i // bt] = s0_ref[i % bt, i // bt] * gl_scr[pl.ds(r0, 1), :] + upd
        return carry

    lax.fori_loop(0, nblk, write_state, 0, unroll=4)

    og = og_ref[...]
    o = o * lax.rsqrt(jnp.mean(o * o, axis=-1, keepdims=True) + NORM_EPS) * og
    for b in range(bt):
        for h in range(DN_HEADS):
            zbh = z_ref[pl.ds(b * lt, lt), h * DN_DV:(h + 1) * DN_DV]
            blk = o[(h * bt + b) * tp:(h * bt + b) * tp + lt, :]
            o_ref[pl.ds(b * lt, lt), h * DN_DV:(h + 1) * DN_DV] = blk * _silu(zbh)


def _gdn_sample(proj, row0, hist, s0, wconv, alog, dtb, og, batch, col_z, col_ab):
    bt, tp, lt = SAMPLE_BT, SAMPLE_TP, DN_CONV
    rows = batch * lt
    nrow = DN_HEADS * bt * tp
    blk = bt * lt
    assert row0 % blk == 0 and batch % bt == 0
    b0 = row0 // blk
    return pl.pallas_call(
        _gdn_sample_kernel,
        grid=(batch // bt,),
        in_specs=[pl.BlockSpec((blk, QKV_W), lambda i: (b0 + i, 0)),
                  pl.BlockSpec((blk, V_W), lambda i: (b0 + i, col_z // V_W)),
                  pl.BlockSpec((blk, LANES), lambda i: (b0 + i, col_ab // LANES)),
                  pl.BlockSpec((bt, DN_CONV - 1, QKV_W), lambda i: (i, 0, 0)),
                  pl.BlockSpec((bt, DN_HEADS, DN_DK, DN_DV), lambda i: (i, 0, 0, 0)),
                  pl.BlockSpec((DN_CONV, QKV_W), lambda i: (0, 0)),
                  pl.BlockSpec((1, LANES), lambda i: (0, 0)),
                  pl.BlockSpec((1, LANES), lambda i: (0, 0)),
                  pl.BlockSpec((1, DN_DV), lambda i: (0, 0))],
        out_specs=[pl.BlockSpec((blk, V_W), lambda i: (i, 0)),
                   pl.BlockSpec((bt, DN_CONV - 1, QKV_W), lambda i: (i, 0, 0)),
                   pl.BlockSpec((bt, DN_HEADS, DN_DK, DN_DV), lambda i: (i, 0, 0, 0))],
        out_shape=[jax.ShapeDtypeStruct((rows, V_W), F32),
                   jax.ShapeDtypeStruct((batch, DN_CONV - 1, QKV_W), F32),
                   jax.ShapeDtypeStruct((batch, DN_HEADS, DN_DK, DN_DV), F32)],
        scratch_shapes=[pltpu.VMEM((bt, SUBLANES + tp, QKV_W), F32),
                        pltpu.VMEM((bt, tp, LANES), F32),
                        pltpu.VMEM((2 * nrow, DN_DK), F32),
                        pltpu.VMEM((2 * nrow, DN_DV), F32),
                        pltpu.VMEM((DN_DK, nrow), F32),
                        pltpu.VMEM((nrow, DN_DV), F32),
                        pltpu.VMEM((nrow, LANES), F32)],
        compiler_params=_params("arbitrary"),
        name="gdn_sample",
    )(proj, proj, proj, hist, s0, wconv, alog, dtb, og)


def _sgu_kernel(n_p, u_ref, v_ref, g_ref, b_ref, wsp_ref, wss_ref, bp_ref, bs_ref, o_ref, vn_ref):
    is_s = pl.program_id(0) >= n_p
    u = jax.nn.gelu(u_ref[...])
    vn = _layer_norm(jax.nn.gelu(v_ref[...]), g_ref[...], b_ref[...])
    vn_ref[...] = vn
    bias = _pick(is_s, bp_ref, bs_ref)
    for g in range(SG_GROUPS):
        sl = slice(g * SG_CH, (g + 1) * SG_CH)
        ws = jnp.where(is_s, wss_ref[g], wsp_ref[g])
        mixed = _dot(ws, vn[:, sl]) + bias[:, sl]
        o_ref[:, sl] = u[:, sl] * mixed


def _sgu(rt, proj, ln_g, ln_b, ws_p, ws_s, bias_p, bias_s, col_u, col_v):
    t = rt.tm
    const = lambda shape: pl.BlockSpec(shape, lambda i: (0,) * len(shape))
    return pl.pallas_call(
        functools.partial(_sgu_kernel, rt.n_p),
        grid=(rt.n,),
        in_specs=[rt.joint(SG_W, col_u // SG_W), rt.joint(SG_W, col_v // SG_W),
                  const((1, SG_W)), const((1, SG_W)),
                  const((SG_GROUPS, t, t)), const((SG_GROUPS, t, t)), const((t, SG_W)), const((t, SG_W))],
        out_specs=[rt.joint(SG_W), rt.sample(SG_W, single=False)],
        out_shape=[jax.ShapeDtypeStruct((rt.n * t, SG_W), F32),
                   jax.ShapeDtypeStruct((rt.n_s * t, SG_W), F32)],
        compiler_params=_params("arbitrary"),
        name="sgu",
    )(proj, proj, ln_g, ln_b, ws_p, ws_s, bias_p, bias_s)


def _merge_kernel(alpha, n_experts, n_p, oap_ref, oas_ref, ob_ref, ga_ref, gb_ref, xp_ref, xs_ref,
                  gtp_ref, gts_ref, scp_ref, scs_ref, shp_ref, shs_ref,
                  pa_ref, pb_ref, wo_ref, lg_ref, lb_ref, rw_ref, rb_ref,
                  x1_ref, h2_ref, ti_ref, tg_ref):
    is_s = pl.program_id(0) >= n_p
    oa = _pick(is_s, oap_ref, oas_ref)
    merged = (_sigmoid(ga_ref[...]) * _dot(oa, pa_ref[...])
              + _sigmoid(gb_ref[...]) * _dot(ob_ref[...], pb_ref[...]))
    y = _dot(merged, wo_ref[...])
    x = _pick(is_s, xp_ref, xs_ref)
    x1 = _layer_norm(alpha * x + _pick(is_s, gtp_ref, gts_ref) * y, lg_ref[...], lb_ref[...])
    x1_ref[...] = x1
    h2 = x1 * (1.0 + _pick(is_s, scp_ref, scs_ref)) + _pick(is_s, shp_ref, shs_ref)
    h2_ref[...] = h2
    hs = _split(h2)
    logits = jnp.dot(jnp.concatenate([hs[0], hs[1], hs[0]], axis=1), rw_ref[...],
                     preferred_element_type=F32) + rb_ref[...]
    lane = lax.broadcasted_iota(jnp.int32, logits.shape, 1)
    logits = jnp.where(lane < n_experts, logits, -jnp.inf)
    ti = jnp.zeros(logits.shape, jnp.int32)
    tv = jnp.zeros(logits.shape, F32)
    top = None
    for kk in range(TOP_K):
        m = jnp.max(logits, axis=-1, keepdims=True)
        idx = jnp.min(jnp.where(logits == m, lane.astype(F32), float(LANES)), axis=-1,
                      keepdims=True).astype(jnp.int32)
        if kk == 0:
            top = m
        ti = jnp.where(lane == kk, idx, ti)
        tv = jnp.where(lane == kk, jnp.exp(m - top), tv)
        logits = jnp.where(lane == idx, -jnp.inf, logits)
    ti_ref[...] = ti
    tg_ref[...] = tv / jnp.sum(tv, axis=-1, keepdims=True)


def _merge(alpha, n_experts, rt, oa_p, oa_s, ob, proj, xp, xs, gt, sc, sh, pa, pb, wo, lg, lb, rw, rb,
           col_ga, col_gb):
    d = xp.shape[1]
    tm = rt.tm
    rows = rt.n * tm
    const = lambda shape: pl.BlockSpec(shape, lambda i: (0,) * len(shape), pipeline_mode=pl.Buffered(1))
    return pl.pallas_call(
        functools.partial(_merge_kernel, alpha, n_experts, rt.n_p),
        grid=(rt.n,),
        in_specs=[rt.prompt(V_W), rt.sample(V_W), rt.joint(SG_W),
                  rt.joint(d, col_ga // d), rt.joint(d, col_gb // d),
                  rt.prompt(d), rt.sample(d),
                  rt.seq_vec(d), rt.row_vec(d), rt.seq_vec(d), rt.row_vec(d), rt.seq_vec(d), rt.row_vec(d),
                  const((V_W, d)), const((SG_W, d)), const((d, d)),
                  const((1, d)), const((1, d)), const((3 * d, LANES)), const((1, LANES))],
        out_specs=[rt.joint(d), rt.joint(d), rt.joint(LANES), rt.joint(LANES)],
        out_shape=[jax.ShapeDtypeStruct((rows, d), F32),
                   jax.ShapeDtypeStruct((rows, d), F32),
                   jax.ShapeDtypeStruct((rows, LANES), jnp.int32),
                   jax.ShapeDtypeStruct((rows, LANES), F32)],
        compiler_params=_params("arbitrary"),
        name="merge",
    )(oa_p, oa_s, ob, proj, proj, xp, xs, gt[0], gt[1], sc[0], sc[1], sh[0], sh[1],
      pa, pb, wo, lg, lb, rw, rb)


def _dispatch_kernel(nblk_ref, tok_ref, tokn_ref, h_hbm, o_ref, buf, sem):
    i = pl.program_id(0)
    nb = nblk_ref[0]
    tm = o_ref.shape[0]
    slot = i % 2

    def issue(ids_ref, sl):
        def start(r, carry):
            pltpu.make_async_copy(h_hbm.at[pl.ds(ids_ref[0, r], 1)], buf.at[sl, pl.ds(r, 1)], sem.at[sl]).start()
            return carry

        lax.fori_loop(0, tm, start, 0, unroll=8)

    @pl.when(jnp.logical_and(i == 0, nb > 0))
    def _():
        issue(tok_ref, 0)

    @pl.when(i + 1 < nb)
    def _():
        issue(tokn_ref, 1 - slot)

    @pl.when(i < nb)
    def _():
        pltpu.make_async_copy(h_hbm.at[pl.ds(0, tm)], buf.at[slot], sem.at[slot]).wait()
        o_ref[...] = buf[slot].astype(BF16)

    @pl.when(i >= nb)
    def _():
        o_ref[...] = jnp.zeros(o_ref.shape, o_ref.dtype)


def _dispatch(nblk_used, slot_tok, h2, tm):
    nslot = slot_tok.shape[0]
    nblk = nslot // tm
    d = h2.shape[1]
    ids = slot_tok.reshape(nblk, 1, tm)
    grid_spec = pltpu.PrefetchScalarGridSpec(
        num_scalar_prefetch=1,
        grid=(nblk,),
        in_specs=[pl.BlockSpec((None, 1, tm), lambda i, nb: (i, 0, 0), memory_space=pltpu.SMEM),
                  pl.BlockSpec((None, 1, tm), lambda i, nb: (jnp.minimum(i + 1, nblk - 1), 0, 0),
                               memory_space=pltpu.SMEM),
                  pl.BlockSpec(memory_space=pl.ANY)],
        out_specs=pl.BlockSpec((tm, d), lambda i, nb: (i, 0)),
        scratch_shapes=[pltpu.VMEM((2, tm, d), F32), pltpu.SemaphoreType.DMA((2,))],
    )
    return pl.pallas_call(
        _dispatch_kernel,
        grid_spec=grid_spec,
        out_shape=jax.ShapeDtypeStruct((nslot, d), BF16),
        compiler_params=_params("arbitrary"),
        name="dispatch",
    )(nblk_used.reshape(1), ids, ids, h2)


def _moe_kernel(nj, e_ref, j_ref, xb_ref, ob_ref, r_ref, flag_ref,
                x_ref, wg_ref, wu_ref, wd_ref, bg_ref, bu_ref, bd_ref, o_ref,
                wg_scr, wu_scr, wd_scr, acc_scr):
    s = pl.program_id(0)
    flags = flag_ref[s]
    j = j_ref[s]
    r = r_ref[s]

    @pl.when((flags & 2) != 0)
    def _():
        wg_scr[...] = wg_ref[...].astype(BF16)
        wu_scr[...] = wu_ref[...].astype(BF16)
        wd_scr[...] = wd_ref[...].astype(BF16)

    @pl.when(flags == 0)
    def _():
        o_ref[...] = jnp.zeros(o_ref.shape, F32)

    @pl.when((flags & 1) != 0)
    def _():
        x = x_ref[...]
        gate = jnp.dot(x, wg_scr[...], preferred_element_type=F32) + bg_ref[...]
        up = jnp.dot(x, wu_scr[...], preferred_element_type=F32) + bu_ref[...]
        gate = jnp.minimum(gate, SWIGLU_LIMIT)
        up = jnp.clip(up, -SWIGLU_LIMIT, SWIGLU_LIMIT)
        act = (up + 1.0) * gate * _sigmoid(SWIGLU_ALPHA * gate)
        y = jnp.dot(act.astype(BF16), wd_scr[...], preferred_element_type=F32)

        @pl.when(j == 0)
        def _():
            acc_scr[r] = y

        @pl.when(jnp.logical_and(j > 0, j < nj - 1))
        def _():
            acc_scr[r] = acc_scr[r] + y

        @pl.when(j == nj - 1)
        def _():
            o_ref[...] = acc_scr[r] + y + bd_ref[...]


def _moe(items, xb, w_gu, b_gu, w_dn, b_dn, tm, tf, subs):
    item_e, item_j, item_xb, item_ob, item_r, item_flag = items
    n_items = item_e.shape[0]
    nslot, d = xb.shape
    n_exp, _, f2 = w_gu.shape
    f = f2 // 2
    nj = f // tf
    assert nj >= 2
    grid_spec = pltpu.PrefetchScalarGridSpec(
        num_scalar_prefetch=6,
        grid=(n_items,),
        in_specs=[pl.BlockSpec((tm, d), lambda s, e, j, xbk, obk, r, fl: (xbk[s], 0)),
                  pl.BlockSpec((None, d, tf), lambda s, e, j, xbk, obk, r, fl: (e[s], 0, j[s])),
                  pl.BlockSpec((None, d, tf), lambda s, e, j, xbk, obk, r, fl: (e[s], 0, nj + j[s])),
                  pl.BlockSpec((None, tf, d), lambda s, e, j, xbk, obk, r, fl: (e[s], j[s], 0)),
                  pl.BlockSpec((None, 1, tf), lambda s, e, j, xbk, obk, r, fl: (e[s], 0, j[s])),
                  pl.BlockSpec((None, 1, tf), lambda s, e, j, xbk, obk, r, fl: (e[s], 0, nj + j[s])),
                  pl.BlockSpec((None, 1, d), lambda s, e, j, xbk, obk, r, fl: (e[s], 0, 0))],
        out_specs=pl.BlockSpec((tm, d), lambda s, e, j, xbk, obk, r, fl: (obk[s], 0)),
        scratch_shapes=[pltpu.VMEM((d, tf), BF16), pltpu.VMEM((d, tf), BF16), pltpu.VMEM((tf, d), BF16),
                        pltpu.VMEM((subs, tm, d), F32)],
    )
    return pl.pallas_call(
        functools.partial(_moe_kernel, nj),
        grid_spec=grid_spec,
        out_shape=jax.ShapeDtypeStruct((nslot, d), F32),
        compiler_params=_params("arbitrary"),
        name="moe",
    )(item_e, item_j, item_xb, item_ob, item_r, item_flag,
      xb, w_gu, w_gu, w_dn, b_gu.reshape(n_exp, 1, f2), b_gu.reshape(n_exp, 1, f2), b_dn.reshape(n_exp, 1, d))


def _combine_kernel(alpha, n_p, dest_ref, destn_ref, yb_hbm, tg_ref, x1_ref, gtp_ref, gts_ref, lg_ref, lb_ref,
                    op_ref, os_ref, buf, sem):
    i = pl.program_id(0)
    tq = x1_ref.shape[0]
    slot = i % 2

    def issue(ids_ref, sl):
        def start(t, carry):
            for kk in range(TOP_K):
                pltpu.make_async_copy(yb_hbm.at[pl.ds(ids_ref[0, t * TOP_K + kk], 1)],
                                      buf.at[sl, kk, pl.ds(t, 1)], sem.at[sl]).start()
            return carry

        lax.fori_loop(0, tq, start, 0, unroll=4)

    @pl.when(i == 0)
    def _():
        issue(dest_ref, 0)

    @pl.when(i + 1 < pl.num_programs(0))
    def _():
        issue(destn_ref, 1 - slot)

    for kk in range(TOP_K):
        pltpu.make_async_copy(yb_hbm.at[pl.ds(0, tq)], buf.at[slot, kk], sem.at[slot]).wait()
    tg = tg_ref[...]
    y = jnp.zeros(x1_ref.shape, F32)
    for kk in range(TOP_K):
        y = y + buf[slot, kk] * tg[:, kk:kk + 1]
    is_s = i >= n_p
    out = _layer_norm(alpha * x1_ref[...] + _pick(is_s, gtp_ref, gts_ref) * y, lg_ref[...], lb_ref[...])

    @pl.when(jnp.logical_not(is_s))
    def _():
        op_ref[...] = out

    @pl.when(is_s)
    def _():
        os_ref[...] = out


def _combine(alpha, rt, dest, yb, tg, x1, gt, lg, lb):
    d = x1.shape[1]
    tq = rt.tm
    ids = dest.reshape(rt.n, 1, tq * TOP_K)
    return pl.pallas_call(
        functools.partial(_combine_kernel, alpha, rt.n_p),
        grid=(rt.n,),
        in_specs=[pl.BlockSpec((None, 1, tq * TOP_K), lambda i: (i, 0, 0), memory_space=pltpu.SMEM),
                  pl.BlockSpec((None, 1, tq * TOP_K), lambda i: (jnp.minimum(i + 1, rt.n - 1), 0, 0),
                               memory_space=pltpu.SMEM),
                  pl.BlockSpec(memory_space=pl.ANY),
                  rt.joint(LANES), rt.joint(d), rt.seq_vec(d), rt.row_vec(d),
                  pl.BlockSpec((1, d), lambda i: (0, 0)),
                  pl.BlockSpec((1, d), lambda i: (0, 0))],
        out_specs=[rt.prompt(d), rt.sample(d, single=False)],
        out_shape=[jax.ShapeDtypeStruct((rt.n_p * tq, d), F32), jax.ShapeDtypeStruct((rt.n_s * tq, d), F32)],
        scratch_shapes=[pltpu.VMEM((2, TOP_K, tq, d), F32), pltpu.SemaphoreType.DMA((2,))],
        compiler_params=_params("arbitrary"),
        name="combine",
    )(ids, ids, yb, tg, x1, gt[0], gt[1], lg, lb)


def _routing(ti, n_experts, tm, nj, subs):
    t = ti.shape[0]
    n_assign = t * TOP_K
    nb_max = n_assign // tm + n_experts
    onehot = (ti[:, :, None] == jnp.arange(n_experts, dtype=jnp.int32)[None, None, :]).astype(jnp.int32)
    per_tok = jnp.sum(onehot, axis=1)
    cum = jnp.cumsum(per_tok, axis=0)
    counts = cum[-1]
    rank = jnp.take_along_axis(cum, ti, axis=1) - 1
    nblk = (counts + tm - 1) // tm
    blk_end = jnp.cumsum(nblk)
    blk_start = blk_end - nblk
    dest = blk_start[ti] * tm + rank
    tok = jnp.broadcast_to(jnp.arange(t, dtype=jnp.int32)[:, None], (t, TOP_K))
    slot_tok = jnp.zeros((nb_max * tm,), jnp.int32).at[dest.reshape(-1)].set(tok.reshape(-1))
    total_blk = blk_end[-1]
    blocks = jnp.arange(nb_max, dtype=jnp.int32)
    blk_e = jnp.minimum(jnp.sum((blk_end[None, :] <= blocks[:, None]).astype(jnp.int32), axis=1), n_experts - 1)
    r_in_e = blocks - blk_start[blk_e]
    g0 = blk_start[blk_e] + (r_in_e // subs) * subs
    nsub = jnp.minimum(subs, nblk[blk_e] - (r_in_e // subs) * subs)
    p = jnp.arange(nb_max * nj, dtype=jnp.int32)
    bp = p // nj
    valid = bp < total_blk
    last = jnp.maximum(total_blk - 1, 0)
    bq = jnp.where(valid, bp, last)
    gq, nq, eq = g0[bq], jnp.maximum(nsub[bq], 1), blk_e[bq]
    local = p - nj * gq
    jq = jnp.where(valid, local // nq, nj - 1)
    rq = jnp.where(valid, local % nq, nq - 1)
    item_xb = gq + rq
    item_ob = jnp.where(valid, jnp.where(jq == nj - 1, gq + rq, gq), bp)
    flags = valid.astype(jnp.int32) + 2 * (valid & (rq == 0)).astype(jnp.int32)
    items = (eq.astype(jnp.int32), jq.astype(jnp.int32), item_xb.astype(jnp.int32),
             item_ob.astype(jnp.int32), rq.astype(jnp.int32), flags)
    return dest.astype(jnp.int32), slot_tok, total_blk.astype(jnp.int32), items


def _win_prep_kernel(w_ref, o_ref):
    o_a = QKV_W + V_W
    o_u = o_a + 2 * DN_HEADS
    x = w_ref[...]
    rest = x.shape[1] - o_u
    o_ref[:, 0:o_a] = x[:, 0:o_a].astype(BF16)
    o_ref[:, o_a:o_a + rest] = x[:, o_u:].astype(BF16)
    lane = lax.broadcasted_iota(jnp.int32, (x.shape[0], LANES), 1)
    o_ref[:, o_a + rest:] = jnp.where(lane < 2 * DN_HEADS, x[:, o_a:o_a + LANES], 0.0).astype(BF16)


def _rearranged_w_in(w_in, d):
    dm, nw = w_in.shape
    tr = 128
    w = pl.pallas_call(
        _win_prep_kernel,
        grid=(dm // tr,),
        in_specs=[pl.BlockSpec((tr, nw), lambda i: (i, 0))],
        out_specs=pl.BlockSpec((tr, nw - 2 * DN_HEADS + LANES), lambda i: (i, 0)),
        out_shape=jax.ShapeDtypeStruct((dm, nw - 2 * DN_HEADS + LANES), BF16),
        compiler_params=_params("arbitrary"),
        name="w_in_prep",
    )(w_in)
    cols = dict(z=QKV_W, u=QKV_W + V_W, v=QKV_W + V_W + SG_W, ga=QKV_W + V_W + 2 * SG_W,
                gb=QKV_W + V_W + 2 * SG_W + d, ab=QKV_W + V_W + 2 * SG_W + 2 * d)
    return w, cols


def kernel(x_prompt, x_sample, state_conv_qkv, state_delta, c_prompt, c_sample, w_ada, b_ada, w_in, w_conv, a_log, dt_bias, o_norm_g, sg_ln_g, sg_ln_b, w_s, b_s, p_a, p_b, w_out, ln1_g, ln1_b, router_w, router_b, w_gu, b_gu, w_dn, b_dn, ln2_g, ln2_b):
    depth = w_ada.shape[0]
    alpha = float((2 * depth) ** 0.25)
    bp, seq, d = x_prompt.shape
    bs, lt, _ = x_sample.shape
    assert lt == DN_CONV and seq % SG_CHUNK == 0
    n_experts = router_w.shape[2]
    rows_p, rows_s = bp * seq, bs * lt
    xp = x_prompt.reshape(rows_p, d)
    xs = x_sample.reshape(rows_s, d)
    c_all = jnp.concatenate([c_prompt, c_sample], axis=0)
    tile = lambda cap: _Rows(math.gcd(math.gcd(cap, rows_s), seq), rows_p, rows_s, seq)
    rt_proj, rt_sgu, rt_merge, rt_comb = tile(PROJ_TM), tile(SG_CHUNK), tile(MERGE_TM), tile(COMB_TQ)
    outs = dict(conv_p=[], delta_p=[], conv_s=[], delta_s=[], vrows=[])

    for l in range(depth):
        mod = _adaln(c_all, w_ada[l], b_ada[l])
        mods = jnp.split(mod, 6, axis=1)
        rows_of = [jnp.repeat(m[bp:], lt, axis=0) for m in mods]

        def vec(i, rt):
            return mods[i][:bp].reshape(bp, 1, d), rows_of[i].reshape(rt.n_s, rt.tm, d)

        w_r, col = _rearranged_w_in(w_in[l], d)
        proj = _inproj(rt_proj, xp, xs, vec(1, rt_proj), vec(0, rt_proj), w_r)

        alog = jnp.pad(a_log[l].reshape(1, DN_HEADS), ((0, 0), (0, LANES - DN_HEADS)))
        dtb = jnp.pad(dt_bias[l].reshape(1, DN_HEADS), ((0, 0), (0, LANES - DN_HEADS)))
        og = o_norm_g[l].reshape(1, DN_DV)
        conv0 = jnp.zeros((bp, DN_CONV - 1, QKV_W), F32)
        s0 = jnp.zeros((bp, DN_HEADS, DN_DK, DN_DV), F32)
        oa_p, conv_p, delta_p = _gdn_prompt(proj, conv0, s0, w_conv[l], alog, dtb, og, bp, seq,
                                            col['z'], col['ab'])
        oa_s, conv_s, delta_s = _gdn_sample(proj, rows_p, state_conv_qkv[l], state_delta[l], w_conv[l],
                                            alog, dtb, og, bs, col['z'], col['ab'])

        ts = rt_sgu.tm
        assert ts == SG_CHUNK
        ws_p = jnp.tril(w_s[l][:, :ts, :ts]).astype(BF16)
        bias_p = jnp.repeat(b_s[l].T[:ts], SG_CH, axis=1)
        eye = jnp.eye(ts // lt, dtype=F32)
        ws_s = jnp.stack([jnp.kron(eye, jnp.tril(w_s[l, g, :lt, :lt])) for g in range(SG_GROUPS)]).astype(BF16)
        bias_s = jnp.tile(jnp.repeat(b_s[l, :, :lt].T, SG_CH, axis=1), (ts // lt, 1))
        ob, vn_s = _sgu(rt_sgu, proj, sg_ln_g[l].reshape(1, SG_W), sg_ln_b[l].reshape(1, SG_W),
                        ws_p, ws_s, bias_p, bias_s, col['u'], col['v'])

        pa, pb, wo = p_a[l].astype(BF16), p_b[l].astype(BF16), w_out[l].astype(BF16)
        rws = _split(jnp.pad(router_w[l], ((0, 0), (0, LANES - n_experts))))
        rw = jnp.concatenate([rws[0], rws[0], rws[1]], axis=0)
        rb = jnp.pad(router_b[l].reshape(1, n_experts), ((0, 0), (0, LANES - n_experts)))
        x1, h2, ti, tg = _merge(alpha, n_experts, rt_merge, oa_p, oa_s, ob, proj, xp, xs,
                                vec(2, rt_merge), vec(4, rt_merge), vec(3, rt_merge),
                                pa, pb, wo, ln1_g[l].reshape(1, d), ln1_b[l].reshape(1, d), rw, rb,
                                col['ga'], col['gb'])

        nj = w_dn.shape[2] // MOE_TF
        dest, slot_tok, nblk_used, items = _routing(ti[:, :TOP_K], n_experts, MOE_TM, nj, MOE_SUBS)
        xb = _dispatch(nblk_used, slot_tok, h2, MOE_TM)
        yb = _moe(items, xb, w_gu[l], b_gu[l], w_dn[l], b_dn[l], MOE_TM, MOE_TF, MOE_SUBS)
        xp, xs = _combine(alpha, rt_comb, dest, yb, tg, x1, vec(5, rt_comb),
                          ln2_g[l].reshape(1, d), ln2_b[l].reshape(1, d))

        outs['conv_p'].append(conv_p)
        outs['delta_p'].append(delta_p)
        outs['conv_s'].append(conv_s)
        outs['delta_s'].append(delta_s)
        outs['vrows'].append(vn_s.reshape(bs, lt, SG_W))

    return (xp.reshape(bp, seq, d), xs.reshape(bs, lt, d),
            jnp.stack(outs['conv_p']), jnp.stack(outs['delta_p']),
            jnp.stack(outs['conv_s']), jnp.stack(outs['delta_s']), jnp.stack(outs['vrows']))
```

```python
import functools
import math

import jax
import jax.numpy as jnp
from jax import lax
from jax.experimental import pallas as pl
from jax.experimental.pallas import tpu as pltpu

F32 = jnp.float32
BF16 = jnp.bfloat16
HIGHEST = lax.Precision.HIGHEST

DN_HEADS = 8
DN_DK = 128
DN_DV = 128
DN_CONV = 4
DN_CHUNK = 64
SG_GROUPS = 8
SG_CH = 128
SG_CHUNK = 128
TOP_K = 4
SWIGLU_LIMIT = 7.0
SWIGLU_ALPHA = 1.702
LN_EPS = 1e-5
NORM_EPS = 1e-6
QK_W = DN_HEADS * DN_DK
V_W = DN_HEADS * DN_DV
QKV_W = 2 * QK_W + V_W
SG_W = SG_GROUPS * SG_CH

LANES = 128
SUBLANES = 8
VMEM_LIMIT = 56 * 1024 * 1024

PROJ_TM = 512
PROJ_TN = 1152
MERGE_TM = 256
MOE_TM = 256
MOE_TF = 512
MOE_SUBS = 6
COMB_TQ = 128
SAMPLE_BT = 4
SAMPLE_TP = 8
GDN_HG = 4


def _sigmoid(x):
    return 1.0 / (1.0 + jnp.exp(-x))


def _silu(x):
    return x * _sigmoid(x)


def _dot(a, b):
    return jnp.dot(a.astype(BF16), b.astype(BF16), preferred_element_type=F32)


def _dot_nt(a, b):
    return lax.dot_general(a.astype(BF16), b.astype(BF16), (((1,), (1,)), ((), ())),
                           preferred_element_type=F32)


def _dot_hi(a, b):
    return jnp.dot(a, b, precision=HIGHEST, preferred_element_type=F32)


def _layer_norm(x, g, b):
    mu = jnp.mean(x, axis=-1, keepdims=True)
    xc = x - mu
    var = jnp.mean(xc * xc, axis=-1, keepdims=True)
    return xc * lax.rsqrt(var + LN_EPS) * g + b


def _params(*sem):
    return pltpu.CompilerParams(dimension_semantics=sem, vmem_limit_bytes=VMEM_LIMIT)


class _Rows:
    def __init__(self, tm, rows_p, rows_s, seq):
        assert rows_p % tm == 0 and rows_s % tm == 0 and seq % tm == 0, (tm, rows_p, rows_s, seq)
        self.tm, self.n_p, self.n_s = tm, rows_p // tm, rows_s // tm
        self.tiles_per_seq = seq // tm
        self.bp = rows_p // seq

    @property
    def n(self):
        return self.n_p + self.n_s

    def prompt(self, width, col=0):
        return pl.BlockSpec((self.tm, width), lambda i, *_: (jnp.minimum(i, self.n_p - 1), col))

    def sample(self, width, col=0, single=True):
        mode = dict(pipeline_mode=pl.Buffered(1)) if single else {}
        return pl.BlockSpec((self.tm, width), lambda i, *_: (jnp.maximum(i - self.n_p, 0), col), **mode)

    def joint(self, width, col=0):
        return pl.BlockSpec((self.tm, width), lambda i, *_: (i, col))

    def seq_vec(self, d):
        return pl.BlockSpec((None, 1, d), lambda i, *_: (jnp.minimum(i // self.tiles_per_seq, self.bp - 1), 0, 0))

    def row_vec(self, d):
        return pl.BlockSpec((None, self.tm, d), lambda i, *_: (jnp.maximum(i - self.n_p, 0), 0, 0),
                            pipeline_mode=pl.Buffered(1))


def _pick(is_sample, prompt_ref, sample_ref):
    return jnp.where(is_sample, sample_ref[...], prompt_ref[...])


def _adaln_kernel(c_ref, w_ref, b_ref, o_ref):
    o_ref[...] = _dot(_silu(c_ref[...]), w_ref[...]) + b_ref[...]


def _adaln(c, w, b):
    rows, d = c.shape
    n = w.shape[1]
    tn = 1024
    return pl.pallas_call(
        _adaln_kernel,
        grid=(n // tn,),
        in_specs=[pl.BlockSpec((rows, d), lambda j: (0, 0)),
                  pl.BlockSpec((d, tn), lambda j: (0, j)),
                  pl.BlockSpec((1, tn), lambda j: (0, j))],
        out_specs=pl.BlockSpec((rows, tn), lambda j: (0, j)),
        out_shape=jax.ShapeDtypeStruct((rows, n), F32),
        compiler_params=_params("arbitrary"),
        name="adaln",
    )(c, w, b.reshape(1, n))


def _inproj_kernel(n_p, xp_ref, xs_ref, scp_ref, scs_ref, shp_ref, shs_ref, w_ref, o_ref, h_scr):
    @pl.when(pl.program_id(1) == 0)
    def _():
        is_s = pl.program_id(0) >= n_p
        x = _pick(is_s, xp_ref, xs_ref)
        h_scr[...] = (x * (1.0 + _pick(is_s, scp_ref, scs_ref)) + _pick(is_s, shp_ref, shs_ref)).astype(BF16)

    o_ref[...] = jnp.dot(h_scr[...], w_ref[...], preferred_element_type=F32)


def _inproj(rt, xp, xs, sc, sh, w):
    d = xp.shape[1]
    nw = w.shape[1]
    tm = rt.tm
    return pl.pallas_call(
        functools.partial(_inproj_kernel, rt.n_p),
        grid=(rt.n, nw // PROJ_TN),
        in_specs=[rt.prompt(d), rt.sample(d), rt.seq_vec(d), rt.row_vec(d), rt.seq_vec(d), rt.row_vec(d),
                  pl.BlockSpec((d, PROJ_TN), lambda i, j: (0, j))],
        out_specs=pl.BlockSpec((tm, PROJ_TN), lambda i, j: (i, j)),
        out_shape=jax.ShapeDtypeStruct((rt.n * tm, nw), F32),
        scratch_shapes=[pltpu.VMEM((tm, d), BF16)],
        compiler_params=_params("arbitrary", "arbitrary"),
        name="inproj",
    )(xp, xs, sc[0], sc[1], sh[0], sh[1], w)


def _softplus(x):
    return jnp.maximum(x, 0.0) + jnp.log1p(jnp.exp(-jnp.abs(x)))


def _split(a):
    hi = a.astype(BF16)
    lo = (a - hi.astype(F32)).astype(BF16)
    return hi, lo


def _dot3(a, b):
    lhs = jnp.concatenate([a[0], a[1], a[0]], axis=1)
    rhs = jnp.concatenate([b[0], b[0], b[1]], axis=0)
    return jnp.dot(lhs, rhs, preferred_element_type=F32)


def _map(f, *lists):
    return [f(*args) for args in zip(*lists)]


def _unit_lower_inverses_minus_eye(mats, rows, cols):
    same = (rows // 16) == (cols // 16)
    n = [jnp.where(same, a, 0.0) for a in mats]
    b = _map(lambda a, x: a - x, mats, n)
    n2 = _map(lambda x: _dot(x, x), n)
    n4 = _map(lambda x: _dot(x, x), n2)
    n8 = _map(lambda x: _dot(x, x), n4)
    r = [-x for x in n]
    r = _map(lambda x, p: x + p + _dot(x, p), r, n2)
    r = _map(lambda x, p: x + p + _dot(x, p), r, n4)
    dm = _map(lambda x, p: x + p + _dot(x, p), r, n8)
    m = _map(lambda x, y: y + _dot(x, y), dm, b)
    m2 = _map(lambda x: _dot(x, x), m)
    xm = _map(lambda x, p: x + p + _dot(p, x), dm, m2)
    return _map(lambda x, p: x - p - _dot(p, x), xm, m)


def _gdn_prompt_kernel(qkv_ref, z_ref, ab_ref, hist_ref, s0_ref, wconv_ref, alog_ref, dt_ref, og_ref,
                       o_ref, conv_ref, s_ref, xp_scr):
    n = pl.program_id(1)
    c = DN_CHUNK
    pad = SUBLANES
    nr = GDN_HG * c
    ngrp = DN_HEADS // GDN_HG

    @pl.when(n == 0)
    def _():
        xp_scr[pl.ds(0, pad), :] = jnp.zeros((pad, QKV_W), F32)
        xp_scr[pl.ds(pad - (DN_CONV - 1), DN_CONV - 1), :] = hist_ref[...]
        s_ref[...] = s0_ref[...]

    x = qkv_ref[...]
    xp_scr[pl.ds(pad, c), :] = x
    y = jnp.zeros((c, QKV_W), F32)
    for j in range(DN_CONV):
        y = y + xp_scr[pl.ds(pad - (DN_CONV - 1) + j, c), :] * wconv_ref[pl.ds(j, 1), :]
    y = _silu(y)
    tail = xp_scr[pl.ds(c + pad - (DN_CONV - 1), DN_CONV - 1), :]
    conv_ref[...] = tail
    xp_scr[pl.ds(pad - (DN_CONV - 1), DN_CONV - 1), :] = tail

    ab = ab_ref[...]
    g = -jnp.exp(alog_ref[...]) * _softplus(ab + dt_ref[...])
    beta_all = _sigmoid(ab)
    r64 = lax.broadcasted_iota(jnp.int32, (c, c), 0)
    c64 = lax.broadcasted_iota(jnp.int32, (c, c), 1)
    gc = _dot_hi((r64 >= c64).astype(F32), g)
    z = z_ref[...]
    og = og_ref[...]

    rows = lax.broadcasted_iota(jnp.int32, (nr, nr), 0)
    cols = lax.broadcasted_iota(jnp.int32, (nr, nr), 1)
    same = (rows // c) == (cols // c)
    causal = same & (rows >= cols)
    strict = same & (rows > cols)
    rowhead = lax.broadcasted_iota(jnp.int32, (nr, 1), 0) // c
    groups = [range(grp * GDN_HG, (grp + 1) * GDN_HG) for grp in range(ngrp)]

    def stack(heads, off, width):
        return jnp.concatenate([y[:, off + h * width:off + (h + 1) * width] for h in heads], axis=0)

    def l2n(t):
        return t * lax.rsqrt(jnp.sum(t * t, axis=-1, keepdims=True) + NORM_EPS)

    q = [l2n(stack(hs, 0, DN_DK)) * (DN_DK ** -0.5) for hs in groups]
    k = [l2n(stack(hs, QK_W, DN_DK)) for hs in groups]
    v = [stack(hs, 2 * QK_W, DN_DV) for hs in groups]
    beta = [jnp.concatenate([beta_all[:, DN_HEADS + h:DN_HEADS + h + 1] for h in hs], axis=0) for hs in groups]
    gcf = [jnp.concatenate([jnp.broadcast_to(gc[:, h:h + 1], (c, LANES)) for h in hs], axis=0) for hs in groups]
    gcc = [t[:, 0:1] for t in gcf]
    grow = [t.T[0:1, :] for t in gcf]
    glast = [[gc[c - 1:c, h:h + 1] for h in hs] for hs in groups]
    gtot = [jnp.concatenate([jnp.broadcast_to(t, (c, 1)) for t in gl], axis=0) for gl in glast]
    decay = _map(lambda a, b: jnp.where(causal, jnp.exp(jnp.where(causal, a - b, 0.0)), 0.0), gcc, grow)
    egc = _map(jnp.exp, gcc)
    kb = _map(lambda a, b: a * b, k, beta)
    a_mat = _map(lambda a, b, dd: jnp.where(strict, _dot_nt(a, b) * dd, 0.0), kb, k, decay)
    tm1 = _unit_lower_inverses_minus_eye(a_mat, rows, cols)
    u = _map(lambda t, a, b: a * b + _dot(t, a * b), tm1, v, beta)
    w = _map(lambda t, a, b: a * b + _dot(t, a * b), tm1, kb, egc)
    attn = _map(lambda a, b, dd: _dot_nt(a, b) * dd, q, k, decay)
    qg = _map(lambda a, b: a * b, q, egc)
    kd_t = _map(lambda a, b, cc: (a * jnp.exp(b - cc)).T, k, gtot, gcc)

    for gi, hs in enumerate(groups):
        v_news, qss = [], []
        for hl, h in enumerate(hs):
            sl = slice(hl * c, (hl + 1) * c)
            rs = _dot(jnp.concatenate([w[gi][sl], qg[gi][sl]], axis=0), s_ref[h])
            v_news.append(u[gi][sl] - rs[:c])
            qss.append(rs[c:])
        v_new = jnp.concatenate(v_news, axis=0)
        o = jnp.concatenate(qss, axis=0) + _dot(attn[gi], v_new)
        for hl, h in enumerate(hs):
            upd = _dot(kd_t[gi], jnp.where(rowhead == hl, v_new, 0.0))
            s_ref[h] = s_ref[h] * jnp.exp(glast[gi][hl]) + upd
        o = o * lax.rsqrt(jnp.mean(o * o, axis=-1, keepdims=True) + NORM_EPS) * og
        for hl, h in enumerate(hs):
            o_ref[:, h * DN_DV:(h + 1) * DN_DV] = o[hl * c:(hl + 1) * c] * _silu(z[:, h * DN_DV:(h + 1) * DN_DV])


def _gdn_prompt(proj, hist, s0, wconv, alog, dtb, og, batch, seq, col_z, col_ab):
    c = DN_CHUNK
    nchunk = seq // c
    rows = batch * seq
    row = lambda b, n: b * nchunk + n
    return pl.pallas_call(
        _gdn_prompt_kernel,
        grid=(batch, nchunk),
        in_specs=[pl.BlockSpec((c, QKV_W), lambda b, n: (row(b, n), 0)),
                  pl.BlockSpec((c, V_W), lambda b, n: (row(b, n), col_z // V_W)),
                  pl.BlockSpec((c, LANES), lambda b, n: (row(b, n), col_ab // LANES)),
                  pl.BlockSpec((None, DN_CONV - 1, QKV_W), lambda b, n: (b, 0, 0)),
                  pl.BlockSpec((None, DN_HEADS, DN_DK, DN_DV), lambda b, n: (b, 0, 0, 0)),
                  pl.BlockSpec((DN_CONV, QKV_W), lambda b, n: (0, 0)),
                  pl.BlockSpec((1, LANES), lambda b, n: (0, 0)),
                  pl.BlockSpec((1, LANES), lambda b, n: (0, 0)),
                  pl.BlockSpec((1, DN_DV), lambda b, n: (0, 0))],
        out_specs=[pl.BlockSpec((c, V_W), lambda b, n: (row(b, n), 0)),
                   pl.BlockSpec((None, DN_CONV - 1, QKV_W), lambda b, n: (b, 0, 0)),
                   pl.BlockSpec((None, DN_HEADS, DN_DK, DN_DV), lambda b, n: (b, 0, 0, 0))],
        out_shape=[jax.ShapeDtypeStruct((rows, V_W), F32),
                   jax.ShapeDtypeStruct((batch, DN_CONV - 1, QKV_W), F32),
                   jax.ShapeDtypeStruct((batch, DN_HEADS, DN_DK, DN_DV), F32)],
        scratch_shapes=[pltpu.VMEM((c + SUBLANES, QKV_W), F32)],
        compiler_params=_params("arbitrary", "arbitrary"),
        name="gdn_prompt",
    )(proj, proj, proj, hist, s0, wconv, alog, dtb, og)


def _gdn_sample_kernel(qkv_ref, z_ref, ab_ref, hist_ref, s0_ref, wconv_ref, alog_ref, dt_ref, og_ref,
                       o_ref, conv_ref, s_ref, xp_scr, ab_scr, wq_scr, r_scr, kdt_scr, vn_scr, gl_scr):
    bt, tp, lt = SAMPLE_BT, SAMPLE_TP, DN_CONV
    nblk = DN_HEADS * bt
    nrow = nblk * tp
    hist_rows = DN_CONV - 1

    xp_scr[...] = jnp.zeros(xp_scr.shape, F32)
    ab_scr[...] = jnp.zeros(ab_scr.shape, F32)
    ys = []
    for b in range(bt):
        xp_scr[b, pl.ds(SUBLANES - hist_rows, hist_rows), :] = hist_ref[b]
        xp_scr[b, pl.ds(SUBLANES, lt), :] = qkv_ref[pl.ds(b * lt, lt), :]
        yb = jnp.zeros((tp, QKV_W), F32)
        for j in range(DN_CONV):
            yb = yb + xp_scr[b, pl.ds(SUBLANES - hist_rows + j, tp), :] * wconv_ref[pl.ds(j, 1), :]
        ys.append(_silu(yb))
        conv_ref[b] = xp_scr[b, pl.ds(SUBLANES + lt - hist_rows, hist_rows), :]
        ab_scr[b, pl.ds(0, lt), :] = ab_ref[pl.ds(b * lt, lt), :]
    y = jnp.concatenate(ys, axis=0)
    ab = jnp.concatenate([ab_scr[b] for b in range(bt)], axis=0)
    tok = lax.broadcasted_iota(jnp.int32, (bt * tp, 1), 0) % tp
    real = tok < lt
    g_all = jnp.where(real, -jnp.exp(alog_ref[...]) * _softplus(ab + dt_ref[...]), 0.0)
    beta_all = jnp.where(real, _sigmoid(ab), 0.0)

    def heads_to_rows(t, off, width):
        return jnp.concatenate([t[:, off + h * width:off + (h + 1) * width] for h in range(DN_HEADS)], axis=0)

    realr = jnp.concatenate([real] * DN_HEADS, axis=0)
    q = jnp.where(realr, heads_to_rows(y, 0, DN_DK), 0.0)
    k = jnp.where(realr, heads_to_rows(y, QK_W, DN_DK), 0.0)
    v = jnp.where(realr, heads_to_rows(y, 2 * QK_W, DN_DV), 0.0)
    gcol = jnp.concatenate([g_all[:, h:h + 1] for h in range(DN_HEADS)], axis=0)
    beta = jnp.concatenate([beta_all[:, DN_HEADS + h:DN_HEADS + h + 1] for h in range(DN_HEADS)], axis=0)
    q = q * lax.rsqrt(jnp.sum(q * q, axis=-1, keepdims=True) + NORM_EPS) * (DN_DK ** -0.5)
    k = k * lax.rsqrt(jnp.sum(k * k, axis=-1, keepdims=True) + NORM_EPS)

    rows = lax.broadcasted_iota(jnp.int32, (nrow, nrow), 0)
    cols = lax.broadcasted_iota(jnp.int32, (nrow, nrow), 1)
    same = (rows // tp) == (cols // tp)
    causal = same & (rows >= cols)
    strict = same & (rows > cols)
    gfull = jnp.broadcast_to(gcol, (nrow, LANES))
    gc = _dot_hi(causal.astype(F32), gfull)
    gtot = _dot_hi(same.astype(F32), gfull)
    gcc = gc[:, 0:1]
    grow = gc.T[0:1, :]
    decay = jnp.where(causal, jnp.exp(jnp.where(causal, gcc - grow, 0.0)), 0.0)
    egc = jnp.exp(gcc)
    kb = k * beta
    a_mat = jnp.where(strict, _dot_nt(kb, k) * decay, 0.0)
    a2 = _dot(a_mat, a_mat)
    tm1 = a2 - a_mat - _dot(a_mat, a2)
    u = v * beta + _dot(tm1, v * beta)
    w = kb * egc + _dot(tm1, kb * egc)
    attn = _dot_nt(q, k) * decay
    qg = q * egc
    for i in range(nblk):
        wq_scr[pl.ds(2 * tp * i, tp), :] = w[i * tp:(i + 1) * tp, :]
        wq_scr[pl.ds(2 * tp * i + tp, tp), :] = qg[i * tp:(i + 1) * tp, :]
    kdt_scr[...] = (k * jnp.exp(gtot[:, 0:1] - gcc)).T
    gl_scr[...] = jnp.exp(gtot)

    def read_state(i, carry):
        r0 = pl.multiple_of(i * 2 * tp, 2 * tp)
        r_scr[pl.ds(r0, 2 * tp), :] = _dot(wq_scr[pl.ds(r0, 2 * tp), :], s0_ref[i % bt, i // bt])
        return carry

    lax.fori_loop(0, nblk, read_state, 0, unroll=4)
    ws = jnp.concatenate([r_scr[pl.ds(2 * tp * i, tp), :] for i in range(nblk)], axis=0)
    qs = jnp.concatenate([r_scr[pl.ds(2 * tp * i + tp, tp), :] for i in range(nblk)], axis=0)
    v_new = u - ws
    vn_scr[...] = v_new
    o = qs + _dot(attn, v_new)
    blockid = lax.broadcasted_iota(jnp.int32, (nrow, 1), 0) // tp

    def write_state(i, carry):
        r0 = pl.multiple_of(i * tp, tp)
        upd = _dot(kdt_scr[...], jnp.where(blockid == i, vn_scr[...], 0.0))
        s_ref[i % bt, i // bt] = s0_ref[i % bt, i // bt] * gl_scr[pl.ds(r0, 1), :] + upd
        return carry

    lax.fori_loop(0, nblk, write_state, 0, unroll=4)

    og = og_ref[...]
    o = o * lax.rsqrt(jnp.mean(o * o, axis=-1, keepdims=True) + NORM_EPS) * og
    for b in range(bt):
        for h in range(DN_HEADS):
            zbh = z_ref[pl.ds(b * lt, lt), h * DN_DV:(h + 1) * DN_DV]
            blk = o[(h * bt + b) * tp:(h * bt + b) * tp + lt, :]
            o_ref[pl.ds(b * lt, lt), h * DN_DV:(h + 1) * DN_DV] = blk * _silu(zbh)


def _gdn_sample(proj, row0, hist, s0, wconv, alog, dtb, og, batch, col_z, col_ab):
    bt, tp, lt = SAMPLE_BT, SAMPLE_TP, DN_CONV
    rows = batch * lt
    nrow = DN_HEADS * bt * tp
    blk = bt * lt
    assert row0 % blk == 0 and batch % bt == 0
    b0 = row0 // blk
    return pl.pallas_call(
        _gdn_sample_kernel,
        grid=(batch // bt,),
        in_specs=[pl.BlockSpec((blk, QKV_W), lambda i: (b0 + i, 0)),
                  pl.BlockSpec((blk, V_W), lambda i: (b0 + i, col_z // V_W)),
                  pl.BlockSpec((blk, LANES), lambda i: (b0 + i, col_ab // LANES)),
                  pl.BlockSpec((bt, DN_CONV - 1, QKV_W), lambda i: (i, 0, 0)),
                  pl.BlockSpec((bt, DN_HEADS, DN_DK, DN_DV), lambda i: (i, 0, 0, 0)),
                  pl.BlockSpec((DN_CONV, QKV_W), lambda i: (0, 0)),
                  pl.BlockSpec((1, LANES), lambda i: (0, 0)),
                  pl.BlockSpec((1, LANES), lambda i: (0, 0)),
                  pl.BlockSpec((1, DN_DV), lambda i: (0, 0))],
        out_specs=[pl.BlockSpec((blk, V_W), lambda i: (i, 0)),
                   pl.BlockSpec((bt, DN_CONV - 1, QKV_W), lambda i: (i, 0, 0)),
                   pl.BlockSpec((bt, DN_HEADS, DN_DK, DN_DV), lambda i: (i, 0, 0, 0))],
        out_shape=[jax.ShapeDtypeStruct((rows, V_W), F32),
                   jax.ShapeDtypeStruct((batch, DN_CONV - 1, QKV_W), F32),
                   jax.ShapeDtypeStruct((batch, DN_HEADS, DN_DK, DN_DV), F32)],
        scratch_shapes=[pltpu.VMEM((bt, SUBLANES + tp, QKV_W), F32),
                        pltpu.VMEM((bt, tp, LANES), F32),
                        pltpu.VMEM((2 * nrow, DN_DK), F32),
                        pltpu.VMEM((2 * nrow, DN_DV), F32),
                        pltpu.VMEM((DN_DK, nrow), F32),
                        pltpu.VMEM((nrow, DN_DV), F32),
                        pltpu.VMEM((nrow, LANES), F32)],
        compiler_params=_params("arbitrary"),
        name="gdn_sample",
    )(proj, proj, proj, hist, s0, wconv, alog, dtb, og)


def _sgu_kernel(n_p, u_ref, v_ref, g_ref, b_ref, wsp_ref, wss_ref, bp_ref, bs_ref, o_ref, vn_ref):
    is_s = pl.program_id(0) >= n_p
    u = jax.nn.gelu(u_ref[...])
    vn = _layer_norm(jax.nn.gelu(v_ref[...]), g_ref[...], b_ref[...])
    vn_ref[...] = vn
    bias = _pick(is_s, bp_ref, bs_ref)
    for g in range(SG_GROUPS):
        sl = slice(g * SG_CH, (g + 1) * SG_CH)
        ws = jnp.where(is_s, wss_ref[g], wsp_ref[g])
        mixed = _dot(ws, vn[:, sl]) + bias[:, sl]
        o_ref[:, sl] = u[:, sl] * mixed


def _sgu(rt, proj, ln_g, ln_b, ws_p, ws_s, bias_p, bias_s, col_u, col_v):
    t = rt.tm
    const = lambda shape: pl.BlockSpec(shape, lambda i: (0,) * len(shape))
    return pl.pallas_call(
        functools.partial(_sgu_kernel, rt.n_p),
        grid=(rt.n,),
        in_specs=[rt.joint(SG_W, col_u // SG_W), rt.joint(SG_W, col_v // SG_W),
                  const((1, SG_W)), const((1, SG_W)),
                  const((SG_GROUPS, t, t)), const((SG_GROUPS, t, t)), const((t, SG_W)), const((t, SG_W))],
        out_specs=[rt.joint(SG_W), rt.sample(SG_W, single=False)],
        out_shape=[jax.ShapeDtypeStruct((rt.n * t, SG_W), F32),
                   jax.ShapeDtypeStruct((rt.n_s * t, SG_W), F32)],
        compiler_params=_params("arbitrary"),
        name="sgu",
    )(proj, proj, ln_g, ln_b, ws_p, ws_s, bias_p, bias_s)


def _merge_kernel(alpha, n_experts, n_p, oap_ref, oas_ref, ob_ref, ga_ref, gb_ref, xp_ref, xs_ref,
                  gtp_ref, gts_ref, scp_ref, scs_ref, shp_ref, shs_ref,
                  pa_ref, pb_ref, wo_ref, lg_ref, lb_ref, rw_ref, rb_ref,
                  x1_ref, h2_ref, ti_ref, tg_ref):
    is_s = pl.program_id(0) >= n_p
    oa = _pick(is_s, oap_ref, oas_ref)
    merged = (_sigmoid(ga_ref[...]) * _dot(oa, pa_ref[...])
              + _sigmoid(gb_ref[...]) * _dot(ob_ref[...], pb_ref[...]))
    y = _dot(merged, wo_ref[...])
    x = _pick(is_s, xp_ref, xs_ref)
    x1 = _layer_norm(alpha * x + _pick(is_s, gtp_ref, gts_ref) * y, lg_ref[...], lb_ref[...])
    x1_ref[...] = x1
    h2 = x1 * (1.0 + _pick(is_s, scp_ref, scs_ref)) + _pick(is_s, shp_ref, shs_ref)
    h2_ref[...] = h2
    hs = _split(h2)
    logits = jnp.dot(jnp.concatenate([hs[0], hs[1], hs[0]], axis=1), rw_ref[...],
                     preferred_element_type=F32) + rb_ref[...]
    lane = lax.broadcasted_iota(jnp.int32, logits.shape, 1)
    logits = jnp.where(lane < n_experts, logits, -jnp.inf)
    ti = jnp.zeros(logits.shape, jnp.int32)
    tv = jnp.zeros(logits.shape, F32)
    top = None
    for kk in range(TOP_K):
        m = jnp.max(logits, axis=-1, keepdims=True)
        idx = jnp.min(jnp.where(logits == m, lane.astype(F32), float(LANES)), axis=-1,
                      keepdims=True).astype(jnp.int32)
        if kk == 0:
            top = m
        ti = jnp.where(lane == kk, idx, ti)
        tv = jnp.where(lane == kk, jnp.exp(m - top), tv)
        logits = jnp.where(lane == idx, -jnp.inf, logits)
    ti_ref[...] = ti
    tg_ref[...] = tv / jnp.sum(tv, axis=-1, keepdims=True)


def _merge(alpha, n_experts, rt, oa_p, oa_s, ob, proj, xp, xs, gt, sc, sh, pa, pb, wo, lg, lb, rw, rb,
           col_ga, col_gb):
    d = xp.shape[1]
    tm = rt.tm
    rows = rt.n * tm
    const = lambda shape: pl.BlockSpec(shape, lambda i: (0,) * len(shape), pipeline_mode=pl.Buffered(1))
    return pl.pallas_call(
        functools.partial(_merge_kernel, alpha, n_experts, rt.n_p),
        grid=(rt.n,),
        in_specs=[rt.prompt(V_W), rt.sample(V_W), rt.joint(SG_W),
                  rt.joint(d, col_ga // d), rt.joint(d, col_gb // d),
                  rt.prompt(d), rt.sample(d),
                  rt.seq_vec(d), rt.row_vec(d), rt.seq_vec(d), rt.row_vec(d), rt.seq_vec(d), rt.row_vec(d),
                  const((V_W, d)), const((SG_W, d)), const((d, d)),
                  const((1, d)), const((1, d)), const((3 * d, LANES)), const((1, LANES))],
        out_specs=[rt.joint(d), rt.joint(d), rt.joint(LANES), rt.joint(LANES)],
        out_shape=[jax.ShapeDtypeStruct((rows, d), F32),
                   jax.ShapeDtypeStruct((rows, d), F32),
                   jax.ShapeDtypeStruct((rows, LANES), jnp.int32),
                   jax.ShapeDtypeStruct((rows, LANES), F32)],
        compiler_params=_params("arbitrary"),
        name="merge",
    )(oa_p, oa_s, ob, proj, proj, xp, xs, gt[0], gt[1], sc[0], sc[1], sh[0], sh[1],
      pa, pb, wo, lg, lb, rw, rb)


def _dispatch_kernel(nblk_ref, tok_ref, tokn_ref, h_hbm, o_ref, buf, sem):
    i = pl.program_id(0)
    nb = nblk_ref[0]
    tm = o_ref.shape[0]
    slot = i % 2

    def issue(ids_ref, sl):
        def start(r, carry):
            pltpu.make_async_copy(h_hbm.at[pl.ds(ids_ref[0, r], 1)], buf.at[sl, pl.ds(r, 1)], sem.at[sl]).start()
            return carry

        lax.fori_loop(0, tm, start, 0, unroll=8)

    @pl.when(jnp.logical_and(i == 0, nb > 0))
    def _():
        issue(tok_ref, 0)

    @pl.when(i + 1 < nb)
    def _():
        issue(tokn_ref, 1 - slot)

    @pl.when(i < nb)
    def _():
        pltpu.make_async_copy(h_hbm.at[pl.ds(0, tm)], buf.at[slot], sem.at[slot]).wait()
        o_ref[...] = buf[slot].astype(BF16)

    @pl.when(i >= nb)
    def _():
        o_ref[...] = jnp.zeros(o_ref.shape, o_ref.dtype)


def _dispatch(nblk_used, slot_tok, h2, tm):
    nslot = slot_tok.shape[0]
    nblk = nslot // tm
    d = h2.shape[1]
    ids = slot_tok.reshape(nblk, 1, tm)
    grid_spec = pltpu.PrefetchScalarGridSpec(
        num_scalar_prefetch=1,
        grid=(nblk,),
        in_specs=[pl.BlockSpec((None, 1, tm), lambda i, nb: (i, 0, 0), memory_space=pltpu.SMEM),
                  pl.BlockSpec((None, 1, tm), lambda i, nb: (jnp.minimum(i + 1, nblk - 1), 0, 0),
                               memory_space=pltpu.SMEM),
                  pl.BlockSpec(memory_space=pl.ANY)],
        out_specs=pl.BlockSpec((tm, d), lambda i, nb: (i, 0)),
        scratch_shapes=[pltpu.VMEM((2, tm, d), F32), pltpu.SemaphoreType.DMA((2,))],
    )
    return pl.pallas_call(
        _dispatch_kernel,
        grid_spec=grid_spec,
        out_shape=jax.ShapeDtypeStruct((nslot, d), BF16),
        compiler_params=_params("arbitrary"),
        name="dispatch",
    )(nblk_used.reshape(1), ids, ids, h2)


def _moe_kernel(nj, e_ref, j_ref, xb_ref, ob_ref, r_ref, flag_ref,
                x_ref, wg_ref, wu_ref, wd_ref, bg_ref, bu_ref, bd_ref, o_ref,
                wg_scr, wu_scr, wd_scr, acc_scr):
    s = pl.program_id(0)
    flags = flag_ref[s]
    j = j_ref[s]
    r = r_ref[s]

    @pl.when((flags & 2) != 0)
    def _():
        wg_scr[...] = wg_ref[...].astype(BF16)
        wu_scr[...] = wu_ref[...].astype(BF16)
        wd_scr[...] = wd_ref[...].astype(BF16)

    @pl.when(flags == 0)
    def _():
        o_ref[...] = jnp.zeros(o_ref.shape, F32)

    @pl.when((flags & 1) != 0)
    def _():
        x = x_ref[...]
        gate = jnp.dot(x, wg_scr[...], preferred_element_type=F32) + bg_ref[...]
        up = jnp.dot(x, wu_scr[...], preferred_element_type=F32) + bu_ref[...]
        gate = jnp.minimum(gate, SWIGLU_LIMIT)
        up = jnp.clip(up, -SWIGLU_LIMIT, SWIGLU_LIMIT)
        act = (up + 1.0) * gate * _sigmoid(SWIGLU_ALPHA * gate)
        y = jnp.dot(act.astype(BF16), wd_scr[...], preferred_element_type=F32)

        @pl.when(j == 0)
        def _():
            acc_scr[r] = y

        @pl.when(jnp.logical_and(j > 0, j < nj - 1))
        def _():
            acc_scr[r] = acc_scr[r] + y

        @pl.when(j == nj - 1)
        def _():
            o_ref[...] = acc_scr[r] + y + bd_ref[...]


def _moe(items, xb, w_gu, b_gu, w_dn, b_dn, tm, tf, subs):
    item_e, item_j, item_xb, item_ob, item_r, item_flag = items
    n_items = item_e.shape[0]
    nslot, d = xb.shape
    n_exp, _, f2 = w_gu.shape
    f = f2 // 2
    nj = f // tf
    assert nj >= 2
    grid_spec = pltpu.PrefetchScalarGridSpec(
        num_scalar_prefetch=6,
        grid=(n_items,),
        in_specs=[pl.BlockSpec((tm, d), lambda s, e, j, xbk, obk, r, fl: (xbk[s], 0)),
                  pl.BlockSpec((None, d, tf), lambda s, e, j, xbk, obk, r, fl: (e[s], 0, j[s])),
                  pl.BlockSpec((None, d, tf), lambda s, e, j, xbk, obk, r, fl: (e[s], 0, nj + j[s])),
                  pl.BlockSpec((None, tf, d), lambda s, e, j, xbk, obk, r, fl: (e[s], j[s], 0)),
                  pl.BlockSpec((None, 1, tf), lambda s, e, j, xbk, obk, r, fl: (e[s], 0, j[s])),
                  pl.BlockSpec((None, 1, tf), lambda s, e, j, xbk, obk, r, fl: (e[s], 0, nj + j[s])),
                  pl.BlockSpec((None, 1, d), lambda s, e, j, xbk, obk, r, fl: (e[s], 0, 0))],
        out_specs=pl.BlockSpec((tm, d), lambda s, e, j, xbk, obk, r, fl: (obk[s], 0)),
        scratch_shapes=[pltpu.VMEM((d, tf), BF16), pltpu.VMEM((d, tf), BF16), pltpu.VMEM((tf, d), BF16),
                        pltpu.VMEM((subs, tm, d), F32)],
    )
    return pl.pallas_call(
        functools.partial(_moe_kernel, nj),
        grid_spec=grid_spec,
        out_shape=jax.ShapeDtypeStruct((nslot, d), F32),
        compiler_params=_params("arbitrary"),
        name="moe",
    )(item_e, item_j, item_xb, item_ob, item_r, item_flag,
      xb, w_gu, w_gu, w_dn, b_gu.reshape(n_exp, 1, f2), b_gu.reshape(n_exp, 1, f2), b_dn.reshape(n_exp, 1, d))


def _combine_kernel(alpha, n_p, dest_ref, destn_ref, yb_hbm, tg_ref, x1_ref, gtp_ref, gts_ref, lg_ref, lb_ref,
                    op_ref, os_ref, buf, sem):
    i = pl.program_id(0)
    tq = x1_ref.shape[0]
    slot = i % 2

    def issue(ids_ref, sl):
        def start(t, carry):
            for kk in range(TOP_K):
                pltpu.make_async_copy(yb_hbm.at[pl.ds(ids_ref[0, t * TOP_K + kk], 1)],
                                      buf.at[sl, kk, pl.ds(t, 1)], sem.at[sl]).start()
            return carry

        lax.fori_loop(0, tq, start, 0, unroll=4)

    @pl.when(i == 0)
    def _():
        issue(dest_ref, 0)

    @pl.when(i + 1 < pl.num_programs(0))
    def _():
        issue(destn_ref, 1 - slot)

    for kk in range(TOP_K):
        pltpu.make_async_copy(yb_hbm.at[pl.ds(0, tq)], buf.at[slot, kk], sem.at[slot]).wait()
    tg = tg_ref[...]
    y = jnp.zeros(x1_ref.shape, F32)
    for kk in range(TOP_K):
        y = y + buf[slot, kk] * tg[:, kk:kk + 1]
    is_s = i >= n_p
    out = _layer_norm(alpha * x1_ref[...] + _pick(is_s, gtp_ref, gts_ref) * y, lg_ref[...], lb_ref[...])

    @pl.when(jnp.logical_not(is_s))
    def _():
        op_ref[...] = out

    @pl.when(is_s)
    def _():
        os_ref[...] = out


def _combine(alpha, rt, dest, yb, tg, x1, gt, lg, lb):
    d = x1.shape[1]
    tq = rt.tm
    ids = dest.reshape(rt.n, 1, tq * TOP_K)
    return pl.pallas_call(
        functools.partial(_combine_kernel, alpha, rt.n_p),
        grid=(rt.n,),
        in_specs=[pl.BlockSpec((None, 1, tq * TOP_K), lambda i: (i, 0, 0), memory_space=pltpu.SMEM),
                  pl.BlockSpec((None, 1, tq * TOP_K), lambda i: (jnp.minimum(i + 1, rt.n - 1), 0, 0),
                               memory_space=pltpu.SMEM),
                  pl.BlockSpec(memory_space=pl.ANY),
                  rt.joint(LANES), rt.joint(d), rt.seq_vec(d), rt.row_vec(d),
                  pl.BlockSpec((1, d), lambda i: (0, 0)),
                  pl.BlockSpec((1, d), lambda i: (0, 0))],
        out_specs=[rt.prompt(d), rt.sample(d, single=False)],
        out_shape=[jax.ShapeDtypeStruct((rt.n_p * tq, d), F32), jax.ShapeDtypeStruct((rt.n_s * tq, d), F32)],
        scratch_shapes=[pltpu.VMEM((2, TOP_K, tq, d), F32), pltpu.SemaphoreType.DMA((2,))],
        compiler_params=_params("arbitrary"),
        name="combine",
    )(ids, ids, yb, tg, x1, gt[0], gt[1], lg, lb)


def _routing(ti, n_experts, tm, nj, subs):
    t = ti.shape[0]
    n_assign = t * TOP_K
    nb_max = n_assign // tm + n_experts
    onehot = (ti[:, :, None] == jnp.arange(n_experts, dtype=jnp.int32)[None, None, :]).astype(jnp.int32)
    per_tok = jnp.sum(onehot, axis=1)
    cum = jnp.cumsum(per_tok, axis=0)
    counts = cum[-1]
    rank = jnp.take_along_axis(cum, ti, axis=1) - 1
    nblk = (counts + tm - 1) // tm
    blk_end = jnp.cumsum(nblk)
    blk_start = blk_end - nblk
    dest = blk_start[ti] * tm + rank
    tok = jnp.broadcast_to(jnp.arange(t, dtype=jnp.int32)[:, None], (t, TOP_K))
    slot_tok = jnp.zeros((nb_max * tm,), jnp.int32).at[dest.reshape(-1)].set(tok.reshape(-1))
    total_blk = blk_end[-1]
    blocks = jnp.arange(nb_max, dtype=jnp.int32)
    blk_e = jnp.minimum(jnp.sum((blk_end[None, :] <= blocks[:, None]).astype(jnp.int32), axis=1), n_experts - 1)
    r_in_e = blocks - blk_start[blk_e]
    g0 = blk_start[blk_e] + (r_in_e // subs) * subs
    nsub = jnp.minimum(subs, nblk[blk_e] - (r_in_e // subs) * subs)
    p = jnp.arange(nb_max * nj, dtype=jnp.int32)
    bp = p // nj
    valid = bp < total_blk
    last = jnp.maximum(total_blk - 1, 0)
    bq = jnp.where(valid, bp, last)
    gq, nq, eq = g0[bq], jnp.maximum(nsub[bq], 1), blk_e[bq]
    local = p - nj * gq
    jq = jnp.where(valid, local // nq, nj - 1)
    rq = jnp.where(valid, local % nq, nq - 1)
    item_xb = gq + rq
    item_ob = jnp.where(valid, jnp.where(jq == nj - 1, gq + rq, gq), bp)
    flags = valid.astype(jnp.int32) + 2 * (valid & (rq == 0)).astype(jnp.int32)
    items = (eq.astype(jnp.int32), jq.astype(jnp.int32), item_xb.astype(jnp.int32),
             item_ob.astype(jnp.int32), rq.astype(jnp.int32), flags)
    return dest.astype(jnp.int32), slot_tok, total_blk.astype(jnp.int32), items


def _win_prep_kernel(w_ref, o_ref):
    o_a = QKV_W + V_W
    o_u = o_a + 2 * DN_HEADS
    x = w_ref[...]
    rest = x.shape[1] - o_u
    o_ref[:, 0:o_a] = x[:, 0:o_a].astype(BF16)
    o_ref[:, o_a:o_a + rest] = x[:, o_u:].astype(BF16)
    lane = lax.broadcasted_iota(jnp.int32, (x.shape[0], LANES), 1)
    o_ref[:, o_a + rest:] = jnp.where(lane < 2 * DN_HEADS, x[:, o_a:o_a + LANES], 0.0).astype(BF16)


def _rearranged_w_in(w_in, d):
    dm, nw = w_in.shape
    tr = 128
    w = pl.pallas_call(
        _win_prep_kernel,
        grid=(dm // tr,),
        in_specs=[pl.BlockSpec((tr, nw), lambda i: (i, 0))],
        out_specs=pl.BlockSpec((tr, nw - 2 * DN_HEADS + LANES), lambda i: (i, 0)),
        out_shape=jax.ShapeDtypeStruct((dm, nw - 2 * DN_HEADS + LANES), BF16),
        compiler_params=_params("arbitrary"),
        name="w_in_prep",
    )(w_in)
    cols = dict(z=QKV_W, u=QKV_W + V_W, v=QKV_W + V_W + SG_W, ga=QKV_W + V_W + 2 * SG_W,
                gb=QKV_W + V_W + 2 * SG_W + d, ab=QKV_W + V_W + 2 * SG_W + 2 * d)
    return w, cols


def kernel(x_prompt, x_sample, state_conv_qkv, state_delta, c_prompt, c_sample, w_ada, b_ada, w_in, w_conv, a_log, dt_bias, o_norm_g, sg_ln_g, sg_ln_b, w_s, b_s, p_a, p_b, w_out, ln1_g, ln1_b, router_w, router_b, w_gu, b_gu, w_dn, b_dn, ln2_g, ln2_b):
    depth = w_ada.shape[0]
    alpha = float((2 * depth) ** 0.25)
    bp, seq, d = x_prompt.shape
    bs, lt, _ = x_sample.shape
    assert lt == DN_CONV and seq % SG_CHUNK == 0
    n_experts = router_w.shape[2]
    rows_p, rows_s = bp * seq, bs * lt
    xp = x_prompt.reshape(rows_p, d)
    xs = x_sample.reshape(rows_s, d)
    c_all = jnp.concatenate([c_prompt, c_sample], axis=0)
    tile = lambda cap: _Rows(math.gcd(math.gcd(cap, rows_s), seq), rows_p, rows_s, seq)
    rt_proj, rt_sgu, rt_merge, rt_comb = tile(PROJ_TM), tile(SG_CHUNK), tile(MERGE_TM), tile(COMB_TQ)
    outs = dict(conv_p=[], delta_p=[], conv_s=[], delta_s=[], vrows=[])

    for l in range(depth):
        mod = _adaln(c_all, w_ada[l], b_ada[l])
        mods = jnp.split(mod, 6, axis=1)
        rows_of = [jnp.repeat(m[bp:], lt, axis=0) for m in mods]

        def vec(i, rt):
            return mods[i][:bp].reshape(bp, 1, d), rows_of[i].reshape(rt.n_s, rt.tm, d)

        w_r, col = _rearranged_w_in(w_in[l], d)
        proj = _inproj(rt_proj, xp, xs, vec(1, rt_proj), vec(0, rt_proj), w_r)

        alog = jnp.pad(a_log[l].reshape(1, DN_HEADS), ((0, 0), (0, LANES - DN_HEADS)))
        dtb = jnp.pad(dt_bias[l].reshape(1, DN_HEADS), ((0, 0), (0, LANES - DN_HEADS)))
        og = o_norm_g[l].reshape(1, DN_DV)
        conv0 = jnp.zeros((bp, DN_CONV - 1, QKV_W), F32)
        s0 = jnp.zeros((bp, DN_HEADS, DN_DK, DN_DV), F32)
        oa_p, conv_p, delta_p = _gdn_prompt(proj, conv0, s0, w_conv[l], alog, dtb, og, bp, seq,
                                            col['z'], col['ab'])
        oa_s, conv_s, delta_s = _gdn_sample(proj, rows_p, state_conv_qkv[l], state_delta[l], w_conv[l],
                                            alog, dtb, og, bs, col['z'], col['ab'])

        ts = rt_sgu.tm
        assert ts == SG_CHUNK
        ws_p = jnp.tril(w_s[l][:, :ts, :ts]).astype(BF16)
        bias_p = jnp.repeat(b_s[l].T[:ts], SG_CH, axis=1)
        eye = jnp.eye(ts // lt, dtype=F32)
        ws_s = jnp.stack([jnp.kron(eye, jnp.tril(w_s[l, g, :lt, :lt])) for g in range(SG_GROUPS)]).astype(BF16)
        bias_s = jnp.tile(jnp.repeat(b_s[l, :, :lt].T, SG_CH, axis=1), (ts // lt, 1))
        ob, vn_s = _sgu(rt_sgu, proj, sg_ln_g[l].reshape(1, SG_W), sg_ln_b[l].reshape(1, SG_W),
                        ws_p, ws_s, bias_p, bias_s, col['u'], col['v'])

        pa, pb, wo = p_a[l].astype(BF16), p_b[l].astype(BF16), w_out[l].astype(BF16)
        rws = _split(jnp.pad(router_w[l], ((0, 0), (0, LANES - n_experts))))
        rw = jnp.concatenate([rws[0], rws[0], rws[1]], axis=0)
        rb = jnp.pad(router_b[l].reshape(1, n_experts), ((0, 0), (0, LANES - n_experts)))
        x1, h2, ti, tg = _merge(alpha, n_experts, rt_merge, oa_p, oa_s, ob, proj, xp, xs,
                                vec(2, rt_merge), vec(4, rt_merge), vec(3, rt_merge),
                                pa, pb, wo, ln1_g[l].reshape(1, d), ln1_b[l].reshape(1, d), rw, rb,
                                col['ga'], col['gb'])

        nj = w_dn.shape[2] // MOE_TF
        dest, slot_tok, nblk_used, items = _routing(ti[:, :TOP_K], n_experts, MOE_TM, nj, MOE_SUBS)
        xb = _dispatch(nblk_used, slot_tok, h2, MOE_TM)
        yb = _moe(items, xb, w_gu[l], b_gu[l], w_dn[l], b_dn[l], MOE_TM, MOE_TF, MOE_SUBS)
        xp, xs = _combine(alpha, rt_comb, dest, yb, tg, x1, vec(5, rt_comb),
                          ln2_g[l].reshape(1, d), ln2_b[l].reshape(1, d))

        outs['conv_p'].append(conv_p)
        outs['delta_p'].append(delta_p)
        outs['conv_s'].append(conv_s)
        outs['delta_s'].append(delta_s)
        outs['vrows'].append(vn_s.reshape(bs, lt, SG_W))

    return (xp.reshape(bp, seq, d), xs.reshape(bs, lt, d),
            jnp.stack(outs['conv_p']), jnp.stack(outs['delta_p']),
            jnp.stack(outs['conv_s']), jnp.stack(outs['delta_s']), jnp.stack(outs['vrows']))
```

```python
import functools
import math

import jax
import jax.numpy as jnp
from jax import lax
from jax.experimental import pallas as pl
from jax.experimental.pallas import tpu as pltpu

F32 = jnp.float32
BF16 = jnp.bfloat16
HIGHEST = lax.Precision.HIGHEST

DN_HEADS = 8
DN_DK = 128
DN_DV = 128
DN_CONV = 4
DN_CHUNK = 64
SG_GROUPS = 8
SG_CH = 128
SG_CHUNK = 128
TOP_K = 4
SWIGLU_LIMIT = 7.0
SWIGLU_ALPHA = 1.702
LN_EPS = 1e-5
NORM_EPS = 1e-6
QK_W = DN_HEADS * DN_DK
V_W = DN_HEADS * DN_DV
QKV_W = 2 * QK_W + V_W
SG_W = SG_GROUPS * SG_CH

LANES = 128
SUBLANES = 8
VMEM_LIMIT = 56 * 1024 * 1024

PROJ_TM = 512
PROJ_TN = 1152
MERGE_TM = 256
MOE_TM = 256
MOE_TF = 512
MOE_SUBS = 6
COMB_TQ = 128
SAMPLE_BT = 4
SAMPLE_TP = 8
GDN_HG = 4


def _sigmoid(x):
    return 1.0 / (1.0 + jnp.exp(-x))


def _silu(x):
    return x * _sigmoid(x)


def _dot(a, b):
    return jnp.dot(a.astype(BF16), b.astype(BF16), preferred_element_type=F32)


def _dot_nt(a, b):
    return lax.dot_general(a.astype(BF16), b.astype(BF16), (((1,), (1,)), ((), ())),
                           preferred_element_type=F32)


def _dot_hi(a, b):
    return jnp.dot(a, b, precision=HIGHEST, preferred_element_type=F32)


def _layer_norm(x, g, b):
    mu = jnp.mean(x, axis=-1, keepdims=True)
    xc = x - mu
    var = jnp.mean(xc * xc, axis=-1, keepdims=True)
    return xc * lax.rsqrt(var + LN_EPS) * g + b


def _params(*sem):
    return pltpu.CompilerParams(dimension_semantics=sem, vmem_limit_bytes=VMEM_LIMIT)


class _Rows:
    def __init__(self, tm, rows_p, rows_s, seq):
        assert rows_p % tm == 0 and rows_s % tm == 0 and seq % tm == 0, (tm, rows_p, rows_s, seq)
        self.tm, self.n_p, self.n_s = tm, rows_p // tm, rows_s // tm
        self.tiles_per_seq = seq // tm
        self.bp = rows_p // seq

    @property
    def n(self):
        return self.n_p + self.n_s

    def prompt(self, width, col=0):
        return pl.BlockSpec((self.tm, width), lambda i, *_: (jnp.minimum(i, self.n_p - 1), col))

    def sample(self, width, col=0, single=True):
        mode = dict(pipeline_mode=pl.Buffered(1)) if single else {}
        return pl.BlockSpec((self.tm, width), lambda i, *_: (jnp.maximum(i - self.n_p, 0), col), **mode)

    def joint(self, width, col=0):
        return pl.BlockSpec((self.tm, width), lambda i, *_: (i, col))

    def seq_vec(self, d):
        return pl.BlockSpec((None, 1, d), lambda i, *_: (jnp.minimum(i // self.tiles_per_seq, self.bp - 1), 0, 0))

    def row_vec(self, d):
        return pl.BlockSpec((None, self.tm, d), lambda i, *_: (jnp.maximum(i - self.n_p, 0), 0, 0),
                            pipeline_mode=pl.Buffered(1))


def _pick(is_sample, prompt_ref, sample_ref):
    return jnp.where(is_sample, sample_ref[...], prompt_ref[...])


def _adaln_kernel(c_ref, w_ref, b_ref, o_ref):
    o_ref[...] = _dot(_silu(c_ref[...]), w_ref[...]) + b_ref[...]


def _adaln(c, w, b):
    rows, d = c.shape
    n = w.shape[1]
    tn = 1024
    return pl.pallas_call(
        _adaln_kernel,
        grid=(n // tn,),
        in_specs=[pl.BlockSpec((rows, d), lambda j: (0, 0)),
                  pl.BlockSpec((d, tn), lambda j: (0, j)),
                  pl.BlockSpec((1, tn), lambda j: (0, j))],
        out_specs=pl.BlockSpec((rows, tn), lambda j: (0, j)),
        out_shape=jax.ShapeDtypeStruct((rows, n), F32),
        compiler_params=_params("arbitrary"),
        name="adaln",
    )(c, w, b.reshape(1, n))


def _inproj_kernel(n_p, xp_ref, xs_ref, scp_ref, scs_ref, shp_ref, shs_ref, w_ref, o_ref, h_scr):
    @pl.when(pl.program_id(1) == 0)
    def _():
        is_s = pl.program_id(0) >= n_p
        x = _pick(is_s, xp_ref, xs_ref)
        h_scr[...] = (x * (1.0 + _pick(is_s, scp_ref, scs_ref)) + _pick(is_s, shp_ref, shs_ref)).astype(BF16)

    o_ref[...] = jnp.dot(h_scr[...], w_ref[...], preferred_element_type=F32)


def _inproj(rt, xp, xs, sc, sh, w):
    d = xp.shape[1]
    nw = w.shape[1]
    tm = rt.tm
    return pl.pallas_call(
        functools.partial(_inproj_kernel, rt.n_p),
        grid=(rt.n, nw // PROJ_TN),
        in_specs=[rt.prompt(d), rt.sample(d), rt.seq_vec(d), rt.row_vec(d), rt.seq_vec(d), rt.row_vec(d),
                  pl.BlockSpec((d, PROJ_TN), lambda i, j: (0, j))],
        out_specs=pl.BlockSpec((tm, PROJ_TN), lambda i, j: (i, j)),
        out_shape=jax.ShapeDtypeStruct((rt.n * tm, nw), F32),
        scratch_shapes=[pltpu.VMEM((tm, d), BF16)],
        compiler_params=_params("arbitrary", "arbitrary"),
        name="inproj",
    )(xp, xs, sc[0], sc[1], sh[0], sh[1], w)


def _softplus(x):
    return jnp.maximum(x, 0.0) + jnp.log1p(jnp.exp(-jnp.abs(x)))


def _split(a):
    hi = a.astype(BF16)
    lo = (a - hi.astype(F32)).astype(BF16)
    return hi, lo


def _dot3(a, b):
    lhs = jnp.concatenate([a[0], a[1], a[0]], axis=1)
    rhs = jnp.concatenate([b[0], b[0], b[1]], axis=0)
    return jnp.dot(lhs, rhs, preferred_element_type=F32)


def _map(f, *lists):
    return [f(*args) for args in zip(*lists)]


def _unit_lower_inverses_minus_eye(mats, rows, cols):
    same = (rows // 16) == (cols // 16)
    n = [jnp.where(same, a, 0.0) for a in mats]
    b = _map(lambda a, x: a - x, mats, n)
    n2 = _map(lambda x: _dot(x, x), n)
    n4 = _map(lambda x: _dot(x, x), n2)
    n8 = _map(lambda x: _dot(x, x), n4)
    r = [-x for x in n]
    r = _map(lambda x, p: x + p + _dot(x, p), r, n2)
    r = _map(lambda x, p: x + p + _dot(x, p), r, n4)
    dm = _map(lambda x, p: x + p + _dot(x, p), r, n8)
    m = _map(lambda x, y: y + _dot(x, y), dm, b)
    m2 = _map(lambda x: _dot(x, x), m)
    xm = _map(lambda x, p: x + p + _dot(p, x), dm, m2)
    return _map(lambda x, p: x - p - _dot(p, x), xm, m)


def _gdn_prompt_kernel(qkv_ref, z_ref, ab_ref, hist_ref, s0_ref, wconv_ref, alog_ref, dt_ref, og_ref,
                       o_ref, conv_ref, s_ref, xp_scr):
    n = pl.program_id(1)
    c = DN_CHUNK
    pad = SUBLANES
    nr = GDN_HG * c
    ngrp = DN_HEADS // GDN_HG

    @pl.when(n == 0)
    def _():
        xp_scr[pl.ds(0, pad), :] = jnp.zeros((pad, QKV_W), F32)
        xp_scr[pl.ds(pad - (DN_CONV - 1), DN_CONV - 1), :] = hist_ref[...]
        s_ref[...] = s0_ref[...]

    x = qkv_ref[...]
    xp_scr[pl.ds(pad, c), :] = x
    y = jnp.zeros((c, QKV_W), F32)
    for j in range(DN_CONV):
        y = y + xp_scr[pl.ds(pad - (DN_CONV - 1) + j, c), :] * wconv_ref[pl.ds(j, 1), :]
    y = _silu(y)
    tail = xp_scr[pl.ds(c + pad - (DN_CONV - 1), DN_CONV - 1), :]
    conv_ref[...] = tail
    xp_scr[pl.ds(pad - (DN_CONV - 1), DN_CONV - 1), :] = tail

    ab = ab_ref[...]
    g = -jnp.exp(alog_ref[...]) * _softplus(ab + dt_ref[...])
    beta_all = _sigmoid(ab)
    r64 = lax.broadcasted_iota(jnp.int32, (c, c), 0)
    c64 = lax.broadcasted_iota(jnp.int32, (c, c), 1)
    gc = _dot_hi((r64 >= c64).astype(F32), g)
    z = z_ref[...]
    og = og_ref[...]

    rows = lax.broadcasted_iota(jnp.int32, (nr, nr), 0)
    cols = lax.broadcasted_iota(jnp.int32, (nr, nr), 1)
    same = (rows // c) == (cols // c)
    causal = same & (rows >= cols)
    strict = same & (rows > cols)
    rowhead = lax.broadcasted_iota(jnp.int32, (nr, 1), 0) // c
    groups = [range(grp * GDN_HG, (grp + 1) * GDN_HG) for grp in range(ngrp)]

    def stack(heads, off, width):
        return jnp.concatenate([y[:, off + h * width:off + (h + 1) * width] for h in heads], axis=0)

    def l2n(t):
        return t * lax.rsqrt(jnp.sum(t * t, axis=-1, keepdims=True) + NORM_EPS)

    q = [l2n(stack(hs, 0, DN_DK)) * (DN_DK ** -0.5) for hs in groups]
    k = [l2n(stack(hs, QK_W, DN_DK)) for hs in groups]
    v = [stack(hs, 2 * QK_W, DN_DV) for hs in groups]
    beta = [jnp.concatenate([beta_all[:, DN_HEADS + h:DN_HEADS + h + 1] for h in hs], axis=0) for hs in groups]
    gcf = [jnp.concatenate([jnp.broadcast_to(gc[:, h:h + 1], (c, LANES)) for h in hs], axis=0) for hs in groups]
    gcc = [t[:, 0:1] for t in gcf]
    grow = [t.T[0:1, :] for t in gcf]
    glast = [[gc[c - 1:c, h:h + 1] for h in hs] for hs in groups]
    gtot = [jnp.concatenate([jnp.broadcast_to(t, (c, 1)) for t in gl], axis=0) for gl in glast]
    decay = _map(lambda a, b: jnp.where(causal, jnp.exp(jnp.where(causal, a - b, 0.0)), 0.0), gcc, grow)
    egc = _map(jnp.exp, gcc)
    kb = _map(lambda a, b: a * b, k, beta)
    a_mat = _map(lambda a, b, dd: jnp.where(strict, _dot_nt(a, b) * dd, 0.0), kb, k, decay)
    tm1 = _unit_lower_inverses_minus_eye(a_mat, rows, cols)
    u = _map(lambda t, a, b: a * b + _dot(t, a * b), tm1, v, beta)
    w = _map(lambda t, a, b: a * b + _dot(t, a * b), tm1, kb, egc)
    attn = _map(lambda a, b, dd: _dot_nt(a, b) * dd, q, k, decay)
    qg = _map(lambda a, b: a * b, q, egc)
    kd_t = _map(lambda a, b, cc: (a * jnp.exp(b - cc)).T, k, gtot, gcc)

    for gi, hs in enumerate(groups):
        v_news, qss = [], []
        for hl, h in enumerate(hs):
            sl = slice(hl * c, (hl + 1) * c)
            rs = _dot(jnp.concatenate([w[gi][sl], qg[gi][sl]], axis=0), s_ref[h])
            v_news.append(u[gi][sl] - rs[:c])
            qss.append(rs[c:])
        v_new = jnp.concatenate(v_news, axis=0)
        o = jnp.concatenate(qss, axis=0) + _dot(attn[gi], v_new)
        for hl, h in enumerate(hs):
            upd = _dot(kd_t[gi], jnp.where(rowhead == hl, v_new, 0.0))
            s_ref[h] = s_ref[h] * jnp.exp(glast[gi][hl]) + upd
        o = o * lax.rsqrt(jnp.mean(o * o, axis=-1, keepdims=True) + NORM_EPS) * og
        for hl, h in enumerate(hs):
            o_ref[:, h * DN_DV:(h + 1) * DN_DV] = o[hl * c:(hl + 1) * c] * _silu(z[:, h * DN_DV:(h + 1) * DN_DV])


def _gdn_prompt(proj, hist, s0, wconv, alog, dtb, og, batch, seq, col_z, col_ab):
    c = DN_CHUNK
    nchunk = seq // c
    rows = batch * seq
    row = lambda b, n: b * nchunk + n
    return pl.pallas_call(
        _gdn_prompt_kernel,
        grid=(batch, nchunk),
        in_specs=[pl.BlockSpec((c, QKV_W), lambda b, n: (row(b, n), 0)),
                  pl.BlockSpec((c, V_W), lambda b, n: (row(b, n), col_z // V_W)),
                  pl.BlockSpec((c, LANES), lambda b, n: (row(b, n), col_ab // LANES)),
                  pl.BlockSpec((None, DN_CONV - 1, QKV_W), lambda b, n: (b, 0, 0)),
                  pl.BlockSpec((None, DN_HEADS, DN_DK, DN_DV), lambda b, n: (b, 0, 0, 0)),
                  pl.BlockSpec((DN_CONV, QKV_W), lambda b, n: (0, 0)),
                  pl.BlockSpec((1, LANES), lambda b, n: (0, 0)),
                  pl.BlockSpec((1, LANES), lambda b, n: (0, 0)),
                  pl.BlockSpec((1, DN_DV), lambda b, n: (0, 0))],
        out_specs=[pl.BlockSpec((c, V_W), lambda b, n: (row(b, n), 0)),
                   pl.BlockSpec((None, DN_CONV - 1, QKV_W), lambda b, n: (b, 0, 0)),
                   pl.BlockSpec((None, DN_HEADS, DN_DK, DN_DV), lambda b, n: (b, 0, 0, 0))],
        out_shape=[jax.ShapeDtypeStruct((rows, V_W), F32),
                   jax.ShapeDtypeStruct((batch, DN_CONV - 1, QKV_W), F32),
                   jax.ShapeDtypeStruct((batch, DN_HEADS, DN_DK, DN_DV), F32)],
        scratch_shapes=[pltpu.VMEM((c + SUBLANES, QKV_W), F32)],
        compiler_params=_params("arbitrary", "arbitrary"),
        name="gdn_prompt",
    )(proj, proj, proj, hist, s0, wconv, alog, dtb, og)


def _gdn_sample_kernel(qkv_ref, z_ref, ab_ref, hist_ref, s0_ref, wconv_ref, alog_ref, dt_ref, og_ref,
                       o_ref, conv_ref, s_ref, xp_scr, ab_scr, wq_scr, r_scr, kdt_scr, vn_scr, gl_scr):
    bt, tp, lt = SAMPLE_BT, SAMPLE_TP, DN_CONV
    nblk = DN_HEADS * bt
    nrow = nblk * tp
    hist_rows = DN_CONV - 1

    xp_scr[...] = jnp.zeros(xp_scr.shape, F32)
    ab_scr[...] = jnp.zeros(ab_scr.shape, F32)
    ys = []
    for b in range(bt):
        xp_scr[b, pl.ds(SUBLANES - hist_rows, hist_rows), :] = hist_ref[b]
        xp_scr[b, pl.ds(SUBLANES, lt), :] = qkv_ref[pl.ds(b * lt, lt), :]
        yb = jnp.zeros((tp, QKV_W), F32)
        for j in range(DN_CONV):
            yb = yb + xp_scr[b, pl.ds(SUBLANES - hist_rows + j, tp), :] * wconv_ref[pl.ds(j, 1), :]
        ys.append(_silu(yb))
        conv_ref[b] = xp_scr[b, pl.ds(SUBLANES + lt - hist_rows, hist_rows), :]
        ab_scr[b, pl.ds(0, lt), :] = ab_ref[pl.ds(b * lt, lt), :]
    y = jnp.concatenate(ys, axis=0)
    ab = jnp.concatenate([ab_scr[b] for b in range(bt)], axis=0)
    tok = lax.broadcasted_iota(jnp.int32, (bt * tp, 1), 0) % tp
    real = tok < lt
    g_all = jnp.where(real, -jnp.exp(alog_ref[...]) * _softplus(ab + dt_ref[...]), 0.0)
    beta_all = jnp.where(real, _sigmoid(ab), 0.0)

    def heads_to_rows(t, off, width):
        return jnp.concatenate([t[:, off + h * width:off + (h + 1) * width] for h in range(DN_HEADS)], axis=0)

    realr = jnp.concatenate([real] * DN_HEADS, axis=0)
    q = jnp.where(realr, heads_to_rows(y, 0, DN_DK), 0.0)
    k = jnp.where(realr, heads_to_rows(y, QK_W, DN_DK), 0.0)
    v = jnp.where(realr, heads_to_rows(y, 2 * QK_W, DN_DV), 0.0)
    gcol = jnp.concatenate([g_all[:, h:h + 1] for h in range(DN_HEADS)], axis=0)
    beta = jnp.concatenate([beta_all[:, DN_HEADS + h:DN_HEADS + h + 1] for h in range(DN_HEADS)], axis=0)
    q = q * lax.rsqrt(jnp.sum(q * q, axis=-1, keepdims=True) + NORM_EPS) * (DN_DK ** -0.5)
    k = k * lax.rsqrt(jnp.sum(k * k, axis=-1, keepdims=True) + NORM_EPS)

    rows = lax.broadcasted_iota(jnp.int32, (nrow, nrow), 0)
    cols = lax.broadcasted_iota(jnp.int32, (nrow, nrow), 1)
    same = (rows // tp) == (cols // tp)
    causal = same & (rows >= cols)
    strict = same & (rows > cols)
    gfull = jnp.broadcast_to(gcol, (nrow, LANES))
    gc = _dot_hi(causal.astype(F32), gfull)
    gtot = _dot_hi(same.astype(F32), gfull)
    gcc = gc[:, 0:1]
    grow = gc.T[0:1, :]
    decay = jnp.where(causal, jnp.exp(jnp.where(causal, gcc - grow, 0.0)), 0.0)
    egc = jnp.exp(gcc)
    kb = k * beta
    a_mat = jnp.where(strict, _dot_nt(kb, k) * decay, 0.0)
    a2 = _dot(a_mat, a_mat)
    tm1 = a2 - a_mat - _dot(a_mat, a2)
    u = v * beta + _dot(tm1, v * beta)
    w = kb * egc + _dot(tm1, kb * egc)
    attn = _dot_nt(q, k) * decay
    qg = q * egc
    for i in range(nblk):
        wq_scr[pl.ds(2 * tp * i, tp), :] = w[i * tp:(i + 1) * tp, :]
        wq_scr[pl.ds(2 * tp * i + tp, tp), :] = qg[i * tp:(i + 1) * tp, :]
    kdt_scr[...] = (k * jnp.exp(gtot[:, 0:1] - gcc)).T
    gl_scr[...] = jnp.exp(gtot)

    def read_state(i, carry):
        r0 = pl.multiple_of(i * 2 * tp, 2 * tp)
        r_scr[pl.ds(r0, 2 * tp), :] = _dot(wq_scr[pl.ds(r0, 2 * tp), :], s0_ref[i % bt, i // bt])
        return carry

    lax.fori_loop(0, nblk, read_state, 0, unroll=4)
    ws = jnp.concatenate([r_scr[pl.ds(2 * tp * i, tp), :] for i in range(nblk)], axis=0)
    qs = jnp.concatenate([r_scr[pl.ds(2 * tp * i + tp, tp), :] for i in range(nblk)], axis=0)
    v_new = u - ws
    vn_scr[...] = v_new
    o = qs + _dot(attn, v_new)
    blockid = lax.broadcasted_iota(jnp.int32, (nrow, 1), 0) // tp

    def write_state(i, carry):
        r0 = pl.multiple_of(i * tp, tp)
        upd = _dot(kdt_scr[...], jnp.where(blockid == i, vn_scr[...], 0.0))
        s_ref[i % bt, i // bt] = s0_ref[i % bt, i // bt] * gl_scr[pl.ds(r0, 1), :] + upd
        return carry

    lax.fori_loop(0, nblk, write_state, 0, unroll=4)

    og = og_ref[...]
    o = o * lax.rsqrt(jnp.mean(o * o, axis=-1, keepdims=True) + NORM_EPS) * og
    for b in range(bt):
        for h in range(DN_HEADS):
            zbh = z_ref[pl.ds(b * lt, lt), h * DN_DV:(h + 1) * DN_DV]
            blk = o[(h * bt + b) * tp:(h * bt + b) * tp + lt, :]
            o_ref[pl.ds(b * lt, lt), h * DN_DV:(h + 1) * DN_DV] = blk * _silu(zbh)


def _gdn_sample(proj, row0, hist, s0, wconv, alog, dtb, og, batch, col_z, col_ab):
    bt, tp, lt = SAMPLE_BT, SAMPLE_TP, DN_CONV
    rows = batch * lt
    nrow = DN_HEADS * bt * tp
    blk = bt * lt
    assert row0 % blk == 0 and batch % bt == 0
    b0 = row0 // blk
    return pl.pallas_call(
        _gdn_sample_kernel,
        grid=(batch // bt,),
        in_specs=[pl.BlockSpec((blk, QKV_W), lambda i: (b0 + i, 0)),
                  pl.BlockSpec((blk, V_W), lambda i: (b0 + i, col_z // V_W)),
                  pl.BlockSpec((blk, LANES), lambda i: (b0 + i, col_ab // LANES)),
                  pl.BlockSpec((bt, DN_CONV - 1, QKV_W), lambda i: (i, 0, 0)),
                  pl.BlockSpec((bt, DN_HEADS, DN_DK, DN_DV), lambda i: (i, 0, 0, 0)),
                  pl.BlockSpec((DN_CONV, QKV_W), lambda i: (0, 0)),
                  pl.BlockSpec((1, LANES), lambda i: (0, 0)),
                  pl.BlockSpec((1, LANES), lambda i: (0, 0)),
                  pl.BlockSpec((1, DN_DV), lambda i: (0, 0))],
        out_specs=[pl.BlockSpec((blk, V_W), lambda i: (i, 0)),
                   pl.BlockSpec((bt, DN_CONV - 1, QKV_W), lambda i: (i, 0, 0)),
                   pl.BlockSpec((bt, DN_HEADS, DN_DK, DN_DV), lambda i: (i, 0, 0, 0))],
        out_shape=[jax.ShapeDtypeStruct((rows, V_W), F32),
                   jax.ShapeDtypeStruct((batch, DN_CONV - 1, QKV_W), F32),
                   jax.ShapeDtypeStruct((batch, DN_HEADS, DN_DK, DN_DV), F32)],
        scratch_shapes=[pltpu.VMEM((bt, SUBLANES + tp, QKV_W), F32),
                        pltpu.VMEM((bt, tp, LANES), F32),
                        pltpu.VMEM((2 * nrow, DN_DK), F32),
                        pltpu.VMEM((2 * nrow, DN_DV), F32),
                        pltpu.VMEM((DN_DK, nrow), F32),
                        pltpu.VMEM((nrow, DN_DV), F32),
                        pltpu.VMEM((nrow, LANES), F32)],
        compiler_params=_params("arbitrary"),
        name="gdn_sample",
    )(proj, proj, proj, hist, s0, wconv, alog, dtb, og)


def _sgu_kernel(n_p, u_ref, v_ref, g_ref, b_ref, wsp_ref, wss_ref, bp_ref, bs_ref, o_ref, vn_ref):
    is_s = pl.program_id(0) >= n_p
    u = jax.nn.gelu(u_ref[...])
    vn = _layer_norm(jax.nn.gelu(v_ref[...]), g_ref[...], b_ref[...])
    vn_ref[...] = vn
    bias = _pick(is_s, bp_ref, bs_ref)
    for g in range(SG_GROUPS):
        sl = slice(g * SG_CH, (g + 1) * SG_CH)
        ws = jnp.where(is_s, wss_ref[g], wsp_ref[g])
        mixed = _dot(ws, vn[:, sl]) + bias[:, sl]
        o_ref[:, sl] = u[:, sl] * mixed


def _sgu(rt, proj, ln_g, ln_b, ws_p, ws_s, bias_p, bias_s, col_u, col_v):
    t = rt.tm
    const = lambda shape: pl.BlockSpec(shape, lambda i: (0,) * len(shape))
    return pl.pallas_call(
        functools.partial(_sgu_kernel, rt.n_p),
        grid=(rt.n,),
        in_specs=[rt.joint(SG_W, col_u // SG_W), rt.joint(SG_W, col_v // SG_W),
                  const((1, SG_W)), const((1, SG_W)),
                  const((SG_GROUPS, t, t)), const((SG_GROUPS, t, t)), const((t, SG_W)), const((t, SG_W))],
        out_specs=[rt.joint(SG_W), rt.sample(SG_W, single=False)],
        out_shape=[jax.ShapeDtypeStruct((rt.n * t, SG_W), F32),
                   jax.ShapeDtypeStruct((rt.n_s * t, SG_W), F32)],
        compiler_params=_params("arbitrary"),
        name="sgu",
    )(proj, proj, ln_g, ln_b, ws_p, ws_s, bias_p, bias_s)


def _merge_kernel(alpha, n_experts, n_p, oap_ref, oas_ref, ob_ref, ga_ref, gb_ref, xp_ref, xs_ref,
                  gtp_ref, gts_ref, scp_ref, scs_ref, shp_ref, shs_ref,
                  pa_ref, pb_ref, wo_ref, lg_ref, lb_ref, rw_ref, rb_ref,
                  x1_ref, h2_ref, ti_ref, tg_ref):
    is_s = pl.program_id(0) >= n_p
    oa = _pick(is_s, oap_ref, oas_ref)
    merged = (_sigmoid(ga_ref[...]) * _dot(oa, pa_ref[...])
              + _sigmoid(gb_ref[...]) * _dot(ob_ref[...], pb_ref[...]))
    y = _dot(merged, wo_ref[...])
    x = _pick(is_s, xp_ref, xs_ref)
    x1 = _layer_norm(alpha * x + _pick(is_s, gtp_ref, gts_ref) * y, lg_ref[...], lb_ref[...])
    x1_ref[...] = x1
    h2 = x1 * (1.0 + _pick(is_s, scp_ref, scs_ref)) + _pick(is_s, shp_ref, shs_ref)
    h2_ref[...] = h2
    hs = _split(h2)
    logits = jnp.dot(jnp.concatenate([hs[0], hs[1], hs[0]], axis=1), rw_ref[...],
                     preferred_element_type=F32) + rb_ref[...]
    lane = lax.broadcasted_iota(jnp.int32, logits.shape, 1)
    logits = jnp.where(lane < n_experts, logits, -jnp.inf)
    ti = jnp.zeros(logits.shape, jnp.int32)
    tv = jnp.zeros(logits.shape, F32)
    top = None
    for kk in range(TOP_K):
        m = jnp.max(logits, axis=-1, keepdims=True)
        idx = jnp.min(jnp.where(logits == m, lane.astype(F32), float(LANES)), axis=-1,
                      keepdims=True).astype(jnp.int32)
        if kk == 0:
            top = m
        ti = jnp.where(lane == kk, idx, ti)
        tv = jnp.where(lane == kk, jnp.exp(m - top), tv)
        logits = jnp.where(lane == idx, -jnp.inf, logits)
    ti_ref[...] = ti
    tg_ref[...] = tv / jnp.sum(tv, axis=-1, keepdims=True)


def _merge(alpha, n_experts, rt, oa_p, oa_s, ob, proj, xp, xs, gt, sc, sh, pa, pb, wo, lg, lb, rw, rb,
           col_ga, col_gb):
    d = xp.shape[1]
    tm = rt.tm
    rows = rt.n * tm
    const = lambda shape: pl.BlockSpec(shape, lambda i: (0,) * len(shape), pipeline_mode=pl.Buffered(1))
    return pl.pallas_call(
        functools.partial(_merge_kernel, alpha, n_experts, rt.n_p),
        grid=(rt.n,),
        in_specs=[rt.prompt(V_W), rt.sample(V_W), rt.joint(SG_W),
                  rt.joint(d, col_ga // d), rt.joint(d, col_gb // d),
                  rt.prompt(d), rt.sample(d),
                  rt.seq_vec(d), rt.row_vec(d), rt.seq_vec(d), rt.row_vec(d), rt.seq_vec(d), rt.row_vec(d),
                  const((V_W, d)), const((SG_W, d)), const((d, d)),
                  const((1, d)), const((1, d)), const((3 * d, LANES)), const((1, LANES))],
        out_specs=[rt.joint(d), rt.joint(d), rt.joint(LANES), rt.joint(LANES)],
        out_shape=[jax.ShapeDtypeStruct((rows, d), F32),
                   jax.ShapeDtypeStruct((rows, d), F32),
                   jax.ShapeDtypeStruct((rows, LANES), jnp.int32),
                   jax.ShapeDtypeStruct((rows, LANES), F32)],
        compiler_params=_params("arbitrary"),
        name="merge",
    )(oa_p, oa_s, ob, proj, proj, xp, xs, gt[0], gt[1], sc[0], sc[1], sh[0], sh[1],
      pa, pb, wo, lg, lb, rw, rb)


FILL_ROWS = (128, 64, 32, 16, 8)


def _dispatch_kernel(tm, n_experts, padlo_ref, padlen_ref, nblk_ref, dest_ref, h_ref, xb_hbm, zero_scr, sem, zsem):
    i = pl.program_id(0)
    tq = h_ref.shape[0]
    nb_max = xb_hbm.shape[0] // tm
    zrows = zero_scr.shape[0]

    def fill(act):
        def per_expert(e, carry):
            lo = padlo_ref[e]
            ln = padlen_ref[e]
            head = (SUBLANES - lo % SUBLANES) % SUBLANES
            for r in range(SUBLANES - 1):
                @pl.when(r < head)
                def _(r=r):
                    act(pltpu.make_async_copy(zero_scr.at[pl.ds(0, 1)], xb_hbm.at[pl.ds(lo + r, 1)], zsem))
            off = lo + head
            rem = ln - head
            for b in FILL_ROWS:
                @pl.when((rem & b) != 0)
                def _(off=off, b=b):
                    dst = xb_hbm.at[pl.ds(pl.multiple_of(off, SUBLANES), b)]
                    act(pltpu.make_async_copy(zero_scr.at[pl.ds(0, b)], dst, zsem))
                off = off + (rem & b)
            return carry

        lax.fori_loop(0, n_experts, per_expert, 0)

        def per_block(blk, carry):
            for part in range(tm // zrows):
                row0 = pl.multiple_of(blk * tm + part * zrows, zrows)
                act(pltpu.make_async_copy(zero_scr, xb_hbm.at[pl.ds(row0, zrows)], zsem))
            return carry

        lax.fori_loop(nblk_ref[0], nb_max, per_block, 0)

    @pl.when(i == 0)
    def _():
        zero_scr[...] = jnp.zeros(zero_scr.shape, F32)
        fill(lambda cp: cp.start())
        fill(lambda cp: cp.wait())

    def start(t, carry):
        for kk in range(TOP_K):
            pltpu.make_async_copy(h_ref.at[pl.ds(t, 1)], xb_hbm.at[pl.ds(dest_ref[0, t * TOP_K + kk], 1)],
                                  sem).start()
        return carry

    lax.fori_loop(0, tq, start, 0, unroll=4)
    for kk in range(TOP_K):
        pltpu.make_async_copy(h_ref, xb_hbm.at[pl.ds(0, tq)], sem).wait()


def _dispatch(padlo, padlen, nblk_used, dest, h2, tm, nb_max):
    t, d = h2.shape
    tq = math.gcd(256, t)
    n_experts = padlo.shape[0]
    grid_spec = pltpu.PrefetchScalarGridSpec(
        num_scalar_prefetch=3,
        grid=(t // tq,),
        in_specs=[pl.BlockSpec((None, 1, tq * TOP_K), lambda i, *_: (i, 0, 0), memory_space=pltpu.SMEM),
                  pl.BlockSpec((tq, d), lambda i, *_: (i, 0))],
        out_specs=pl.BlockSpec(memory_space=pl.ANY),
        scratch_shapes=[pltpu.VMEM((FILL_ROWS[0], d), F32), pltpu.SemaphoreType.DMA(()),
                        pltpu.SemaphoreType.DMA(())],
    )
    return pl.pallas_call(
        functools.partial(_dispatch_kernel, tm, n_experts),
        grid_spec=grid_spec,
        out_shape=jax.ShapeDtypeStruct((nb_max * tm, d), F32),
        compiler_params=pltpu.CompilerParams(dimension_semantics=("arbitrary",), vmem_limit_bytes=VMEM_LIMIT,
                                             has_side_effects=True),
        name="dispatch",
    )(padlo, padlen, nblk_used.reshape(1), dest.reshape(t // tq, 1, tq * TOP_K), h2)


def _moe_kernel(nj, e_ref, j_ref, xb_ref, ob_ref, r_ref, flag_ref,
                x_ref, wg_ref, wu_ref, wd_ref, bg_ref, bu_ref, bd_ref, o_ref,
                wg_scr, wu_scr, wd_scr, acc_scr):
    s = pl.program_id(0)
    flags = flag_ref[s]
    j = j_ref[s]
    r = r_ref[s]

    @pl.when((flags & 2) != 0)
    def _():
        wg_scr[...] = wg_ref[...].astype(BF16)
        wu_scr[...] = wu_ref[...].astype(BF16)
        wd_scr[...] = wd_ref[...].astype(BF16)

    @pl.when(flags == 0)
    def _():
        o_ref[...] = jnp.zeros(o_ref.shape, F32)

    @pl.when((flags & 1) != 0)
    def _():
        x = x_ref[...].astype(BF16)
        gate = jnp.dot(x, wg_scr[...], preferred_element_type=F32) + bg_ref[...]
        up = jnp.dot(x, wu_scr[...], preferred_element_type=F32) + bu_ref[...]
        gate = jnp.minimum(gate, SWIGLU_LIMIT)
        up = jnp.clip(up, -SWIGLU_LIMIT, SWIGLU_LIMIT)
        act = (up + 1.0) * gate * _sigmoid(SWIGLU_ALPHA * gate)
        y = jnp.dot(act.astype(BF16), wd_scr[...], preferred_element_type=F32)

        @pl.when(j == 0)
        def _():
            acc_scr[r] = y

        @pl.when(jnp.logical_and(j > 0, j < nj - 1))
        def _():
            acc_scr[r] = acc_scr[r] + y

        @pl.when(j == nj - 1)
        def _():
            o_ref[...] = acc_scr[r] + y + bd_ref[...]


def _moe(items, xb, w_gu, b_gu, w_dn, b_dn, tm, tf, subs):
    item_e, item_j, item_xb, item_ob, item_r, item_flag = items
    n_items = item_e.shape[0]
    nslot, d = xb.shape
    n_exp, _, f2 = w_gu.shape
    f = f2 // 2
    nj = f // tf
    assert nj >= 2
    grid_spec = pltpu.PrefetchScalarGridSpec(
        num_scalar_prefetch=6,
        grid=(n_items,),
        in_specs=[pl.BlockSpec((tm, d), lambda s, e, j, xbk, obk, r, fl: (xbk[s], 0)),
                  pl.BlockSpec((None, d, tf), lambda s, e, j, xbk, obk, r, fl: (e[s], 0, j[s])),
                  pl.BlockSpec((None, d, tf), lambda s, e, j, xbk, obk, r, fl: (e[s], 0, nj + j[s])),
                  pl.BlockSpec((None, tf, d), lambda s, e, j, xbk, obk, r, fl: (e[s], j[s], 0)),
                  pl.BlockSpec((None, 1, tf), lambda s, e, j, xbk, obk, r, fl: (e[s], 0, j[s])),
                  pl.BlockSpec((None, 1, tf), lambda s, e, j, xbk, obk, r, fl: (e[s], 0, nj + j[s])),
                  pl.BlockSpec((None, 1, d), lambda s, e, j, xbk, obk, r, fl: (e[s], 0, 0))],
        out_specs=pl.BlockSpec((tm, d), lambda s, e, j, xbk, obk, r, fl: (obk[s], 0)),
        scratch_shapes=[pltpu.VMEM((d, tf), BF16), pltpu.VMEM((d, tf), BF16), pltpu.VMEM((tf, d), BF16),
                        pltpu.VMEM((subs, tm, d), F32)],
    )
    return pl.pallas_call(
        functools.partial(_moe_kernel, nj),
        grid_spec=grid_spec,
        out_shape=jax.ShapeDtypeStruct((nslot, d), F32),
        compiler_params=_params("arbitrary"),
        name="moe",
    )(item_e, item_j, item_xb, item_ob, item_r, item_flag,
      xb, w_gu, w_gu, w_dn, b_gu.reshape(n_exp, 1, f2), b_gu.reshape(n_exp, 1, f2), b_dn.reshape(n_exp, 1, d))


def _combine_kernel(alpha, n_p, dest_ref, destn_ref, yb_hbm, tg_ref, x1_ref, gtp_ref, gts_ref, lg_ref, lb_ref,
                    op_ref, os_ref, buf, sem):
    i = pl.program_id(0)
    tq = x1_ref.shape[0]
    slot = i % 2

    def issue(ids_ref, sl):
        def start(t, carry):
            for kk in range(TOP_K):
                pltpu.make_async_copy(yb_hbm.at[pl.ds(ids_ref[0, t * TOP_K + kk], 1)],
                                      buf.at[sl, kk, pl.ds(t, 1)], sem.at[sl]).start()
            return carry

        lax.fori_loop(0, tq, start, 0, unroll=4)

    @pl.when(i == 0)
    def _():
        issue(dest_ref, 0)

    @pl.when(i + 1 < pl.num_programs(0))
    def _():
        issue(destn_ref, 1 - slot)

    for kk in range(TOP_K):
        pltpu.make_async_copy(yb_hbm.at[pl.ds(0, tq)], buf.at[slot, kk], sem.at[slot]).wait()
    tg = tg_ref[...]
    y = jnp.zeros(x1_ref.shape, F32)
    for kk in range(TOP_K):
        y = y + buf[slot, kk] * tg[:, kk:kk + 1]
    is_s = i >= n_p
    out = _layer_norm(alpha * x1_ref[...] + _pick(is_s, gtp_ref, gts_ref) * y, lg_ref[...], lb_ref[...])

    @pl.when(jnp.logical_not(is_s))
    def _():
        op_ref[...] = out

    @pl.when(is_s)
    def _():
        os_ref[...] = out


def _combine(alpha, rt, dest, yb, tg, x1, gt, lg, lb):
    d = x1.shape[1]
    tq = rt.tm
    ids = dest.reshape(rt.n, 1, tq * TOP_K)
    return pl.pallas_call(
        functools.partial(_combine_kernel, alpha, rt.n_p),
        grid=(rt.n,),
        in_specs=[pl.BlockSpec((None, 1, tq * TOP_K), lambda i: (i, 0, 0), memory_space=pltpu.SMEM),
                  pl.BlockSpec((None, 1, tq * TOP_K), lambda i: (jnp.minimum(i + 1, rt.n - 1), 0, 0),
                               memory_space=pltpu.SMEM),
                  pl.BlockSpec(memory_space=pl.ANY),
                  rt.joint(LANES), rt.joint(d), rt.seq_vec(d), rt.row_vec(d),
                  pl.BlockSpec((1, d), lambda i: (0, 0)),
                  pl.BlockSpec((1, d), lambda i: (0, 0))],
        out_specs=[rt.prompt(d), rt.sample(d, single=False)],
        out_shape=[jax.ShapeDtypeStruct((rt.n_p * tq, d), F32), jax.ShapeDtypeStruct((rt.n_s * tq, d), F32)],
        scratch_shapes=[pltpu.VMEM((2, TOP_K, tq, d), F32), pltpu.SemaphoreType.DMA((2,))],
        compiler_params=_params("arbitrary"),
        name="combine",
    )(ids, ids, yb, tg, x1, gt[0], gt[1], lg, lb)


def _routing(ti, n_experts, tm, nj, subs):
    t = ti.shape[0]
    n_assign = t * TOP_K
    nb_max = n_assign // tm + n_experts
    onehot = (ti[:, :, None] == jnp.arange(n_experts, dtype=jnp.int32)[None, None, :]).astype(jnp.int32)
    per_tok = jnp.sum(onehot, axis=1)
    cum = jnp.cumsum(per_tok, axis=0)
    counts = cum[-1]
    rank = jnp.take_along_axis(cum, ti, axis=1) - 1
    nblk = (counts + tm - 1) // tm
    blk_end = jnp.cumsum(nblk)
    blk_start = blk_end - nblk
    dest = blk_start[ti] * tm + rank
    pad_lo = (blk_start * tm + counts).astype(jnp.int32)
    pad_len = (nblk * tm - counts).astype(jnp.int32)
    total_blk = blk_end[-1]
    blocks = jnp.arange(nb_max, dtype=jnp.int32)
    blk_e = jnp.minimum(jnp.sum((blk_end[None, :] <= blocks[:, None]).astype(jnp.int32), axis=1), n_experts - 1)
    r_in_e = blocks - blk_start[blk_e]
    g0 = blk_start[blk_e] + (r_in_e // subs) * subs
    nsub = jnp.minimum(subs, nblk[blk_e] - (r_in_e // subs) * subs)
    p = jnp.arange(nb_max * nj, dtype=jnp.int32)
    bp = p // nj
    valid = bp < total_blk
    last = jnp.maximum(total_blk - 1, 0)
    bq = jnp.where(valid, bp, last)
    gq, nq, eq = g0[bq], jnp.maximum(nsub[bq], 1), blk_e[bq]
    local = p - nj * gq
    jq = jnp.where(valid, local // nq, nj - 1)
    rq = jnp.where(valid, local % nq, nq - 1)
    item_xb = gq + rq
    item_ob = jnp.where(valid, jnp.where(jq == nj - 1, gq + rq, gq), bp)
    flags = valid.astype(jnp.int32) + 2 * (valid & (rq == 0)).astype(jnp.int32)
    items = (eq.astype(jnp.int32), jq.astype(jnp.int32), item_xb.astype(jnp.int32),
             item_ob.astype(jnp.int32), rq.astype(jnp.int32), flags)
    return dest.astype(jnp.int32), pad_lo, pad_len, total_blk.astype(jnp.int32), nb_max, items


def _win_prep_kernel(w_ref, o_ref):
    o_a = QKV_W + V_W
    o_u = o_a + 2 * DN_HEADS
    x = w_ref[...]
    rest = x.shape[1] - o_u
    o_ref[:, 0:o_a] = x[:, 0:o_a].astype(BF16)
    o_ref[:, o_a:o_a + rest] = x[:, o_u:].astype(BF16)
    lane = lax.broadcasted_iota(jnp.int32, (x.shape[0], LANES), 1)
    o_ref[:, o_a + rest:] = jnp.where(lane < 2 * DN_HEADS, x[:, o_a:o_a + LANES], 0.0).astype(BF16)


def _rearranged_w_in(w_in, d):
    dm, nw = w_in.shape
    tr = 128
    w = pl.pallas_call(
        _win_prep_kernel,
        grid=(dm // tr,),
        in_specs=[pl.BlockSpec((tr, nw), lambda i: (i, 0))],
        out_specs=pl.BlockSpec((tr, nw - 2 * DN_HEADS + LANES), lambda i: (i, 0)),
        out_shape=jax.ShapeDtypeStruct((dm, nw - 2 * DN_HEADS + LANES), BF16),
        compiler_params=_params("arbitrary"),
        name="w_in_prep",
    )(w_in)
    cols = dict(z=QKV_W, u=QKV_W + V_W, v=QKV_W + V_W + SG_W, ga=QKV_W + V_W + 2 * SG_W,
                gb=QKV_W + V_W + 2 * SG_W + d, ab=QKV_W + V_W + 2 * SG_W + 2 * d)
    return w, cols


def kernel(x_prompt, x_sample, state_conv_qkv, state_delta, c_prompt, c_sample, w_ada, b_ada, w_in, w_conv, a_log, dt_bias, o_norm_g, sg_ln_g, sg_ln_b, w_s, b_s, p_a, p_b, w_out, ln1_g, ln1_b, router_w, router_b, w_gu, b_gu, w_dn, b_dn, ln2_g, ln2_b):
    depth = w_ada.shape[0]
    alpha = float((2 * depth) ** 0.25)
    bp, seq, d = x_prompt.shape
    bs, lt, _ = x_sample.shape
    assert lt == DN_CONV and seq % SG_CHUNK == 0
    n_experts = router_w.shape[2]
    rows_p, rows_s = bp * seq, bs * lt
    xp = x_prompt.reshape(rows_p, d)
    xs = x_sample.reshape(rows_s, d)
    c_all = jnp.concatenate([c_prompt, c_sample], axis=0)
    tile = lambda cap: _Rows(math.gcd(math.gcd(cap, rows_s), seq), rows_p, rows_s, seq)
    rt_proj, rt_sgu, rt_merge, rt_comb = tile(PROJ_TM), tile(SG_CHUNK), tile(MERGE_TM), tile(COMB_TQ)
    outs = dict(conv_p=[], delta_p=[], conv_s=[], delta_s=[], vrows=[])

    for l in range(depth):
        mod = _adaln(c_all, w_ada[l], b_ada[l])
        mods = jnp.split(mod, 6, axis=1)
        rows_of = [jnp.repeat(m[bp:], lt, axis=0) for m in mods]

        def vec(i, rt):
            return mods[i][:bp].reshape(bp, 1, d), rows_of[i].reshape(rt.n_s, rt.tm, d)

        w_r, col = _rearranged_w_in(w_in[l], d)
        proj = _inproj(rt_proj, xp, xs, vec(1, rt_proj), vec(0, rt_proj), w_r)

        alog = jnp.pad(a_log[l].reshape(1, DN_HEADS), ((0, 0), (0, LANES - DN_HEADS)))
        dtb = jnp.pad(dt_bias[l].reshape(1, DN_HEADS), ((0, 0), (0, LANES - DN_HEADS)))
        og = o_norm_g[l].reshape(1, DN_DV)
        conv0 = jnp.zeros((bp, DN_CONV - 1, QKV_W), F32)
        s0 = jnp.zeros((bp, DN_HEADS, DN_DK, DN_DV), F32)
        oa_p, conv_p, delta_p = _gdn_prompt(proj, conv0, s0, w_conv[l], alog, dtb, og, bp, seq,
                                            col['z'], col['ab'])
        oa_s, conv_s, delta_s = _gdn_sample(proj, rows_p, state_conv_qkv[l], state_delta[l], w_conv[l],
                                            alog, dtb, og, bs, col['z'], col['ab'])

        ts = rt_sgu.tm
        assert ts == SG_CHUNK
        ws_p = jnp.tril(w_s[l][:, :ts, :ts]).astype(BF16)
        bias_p = jnp.repeat(b_s[l].T[:ts], SG_CH, axis=1)
        eye = jnp.eye(ts // lt, dtype=F32)
        ws_s = jnp.stack([jnp.kron(eye, jnp.tril(w_s[l, g, :lt, :lt])) for g in range(SG_GROUPS)]).astype(BF16)
        bias_s = jnp.tile(jnp.repeat(b_s[l, :, :lt].T, SG_CH, axis=1), (ts // lt, 1))
        ob, vn_s = _sgu(rt_sgu, proj, sg_ln_g[l].reshape(1, SG_W), sg_ln_b[l].reshape(1, SG_W),
                        ws_p, ws_s, bias_p, bias_s, col['u'], col['v'])

        pa, pb, wo = p_a[l].astype(BF16), p_b[l].astype(BF16), w_out[l].astype(BF16)
        rws = _split(jnp.pad(router_w[l], ((0, 0), (0, LANES - n_experts))))
        rw = jnp.concatenate([rws[0], rws[0], rws[1]], axis=0)
        rb = jnp.pad(router_b[l].reshape(1, n_experts), ((0, 0), (0, LANES - n_experts)))
        x1, h2, ti, tg = _merge(alpha, n_experts, rt_merge, oa_p, oa_s, ob, proj, xp, xs,
                                vec(2, rt_merge), vec(4, rt_merge), vec(3, rt_merge),
                                pa, pb, wo, ln1_g[l].reshape(1, d), ln1_b[l].reshape(1, d), rw, rb,
                                col['ga'], col['gb'])

        nj = w_dn.shape[2] // MOE_TF
        dest, pad_lo, pad_len, nblk_used, nb_max, items = _routing(ti[:, :TOP_K], n_experts, MOE_TM, nj, MOE_SUBS)
        xb = _dispatch(pad_lo, pad_len, nblk_used, dest, h2, MOE_TM, nb_max)
        yb = _moe(items, xb, w_gu[l], b_gu[l], w_dn[l], b_dn[l], MOE_TM, MOE_TF, MOE_SUBS)
        xp, xs = _combine(alpha, rt_comb, dest, yb, tg, x1, vec(5, rt_comb),
                          ln2_g[l].reshape(1, d), ln2_b[l].reshape(1, d))

        outs['conv_p'].append(conv_p)
        outs['delta_p'].append(delta_p)
        outs['conv_s'].append(conv_s)
        outs['delta_s'].append(delta_s)
        outs['vrows'].append(vn_s.reshape(bs, lt, SG_W))

    return (xp.reshape(bp, seq, d), xs.reshape(bs, lt, d),
            jnp.stack(outs['conv_p']), jnp.stack(outs['delta_p']),
            jnp.stack(outs['conv_s']), jnp.stack(outs['delta_s']), jnp.stack(outs['vrows']))
```

```python
import functools
import math

import jax
import jax.numpy as jnp
from jax import lax
from jax.experimental import pallas as pl
from jax.experimental.pallas import tpu as pltpu

F32 = jnp.float32
BF16 = jnp.bfloat16
HIGHEST = lax.Precision.HIGHEST

DN_HEADS = 8
DN_DK = 128
DN_DV = 128
DN_CONV = 4
DN_CHUNK = 64
SG_GROUPS = 8
SG_CH = 128
SG_CHUNK = 128
TOP_K = 4
SWIGLU_LIMIT = 7.0
SWIGLU_ALPHA = 1.702
LN_EPS = 1e-5
NORM_EPS = 1e-6
QK_W = DN_HEADS * DN_DK
V_W = DN_HEADS * DN_DV
QKV_W = 2 * QK_W + V_W
SG_W = SG_GROUPS * SG_CH

LANES = 128
SUBLANES = 8
VMEM_LIMIT = 56 * 1024 * 1024

PROJ_TM = 512
PROJ_TN = 1536
MERGE_TM = 256
MOE_TM = 256
MOE_TF = 512
MOE_SUBS = 6
COMB_TQ = 128
SAMPLE_BT = 4
SAMPLE_TP = 8
GDN_HG = 4


def _sigmoid(x):
    return 1.0 / (1.0 + jnp.exp(-x))


def _silu(x):
    return x * _sigmoid(x)


def _dot(a, b):
    return jnp.dot(a.astype(BF16), b.astype(BF16), preferred_element_type=F32)


def _dot_nt(a, b):
    return lax.dot_general(a.astype(BF16), b.astype(BF16), (((1,), (1,)), ((), ())),
                           preferred_element_type=F32)


def _dot_hi(a, b):
    return jnp.dot(a, b, precision=HIGHEST, preferred_element_type=F32)


def _layer_norm(x, g, b):
    mu = jnp.mean(x, axis=-1, keepdims=True)
    xc = x - mu
    var = jnp.mean(xc * xc, axis=-1, keepdims=True)
    return xc * lax.rsqrt(var + LN_EPS) * g + b


def _params(*sem):
    return pltpu.CompilerParams(dimension_semantics=sem, vmem_limit_bytes=VMEM_LIMIT)


class _Rows:
    def __init__(self, tm, rows_p, rows_s, seq):
        assert rows_p % tm == 0 and rows_s % tm == 0 and seq % tm == 0, (tm, rows_p, rows_s, seq)
        self.tm, self.n_p, self.n_s = tm, rows_p // tm, rows_s // tm
        self.tiles_per_seq = seq // tm
        self.bp = rows_p // seq

    @property
    def n(self):
        return self.n_p + self.n_s

    def prompt(self, width, col=0):
        return pl.BlockSpec((self.tm, width), lambda i, *_: (jnp.minimum(i, self.n_p - 1), col))

    def sample(self, width, col=0, single=True):
        mode = dict(pipeline_mode=pl.Buffered(1)) if single else {}
        return pl.BlockSpec((self.tm, width), lambda i, *_: (jnp.maximum(i - self.n_p, 0), col), **mode)

    def joint(self, width, col=0):
        return pl.BlockSpec((self.tm, width), lambda i, *_: (i, col))

    def seq_vec(self, d):
        return pl.BlockSpec((None, 1, d), lambda i, *_: (jnp.minimum(i // self.tiles_per_seq, self.bp - 1), 0, 0))

    def row_vec(self, d):
        return pl.BlockSpec((None, self.tm, d), lambda i, *_: (jnp.maximum(i - self.n_p, 0), 0, 0),
                            pipeline_mode=pl.Buffered(1))


def _pick(is_sample, prompt_ref, sample_ref):
    return jnp.where(is_sample, sample_ref[...], prompt_ref[...])


def _adaln_kernel(c_ref, w_ref, b_ref, o_ref):
    o_ref[...] = _dot(_silu(c_ref[...]), w_ref[...]) + b_ref[...]


def _adaln(c, w, b):
    rows, d = c.shape
    n = w.shape[1]
    tn = 1024
    return pl.pallas_call(
        _adaln_kernel,
        grid=(n // tn,),
        in_specs=[pl.BlockSpec((rows, d), lambda j: (0, 0)),
                  pl.BlockSpec((d, tn), lambda j: (0, j)),
                  pl.BlockSpec((1, tn), lambda j: (0, j))],
        out_specs=pl.BlockSpec((rows, tn), lambda j: (0, j)),
        out_shape=jax.ShapeDtypeStruct((rows, n), F32),
        compiler_params=_params("arbitrary"),
        name="adaln",
    )(c, w, b.reshape(1, n))


def _inproj_kernel(n_p, xp_ref, xs_ref, scp_ref, scs_ref, shp_ref, shs_ref, w_ref, o_ref, h_scr):
    @pl.when(pl.program_id(1) == 0)
    def _():
        is_s = pl.program_id(0) >= n_p
        x = _pick(is_s, xp_ref, xs_ref)
        h_scr[...] = (x * (1.0 + _pick(is_s, scp_ref, scs_ref)) + _pick(is_s, shp_ref, shs_ref)).astype(BF16)

    o_ref[...] = jnp.dot(h_scr[...], w_ref[...], preferred_element_type=F32)


def _inproj(rt, xp, xs, sc, sh, w):
    d = xp.shape[1]
    nw = w.shape[1]
    tm = rt.tm
    return pl.pallas_call(
        functools.partial(_inproj_kernel, rt.n_p),
        grid=(rt.n, nw // PROJ_TN),
        in_specs=[rt.prompt(d), rt.sample(d), rt.seq_vec(d), rt.row_vec(d), rt.seq_vec(d), rt.row_vec(d),
                  pl.BlockSpec((d, PROJ_TN), lambda i, j: (0, j))],
        out_specs=pl.BlockSpec((tm, PROJ_TN), lambda i, j: (i, j)),
        out_shape=jax.ShapeDtypeStruct((rt.n * tm, nw), F32),
        scratch_shapes=[pltpu.VMEM((tm, d), BF16)],
        compiler_params=_params("arbitrary", "arbitrary"),
        name="inproj",
    )(xp, xs, sc[0], sc[1], sh[0], sh[1], w)


def _softplus(x):
    return jnp.maximum(x, 0.0) + jnp.log1p(jnp.exp(-jnp.abs(x)))


def _split(a):
    hi = a.astype(BF16)
    lo = (a - hi.astype(F32)).astype(BF16)
    return hi, lo


def _dot3(a, b):
    lhs = jnp.concatenate([a[0], a[1], a[0]], axis=1)
    rhs = jnp.concatenate([b[0], b[0], b[1]], axis=0)
    return jnp.dot(lhs, rhs, preferred_element_type=F32)


def _map(f, *lists):
    return [f(*args) for args in zip(*lists)]


def _unit_lower_inverses_minus_eye(mats, rows, cols):
    same = (rows // 16) == (cols // 16)
    n = [jnp.where(same, a, 0.0) for a in mats]
    b = _map(lambda a, x: a - x, mats, n)
    n2 = _map(lambda x: _dot(x, x), n)
    n4 = _map(lambda x: _dot(x, x), n2)
    n8 = _map(lambda x: _dot(x, x), n4)
    r = [-x for x in n]
    r = _map(lambda x, p: x + p + _dot(x, p), r, n2)
    r = _map(lambda x, p: x + p + _dot(x, p), r, n4)
    dm = _map(lambda x, p: x + p + _dot(x, p), r, n8)
    m = _map(lambda x, y: y + _dot(x, y), dm, b)
    m2 = _map(lambda x: _dot(x, x), m)
    xm = _map(lambda x, p: x + p + _dot(p, x), dm, m2)
    return _map(lambda x, p: x - p - _dot(p, x), xm, m)


def _gdn_prompt_kernel(qkv_ref, z_ref, ab_ref, hist_ref, s0_ref, wconv_ref, alog_ref, dt_ref, og_ref,
                       o_ref, conv_ref, s_ref, xp_scr):
    n = pl.program_id(1)
    c = DN_CHUNK
    pad = SUBLANES
    nr = GDN_HG * c
    ngrp = DN_HEADS // GDN_HG

    @pl.when(n == 0)
    def _():
        xp_scr[pl.ds(0, pad), :] = jnp.zeros((pad, QKV_W), F32)
        xp_scr[pl.ds(pad - (DN_CONV - 1), DN_CONV - 1), :] = hist_ref[...]
        s_ref[...] = s0_ref[...]

    x = qkv_ref[...]
    xp_scr[pl.ds(pad, c), :] = x
    y = jnp.zeros((c, QKV_W), F32)
    for j in range(DN_CONV):
        y = y + xp_scr[pl.ds(pad - (DN_CONV - 1) + j, c), :] * wconv_ref[pl.ds(j, 1), :]
    y = _silu(y)
    tail = xp_scr[pl.ds(c + pad - (DN_CONV - 1), DN_CONV - 1), :]
    conv_ref[...] = tail
    xp_scr[pl.ds(pad - (DN_CONV - 1), DN_CONV - 1), :] = tail

    ab = ab_ref[...]
    g = -jnp.exp(alog_ref[...]) * _softplus(ab + dt_ref[...])
    beta_all = _sigmoid(ab)
    r64 = lax.broadcasted_iota(jnp.int32, (c, c), 0)
    c64 = lax.broadcasted_iota(jnp.int32, (c, c), 1)
    gc = _dot_hi((r64 >= c64).astype(F32), g)
    z = z_ref[...]
    og = og_ref[...]

    rows = lax.broadcasted_iota(jnp.int32, (nr, nr), 0)
    cols = lax.broadcasted_iota(jnp.int32, (nr, nr), 1)
    same = (rows // c) == (cols // c)
    causal = same & (rows >= cols)
    strict = same & (rows > cols)
    rowhead = lax.broadcasted_iota(jnp.int32, (nr, 1), 0) // c
    groups = [range(grp * GDN_HG, (grp + 1) * GDN_HG) for grp in range(ngrp)]

    def stack(heads, off, width):
        return jnp.concatenate([y[:, off + h * width:off + (h + 1) * width] for h in heads], axis=0)

    def l2n(t):
        return t * lax.rsqrt(jnp.sum(t * t, axis=-1, keepdims=True) + NORM_EPS)

    q = [l2n(stack(hs, 0, DN_DK)) * (DN_DK ** -0.5) for hs in groups]
    k = [l2n(stack(hs, QK_W, DN_DK)) for hs in groups]
    v = [stack(hs, 2 * QK_W, DN_DV) for hs in groups]
    beta = [jnp.concatenate([beta_all[:, DN_HEADS + h:DN_HEADS + h + 1] for h in hs], axis=0) for hs in groups]
    gcf = [jnp.concatenate([jnp.broadcast_to(gc[:, h:h + 1], (c, LANES)) for h in hs], axis=0) for hs in groups]
    gcc = [t[:, 0:1] for t in gcf]
    grow = [t.T[0:1, :] for t in gcf]
    glast = [[gc[c - 1:c, h:h + 1] for h in hs] for hs in groups]
    gtot = [jnp.concatenate([jnp.broadcast_to(t, (c, 1)) for t in gl], axis=0) for gl in glast]
    decay = _map(lambda a, b: jnp.where(causal, jnp.exp(jnp.where(causal, a - b, 0.0)), 0.0), gcc, grow)
    egc = _map(jnp.exp, gcc)
    kb = _map(lambda a, b: a * b, k, beta)
    a_mat = _map(lambda a, b, dd: jnp.where(strict, _dot_nt(a, b) * dd, 0.0), kb, k, decay)
    tm1 = _unit_lower_inverses_minus_eye(a_mat, rows, cols)
    u = _map(lambda t, a, b: a * b + _dot(t, a * b), tm1, v, beta)
    w = _map(lambda t, a, b: a * b + _dot(t, a * b), tm1, kb, egc)
    attn = _map(lambda a, b, dd: _dot_nt(a, b) * dd, q, k, decay)
    qg = _map(lambda a, b: a * b, q, egc)
    kd_t = _map(lambda a, b, cc: (a * jnp.exp(b - cc)).T, k, gtot, gcc)

    for gi, hs in enumerate(groups):
        v_news, qss = [], []
        for hl, h in enumerate(hs):
            sl = slice(hl * c, (hl + 1) * c)
            rs = _dot(jnp.concatenate([w[gi][sl], qg[gi][sl]], axis=0), s_ref[h])
            v_news.append(u[gi][sl] - rs[:c])
            qss.append(rs[c:])
        v_new = jnp.concatenate(v_news, axis=0)
        o = jnp.concatenate(qss, axis=0) + _dot(attn[gi], v_new)
        for hl, h in enumerate(hs):
            upd = _dot(kd_t[gi], jnp.where(rowhead == hl, v_new, 0.0))
            s_ref[h] = s_ref[h] * jnp.exp(glast[gi][hl]) + upd
        o = o * lax.rsqrt(jnp.mean(o * o, axis=-1, keepdims=True) + NORM_EPS) * og
        for hl, h in enumerate(hs):
            o_ref[:, h * DN_DV:(h + 1) * DN_DV] = o[hl * c:(hl + 1) * c] * _silu(z[:, h * DN_DV:(h + 1) * DN_DV])


def _gdn_prompt(proj, hist, s0, wconv, alog, dtb, og, batch, seq, col_z, col_ab):
    c = DN_CHUNK
    nchunk = seq // c
    rows = batch * seq
    row = lambda b, n: b * nchunk + n
    return pl.pallas_call(
        _gdn_prompt_kernel,
        grid=(batch, nchunk),
        in_specs=[pl.BlockSpec((c, QKV_W), lambda b, n: (row(b, n), 0)),
                  pl.BlockSpec((c, V_W), lambda b, n: (row(b, n), col_z // V_W)),
                  pl.BlockSpec((c, LANES), lambda b, n: (row(b, n), col_ab // LANES)),
                  pl.BlockSpec((None, DN_CONV - 1, QKV_W), lambda b, n: (b, 0, 0)),
                  pl.BlockSpec((None, DN_HEADS, DN_DK, DN_DV), lambda b, n: (b, 0, 0, 0)),
                  pl.BlockSpec((DN_CONV, QKV_W), lambda b, n: (0, 0)),
                  pl.BlockSpec((1, LANES), lambda b, n: (0, 0)),
                  pl.BlockSpec((1, LANES), lambda b, n: (0, 0)),
                  pl.BlockSpec((1, DN_DV), lambda b, n: (0, 0))],
        out_specs=[pl.BlockSpec((c, V_W), lambda b, n: (row(b, n), 0)),
                   pl.BlockSpec((None, DN_CONV - 1, QKV_W), lambda b, n: (b, 0, 0)),
                   pl.BlockSpec((None, DN_HEADS, DN_DK, DN_DV), lambda b, n: (b, 0, 0, 0))],
        out_shape=[jax.ShapeDtypeStruct((rows, V_W), F32),
                   jax.ShapeDtypeStruct((batch, DN_CONV - 1, QKV_W), F32),
                   jax.ShapeDtypeStruct((batch, DN_HEADS, DN_DK, DN_DV), F32)],
        scratch_shapes=[pltpu.VMEM((c + SUBLANES, QKV_W), F32)],
        compiler_params=_params("arbitrary", "arbitrary"),
        name="gdn_prompt",
    )(proj, proj, proj, hist, s0, wconv, alog, dtb, og)


def _gdn_sample_kernel(qkv_ref, z_ref, ab_ref, hist_ref, s0_ref, wconv_ref, alog_ref, dt_ref, og_ref,
                       o_ref, conv_ref, s_ref, xp_scr, ab_scr, wq_scr, r_scr, kdt_scr, vn_scr, gl_scr):
    bt, tp, lt = SAMPLE_BT, SAMPLE_TP, DN_CONV
    nblk = DN_HEADS * bt
    nrow = nblk * tp
    hist_rows = DN_CONV - 1

    xp_scr[...] = jnp.zeros(xp_scr.shape, F32)
    ab_scr[...] = jnp.zeros(ab_scr.shape, F32)
    ys = []
    for b in range(bt):
        xp_scr[b, pl.ds(SUBLANES - hist_rows, hist_rows), :] = hist_ref[b]
        xp_scr[b, pl.ds(SUBLANES, lt), :] = qkv_ref[pl.ds(b * lt, lt), :]
        yb = jnp.zeros((tp, QKV_W), F32)
        for j in range(DN_CONV):
            yb = yb + xp_scr[b, pl.ds(SUBLANES - hist_rows + j, tp), :] * wconv_ref[pl.ds(j, 1), :]
        ys.append(_silu(yb))
        conv_ref[b] = xp_scr[b, pl.ds(SUBLANES + lt - hist_rows, hist_rows), :]
        ab_scr[b, pl.ds(0, lt), :] = ab_ref[pl.ds(b * lt, lt), :]
    y = jnp.concatenate(ys, axis=0)
    ab = jnp.concatenate([ab_scr[b] for b in range(bt)], axis=0)
    tok = lax.broadcasted_iota(jnp.int32, (bt * tp, 1), 0) % tp
    real = tok < lt
    g_all = jnp.where(real, -jnp.exp(alog_ref[...]) * _softplus(ab + dt_ref[...]), 0.0)
    beta_all = jnp.where(real, _sigmoid(ab), 0.0)

    def heads_to_rows(t, off, width):
        return jnp.concatenate([t[:, off + h * width:off + (h + 1) * width] for h in range(DN_HEADS)], axis=0)

    realr = jnp.concatenate([real] * DN_HEADS, axis=0)
    q = jnp.where(realr, heads_to_rows(y, 0, DN_DK), 0.0)
    k = jnp.where(realr, heads_to_rows(y, QK_W, DN_DK), 0.0)
    v = jnp.where(realr, heads_to_rows(y, 2 * QK_W, DN_DV), 0.0)
    gcol = jnp.concatenate([g_all[:, h:h + 1] for h in range(DN_HEADS)], axis=0)
    beta = jnp.concatenate([beta_all[:, DN_HEADS + h:DN_HEADS + h + 1] for h in range(DN_HEADS)], axis=0)
    q = q * lax.rsqrt(jnp.sum(q * q, axis=-1, keepdims=True) + NORM_EPS) * (DN_DK ** -0.5)
    k = k * lax.rsqrt(jnp.sum(k * k, axis=-1, keepdims=True) + NORM_EPS)

    rows = lax.broadcasted_iota(jnp.int32, (nrow, nrow), 0)
    cols = lax.broadcasted_iota(jnp.int32, (nrow, nrow), 1)
    same = (rows // tp) == (cols // tp)
    causal = same & (rows >= cols)
    strict = same & (rows > cols)
    gfull = jnp.broadcast_to(gcol, (nrow, LANES))
    gc = _dot_hi(causal.astype(F32), gfull)
    gtot = _dot_hi(same.astype(F32), gfull)
    gcc = gc[:, 0:1]
    grow = gc.T[0:1, :]
    decay = jnp.where(causal, jnp.exp(jnp.where(causal, gcc - grow, 0.0)), 0.0)
    egc = jnp.exp(gcc)
    kb = k * beta
    a_mat = jnp.where(strict, _dot_nt(kb, k) * decay, 0.0)
    a2 = _dot(a_mat, a_mat)
    tm1 = a2 - a_mat - _dot(a_mat, a2)
    u = v * beta + _dot(tm1, v * beta)
    w = kb * egc + _dot(tm1, kb * egc)
    attn = _dot_nt(q, k) * decay
    qg = q * egc
    for i in range(nblk):
        wq_scr[pl.ds(2 * tp * i, tp), :] = w[i * tp:(i + 1) * tp, :]
        wq_scr[pl.ds(2 * tp * i + tp, tp), :] = qg[i * tp:(i + 1) * tp, :]
    kdt_scr[...] = (k * jnp.exp(gtot[:, 0:1] - gcc)).T
    gl_scr[...] = jnp.exp(gtot)

    def read_state(i, carry):
        r0 = pl.multiple_of(i * 2 * tp, 2 * tp)
        r_scr[pl.ds(r0, 2 * tp), :] = _dot(wq_scr[pl.ds(r0, 2 * tp), :], s0_ref[i % bt, i // bt])
        return carry

    lax.fori_loop(0, nblk, read_state, 0, unroll=4)
    ws = jnp.concatenate([r_scr[pl.ds(2 * tp * i, tp), :] for i in range(nblk)], axis=0)
    qs = jnp.concatenate([r_scr[pl.ds(2 * tp * i + tp, tp), :] for i in range(nblk)], axis=0)
    v_new = u - ws
    vn_scr[...] = v_new
    o = qs + _dot(attn, v_new)
    blockid = lax.broadcasted_iota(jnp.int32, (nrow, 1), 0) // tp

    def write_state(i, carry):
        r0 = pl.multiple_of(i * tp, tp)
        upd = _dot(kdt_scr[...], jnp.where(blockid == i, vn_scr[...], 0.0))
        s_ref[i % bt, i // bt] = s0_ref[i % bt, i // bt] * gl_scr[pl.ds(r0, 1), :] + upd
        return carry

    lax.fori_loop(0, nblk, write_state, 0, unroll=4)

    og = og_ref[...]
    o = o * lax.rsqrt(jnp.mean(o * o, axis=-1, keepdims=True) + NORM_EPS) * og
    for b in range(bt):
        for h in range(DN_HEADS):
            zbh = z_ref[pl.ds(b * lt, lt), h * DN_DV:(h + 1) * DN_DV]
            blk = o[(h * bt + b) * tp:(h * bt + b) * tp + lt, :]
            o_ref[pl.ds(b * lt, lt), h * DN_DV:(h + 1) * DN_DV] = blk * _silu(zbh)


def _gdn_sample(proj, row0, hist, s0, wconv, alog, dtb, og, batch, col_z, col_ab):
    bt, tp, lt = SAMPLE_BT, SAMPLE_TP, DN_CONV
    rows = batch * lt
    nrow = DN_HEADS * bt * tp
    blk = bt * lt
    assert row0 % blk == 0 and batch % bt == 0
    b0 = row0 // blk
    return pl.pallas_call(
        _gdn_sample_kernel,
        grid=(batch // bt,),
        in_specs=[pl.BlockSpec((blk, QKV_W), lambda i: (b0 + i, 0)),
                  pl.BlockSpec((blk, V_W), lambda i: (b0 + i, col_z // V_W)),
                  pl.BlockSpec((blk, LANES), lambda i: (b0 + i, col_ab // LANES)),
                  pl.BlockSpec((bt, DN_CONV - 1, QKV_W), lambda i: (i, 0, 0)),
                  pl.BlockSpec((bt, DN_HEADS, DN_DK, DN_DV), lambda i: (i, 0, 0, 0)),
                  pl.BlockSpec((DN_CONV, QKV_W), lambda i: (0, 0)),
                  pl.BlockSpec((1, LANES), lambda i: (0, 0)),
                  pl.BlockSpec((1, LANES), lambda i: (0, 0)),
                  pl.BlockSpec((1, DN_DV), lambda i: (0, 0))],
        out_specs=[pl.BlockSpec((blk, V_W), lambda i: (i, 0)),
                   pl.BlockSpec((bt, DN_CONV - 1, QKV_W), lambda i: (i, 0, 0)),
                   pl.BlockSpec((bt, DN_HEADS, DN_DK, DN_DV), lambda i: (i, 0, 0, 0))],
        out_shape=[jax.ShapeDtypeStruct((rows, V_W), F32),
                   jax.ShapeDtypeStruct((batch, DN_CONV - 1, QKV_W), F32),
                   jax.ShapeDtypeStruct((batch, DN_HEADS, DN_DK, DN_DV), F32)],
        scratch_shapes=[pltpu.VMEM((bt, SUBLANES + tp, QKV_W), F32),
                        pltpu.VMEM((bt, tp, LANES), F32),
                        pltpu.VMEM((2 * nrow, DN_DK), F32),
                        pltpu.VMEM((2 * nrow, DN_DV), F32),
                        pltpu.VMEM((DN_DK, nrow), F32),
                        pltpu.VMEM((nrow, DN_DV), F32),
                        pltpu.VMEM((nrow, LANES), F32)],
        compiler_params=_params("arbitrary"),
        name="gdn_sample",
    )(proj, proj, proj, hist, s0, wconv, alog, dtb, og)


def _sgu_kernel(n_p, u_ref, v_ref, g_ref, b_ref, wsp_ref, wss_ref, bp_ref, bs_ref, o_ref, vn_ref):
    is_s = pl.program_id(0) >= n_p
    u = jax.nn.gelu(u_ref[...])
    vn = _layer_norm(jax.nn.gelu(v_ref[...]), g_ref[...], b_ref[...])
    vn_ref[...] = vn
    bias = _pick(is_s, bp_ref, bs_ref)
    for g in range(SG_GROUPS):
        sl = slice(g * SG_CH, (g + 1) * SG_CH)
        ws = jnp.where(is_s, wss_ref[g], wsp_ref[g])
        mixed = _dot(ws, vn[:, sl]) + bias[:, sl]
        o_ref[:, sl] = u[:, sl] * mixed


def _sgu(rt, proj, ln_g, ln_b, ws_p, ws_s, bias_p, bias_s, col_u, col_v):
    t = rt.tm
    const = lambda shape: pl.BlockSpec(shape, lambda i: (0,) * len(shape))
    return pl.pallas_call(
        functools.partial(_sgu_kernel, rt.n_p),
        grid=(rt.n,),
        in_specs=[rt.joint(SG_W, col_u // SG_W), rt.joint(SG_W, col_v // SG_W),
                  const((1, SG_W)), const((1, SG_W)),
                  const((SG_GROUPS, t, t)), const((SG_GROUPS, t, t)), const((t, SG_W)), const((t, SG_W))],
        out_specs=[rt.joint(SG_W), rt.sample(SG_W, single=False)],
        out_shape=[jax.ShapeDtypeStruct((rt.n * t, SG_W), F32),
                   jax.ShapeDtypeStruct((rt.n_s * t, SG_W), F32)],
        compiler_params=_params("arbitrary"),
        name="sgu",
    )(proj, proj, ln_g, ln_b, ws_p, ws_s, bias_p, bias_s)


def _merge_kernel(alpha, n_experts, n_p, oap_ref, oas_ref, ob_ref, ga_ref, gb_ref, xp_ref, xs_ref,
                  gtp_ref, gts_ref, scp_ref, scs_ref, shp_ref, shs_ref,
                  pa_ref, pb_ref, wo_ref, lg_ref, lb_ref, rw_ref, rb_ref,
                  x1_ref, h2_ref, ti_ref, tg_ref):
    is_s = pl.program_id(0) >= n_p
    oa = _pick(is_s, oap_ref, oas_ref)
    merged = (_sigmoid(ga_ref[...]) * _dot(oa, pa_ref[...])
              + _sigmoid(gb_ref[...]) * _dot(ob_ref[...], pb_ref[...]))
    y = _dot(merged, wo_ref[...])
    x = _pick(is_s, xp_ref, xs_ref)
    x1 = _layer_norm(alpha * x + _pick(is_s, gtp_ref, gts_ref) * y, lg_ref[...], lb_ref[...])
    x1_ref[...] = x1
    h2 = x1 * (1.0 + _pick(is_s, scp_ref, scs_ref)) + _pick(is_s, shp_ref, shs_ref)
    h2_ref[...] = h2
    hs = _split(h2)
    logits = jnp.dot(jnp.concatenate([hs[0], hs[1], hs[0]], axis=1), rw_ref[...],
                     preferred_element_type=F32) + rb_ref[...]
    lane = lax.broadcasted_iota(jnp.int32, logits.shape, 1)
    logits = jnp.where(lane < n_experts, logits, -jnp.inf)
    ti = jnp.zeros(logits.shape, jnp.int32)
    tv = jnp.zeros(logits.shape, F32)
    top = None
    for kk in range(TOP_K):
        m = jnp.max(logits, axis=-1, keepdims=True)
        idx = jnp.min(jnp.where(logits == m, lane.astype(F32), float(LANES)), axis=-1,
                      keepdims=True).astype(jnp.int32)
        if kk == 0:
            top = m
        ti = jnp.where(lane == kk, idx, ti)
        tv = jnp.where(lane == kk, jnp.exp(m - top), tv)
        logits = jnp.where(lane == idx, -jnp.inf, logits)
    ti_ref[...] = ti
    tg_ref[...] = tv / jnp.sum(tv, axis=-1, keepdims=True)


def _merge(alpha, n_experts, rt, oa_p, oa_s, ob, proj, xp, xs, gt, sc, sh, pa, pb, wo, lg, lb, rw, rb,
           col_ga, col_gb):
    d = xp.shape[1]
    tm = rt.tm
    rows = rt.n * tm
    const = lambda shape: pl.BlockSpec(shape, lambda i: (0,) * len(shape), pipeline_mode=pl.Buffered(1))
    return pl.pallas_call(
        functools.partial(_merge_kernel, alpha, n_experts, rt.n_p),
        grid=(rt.n,),
        in_specs=[rt.prompt(V_W), rt.sample(V_W), rt.joint(SG_W),
                  rt.joint(d, col_ga // d), rt.joint(d, col_gb // d),
                  rt.prompt(d), rt.sample(d),
                  rt.seq_vec(d), rt.row_vec(d), rt.seq_vec(d), rt.row_vec(d), rt.seq_vec(d), rt.row_vec(d),
                  const((V_W, d)), const((SG_W, d)), const((d, d)),
                  const((1, d)), const((1, d)), const((3 * d, LANES)), const((1, LANES))],
        out_specs=[rt.joint(d), rt.joint(d), rt.joint(LANES), rt.joint(LANES)],
        out_shape=[jax.ShapeDtypeStruct((rows, d), F32),
                   jax.ShapeDtypeStruct((rows, d), F32),
                   jax.ShapeDtypeStruct((rows, LANES), jnp.int32),
                   jax.ShapeDtypeStruct((rows, LANES), F32)],
        compiler_params=_params("arbitrary"),
        name="merge",
    )(oa_p, oa_s, ob, proj, proj, xp, xs, gt[0], gt[1], sc[0], sc[1], sh[0], sh[1],
      pa, pb, wo, lg, lb, rw, rb)


FILL_ROWS = (128, 64, 32, 16, 8)


def _dispatch_kernel(tm, n_experts, padlo_ref, padlen_ref, nblk_ref, dest_ref, h_ref, xb_hbm, zero_scr, sem, zsem):
    i = pl.program_id(0)
    tq = h_ref.shape[0]
    nb_max = xb_hbm.shape[0] // tm
    zrows = zero_scr.shape[0]

    def fill(act):
        def per_expert(e, carry):
            lo = padlo_ref[e]
            ln = padlen_ref[e]
            head = (SUBLANES - lo % SUBLANES) % SUBLANES
            for r in range(SUBLANES - 1):
                @pl.when(r < head)
                def _(r=r):
                    act(pltpu.make_async_copy(zero_scr.at[pl.ds(0, 1)], xb_hbm.at[pl.ds(lo + r, 1)], zsem))
            off = lo + head
            rem = ln - head
            for b in FILL_ROWS:
                @pl.when((rem & b) != 0)
                def _(off=off, b=b):
                    dst = xb_hbm.at[pl.ds(pl.multiple_of(off, SUBLANES), b)]
                    act(pltpu.make_async_copy(zero_scr.at[pl.ds(0, b)], dst, zsem))
                off = off + (rem & b)
            return carry

        lax.fori_loop(0, n_experts, per_expert, 0)

        def per_block(blk, carry):
            for part in range(tm // zrows):
                row0 = pl.multiple_of(blk * tm + part * zrows, zrows)
                act(pltpu.make_async_copy(zero_scr, xb_hbm.at[pl.ds(row0, zrows)], zsem))
            return carry

        lax.fori_loop(nblk_ref[0], nb_max, per_block, 0)

    @pl.when(i == 0)
    def _():
        zero_scr[...] = jnp.zeros(zero_scr.shape, F32)
        fill(lambda cp: cp.start())
        fill(lambda cp: cp.wait())

    def start(t, carry):
        for kk in range(TOP_K):
            pltpu.make_async_copy(h_ref.at[pl.ds(t, 1)], xb_hbm.at[pl.ds(dest_ref[0, t * TOP_K + kk], 1)],
                                  sem).start()
        return carry

    lax.fori_loop(0, tq, start, 0, unroll=4)
    for kk in range(TOP_K):
        pltpu.make_async_copy(h_ref, xb_hbm.at[pl.ds(0, tq)], sem).wait()


def _dispatch(padlo, padlen, nblk_used, dest, h2, tm, nb_max):
    t, d = h2.shape
    tq = math.gcd(256, t)
    n_experts = padlo.shape[0]
    grid_spec = pltpu.PrefetchScalarGridSpec(
        num_scalar_prefetch=3,
        grid=(t // tq,),
        in_specs=[pl.BlockSpec((None, 1, tq * TOP_K), lambda i, *_: (i, 0, 0), memory_space=pltpu.SMEM),
                  pl.BlockSpec((tq, d), lambda i, *_: (i, 0))],
        out_specs=pl.BlockSpec(memory_space=pl.ANY),
        scratch_shapes=[pltpu.VMEM((FILL_ROWS[0], d), F32), pltpu.SemaphoreType.DMA(()),
                        pltpu.SemaphoreType.DMA(())],
    )
    return pl.pallas_call(
        functools.partial(_dispatch_kernel, tm, n_experts),
        grid_spec=grid_spec,
        out_shape=jax.ShapeDtypeStruct((nb_max * tm, d), F32),
        compiler_params=pltpu.CompilerParams(dimension_semantics=("arbitrary",), vmem_limit_bytes=VMEM_LIMIT,
                                             has_side_effects=True),
        name="dispatch",
    )(padlo, padlen, nblk_used.reshape(1), dest.reshape(t // tq, 1, tq * TOP_K), h2)


def _moe_kernel(nj, e_ref, j_ref, xb_ref, ob_ref, r_ref, flag_ref,
                x_ref, wg_ref, wu_ref, wd_ref, bg_ref, bu_ref, bd_ref, o_ref,
                wg_scr, wu_scr, wd_scr, acc_scr):
    s = pl.program_id(0)
    flags = flag_ref[s]
    j = j_ref[s]
    r = r_ref[s]

    @pl.when((flags & 2) != 0)
    def _():
        wg_scr[...] = wg_ref[...].astype(BF16)
        wu_scr[...] = wu_ref[...].astype(BF16)
        wd_scr[...] = wd_ref[...].astype(BF16)

    @pl.when(flags == 0)
    def _():
        o_ref[...] = jnp.zeros(o_ref.shape, F32)

    @pl.when((flags & 1) != 0)
    def _():
        x = x_ref[...].astype(BF16)
        gate = jnp.dot(x, wg_scr[...], preferred_element_type=F32) + bg_ref[...]
        up = jnp.dot(x, wu_scr[...], preferred_element_type=F32) + bu_ref[...]
        gate = jnp.minimum(gate, SWIGLU_LIMIT)
        up = jnp.clip(up, -SWIGLU_LIMIT, SWIGLU_LIMIT)
        act = (up + 1.0) * gate * _sigmoid(SWIGLU_ALPHA * gate)
        y = jnp.dot(act.astype(BF16), wd_scr[...], preferred_element_type=F32)

        @pl.when(j == 0)
        def _():
            acc_scr[r] = y

        @pl.when(jnp.logical_and(j > 0, j < nj - 1))
        def _():
            acc_scr[r] = acc_scr[r] + y

        @pl.when(j == nj - 1)
        def _():
            o_ref[...] = acc_scr[r] + y + bd_ref[...]


def _moe(items, xb, w_gu, b_gu, w_dn, b_dn, tm, tf, subs):
    item_e, item_j, item_xb, item_ob, item_r, item_flag = items
    n_items = item_e.shape[0]
    nslot, d = xb.shape
    n_exp, _, f2 = w_gu.shape
    f = f2 // 2
    nj = f // tf
    assert nj >= 2
    grid_spec = pltpu.PrefetchScalarGridSpec(
        num_scalar_prefetch=6,
        grid=(n_items,),
        in_specs=[pl.BlockSpec((tm, d), lambda s, e, j, xbk, obk, r, fl: (xbk[s], 0)),
                  pl.BlockSpec((None, d, tf), lambda s, e, j, xbk, obk, r, fl: (e[s], 0, j[s])),
                  pl.BlockSpec((None, d, tf), lambda s, e, j, xbk, obk, r, fl: (e[s], 0, nj + j[s])),
                  pl.BlockSpec((None, tf, d), lambda s, e, j, xbk, obk, r, fl: (e[s], j[s], 0)),
                  pl.BlockSpec((None, 1, tf), lambda s, e, j, xbk, obk, r, fl: (e[s], 0, j[s])),
                  pl.BlockSpec((None, 1, tf), lambda s, e, j, xbk, obk, r, fl: (e[s], 0, nj + j[s])),
                  pl.BlockSpec((None, 1, d), lambda s, e, j, xbk, obk, r, fl: (e[s], 0, 0))],
        out_specs=pl.BlockSpec((tm, d), lambda s, e, j, xbk, obk, r, fl: (obk[s], 0)),
        scratch_shapes=[pltpu.VMEM((d, tf), BF16), pltpu.VMEM((d, tf), BF16), pltpu.VMEM((tf, d), BF16),
                        pltpu.VMEM((subs, tm, d), F32)],
    )
    return pl.pallas_call(
        functools.partial(_moe_kernel, nj),
        grid_spec=grid_spec,
        out_shape=jax.ShapeDtypeStruct((nslot, d), F32),
        compiler_params=_params("arbitrary"),
        name="moe",
    )(item_e, item_j, item_xb, item_ob, item_r, item_flag,
      xb, w_gu, w_gu, w_dn, b_gu.reshape(n_exp, 1, f2), b_gu.reshape(n_exp, 1, f2), b_dn.reshape(n_exp, 1, d))


def _combine_kernel(alpha, n_p, dest_ref, destn_ref, yb_hbm, tg_ref, x1_ref, gtp_ref, gts_ref, lg_ref, lb_ref,
                    op_ref, os_ref, buf, sem):
    i = pl.program_id(0)
    last = pl.num_programs(0) - 1
    tq = x1_ref.shape[0]
    slot = i % 2

    def start(ids_ref, sl, t):
        for kk in range(TOP_K):
            pltpu.make_async_copy(yb_hbm.at[pl.ds(ids_ref[0, t * TOP_K + kk], 1)],
                                  buf.at[sl, kk, pl.ds(t, 1)], sem.at[sl]).start()

    def wait(sl):
        for kk in range(TOP_K):
            pltpu.make_async_copy(yb_hbm.at[pl.ds(0, tq)], buf.at[sl, kk], sem.at[sl]).wait()

    @pl.when(i == 0)
    def _():
        lax.fori_loop(0, tq, lambda t, c: (start(dest_ref, 0, t), c)[1], 0, unroll=4)

    for t in range(tq):
        start(destn_ref, 1 - slot, t)
    wait(slot)
    tg = tg_ref[...]
    y = jnp.zeros(x1_ref.shape, F32)
    for kk in range(TOP_K):
        y = y + buf[slot, kk] * tg[:, kk:kk + 1]
    is_s = i >= n_p
    out = _layer_norm(alpha * x1_ref[...] + _pick(is_s, gtp_ref, gts_ref) * y, lg_ref[...], lb_ref[...])

    @pl.when(jnp.logical_not(is_s))
    def _():
        op_ref[...] = out

    @pl.when(is_s)
    def _():
        os_ref[...] = out

    @pl.when(i == last)
    def _():
        wait(1 - slot)


def _combine(alpha, rt, dest, yb, tg, x1, gt, lg, lb):
    d = x1.shape[1]
    tq = rt.tm
    ids = dest.reshape(rt.n, 1, tq * TOP_K)
    return pl.pallas_call(
        functools.partial(_combine_kernel, alpha, rt.n_p),
        grid=(rt.n,),
        in_specs=[pl.BlockSpec((None, 1, tq * TOP_K), lambda i: (i, 0, 0), memory_space=pltpu.SMEM),
                  pl.BlockSpec((None, 1, tq * TOP_K), lambda i: (jnp.minimum(i + 1, rt.n - 1), 0, 0),
                               memory_space=pltpu.SMEM),
                  pl.BlockSpec(memory_space=pl.ANY),
                  rt.joint(LANES), rt.joint(d), rt.seq_vec(d), rt.row_vec(d),
                  pl.BlockSpec((1, d), lambda i: (0, 0)),
                  pl.BlockSpec((1, d), lambda i: (0, 0))],
        out_specs=[rt.prompt(d), rt.sample(d, single=False)],
        out_shape=[jax.ShapeDtypeStruct((rt.n_p * tq, d), F32), jax.ShapeDtypeStruct((rt.n_s * tq, d), F32)],
        scratch_shapes=[pltpu.VMEM((2, TOP_K, tq, d), F32), pltpu.SemaphoreType.DMA((2,))],
        compiler_params=_params("arbitrary"),
        name="combine",
    )(ids, ids, yb, tg, x1, gt[0], gt[1], lg, lb)


def _routing(ti, n_experts, tm, nj, subs):
    t = ti.shape[0]
    n_assign = t * TOP_K
    nb_max = n_assign // tm + n_experts
    onehot = (ti[:, :, None] == jnp.arange(n_experts, dtype=jnp.int32)[None, None, :]).astype(jnp.int32)
    per_tok = jnp.sum(onehot, axis=1)
    cum = jnp.cumsum(per_tok, axis=0)
    counts = cum[-1]
    rank = jnp.take_along_axis(cum, ti, axis=1) - 1
    nblk = (counts + tm - 1) // tm
    blk_end = jnp.cumsum(nblk)
    blk_start = blk_end - nblk
    dest = blk_start[ti] * tm + rank
    pad_lo = (blk_start * tm + counts).astype(jnp.int32)
    pad_len = (nblk * tm - counts).astype(jnp.int32)
    total_blk = blk_end[-1]
    blocks = jnp.arange(nb_max, dtype=jnp.int32)
    blk_e = jnp.minimum(jnp.sum((blk_end[None, :] <= blocks[:, None]).astype(jnp.int32), axis=1), n_experts - 1)
    r_in_e = blocks - blk_start[blk_e]
    g0 = blk_start[blk_e] + (r_in_e // subs) * subs
    nsub = jnp.minimum(subs, nblk[blk_e] - (r_in_e // subs) * subs)
    p = jnp.arange(nb_max * nj, dtype=jnp.int32)
    bp = p // nj
    valid = bp < total_blk
    last = jnp.maximum(total_blk - 1, 0)
    bq = jnp.where(valid, bp, last)
    gq, nq, eq = g0[bq], jnp.maximum(nsub[bq], 1), blk_e[bq]
    local = p - nj * gq
    jq = jnp.where(valid, local // nq, nj - 1)
    rq = jnp.where(valid, local % nq, nq - 1)
    item_xb = gq + rq
    item_ob = jnp.where(valid, jnp.where(jq == nj - 1, gq + rq, gq), bp)
    flags = valid.astype(jnp.int32) + 2 * (valid & (rq == 0)).astype(jnp.int32)
    items = (eq.astype(jnp.int32), jq.astype(jnp.int32), item_xb.astype(jnp.int32),
             item_ob.astype(jnp.int32), rq.astype(jnp.int32), flags)
    return dest.astype(jnp.int32), pad_lo, pad_len, total_blk.astype(jnp.int32), nb_max, items


def _win_prep_kernel(w_ref, o_ref):
    o_a = QKV_W + V_W
    o_u = o_a + 2 * DN_HEADS
    x = w_ref[...]
    rest = x.shape[1] - o_u
    o_ref[:, 0:o_a] = x[:, 0:o_a].astype(BF16)
    o_ref[:, o_a:o_a + rest] = x[:, o_u:].astype(BF16)
    lane = lax.broadcasted_iota(jnp.int32, (x.shape[0], LANES), 1)
    ab = o_a + rest
    o_ref[:, ab:ab + LANES] = jnp.where(lane < 2 * DN_HEADS, x[:, o_a:o_a + LANES], 0.0).astype(BF16)
    if o_ref.shape[1] > ab + LANES:
        o_ref[:, ab + LANES:] = jnp.zeros((x.shape[0], o_ref.shape[1] - ab - LANES), BF16)


def _rearranged_w_in(w_in, d):
    dm, nw = w_in.shape
    tr = 128
    nw_out = -(-(nw - 2 * DN_HEADS + LANES) // PROJ_TN) * PROJ_TN
    w = pl.pallas_call(
        _win_prep_kernel,
        grid=(dm // tr,),
        in_specs=[pl.BlockSpec((tr, nw), lambda i: (i, 0))],
        out_specs=pl.BlockSpec((tr, nw_out), lambda i: (i, 0)),
        out_shape=jax.ShapeDtypeStruct((dm, nw_out), BF16),
        compiler_params=_params("arbitrary"),
        name="w_in_prep",
    )(w_in)
    cols = dict(z=QKV_W, u=QKV_W + V_W, v=QKV_W + V_W + SG_W, ga=QKV_W + V_W + 2 * SG_W,
                gb=QKV_W + V_W + 2 * SG_W + d, ab=QKV_W + V_W + 2 * SG_W + 2 * d)
    return w, cols


def kernel(x_prompt, x_sample, state_conv_qkv, state_delta, c_prompt, c_sample, w_ada, b_ada, w_in, w_conv, a_log, dt_bias, o_norm_g, sg_ln_g, sg_ln_b, w_s, b_s, p_a, p_b, w_out, ln1_g, ln1_b, router_w, router_b, w_gu, b_gu, w_dn, b_dn, ln2_g, ln2_b):
    depth = w_ada.shape[0]
    alpha = float((2 * depth) ** 0.25)
    bp, seq, d = x_prompt.shape
    bs, lt, _ = x_sample.shape
    assert lt == DN_CONV and seq % SG_CHUNK == 0
    n_experts = router_w.shape[2]
    rows_p, rows_s = bp * seq, bs * lt
    xp = x_prompt.reshape(rows_p, d)
    xs = x_sample.reshape(rows_s, d)
    c_all = jnp.concatenate([c_prompt, c_sample], axis=0)
    tile = lambda cap: _Rows(math.gcd(math.gcd(cap, rows_s), seq), rows_p, rows_s, seq)
    rt_proj, rt_sgu, rt_merge, rt_comb = tile(PROJ_TM), tile(SG_CHUNK), tile(MERGE_TM), tile(COMB_TQ)
    outs = dict(conv_p=[], delta_p=[], conv_s=[], delta_s=[], vrows=[])

    for l in range(depth):
        mod = _adaln(c_all, w_ada[l], b_ada[l])
        mods = jnp.split(mod, 6, axis=1)
        rows_of = [jnp.repeat(m[bp:], lt, axis=0) for m in mods]

        def vec(i, rt):
            return mods[i][:bp].reshape(bp, 1, d), rows_of[i].reshape(rt.n_s, rt.tm, d)

        w_r, col = _rearranged_w_in(w_in[l], d)
        proj = _inproj(rt_proj, xp, xs, vec(1, rt_proj), vec(0, rt_proj), w_r)

        alog = jnp.pad(a_log[l].reshape(1, DN_HEADS), ((0, 0), (0, LANES - DN_HEADS)))
        dtb = jnp.pad(dt_bias[l].reshape(1, DN_HEADS), ((0, 0), (0, LANES - DN_HEADS)))
        og = o_norm_g[l].reshape(1, DN_DV)
        conv0 = jnp.zeros((bp, DN_CONV - 1, QKV_W), F32)
        s0 = jnp.zeros((bp, DN_HEADS, DN_DK, DN_DV), F32)
        oa_p, conv_p, delta_p = _gdn_prompt(proj, conv0, s0, w_conv[l], alog, dtb, og, bp, seq,
                                            col['z'], col['ab'])
        oa_s, conv_s, delta_s = _gdn_sample(proj, rows_p, state_conv_qkv[l], state_delta[l], w_conv[l],
                                            alog, dtb, og, bs, col['z'], col['ab'])

        ts = rt_sgu.tm
        assert ts == SG_CHUNK
        ws_p = jnp.tril(w_s[l][:, :ts, :ts]).astype(BF16)
        bias_p = jnp.repeat(b_s[l].T[:ts], SG_CH, axis=1)
        eye = jnp.eye(ts // lt, dtype=F32)
        ws_s = jnp.stack([jnp.kron(eye, jnp.tril(w_s[l, g, :lt, :lt])) for g in range(SG_GROUPS)]).astype(BF16)
        bias_s = jnp.tile(jnp.repeat(b_s[l, :, :lt].T, SG_CH, axis=1), (ts // lt, 1))
        ob, vn_s = _sgu(rt_sgu, proj, sg_ln_g[l].reshape(1, SG_W), sg_ln_b[l].reshape(1, SG_W),
                        ws_p, ws_s, bias_p, bias_s, col['u'], col['v'])

        pa, pb, wo = p_a[l].astype(BF16), p_b[l].astype(BF16), w_out[l].astype(BF16)
        rws = _split(jnp.pad(router_w[l], ((0, 0), (0, LANES - n_experts))))
        rw = jnp.concatenate([rws[0], rws[0], rws[1]], axis=0)
        rb = jnp.pad(router_b[l].reshape(1, n_experts), ((0, 0), (0, LANES - n_experts)))
        x1, h2, ti, tg = _merge(alpha, n_experts, rt_merge, oa_p, oa_s, ob, proj, xp, xs,
                                vec(2, rt_merge), vec(4, rt_merge), vec(3, rt_merge),
                                pa, pb, wo, ln1_g[l].reshape(1, d), ln1_b[l].reshape(1, d), rw, rb,
                                col['ga'], col['gb'])

        nj = w_dn.shape[2] // MOE_TF
        dest, pad_lo, pad_len, nblk_used, nb_max, items = _routing(ti[:, :TOP_K], n_experts, MOE_TM, nj, MOE_SUBS)
        xb = _dispatch(pad_lo, pad_len, nblk_used, dest, h2, MOE_TM, nb_max)
        yb = _moe(items, xb, w_gu[l], b_gu[l], w_dn[l], b_dn[l], MOE_TM, MOE_TF, MOE_SUBS)
        xp, xs = _combine(alpha, rt_comb, dest, yb, tg, x1, vec(5, rt_comb),
                          ln2_g[l].reshape(1, d), ln2_b[l].reshape(1, d))

        outs['conv_p'].append(conv_p)
        outs['delta_p'].append(delta_p)
        outs['conv_s'].append(conv_s)
        outs['delta_s'].append(delta_s)
        outs['vrows'].append(vn_s.reshape(bs, lt, SG_W))

    return (xp.reshape(bp, seq, d), xs.reshape(bs, lt, d),
            jnp.stack(outs['conv_p']), jnp.stack(outs['delta_p']),
            jnp.stack(outs['conv_s']), jnp.stack(outs['delta_s']), jnp.stack(outs['vrows']))
```

```python
import functools
import math

import jax
import jax.numpy as jnp
from jax import lax
from jax.experimental import pallas as pl
from jax.experimental.pallas import tpu as pltpu

F32 = jnp.float32
BF16 = jnp.bfloat16
HIGHEST = lax.Precision.HIGHEST

DN_HEADS = 8
DN_DK = 128
DN_DV = 128
DN_CONV = 4
DN_CHUNK = 64
SG_GROUPS = 8
SG_CH = 128
SG_CHUNK = 128
TOP_K = 4
SWIGLU_LIMIT = 7.0
SWIGLU_ALPHA = 1.702
LN_EPS = 1e-5
NORM_EPS = 1e-6
QK_W = DN_HEADS * DN_DK
V_W = DN_HEADS * DN_DV
QKV_W = 2 * QK_W + V_W
SG_W = SG_GROUPS * SG_CH

LANES = 128
SUBLANES = 8
VMEM_LIMIT = 56 * 1024 * 1024

PROJ_TM = 512
PROJ_TN = 1536
MERGE_TM = 256
MOE_TM = 256
MOE_TF = 512
MOE_SUBS = 6
COMB_TQ = 128
SAMPLE_BT = 4
SAMPLE_TP = 8
GDN_HG = 4


def _sigmoid(x):
    return 1.0 / (1.0 + jnp.exp(-x))


def _silu(x):
    return x * _sigmoid(x)


def _dot(a, b):
    return jnp.dot(a.astype(BF16), b.astype(BF16), preferred_element_type=F32)


def _dot_nt(a, b):
    return lax.dot_general(a.astype(BF16), b.astype(BF16), (((1,), (1,)), ((), ())),
                           preferred_element_type=F32)


def _dot_hi(a, b):
    return jnp.dot(a, b, precision=HIGHEST, preferred_element_type=F32)


def _layer_norm(x, g, b):
    mu = jnp.mean(x, axis=-1, keepdims=True)
    xc = x - mu
    var = jnp.mean(xc * xc, axis=-1, keepdims=True)
    return xc * lax.rsqrt(var + LN_EPS) * g + b


def _params(*sem):
    return pltpu.CompilerParams(dimension_semantics=sem, vmem_limit_bytes=VMEM_LIMIT)


class _Rows:
    def __init__(self, tm, rows_p, rows_s, seq):
        assert rows_p % tm == 0 and rows_s % tm == 0 and seq % tm == 0, (tm, rows_p, rows_s, seq)
        self.tm, self.n_p, self.n_s = tm, rows_p // tm, rows_s // tm
        self.tiles_per_seq = seq // tm
        self.bp = rows_p // seq

    @property
    def n(self):
        return self.n_p + self.n_s

    def prompt(self, width, col=0):
        return pl.BlockSpec((self.tm, width), lambda i, *_: (jnp.minimum(i, self.n_p - 1), col))

    def sample(self, width, col=0, single=True):
        mode = dict(pipeline_mode=pl.Buffered(1)) if single else {}
        return pl.BlockSpec((self.tm, width), lambda i, *_: (jnp.maximum(i - self.n_p, 0), col), **mode)

    def joint(self, width, col=0):
        return pl.BlockSpec((self.tm, width), lambda i, *_: (i, col))

    def seq_vec(self, d):
        return pl.BlockSpec((None, 1, d), lambda i, *_: (jnp.minimum(i // self.tiles_per_seq, self.bp - 1), 0, 0))

    def row_vec(self, d):
        return pl.BlockSpec((None, self.tm, d), lambda i, *_: (jnp.maximum(i - self.n_p, 0), 0, 0),
                            pipeline_mode=pl.Buffered(1))


def _pick(is_sample, prompt_ref, sample_ref):
    return jnp.where(is_sample, sample_ref[...], prompt_ref[...])


def _adaln_kernel(c_ref, w_ref, b_ref, o_ref):
    o_ref[...] = _dot(_silu(c_ref[...]), w_ref[...]) + b_ref[...]


def _adaln(c, w, b):
    rows, d = c.shape
    n = w.shape[1]
    tn = 1024
    return pl.pallas_call(
        _adaln_kernel,
        grid=(n // tn,),
        in_specs=[pl.BlockSpec((rows, d), lambda j: (0, 0)),
                  pl.BlockSpec((d, tn), lambda j: (0, j)),
                  pl.BlockSpec((1, tn), lambda j: (0, j))],
        out_specs=pl.BlockSpec((rows, tn), lambda j: (0, j)),
        out_shape=jax.ShapeDtypeStruct((rows, n), F32),
        compiler_params=_params("arbitrary"),
        name="adaln",
    )(c, w, b.reshape(1, n))


def _inproj_kernel(n_p, xp_ref, xs_ref, scp_ref, scs_ref, shp_ref, shs_ref, w_ref, o_ref, h_scr):
    @pl.when(pl.program_id(1) == 0)
    def _():
        is_s = pl.program_id(0) >= n_p
        x = _pick(is_s, xp_ref, xs_ref)
        h_scr[...] = (x * (1.0 + _pick(is_s, scp_ref, scs_ref)) + _pick(is_s, shp_ref, shs_ref)).astype(BF16)

    o_ref[...] = jnp.dot(h_scr[...], w_ref[...], preferred_element_type=F32)


def _inproj(rt, xp, xs, sc, sh, w):
    d = xp.shape[1]
    nw = w.shape[1]
    tm = rt.tm
    return pl.pallas_call(
        functools.partial(_inproj_kernel, rt.n_p),
        grid=(rt.n, nw // PROJ_TN),
        in_specs=[rt.prompt(d), rt.sample(d), rt.seq_vec(d), rt.row_vec(d), rt.seq_vec(d), rt.row_vec(d),
                  pl.BlockSpec((d, PROJ_TN), lambda i, j: (0, j))],
        out_specs=pl.BlockSpec((tm, PROJ_TN), lambda i, j: (i, j)),
        out_shape=jax.ShapeDtypeStruct((rt.n * tm, nw), F32),
        scratch_shapes=[pltpu.VMEM((tm, d), BF16)],
        compiler_params=_params("arbitrary", "arbitrary"),
        name="inproj",
    )(xp, xs, sc[0], sc[1], sh[0], sh[1], w)


def _softplus(x):
    return jnp.maximum(x, 0.0) + jnp.log1p(jnp.exp(-jnp.abs(x)))


def _split(a):
    hi = a.astype(BF16)
    lo = (a - hi.astype(F32)).astype(BF16)
    return hi, lo


def _dot3(a, b):
    lhs = jnp.concatenate([a[0], a[1], a[0]], axis=1)
    rhs = jnp.concatenate([b[0], b[0], b[1]], axis=0)
    return jnp.dot(lhs, rhs, preferred_element_type=F32)


def _map(f, *lists):
    return [f(*args) for args in zip(*lists)]


def _unit_lower_inverses_minus_eye(mats, rows, cols):
    same = (rows // 16) == (cols // 16)
    n = [jnp.where(same, a, 0.0) for a in mats]
    b = _map(lambda a, x: a - x, mats, n)
    n2 = _map(lambda x: _dot(x, x), n)
    n4 = _map(lambda x: _dot(x, x), n2)
    n8 = _map(lambda x: _dot(x, x), n4)
    r = [-x for x in n]
    r = _map(lambda x, p: x + p + _dot(x, p), r, n2)
    r = _map(lambda x, p: x + p + _dot(x, p), r, n4)
    dm = _map(lambda x, p: x + p + _dot(x, p), r, n8)
    m = _map(lambda x, y: y + _dot(x, y), dm, b)
    m2 = _map(lambda x: _dot(x, x), m)
    xm = _map(lambda x, p: x + p + _dot(p, x), dm, m2)
    return _map(lambda x, p: x - p - _dot(p, x), xm, m)


def _gdn_prompt_kernel(qkv_ref, z_ref, ab_ref, hist_ref, s0_ref, wconv_ref, alog_ref, dt_ref, og_ref,
                       o_ref, conv_ref, s_ref, xp_scr):
    n = pl.program_id(1)
    c = DN_CHUNK
    pad = SUBLANES
    nr = GDN_HG * c
    ngrp = DN_HEADS // GDN_HG

    @pl.when(n == 0)
    def _():
        xp_scr[pl.ds(0, pad), :] = jnp.zeros((pad, QKV_W), F32)
        xp_scr[pl.ds(pad - (DN_CONV - 1), DN_CONV - 1), :] = hist_ref[...]
        s_ref[...] = s0_ref[...]

    x = qkv_ref[...]
    xp_scr[pl.ds(pad, c), :] = x
    y = jnp.zeros((c, QKV_W), F32)
    for j in range(DN_CONV):
        y = y + xp_scr[pl.ds(pad - (DN_CONV - 1) + j, c), :] * wconv_ref[pl.ds(j, 1), :]
    y = _silu(y)
    tail = xp_scr[pl.ds(c + pad - (DN_CONV - 1), DN_CONV - 1), :]
    conv_ref[...] = tail
    xp_scr[pl.ds(pad - (DN_CONV - 1), DN_CONV - 1), :] = tail

    ab = ab_ref[...]
    g = -jnp.exp(alog_ref[...]) * _softplus(ab + dt_ref[...])
    beta_all = _sigmoid(ab)
    r64 = lax.broadcasted_iota(jnp.int32, (c, c), 0)
    c64 = lax.broadcasted_iota(jnp.int32, (c, c), 1)
    gc = _dot_hi((r64 >= c64).astype(F32), g)
    z = z_ref[...]
    og = og_ref[...]

    rows = lax.broadcasted_iota(jnp.int32, (nr, nr), 0)
    cols = lax.broadcasted_iota(jnp.int32, (nr, nr), 1)
    same = (rows // c) == (cols // c)
    causal = same & (rows >= cols)
    strict = same & (rows > cols)
    rowhead = lax.broadcasted_iota(jnp.int32, (nr, 1), 0) // c
    groups = [range(grp * GDN_HG, (grp + 1) * GDN_HG) for grp in range(ngrp)]

    def stack(heads, off, width):
        return jnp.concatenate([y[:, off + h * width:off + (h + 1) * width] for h in heads], axis=0)

    def l2n(t):
        return t * lax.rsqrt(jnp.sum(t * t, axis=-1, keepdims=True) + NORM_EPS)

    q = [l2n(stack(hs, 0, DN_DK)) * (DN_DK ** -0.5) for hs in groups]
    k = [l2n(stack(hs, QK_W, DN_DK)) for hs in groups]
    v = [stack(hs, 2 * QK_W, DN_DV) for hs in groups]
    beta = [jnp.concatenate([beta_all[:, DN_HEADS + h:DN_HEADS + h + 1] for h in hs], axis=0) for hs in groups]
    gcf = [jnp.concatenate([jnp.broadcast_to(gc[:, h:h + 1], (c, LANES)) for h in hs], axis=0) for hs in groups]
    gcc = [t[:, 0:1] for t in gcf]
    grow = [t.T[0:1, :] for t in gcf]
    glast = [[gc[c - 1:c, h:h + 1] for h in hs] for hs in groups]
    gtot = [jnp.concatenate([jnp.broadcast_to(t, (c, 1)) for t in gl], axis=0) for gl in glast]
    decay = _map(lambda a, b: jnp.where(causal, jnp.exp(jnp.where(causal, a - b, 0.0)), 0.0), gcc, grow)
    egc = _map(jnp.exp, gcc)
    kb = _map(lambda a, b: a * b, k, beta)
    a_mat = _map(lambda a, b, dd: jnp.where(strict, _dot_nt(a, b) * dd, 0.0), kb, k, decay)
    tm1 = _unit_lower_inverses_minus_eye(a_mat, rows, cols)
    u = _map(lambda t, a, b: a * b + _dot(t, a * b), tm1, v, beta)
    w = _map(lambda t, a, b: a * b + _dot(t, a * b), tm1, kb, egc)
    attn = _map(lambda a, b, dd: _dot_nt(a, b) * dd, q, k, decay)
    qg = _map(lambda a, b: a * b, q, egc)
    kd_t = _map(lambda a, b, cc: (a * jnp.exp(b - cc)).T, k, gtot, gcc)

    for gi, hs in enumerate(groups):
        v_news, qss = [], []
        for hl, h in enumerate(hs):
            sl = slice(hl * c, (hl + 1) * c)
            rs = _dot(jnp.concatenate([w[gi][sl], qg[gi][sl]], axis=0), s_ref[h])
            v_news.append(u[gi][sl] - rs[:c])
            qss.append(rs[c:])
        v_new = jnp.concatenate(v_news, axis=0)
        o = jnp.concatenate(qss, axis=0) + _dot(attn[gi], v_new)
        for hl, h in enumerate(hs):
            upd = _dot(kd_t[gi], jnp.where(rowhead == hl, v_new, 0.0))
            s_ref[h] = s_ref[h] * jnp.exp(glast[gi][hl]) + upd
        o = o * lax.rsqrt(jnp.mean(o * o, axis=-1, keepdims=True) + NORM_EPS) * og
        for hl, h in enumerate(hs):
            o_ref[:, h * DN_DV:(h + 1) * DN_DV] = o[hl * c:(hl + 1) * c] * _silu(z[:, h * DN_DV:(h + 1) * DN_DV])


def _gdn_prompt(proj, hist, s0, wconv, alog, dtb, og, batch, seq, col_z, col_ab):
    c = DN_CHUNK
    nchunk = seq // c
    rows = batch * seq
    row = lambda b, n: b * nchunk + n
    return pl.pallas_call(
        _gdn_prompt_kernel,
        grid=(batch, nchunk),
        in_specs=[pl.BlockSpec((c, QKV_W), lambda b, n: (row(b, n), 0)),
                  pl.BlockSpec((c, V_W), lambda b, n: (row(b, n), col_z // V_W)),
                  pl.BlockSpec((c, LANES), lambda b, n: (row(b, n), col_ab // LANES)),
                  pl.BlockSpec((None, DN_CONV - 1, QKV_W), lambda b, n: (b, 0, 0)),
                  pl.BlockSpec((None, DN_HEADS, DN_DK, DN_DV), lambda b, n: (b, 0, 0, 0)),
                  pl.BlockSpec((DN_CONV, QKV_W), lambda b, n: (0, 0)),
                  pl.BlockSpec((1, LANES), lambda b, n: (0, 0)),
                  pl.BlockSpec((1, LANES), lambda b, n: (0, 0)),
                  pl.BlockSpec((1, DN_DV), lambda b, n: (0, 0))],
        out_specs=[pl.BlockSpec((c, V_W), lambda b, n: (row(b, n), 0)),
                   pl.BlockSpec((None, DN_CONV - 1, QKV_W), lambda b, n: (b, 0, 0)),
                   pl.BlockSpec((None, DN_HEADS, DN_DK, DN_DV), lambda b, n: (b, 0, 0, 0))],
        out_shape=[jax.ShapeDtypeStruct((rows, V_W), F32),
                   jax.ShapeDtypeStruct((batch, DN_CONV - 1, QKV_W), F32),
                   jax.ShapeDtypeStruct((batch, DN_HEADS, DN_DK, DN_DV), F32)],
        scratch_shapes=[pltpu.VMEM((c + SUBLANES, QKV_W), F32)],
        compiler_params=_params("arbitrary", "arbitrary"),
        name="gdn_prompt",
    )(proj, proj, proj, hist, s0, wconv, alog, dtb, og)


def _gdn_sample_kernel(qkv_ref, z_ref, ab_ref, hist_ref, s0_ref, wconv_ref, alog_ref, dt_ref, og_ref,
                       o_ref, conv_ref, s_ref, xp_scr, ab_scr, wq_scr, r_scr, kdt_scr, vn_scr, gl_scr):
    bt, tp, lt = SAMPLE_BT, SAMPLE_TP, DN_CONV
    nblk = DN_HEADS * bt
    nrow = nblk * tp
    hist_rows = DN_CONV - 1

    xp_scr[...] = jnp.zeros(xp_scr.shape, F32)
    ab_scr[...] = jnp.zeros(ab_scr.shape, F32)
    ys = []
    for b in range(bt):
        xp_scr[b, pl.ds(SUBLANES - hist_rows, hist_rows), :] = hist_ref[b]
        xp_scr[b, pl.ds(SUBLANES, lt), :] = qkv_ref[pl.ds(b * lt, lt), :]
        yb = jnp.zeros((tp, QKV_W), F32)
        for j in range(DN_CONV):
            yb = yb + xp_scr[b, pl.ds(SUBLANES - hist_rows + j, tp), :] * wconv_ref[pl.ds(j, 1), :]
        ys.append(_silu(yb))
        conv_ref[b] = xp_scr[b, pl.ds(SUBLANES + lt - hist_rows, hist_rows), :]
        ab_scr[b, pl.ds(0, lt), :] = ab_ref[pl.ds(b * lt, lt), :]
    y = jnp.concatenate(ys, axis=0)
    ab = jnp.concatenate([ab_scr[b] for b in range(bt)], axis=0)
    tok = lax.broadcasted_iota(jnp.int32, (bt * tp, 1), 0) % tp
    real = tok < lt
    g_all = jnp.where(real, -jnp.exp(alog_ref[...]) * _softplus(ab + dt_ref[...]), 0.0)
    beta_all = jnp.where(real, _sigmoid(ab), 0.0)

    def heads_to_rows(t, off, width):
        return jnp.concatenate([t[:, off + h * width:off + (h + 1) * width] for h in range(DN_HEADS)], axis=0)

    realr = jnp.concatenate([real] * DN_HEADS, axis=0)
    q = jnp.where(realr, heads_to_rows(y, 0, DN_DK), 0.0)
    k = jnp.where(realr, heads_to_rows(y, QK_W, DN_DK), 0.0)
    v = jnp.where(realr, heads_to_rows(y, 2 * QK_W, DN_DV), 0.0)
    gcol = jnp.concatenate([g_all[:, h:h + 1] for h in range(DN_HEADS)], axis=0)
    beta = jnp.concatenate([beta_all[:, DN_HEADS + h:DN_HEADS + h + 1] for h in range(DN_HEADS)], axis=0)
    q = q * lax.rsqrt(jnp.sum(q * q, axis=-1, keepdims=True) + NORM_EPS) * (DN_DK ** -0.5)
    k = k * lax.rsqrt(jnp.sum(k * k, axis=-1, keepdims=True) + NORM_EPS)

    rows = lax.broadcasted_iota(jnp.int32, (nrow, nrow), 0)
    cols = lax.broadcasted_iota(jnp.int32, (nrow, nrow), 1)
    same = (rows // tp) == (cols // tp)
    causal = same & (rows >= cols)
    strict = same & (rows > cols)
    gfull = jnp.broadcast_to(gcol, (nrow, LANES))
    gc = _dot_hi(causal.astype(F32), gfull)
    gtot = _dot_hi(same.astype(F32), gfull)
    gcc = gc[:, 0:1]
    grow = gc.T[0:1, :]
    decay = jnp.where(causal, jnp.exp(jnp.where(causal, gcc - grow, 0.0)), 0.0)
    egc = jnp.exp(gcc)
    kb = k * beta
    a_mat = jnp.where(strict, _dot_nt(kb, k) * decay, 0.0)
    a2 = _dot(a_mat, a_mat)
    tm1 = a2 - a_mat - _dot(a_mat, a2)
    u = v * beta + _dot(tm1, v * beta)
    w = kb * egc + _dot(tm1, kb * egc)
    attn = _dot_nt(q, k) * decay
    qg = q * egc
    for i in range(nblk):
        wq_scr[pl.ds(2 * tp * i, tp), :] = w[i * tp:(i + 1) * tp, :]
        wq_scr[pl.ds(2 * tp * i + tp, tp), :] = qg[i * tp:(i + 1) * tp, :]
    kdt_scr[...] = (k * jnp.exp(gtot[:, 0:1] - gcc)).T
    gl_scr[...] = jnp.exp(gtot)

    def read_state(i, carry):
        r0 = pl.multiple_of(i * 2 * tp, 2 * tp)
        r_scr[pl.ds(r0, 2 * tp), :] = _dot(wq_scr[pl.ds(r0, 2 * tp), :], s0_ref[i % bt, i // bt])
        return carry

    lax.fori_loop(0, nblk, read_state, 0, unroll=4)
    ws = jnp.concatenate([r_scr[pl.ds(2 * tp * i, tp), :] for i in range(nblk)], axis=0)
    qs = jnp.concatenate([r_scr[pl.ds(2 * tp * i + tp, tp), :] for i in range(nblk)], axis=0)
    v_new = u - ws
    vn_scr[...] = v_new
    o = qs + _dot(attn, v_new)
    blockid = lax.broadcasted_iota(jnp.int32, (nrow, 1), 0) // tp

    def write_state(i, carry):
        r0 = pl.multiple_of(i * tp, tp)
        upd = _dot(kdt_scr[...], jnp.where(blockid == i, vn_scr[...], 0.0))
        s_ref[i % bt, i // bt] = s0_ref[i % bt, i // bt] * gl_scr[pl.ds(r0, 1), :] + upd
        return carry

    lax.fori_loop(0, nblk, write_state, 0, unroll=4)

    og = og_ref[...]
    o = o * lax.rsqrt(jnp.mean(o * o, axis=-1, keepdims=True) + NORM_EPS) * og
    for b in range(bt):
        for h in range(DN_HEADS):
            zbh = z_ref[pl.ds(b * lt, lt), h * DN_DV:(h + 1) * DN_DV]
            blk = o[(h * bt + b) * tp:(h * bt + b) * tp + lt, :]
            o_ref[pl.ds(b * lt, lt), h * DN_DV:(h + 1) * DN_DV] = blk * _silu(zbh)


def _gdn_sample(proj, row0, hist, s0, wconv, alog, dtb, og, batch, col_z, col_ab):
    bt, tp, lt = SAMPLE_BT, SAMPLE_TP, DN_CONV
    rows = batch * lt
    nrow = DN_HEADS * bt * tp
    blk = bt * lt
    assert row0 % blk == 0 and batch % bt == 0
    b0 = row0 // blk
    return pl.pallas_call(
        _gdn_sample_kernel,
        grid=(batch // bt,),
        in_specs=[pl.BlockSpec((blk, QKV_W), lambda i: (b0 + i, 0)),
                  pl.BlockSpec((blk, V_W), lambda i: (b0 + i, col_z // V_W)),
                  pl.BlockSpec((blk, LANES), lambda i: (b0 + i, col_ab // LANES)),
                  pl.BlockSpec((bt, DN_CONV - 1, QKV_W), lambda i: (i, 0, 0)),
                  pl.BlockSpec((bt, DN_HEADS, DN_DK, DN_DV), lambda i: (i, 0, 0, 0)),
                  pl.BlockSpec((DN_CONV, QKV_W), lambda i: (0, 0)),
                  pl.BlockSpec((1, LANES), lambda i: (0, 0)),
                  pl.BlockSpec((1, LANES), lambda i: (0, 0)),
                  pl.BlockSpec((1, DN_DV), lambda i: (0, 0))],
        out_specs=[pl.BlockSpec((blk, V_W), lambda i: (i, 0)),
                   pl.BlockSpec((bt, DN_CONV - 1, QKV_W), lambda i: (i, 0, 0)),
                   pl.BlockSpec((bt, DN_HEADS, DN_DK, DN_DV), lambda i: (i, 0, 0, 0))],
        out_shape=[jax.ShapeDtypeStruct((rows, V_W), F32),
                   jax.ShapeDtypeStruct((batch, DN_CONV - 1, QKV_W), F32),
                   jax.ShapeDtypeStruct((batch, DN_HEADS, DN_DK, DN_DV), F32)],
        scratch_shapes=[pltpu.VMEM((bt, SUBLANES + tp, QKV_W), F32),
                        pltpu.VMEM((bt, tp, LANES), F32),
                        pltpu.VMEM((2 * nrow, DN_DK), F32),
                        pltpu.VMEM((2 * nrow, DN_DV), F32),
                        pltpu.VMEM((DN_DK, nrow), F32),
                        pltpu.VMEM((nrow, DN_DV), F32),
                        pltpu.VMEM((nrow, LANES), F32)],
        compiler_params=_params("arbitrary"),
        name="gdn_sample",
    )(proj, proj, proj, hist, s0, wconv, alog, dtb, og)


def _sgu_kernel(n_p, u_ref, v_ref, g_ref, b_ref, wsp_ref, wss_ref, bp_ref, bs_ref, o_ref, vn_ref):
    is_s = pl.program_id(0) >= n_p
    u = jax.nn.gelu(u_ref[...])
    vn = _layer_norm(jax.nn.gelu(v_ref[...]), g_ref[...], b_ref[...])
    vn_ref[...] = vn
    bias = _pick(is_s, bp_ref, bs_ref)
    for g in range(SG_GROUPS):
        sl = slice(g * SG_CH, (g + 1) * SG_CH)
        ws = jnp.where(is_s, wss_ref[g], wsp_ref[g])
        mixed = _dot(ws, vn[:, sl]) + bias[:, sl]
        o_ref[:, sl] = u[:, sl] * mixed


def _sgu(rt, proj, ln_g, ln_b, ws_p, ws_s, bias_p, bias_s, col_u, col_v):
    t = rt.tm
    const = lambda shape: pl.BlockSpec(shape, lambda i: (0,) * len(shape))
    return pl.pallas_call(
        functools.partial(_sgu_kernel, rt.n_p),
        grid=(rt.n,),
        in_specs=[rt.joint(SG_W, col_u // SG_W), rt.joint(SG_W, col_v // SG_W),
                  const((1, SG_W)), const((1, SG_W)),
                  const((SG_GROUPS, t, t)), const((SG_GROUPS, t, t)), const((t, SG_W)), const((t, SG_W))],
        out_specs=[rt.joint(SG_W), rt.sample(SG_W, single=False)],
        out_shape=[jax.ShapeDtypeStruct((rt.n * t, SG_W), F32),
                   jax.ShapeDtypeStruct((rt.n_s * t, SG_W), F32)],
        compiler_params=_params("arbitrary"),
        name="sgu",
    )(proj, proj, ln_g, ln_b, ws_p, ws_s, bias_p, bias_s)


def _merge_kernel(alpha, n_experts, n_p, oap_ref, oas_ref, ob_ref, ga_ref, gb_ref, xp_ref, xs_ref,
                  gtp_ref, gts_ref, scp_ref, scs_ref, shp_ref, shs_ref,
                  pa_ref, pb_ref, wo_ref, lg_ref, lb_ref, rw_ref, rb_ref,
                  x1_ref, h2_ref, ti_ref, tg_ref):
    is_s = pl.program_id(0) >= n_p
    oa = _pick(is_s, oap_ref, oas_ref)
    merged = (_sigmoid(ga_ref[...]) * _dot(oa, pa_ref[...])
              + _sigmoid(gb_ref[...]) * _dot(ob_ref[...], pb_ref[...]))
    y = _dot(merged, wo_ref[...])
    x = _pick(is_s, xp_ref, xs_ref)
    x1 = _layer_norm(alpha * x + _pick(is_s, gtp_ref, gts_ref) * y, lg_ref[...], lb_ref[...])
    x1_ref[...] = x1
    h2 = x1 * (1.0 + _pick(is_s, scp_ref, scs_ref)) + _pick(is_s, shp_ref, shs_ref)
    h2_ref[...] = h2
    hs = _split(h2)
    logits = jnp.dot(jnp.concatenate([hs[0], hs[1], hs[0]], axis=1), rw_ref[...],
                     preferred_element_type=F32) + rb_ref[...]
    lane = lax.broadcasted_iota(jnp.int32, logits.shape, 1)
    logits = jnp.where(lane < n_experts, logits, -jnp.inf)
    ti = jnp.zeros(logits.shape, jnp.int32)
    tv = jnp.zeros(logits.shape, F32)
    top = None
    for kk in range(TOP_K):
        m = jnp.max(logits, axis=-1, keepdims=True)
        idx = jnp.min(jnp.where(logits == m, lane.astype(F32), float(LANES)), axis=-1,
                      keepdims=True).astype(jnp.int32)
        if kk == 0:
            top = m
        ti = jnp.where(lane == kk, idx, ti)
        tv = jnp.where(lane == kk, jnp.exp(m - top), tv)
        logits = jnp.where(lane == idx, -jnp.inf, logits)
    ti_ref[...] = ti
    tg_ref[...] = tv / jnp.sum(tv, axis=-1, keepdims=True)


def _merge(alpha, n_experts, rt, oa_p, oa_s, ob, proj, xp, xs, gt, sc, sh, pa, pb, wo, lg, lb, rw, rb,
           col_ga, col_gb):
    d = xp.shape[1]
    tm = rt.tm
    rows = rt.n * tm
    const = lambda shape: pl.BlockSpec(shape, lambda i: (0,) * len(shape), pipeline_mode=pl.Buffered(1))
    return pl.pallas_call(
        functools.partial(_merge_kernel, alpha, n_experts, rt.n_p),
        grid=(rt.n,),
        in_specs=[rt.prompt(V_W), rt.sample(V_W), rt.joint(SG_W),
                  rt.joint(d, col_ga // d), rt.joint(d, col_gb // d),
                  rt.prompt(d), rt.sample(d),
                  rt.seq_vec(d), rt.row_vec(d), rt.seq_vec(d), rt.row_vec(d), rt.seq_vec(d), rt.row_vec(d),
                  const((V_W, d)), const((SG_W, d)), const((d, d)),
                  const((1, d)), const((1, d)), const((3 * d, LANES)), const((1, LANES))],
        out_specs=[rt.joint(d), rt.joint(d), rt.joint(LANES), rt.joint(LANES)],
        out_shape=[jax.ShapeDtypeStruct((rows, d), F32),
                   jax.ShapeDtypeStruct((rows, d), F32),
                   jax.ShapeDtypeStruct((rows, LANES), jnp.int32),
                   jax.ShapeDtypeStruct((rows, LANES), F32)],
        compiler_params=_params("arbitrary"),
        name="merge",
    )(oa_p, oa_s, ob, proj, proj, xp, xs, gt[0], gt[1], sc[0], sc[1], sh[0], sh[1],
      pa, pb, wo, lg, lb, rw, rb)


FILL_ROWS = (128, 64, 32, 16, 8)


def _dispatch_kernel(tm, n_experts, padlo_ref, padlen_ref, nblk_ref, dest_ref, h_ref, xb_hbm, zero_scr, sem, zsem):
    i = pl.program_id(0)
    tq = h_ref.shape[0]
    nb_max = xb_hbm.shape[0] // tm
    zrows = zero_scr.shape[0]

    def fill(act):
        def per_expert(e, carry):
            lo = padlo_ref[e]
            ln = padlen_ref[e]
            head = (SUBLANES - lo % SUBLANES) % SUBLANES
            for r in range(SUBLANES - 1):
                @pl.when(r < head)
                def _(r=r):
                    act(pltpu.make_async_copy(zero_scr.at[pl.ds(0, 1)], xb_hbm.at[pl.ds(lo + r, 1)], zsem))
            off = lo + head
            rem = ln - head
            for b in FILL_ROWS:
                @pl.when((rem & b) != 0)
                def _(off=off, b=b):
                    dst = xb_hbm.at[pl.ds(pl.multiple_of(off, SUBLANES), b)]
                    act(pltpu.make_async_copy(zero_scr.at[pl.ds(0, b)], dst, zsem))
                off = off + (rem & b)
            return carry

        lax.fori_loop(0, n_experts, per_expert, 0)

        def per_block(blk, carry):
            for part in range(tm // zrows):
                row0 = pl.multiple_of(blk * tm + part * zrows, zrows)
                act(pltpu.make_async_copy(zero_scr, xb_hbm.at[pl.ds(row0, zrows)], zsem))
            return carry

        lax.fori_loop(nblk_ref[0], nb_max, per_block, 0)

    @pl.when(i == 0)
    def _():
        zero_scr[...] = jnp.zeros(zero_scr.shape, F32)
        fill(lambda cp: cp.start())
        fill(lambda cp: cp.wait())

    def start(t, carry):
        for kk in range(TOP_K):
            pltpu.make_async_copy(h_ref.at[pl.ds(t, 1)], xb_hbm.at[pl.ds(dest_ref[0, t * TOP_K + kk], 1)],
                                  sem).start()
        return carry

    lax.fori_loop(0, tq, start, 0, unroll=4)
    for kk in range(TOP_K):
        pltpu.make_async_copy(h_ref, xb_hbm.at[pl.ds(0, tq)], sem).wait()


def _dispatch(padlo, padlen, nblk_used, dest, h2, tm, nb_max):
    t, d = h2.shape
    tq = math.gcd(256, t)
    n_experts = padlo.shape[0]
    grid_spec = pltpu.PrefetchScalarGridSpec(
        num_scalar_prefetch=3,
        grid=(t // tq,),
        in_specs=[pl.BlockSpec((None, 1, tq * TOP_K), lambda i, *_: (i, 0, 0), memory_space=pltpu.SMEM),
                  pl.BlockSpec((tq, d), lambda i, *_: (i, 0))],
        out_specs=pl.BlockSpec(memory_space=pl.ANY),
        scratch_shapes=[pltpu.VMEM((FILL_ROWS[0], d), F32), pltpu.SemaphoreType.DMA(()),
                        pltpu.SemaphoreType.DMA(())],
    )
    return pl.pallas_call(
        functools.partial(_dispatch_kernel, tm, n_experts),
        grid_spec=grid_spec,
        out_shape=jax.ShapeDtypeStruct((nb_max * tm, d), F32),
        compiler_params=pltpu.CompilerParams(dimension_semantics=("arbitrary",), vmem_limit_bytes=VMEM_LIMIT,
                                             has_side_effects=True),
        name="dispatch",
    )(padlo, padlen, nblk_used.reshape(1), dest.reshape(t // tq, 1, tq * TOP_K), h2)


def _moe_kernel(nj, e_ref, j_ref, xb_ref, ob_ref, r_ref, flag_ref, wg_code_ref, wu_code_ref,
                x_ref, wg_ref, wu_ref, wd_ref, bg_ref, bu_ref, bd_ref, o_ref,
                wg_scr, wu_scr, wd_scr, acc_scr):
    s = pl.program_id(0)
    flags = flag_ref[s]
    j = j_ref[s]
    r = r_ref[s]

    @pl.when((flags & 2) != 0)
    def _():
        wg_scr[...] = wg_ref[...].astype(BF16)
        wu_scr[...] = wu_ref[...].astype(BF16)
        wd_scr[...] = wd_ref[...].astype(BF16)

    @pl.when(flags == 0)
    def _():
        o_ref[...] = jnp.zeros(o_ref.shape, F32)

    @pl.when((flags & 1) != 0)
    def _():
        x = x_ref[...].astype(BF16)
        gate = jnp.dot(x, wg_scr[...], preferred_element_type=F32) + bg_ref[...]
        up = jnp.dot(x, wu_scr[...], preferred_element_type=F32) + bu_ref[...]
        gate = jnp.minimum(gate, SWIGLU_LIMIT)
        up = jnp.clip(up, -SWIGLU_LIMIT, SWIGLU_LIMIT)
        act = (up + 1.0) * gate * _sigmoid(SWIGLU_ALPHA * gate)
        y = jnp.dot(act.astype(BF16), wd_scr[...], preferred_element_type=F32)

        @pl.when(j == 0)
        def _():
            acc_scr[r] = y

        @pl.when(jnp.logical_and(j > 0, j < nj - 1))
        def _():
            acc_scr[r] = acc_scr[r] + y

        @pl.when(j == nj - 1)
        def _():
            o_ref[...] = acc_scr[r] + y + bd_ref[...]


def _moe(items, xb, w_gu, b_gu, w_dn, b_dn, tm, tf, subs):
    item_e, item_j, item_xb, item_ob, item_r, item_flag, wg_code, wu_code = items
    n_items = item_e.shape[0]
    nslot, d = xb.shape
    n_exp, _, f2 = w_gu.shape
    f = f2 // 2
    nj = f // tf
    assert nj >= 2
    grid_spec = pltpu.PrefetchScalarGridSpec(
        num_scalar_prefetch=8,
        grid=(n_items,),
        in_specs=[pl.BlockSpec((tm, d), lambda s, e, j, xbk, obk, r, fl, cg, cu: (xbk[s], 0)),
                  pl.BlockSpec((None, d, tf), lambda s, e, j, xbk, obk, r, fl, cg, cu: (cg[s] // nj, 0, cg[s] % nj)),
                  pl.BlockSpec((None, d, tf),
                               lambda s, e, j, xbk, obk, r, fl, cg, cu: (cu[s] // nj, 0, nj + cu[s] % nj)),
                  pl.BlockSpec((None, tf, d), lambda s, e, j, xbk, obk, r, fl, cg, cu: (e[s], j[s], 0)),
                  pl.BlockSpec((None, 1, tf), lambda s, e, j, xbk, obk, r, fl, cg, cu: (e[s], 0, j[s])),
                  pl.BlockSpec((None, 1, tf), lambda s, e, j, xbk, obk, r, fl, cg, cu: (e[s], 0, nj + j[s])),
                  pl.BlockSpec((None, 1, d), lambda s, e, j, xbk, obk, r, fl, cg, cu: (e[s], 0, 0))],
        out_specs=pl.BlockSpec((tm, d), lambda s, e, j, xbk, obk, r, fl, cg, cu: (obk[s], 0)),
        scratch_shapes=[pltpu.VMEM((d, tf), BF16), pltpu.VMEM((d, tf), BF16), pltpu.VMEM((tf, d), BF16),
                        pltpu.VMEM((subs, tm, d), F32)],
    )
    return pl.pallas_call(
        functools.partial(_moe_kernel, nj),
        grid_spec=grid_spec,
        out_shape=jax.ShapeDtypeStruct((nslot, d), F32),
        compiler_params=_params("arbitrary"),
        name="moe",
    )(item_e, item_j, item_xb, item_ob, item_r, item_flag, wg_code, wu_code,
      xb, w_gu, w_gu, w_dn, b_gu.reshape(n_exp, 1, f2), b_gu.reshape(n_exp, 1, f2), b_dn.reshape(n_exp, 1, d))


def _combine_kernel(alpha, n_p, dest_ref, destn_ref, yb_hbm, tg_ref, x1_ref, gtp_ref, gts_ref, lg_ref, lb_ref,
                    op_ref, os_ref, buf, sem):
    i = pl.program_id(0)
    last = pl.num_programs(0) - 1
    tq = x1_ref.shape[0]
    slot = i % 2

    def start(ids_ref, sl, t):
        for kk in range(TOP_K):
            pltpu.make_async_copy(yb_hbm.at[pl.ds(ids_ref[0, t * TOP_K + kk], 1)],
                                  buf.at[sl, kk, pl.ds(t, 1)], sem.at[sl]).start()

    def wait(sl):
        for kk in range(TOP_K):
            pltpu.make_async_copy(yb_hbm.at[pl.ds(0, tq)], buf.at[sl, kk], sem.at[sl]).wait()

    @pl.when(i == 0)
    def _():
        lax.fori_loop(0, tq, lambda t, c: (start(dest_ref, 0, t), c)[1], 0, unroll=4)

    for t in range(tq):
        start(destn_ref, 1 - slot, t)
    wait(slot)
    tg = tg_ref[...]
    y = jnp.zeros(x1_ref.shape, F32)
    for kk in range(TOP_K):
        y = y + buf[slot, kk] * tg[:, kk:kk + 1]
    is_s = i >= n_p
    out = _layer_norm(alpha * x1_ref[...] + _pick(is_s, gtp_ref, gts_ref) * y, lg_ref[...], lb_ref[...])

    @pl.when(jnp.logical_not(is_s))
    def _():
        op_ref[...] = out

    @pl.when(is_s)
    def _():
        os_ref[...] = out

    @pl.when(i == last)
    def _():
        wait(1 - slot)


def _combine(alpha, rt, dest, yb, tg, x1, gt, lg, lb):
    d = x1.shape[1]
    tq = rt.tm
    ids = dest.reshape(rt.n, 1, tq * TOP_K)
    return pl.pallas_call(
        functools.partial(_combine_kernel, alpha, rt.n_p),
        grid=(rt.n,),
        in_specs=[pl.BlockSpec((None, 1, tq * TOP_K), lambda i: (i, 0, 0), memory_space=pltpu.SMEM),
                  pl.BlockSpec((None, 1, tq * TOP_K), lambda i: (jnp.minimum(i + 1, rt.n - 1), 0, 0),
                               memory_space=pltpu.SMEM),
                  pl.BlockSpec(memory_space=pl.ANY),
                  rt.joint(LANES), rt.joint(d), rt.seq_vec(d), rt.row_vec(d),
                  pl.BlockSpec((1, d), lambda i: (0, 0)),
                  pl.BlockSpec((1, d), lambda i: (0, 0))],
        out_specs=[rt.prompt(d), rt.sample(d, single=False)],
        out_shape=[jax.ShapeDtypeStruct((rt.n_p * tq, d), F32), jax.ShapeDtypeStruct((rt.n_s * tq, d), F32)],
        scratch_shapes=[pltpu.VMEM((2, TOP_K, tq, d), F32), pltpu.SemaphoreType.DMA((2,))],
        compiler_params=_params("arbitrary"),
        name="combine",
    )(ids, ids, yb, tg, x1, gt[0], gt[1], lg, lb)


def _routing(ti, n_experts, tm, nj, subs):
    t = ti.shape[0]
    n_assign = t * TOP_K
    nb_max = n_assign // tm + n_experts
    onehot = (ti[:, :, None] == jnp.arange(n_experts, dtype=jnp.int32)[None, None, :]).astype(jnp.int32)
    per_tok = jnp.sum(onehot, axis=1)
    cum = jnp.cumsum(per_tok, axis=0)
    counts = cum[-1]
    rank = jnp.take_along_axis(cum, ti, axis=1) - 1
    nblk = (counts + tm - 1) // tm
    blk_end = jnp.cumsum(nblk)
    blk_start = blk_end - nblk
    dest = blk_start[ti] * tm + rank
    pad_lo = (blk_start * tm + counts).astype(jnp.int32)
    pad_len = (nblk * tm - counts).astype(jnp.int32)
    total_blk = blk_end[-1]
    blocks = jnp.arange(nb_max, dtype=jnp.int32)
    blk_e = jnp.minimum(jnp.sum((blk_end[None, :] <= blocks[:, None]).astype(jnp.int32), axis=1), n_experts - 1)
    r_in_e = blocks - blk_start[blk_e]
    g0 = blk_start[blk_e] + (r_in_e // subs) * subs
    nsub = jnp.minimum(subs, nblk[blk_e] - (r_in_e // subs) * subs)
    p = jnp.arange(nb_max * nj, dtype=jnp.int32)
    bp = p // nj
    valid = bp < total_blk
    last = jnp.maximum(total_blk - 1, 0)
    bq = jnp.where(valid, bp, last)
    gq, nq, eq = g0[bq], jnp.maximum(nsub[bq], 1), blk_e[bq]
    local = p - nj * gq
    jq = jnp.where(valid, local // nq, nj - 1)
    rq = jnp.where(valid, local % nq, nq - 1)
    item_xb = gq + rq
    item_ob = jnp.where(valid, jnp.where(jq == nj - 1, gq + rq, gq), bp)
    flags = valid.astype(jnp.int32) + 2 * (valid & (rq == 0)).astype(jnp.int32)
    nxt_blk = gq + nq
    has_next = nxt_blk < total_blk
    last_j = jq == nj - 1
    e_n = jnp.where(last_j & has_next, blk_e[jnp.minimum(nxt_blk, nb_max - 1)], eq)
    j_n = jnp.where(last_j, jnp.where(has_next, 0, jq), jq + 1)
    cur, nxt = eq * nj + jq, e_n * nj + j_n
    wg_code = jnp.where(valid & (rq >= 1), nxt, cur)
    wu_code = jnp.where(valid & (rq >= 2), nxt, cur)
    items = (eq.astype(jnp.int32), jq.astype(jnp.int32), item_xb.astype(jnp.int32),
             item_ob.astype(jnp.int32), rq.astype(jnp.int32), flags,
             wg_code.astype(jnp.int32), wu_code.astype(jnp.int32))
    return dest.astype(jnp.int32), pad_lo, pad_len, total_blk.astype(jnp.int32), nb_max, items


def _win_prep_kernel(w_ref, o_ref):
    o_a = QKV_W + V_W
    o_u = o_a + 2 * DN_HEADS
    x = w_ref[...]
    rest = x.shape[1] - o_u
    o_ref[:, 0:o_a] = x[:, 0:o_a].astype(BF16)
    o_ref[:, o_a:o_a + rest] = x[:, o_u:].astype(BF16)
    lane = lax.broadcasted_iota(jnp.int32, (x.shape[0], LANES), 1)
    ab = o_a + rest
    o_ref[:, ab:ab + LANES] = jnp.where(lane < 2 * DN_HEADS, x[:, o_a:o_a + LANES], 0.0).astype(BF16)
    if o_ref.shape[1] > ab + LANES:
        o_ref[:, ab + LANES:] = jnp.zeros((x.shape[0], o_ref.shape[1] - ab - LANES), BF16)


def _rearranged_w_in(w_in, d):
    dm, nw = w_in.shape
    tr = 128
    nw_out = -(-(nw - 2 * DN_HEADS + LANES) // PROJ_TN) * PROJ_TN
    w = pl.pallas_call(
        _win_prep_kernel,
        grid=(dm // tr,),
        in_specs=[pl.BlockSpec((tr, nw), lambda i: (i, 0))],
        out_specs=pl.BlockSpec((tr, nw_out), lambda i: (i, 0)),
        out_shape=jax.ShapeDtypeStruct((dm, nw_out), BF16),
        compiler_params=_params("arbitrary"),
        name="w_in_prep",
    )(w_in)
    cols = dict(z=QKV_W, u=QKV_W + V_W, v=QKV_W + V_W + SG_W, ga=QKV_W + V_W + 2 * SG_W,
                gb=QKV_W + V_W + 2 * SG_W + d, ab=QKV_W + V_W + 2 * SG_W + 2 * d)
    return w, cols


def kernel(x_prompt, x_sample, state_conv_qkv, state_delta, c_prompt, c_sample, w_ada, b_ada, w_in, w_conv, a_log, dt_bias, o_norm_g, sg_ln_g, sg_ln_b, w_s, b_s, p_a, p_b, w_out, ln1_g, ln1_b, router_w, router_b, w_gu, b_gu, w_dn, b_dn, ln2_g, ln2_b):
    depth = w_ada.shape[0]
    alpha = float((2 * depth) ** 0.25)
    bp, seq, d = x_prompt.shape
    bs, lt, _ = x_sample.shape
    assert lt == DN_CONV and seq % SG_CHUNK == 0
    n_experts = router_w.shape[2]
    rows_p, rows_s = bp * seq, bs * lt
    xp = x_prompt.reshape(rows_p, d)
    xs = x_sample.reshape(rows_s, d)
    c_all = jnp.concatenate([c_prompt, c_sample], axis=0)
    tile = lambda cap: _Rows(math.gcd(math.gcd(cap, rows_s), seq), rows_p, rows_s, seq)
    rt_proj, rt_sgu, rt_merge, rt_comb = tile(PROJ_TM), tile(SG_CHUNK), tile(MERGE_TM), tile(COMB_TQ)
    outs = dict(conv_p=[], delta_p=[], conv_s=[], delta_s=[], vrows=[])

    for l in range(depth):
        mod = _adaln(c_all, w_ada[l], b_ada[l])
        mods = jnp.split(mod, 6, axis=1)
        rows_of = [jnp.repeat(m[bp:], lt, axis=0) for m in mods]

        def vec(i, rt):
            return mods[i][:bp].reshape(bp, 1, d), rows_of[i].reshape(rt.n_s, rt.tm, d)

        w_r, col = _rearranged_w_in(w_in[l], d)
        proj = _inproj(rt_proj, xp, xs, vec(1, rt_proj), vec(0, rt_proj), w_r)

        alog = jnp.pad(a_log[l].reshape(1, DN_HEADS), ((0, 0), (0, LANES - DN_HEADS)))
        dtb = jnp.pad(dt_bias[l].reshape(1, DN_HEADS), ((0, 0), (0, LANES - DN_HEADS)))
        og = o_norm_g[l].reshape(1, DN_DV)
        conv0 = jnp.zeros((bp, DN_CONV - 1, QKV_W), F32)
        s0 = jnp.zeros((bp, DN_HEADS, DN_DK, DN_DV), F32)
        oa_p, conv_p, delta_p = _gdn_prompt(proj, conv0, s0, w_conv[l], alog, dtb, og, bp, seq,
                                            col['z'], col['ab'])
        oa_s, conv_s, delta_s = _gdn_sample(proj, rows_p, state_conv_qkv[l], state_delta[l], w_conv[l],
                                            alog, dtb, og, bs, col['z'], col['ab'])

        ts = rt_sgu.tm
        assert ts == SG_CHUNK
        ws_p = jnp.tril(w_s[l][:, :ts, :ts]).astype(BF16)
        bias_p = jnp.repeat(b_s[l].T[:ts], SG_CH, axis=1)
        eye = jnp.eye(ts // lt, dtype=F32)
        ws_s = jnp.stack([jnp.kron(eye, jnp.tril(w_s[l, g, :lt, :lt])) for g in range(SG_GROUPS)]).astype(BF16)
        bias_s = jnp.tile(jnp.repeat(b_s[l, :, :lt].T, SG_CH, axis=1), (ts // lt, 1))
        ob, vn_s = _sgu(rt_sgu, proj, sg_ln_g[l].reshape(1, SG_W), sg_ln_b[l].reshape(1, SG_W),
                        ws_p, ws_s, bias_p, bias_s, col['u'], col['v'])

        pa, pb, wo = p_a[l].astype(BF16), p_b[l].astype(BF16), w_out[l].astype(BF16)
        rws = _split(jnp.pad(router_w[l], ((0, 0), (0, LANES - n_experts))))
        rw = jnp.concatenate([rws[0], rws[0], rws[1]], axis=0)
        rb = jnp.pad(router_b[l].reshape(1, n_experts), ((0, 0), (0, LANES - n_experts)))
        x1, h2, ti, tg = _merge(alpha, n_experts, rt_merge, oa_p, oa_s, ob, proj, xp, xs,
                                vec(2, rt_merge), vec(4, rt_merge), vec(3, rt_merge),
                                pa, pb, wo, ln1_g[l].reshape(1, d), ln1_b[l].reshape(1, d), rw, rb,
                                col['ga'], col['gb'])

        nj = w_dn.shape[2] // MOE_TF
        dest, pad_lo, pad_len, nblk_used, nb_max, items = _routing(ti[:, :TOP_K], n_experts, MOE_TM, nj, MOE_SUBS)
        xb = _dispatch(pad_lo, pad_len, nblk_used, dest, h2, MOE_TM, nb_max)
        yb = _moe(items, xb, w_gu[l], b_gu[l], w_dn[l], b_dn[l], MOE_TM, MOE_TF, MOE_SUBS)
        xp, xs = _combine(alpha, rt_comb, dest, yb, tg, x1, vec(5, rt_comb),
                          ln2_g[l].reshape(1, d), ln2_b[l].reshape(1, d))

        outs['conv_p'].append(conv_p)
        outs['delta_p'].append(delta_p)
        outs['conv_s'].append(conv_s)
        outs['delta_s'].append(delta_s)
        outs['vrows'].append(vn_s.reshape(bs, lt, SG_W))

    return (xp.reshape(bp, seq, d), xs.reshape(bs, lt, d),
            jnp.stack(outs['conv_p']), jnp.stack(outs['delta_p']),
            jnp.stack(outs['conv_s']), jnp.stack(outs['delta_s']), jnp.stack(outs['vrows']))
```

```python
import functools
import math

import jax
import jax.numpy as jnp
from jax import lax
from jax.experimental import pallas as pl
from jax.experimental.pallas import tpu as pltpu

F32 = jnp.float32
BF16 = jnp.bfloat16
HIGHEST = lax.Precision.HIGHEST

DN_HEADS = 8
DN_DK = 128
DN_DV = 128
DN_CONV = 4
DN_CHUNK = 64
SG_GROUPS = 8
SG_CH = 128
SG_CHUNK = 128
TOP_K = 4
SWIGLU_LIMIT = 7.0
SWIGLU_ALPHA = 1.702
LN_EPS = 1e-5
NORM_EPS = 1e-6
QK_W = DN_HEADS * DN_DK
V_W = DN_HEADS * DN_DV
QKV_W = 2 * QK_W + V_W
SG_W = SG_GROUPS * SG_CH

LANES = 128
SUBLANES = 8
VMEM_LIMIT = 56 * 1024 * 1024

PROJ_TM = 512
PROJ_TN = 1536
MERGE_TM = 256
MOE_TM = 256
MOE_TF = 512
MOE_SUBS = 6
COMB_TQ = 128
SAMPLE_BT = 4
SAMPLE_TP = 8
GDN_HG = 4

MOD_SHIFT1, MOD_SCALE1, MOD_GATE1, MOD_SHIFT2, MOD_SCALE2, MOD_GATE2 = range(6)
MOD_PARTS = 6


def _sigmoid(x):
    return 1.0 / (1.0 + jnp.exp(-x))


def _silu(x):
    return x * _sigmoid(x)


def _dot(a, b):
    return jnp.dot(a.astype(BF16), b.astype(BF16), preferred_element_type=F32)


def _dot_nt(a, b):
    return lax.dot_general(a.astype(BF16), b.astype(BF16), (((1,), (1,)), ((), ())),
                           preferred_element_type=F32)


def _dot_hi(a, b):
    return jnp.dot(a, b, precision=HIGHEST, preferred_element_type=F32)


def _layer_norm(x, g, b):
    mu = jnp.mean(x, axis=-1, keepdims=True)
    xc = x - mu
    var = jnp.mean(xc * xc, axis=-1, keepdims=True)
    return xc * lax.rsqrt(var + LN_EPS) * g + b


def _params(*sem):
    return pltpu.CompilerParams(dimension_semantics=sem, vmem_limit_bytes=VMEM_LIMIT)


class _Rows:
    def __init__(self, tm, rows_p, rows_s, seq):
        assert rows_p % tm == 0 and rows_s % tm == 0 and seq % tm == 0, (tm, rows_p, rows_s, seq)
        self.tm, self.n_p, self.n_s = tm, rows_p // tm, rows_s // tm
        self.tiles_per_seq = seq // tm
        self.bp = rows_p // seq

    @property
    def n(self):
        return self.n_p + self.n_s

    def prompt(self, width, col=0):
        return pl.BlockSpec((self.tm, width), lambda i, *_: (jnp.minimum(i, self.n_p - 1), col))

    def sample(self, width, col=0, single=True):
        mode = dict(pipeline_mode=pl.Buffered(1)) if single else {}
        return pl.BlockSpec((self.tm, width), lambda i, *_: (jnp.maximum(i - self.n_p, 0), col), **mode)

    def joint(self, width, col=0):
        return pl.BlockSpec((self.tm, width), lambda i, *_: (i, col))

    def _seq(self, i):
        return jnp.minimum(i // self.tiles_per_seq, self.bp - 1)

    def seq_vec(self, d, part):
        return pl.BlockSpec((SUBLANES, d), lambda i, *_: (self._seq(i) // SUBLANES, part))

    def row_vec(self, d, part):
        return pl.BlockSpec((self.tm, d), lambda i, *_: (jnp.maximum(i - self.n_p, 0), part),
                            pipeline_mode=pl.Buffered(1))

    def pick_vec(self, i, seq_ref, row_ref):
        prompt = seq_ref[pl.ds(self._seq(i) % SUBLANES, 1), :]
        return jnp.where(i >= self.n_p, row_ref[...], prompt)


def _pick(is_sample, prompt_ref, sample_ref):
    return jnp.where(is_sample, sample_ref[...], prompt_ref[...])


def _adaln_kernel(c_ref, w_ref, b_ref, o_ref):
    o_ref[...] = _dot(_silu(c_ref[...]), w_ref[...]) + b_ref[...]


def _adaln(c, w, b):
    rows, d = c.shape
    n = w.shape[1]
    tn = 1024
    return pl.pallas_call(
        _adaln_kernel,
        grid=(n // tn,),
        in_specs=[pl.BlockSpec((rows, d), lambda j: (0, 0)),
                  pl.BlockSpec((d, tn), lambda j: (0, j)),
                  pl.BlockSpec((1, tn), lambda j: (0, j))],
        out_specs=pl.BlockSpec((rows, tn), lambda j: (0, j)),
        out_shape=jax.ShapeDtypeStruct((rows, n), F32),
        compiler_params=_params("arbitrary"),
        name="adaln",
    )(c, w, b.reshape(1, n))


def _inproj_kernel(rt, xp_ref, xs_ref, scp_ref, scs_ref, shp_ref, shs_ref, w_ref, o_ref, h_scr):
    @pl.when(pl.program_id(1) == 0)
    def _():
        i = pl.program_id(0)
        x = _pick(i >= rt.n_p, xp_ref, xs_ref)
        h_scr[...] = (x * (1.0 + rt.pick_vec(i, scp_ref, scs_ref)) + rt.pick_vec(i, shp_ref, shs_ref)).astype(BF16)

    o_ref[...] = jnp.dot(h_scr[...], w_ref[...], preferred_element_type=F32)


def _inproj(rt, xp, xs, mod_seq, mod_row, w):
    d = xp.shape[1]
    nw = w.shape[1]
    tm = rt.tm
    return pl.pallas_call(
        functools.partial(_inproj_kernel, rt),
        grid=(rt.n, nw // PROJ_TN),
        in_specs=[rt.prompt(d), rt.sample(d), rt.seq_vec(d, MOD_SCALE1), rt.row_vec(d, MOD_SCALE1),
                  rt.seq_vec(d, MOD_SHIFT1), rt.row_vec(d, MOD_SHIFT1),
                  pl.BlockSpec((d, PROJ_TN), lambda i, j: (0, j))],
        out_specs=pl.BlockSpec((tm, PROJ_TN), lambda i, j: (i, j)),
        out_shape=jax.ShapeDtypeStruct((rt.n * tm, nw), F32),
        scratch_shapes=[pltpu.VMEM((tm, d), BF16)],
        compiler_params=_params("arbitrary", "arbitrary"),
        name="inproj",
    )(xp, xs, mod_seq, mod_row, mod_seq, mod_row, w)


def _softplus(x):
    return jnp.maximum(x, 0.0) + jnp.log1p(jnp.exp(-jnp.abs(x)))


def _split(a):
    hi = a.astype(BF16)
    lo = (a - hi.astype(F32)).astype(BF16)
    return hi, lo


def _dot3(a, b):
    lhs = jnp.concatenate([a[0], a[1], a[0]], axis=1)
    rhs = jnp.concatenate([b[0], b[0], b[1]], axis=0)
    return jnp.dot(lhs, rhs, preferred_element_type=F32)


def _map(f, *lists):
    return [f(*args) for args in zip(*lists)]


def _unit_lower_inverses_minus_eye(mats, rows, cols):
    same = (rows // 16) == (cols // 16)
    n = [jnp.where(same, a, 0.0) for a in mats]
    b = _map(lambda a, x: a - x, mats, n)
    n2 = _map(lambda x: _dot(x, x), n)
    n4 = _map(lambda x: _dot(x, x), n2)
    n8 = _map(lambda x: _dot(x, x), n4)
    r = [-x for x in n]
    r = _map(lambda x, p: x + p + _dot(x, p), r, n2)
    r = _map(lambda x, p: x + p + _dot(x, p), r, n4)
    dm = _map(lambda x, p: x + p + _dot(x, p), r, n8)
    m = _map(lambda x, y: y + _dot(x, y), dm, b)
    m2 = _map(lambda x: _dot(x, x), m)
    xm = _map(lambda x, p: x + p + _dot(p, x), dm, m2)
    return _map(lambda x, p: x - p - _dot(p, x), xm, m)


def _gdn_prompt_kernel(qkv_ref, z_ref, ab_ref, hist_ref, s0_ref, wconv_ref, alog_ref, dt_ref, og_ref,
                       o_ref, conv_ref, s_ref, xp_scr):
    n = pl.program_id(1)
    c = DN_CHUNK
    pad = SUBLANES
    nr = GDN_HG * c
    ngrp = DN_HEADS // GDN_HG

    @pl.when(n == 0)
    def _():
        xp_scr[pl.ds(0, pad), :] = jnp.zeros((pad, QKV_W), F32)
        xp_scr[pl.ds(pad - (DN_CONV - 1), DN_CONV - 1), :] = hist_ref[...]
        s_ref[...] = s0_ref[...]

    x = qkv_ref[...]
    xp_scr[pl.ds(pad, c), :] = x
    y = jnp.zeros((c, QKV_W), F32)
    for j in range(DN_CONV):
        y = y + xp_scr[pl.ds(pad - (DN_CONV - 1) + j, c), :] * wconv_ref[pl.ds(j, 1), :]
    y = _silu(y)
    tail = xp_scr[pl.ds(c + pad - (DN_CONV - 1), DN_CONV - 1), :]
    conv_ref[...] = tail
    xp_scr[pl.ds(pad - (DN_CONV - 1), DN_CONV - 1), :] = tail

    ab = ab_ref[...]
    g = -jnp.exp(alog_ref[...]) * _softplus(ab + dt_ref[...])
    beta_all = _sigmoid(ab)
    r64 = lax.broadcasted_iota(jnp.int32, (c, c), 0)
    c64 = lax.broadcasted_iota(jnp.int32, (c, c), 1)
    gc = _dot_hi((r64 >= c64).astype(F32), g)
    z = z_ref[...]
    og = og_ref[...]

    rows = lax.broadcasted_iota(jnp.int32, (nr, nr), 0)
    cols = lax.broadcasted_iota(jnp.int32, (nr, nr), 1)
    same = (rows // c) == (cols // c)
    causal = same & (rows >= cols)
    strict = same & (rows > cols)
    rowhead = lax.broadcasted_iota(jnp.int32, (nr, 1), 0) // c
    groups = [range(grp * GDN_HG, (grp + 1) * GDN_HG) for grp in range(ngrp)]

    def stack(heads, off, width):
        return jnp.concatenate([y[:, off + h * width:off + (h + 1) * width] for h in heads], axis=0)

    def l2n(t):
        return t * lax.rsqrt(jnp.sum(t * t, axis=-1, keepdims=True) + NORM_EPS)

    q = [l2n(stack(hs, 0, DN_DK)) * (DN_DK ** -0.5) for hs in groups]
    k = [l2n(stack(hs, QK_W, DN_DK)) for hs in groups]
    v = [stack(hs, 2 * QK_W, DN_DV) for hs in groups]
    beta = [jnp.concatenate([beta_all[:, DN_HEADS + h:DN_HEADS + h + 1] for h in hs], axis=0) for hs in groups]
    gcf = [jnp.concatenate([jnp.broadcast_to(gc[:, h:h + 1], (c, LANES)) for h in hs], axis=0) for hs in groups]
    gcc = [t[:, 0:1] for t in gcf]
    grow = [t.T[0:1, :] for t in gcf]
    glast = [[gc[c - 1:c, h:h + 1] for h in hs] for hs in groups]
    gtot = [jnp.concatenate([jnp.broadcast_to(t, (c, 1)) for t in gl], axis=0) for gl in glast]
    decay = _map(lambda a, b: jnp.where(causal, jnp.exp(jnp.where(causal, a - b, 0.0)), 0.0), gcc, grow)
    egc = _map(jnp.exp, gcc)
    kb = _map(lambda a, b: a * b, k, beta)
    a_mat = _map(lambda a, b, dd: jnp.where(strict, _dot_nt(a, b) * dd, 0.0), kb, k, decay)
    tm1 = _unit_lower_inverses_minus_eye(a_mat, rows, cols)
    u = _map(lambda t, a, b: a * b + _dot(t, a * b), tm1, v, beta)
    w = _map(lambda t, a, b: a * b + _dot(t, a * b), tm1, kb, egc)
    attn = _map(lambda a, b, dd: _dot_nt(a, b) * dd, q, k, decay)
    qg = _map(lambda a, b: a * b, q, egc)
    kd_t = _map(lambda a, b, cc: (a * jnp.exp(b - cc)).T, k, gtot, gcc)

    for gi, hs in enumerate(groups):
        v_news, qss = [], []
        for hl, h in enumerate(hs):
            sl = slice(hl * c, (hl + 1) * c)
            rs = _dot(jnp.concatenate([w[gi][sl], qg[gi][sl]], axis=0), s_ref[h])
            v_news.append(u[gi][sl] - rs[:c])
            qss.append(rs[c:])
        v_new = jnp.concatenate(v_news, axis=0)
        o = jnp.concatenate(qss, axis=0) + _dot(attn[gi], v_new)
        for hl, h in enumerate(hs):
            upd = _dot(kd_t[gi], jnp.where(rowhead == hl, v_new, 0.0))
            s_ref[h] = s_ref[h] * jnp.exp(glast[gi][hl]) + upd
        o = o * lax.rsqrt(jnp.mean(o * o, axis=-1, keepdims=True) + NORM_EPS) * og
        for hl, h in enumerate(hs):
            o_ref[:, h * DN_DV:(h + 1) * DN_DV] = o[hl * c:(hl + 1) * c] * _silu(z[:, h * DN_DV:(h + 1) * DN_DV])


def _gdn_prompt(proj, hist, s0, wconv, alog, dtb, og, batch, seq, col_z, col_ab):
    c = DN_CHUNK
    nchunk = seq // c
    rows = batch * seq
    row = lambda b, n: b * nchunk + n
    return pl.pallas_call(
        _gdn_prompt_kernel,
        grid=(batch, nchunk),
        in_specs=[pl.BlockSpec((c, QKV_W), lambda b, n: (row(b, n), 0)),
                  pl.BlockSpec((c, V_W), lambda b, n: (row(b, n), col_z // V_W)),
                  pl.BlockSpec((c, LANES), lambda b, n: (row(b, n), col_ab // LANES)),
                  pl.BlockSpec((None, DN_CONV - 1, QKV_W), lambda b, n: (b, 0, 0)),
                  pl.BlockSpec((None, DN_HEADS, DN_DK, DN_DV), lambda b, n: (b, 0, 0, 0)),
                  pl.BlockSpec((DN_CONV, QKV_W), lambda b, n: (0, 0)),
                  pl.BlockSpec((1, LANES), lambda b, n: (0, 0)),
                  pl.BlockSpec((1, LANES), lambda b, n: (0, 0)),
                  pl.BlockSpec((1, DN_DV), lambda b, n: (0, 0))],
        out_specs=[pl.BlockSpec((c, V_W), lambda b, n: (row(b, n), 0)),
                   pl.BlockSpec((None, DN_CONV - 1, QKV_W), lambda b, n: (b, 0, 0)),
                   pl.BlockSpec((None, DN_HEADS, DN_DK, DN_DV), lambda b, n: (b, 0, 0, 0))],
        out_shape=[jax.ShapeDtypeStruct((rows, V_W), F32),
                   jax.ShapeDtypeStruct((batch, DN_CONV - 1, QKV_W), F32),
                   jax.ShapeDtypeStruct((batch, DN_HEADS, DN_DK, DN_DV), F32)],
        scratch_shapes=[pltpu.VMEM((c + SUBLANES, QKV_W), F32)],
        compiler_params=_params("arbitrary", "arbitrary"),
        name="gdn_prompt",
    )(proj, proj, proj, hist, s0, wconv, alog, dtb, og)


def _gdn_sample_kernel(qkv_ref, z_ref, ab_ref, hist_ref, s0_ref, wconv_ref, alog_ref, dt_ref, og_ref,
                       o_ref, conv_ref, s_ref, xp_scr, ab_scr, wq_scr, r_scr, kdt_scr, vn_scr, gl_scr):
    bt, tp, lt = SAMPLE_BT, SAMPLE_TP, DN_CONV
    nblk = DN_HEADS * bt
    nrow = nblk * tp
    hist_rows = DN_CONV - 1

    xp_scr[...] = jnp.zeros(xp_scr.shape, F32)
    ab_scr[...] = jnp.zeros(ab_scr.shape, F32)
    ys = []
    for b in range(bt):
        xp_scr[b, pl.ds(SUBLANES - hist_rows, hist_rows), :] = hist_ref[b]
        xp_scr[b, pl.ds(SUBLANES, lt), :] = qkv_ref[pl.ds(b * lt, lt), :]
        yb = jnp.zeros((tp, QKV_W), F32)
        for j in range(DN_CONV):
            yb = yb + xp_scr[b, pl.ds(SUBLANES - hist_rows + j, tp), :] * wconv_ref[pl.ds(j, 1), :]
        ys.append(_silu(yb))
        conv_ref[b] = xp_scr[b, pl.ds(SUBLANES + lt - hist_rows, hist_rows), :]
        ab_scr[b, pl.ds(0, lt), :] = ab_ref[pl.ds(b * lt, lt), :]
    y = jnp.concatenate(ys, axis=0)
    ab = jnp.concatenate([ab_scr[b] for b in range(bt)], axis=0)
    tok = lax.broadcasted_iota(jnp.int32, (bt * tp, 1), 0) % tp
    real = tok < lt
    g_all = jnp.where(real, -jnp.exp(alog_ref[...]) * _softplus(ab + dt_ref[...]), 0.0)
    beta_all = jnp.where(real, _sigmoid(ab), 0.0)

    def heads_to_rows(t, off, width):
        return jnp.concatenate([t[:, off + h * width:off + (h + 1) * width] for h in range(DN_HEADS)], axis=0)

    realr = jnp.concatenate([real] * DN_HEADS, axis=0)
    q = jnp.where(realr, heads_to_rows(y, 0, DN_DK), 0.0)
    k = jnp.where(realr, heads_to_rows(y, QK_W, DN_DK), 0.0)
    v = jnp.where(realr, heads_to_rows(y, 2 * QK_W, DN_DV), 0.0)
    gcol = jnp.concatenate([g_all[:, h:h + 1] for h in range(DN_HEADS)], axis=0)
    beta = jnp.concatenate([beta_all[:, DN_HEADS + h:DN_HEADS + h + 1] for h in range(DN_HEADS)], axis=0)
    q = q * lax.rsqrt(jnp.sum(q * q, axis=-1, keepdims=True) + NORM_EPS) * (DN_DK ** -0.5)
    k = k * lax.rsqrt(jnp.sum(k * k, axis=-1, keepdims=True) + NORM_EPS)

    rows = lax.broadcasted_iota(jnp.int32, (nrow, nrow), 0)
    cols = lax.broadcasted_iota(jnp.int32, (nrow, nrow), 1)
    same = (rows // tp) == (cols // tp)
    causal = same & (rows >= cols)
    strict = same & (rows > cols)
    gfull = jnp.broadcast_to(gcol, (nrow, LANES))
    gc = _dot_hi(causal.astype(F32), gfull)
    gtot = _dot_hi(same.astype(F32), gfull)
    gcc = gc[:, 0:1]
    grow = gc.T[0:1, :]
    decay = jnp.where(causal, jnp.exp(jnp.where(causal, gcc - grow, 0.0)), 0.0)
    egc = jnp.exp(gcc)
    kb = k * beta
    a_mat = jnp.where(strict, _dot_nt(kb, k) * decay, 0.0)
    a2 = _dot(a_mat, a_mat)
    tm1 = a2 - a_mat - _dot(a_mat, a2)
    u = v * beta + _dot(tm1, v * beta)
    w = kb * egc + _dot(tm1, kb * egc)
    attn = _dot_nt(q, k) * decay
    qg = q * egc
    for i in range(nblk):
        wq_scr[pl.ds(2 * tp * i, tp), :] = w[i * tp:(i + 1) * tp, :]
        wq_scr[pl.ds(2 * tp * i + tp, tp), :] = qg[i * tp:(i + 1) * tp, :]
    kdt_scr[...] = (k * jnp.exp(gtot[:, 0:1] - gcc)).T
    gl_scr[...] = jnp.exp(gtot)

    def read_state(i, carry):
        r0 = pl.multiple_of(i * 2 * tp, 2 * tp)
        r_scr[pl.ds(r0, 2 * tp), :] = _dot(wq_scr[pl.ds(r0, 2 * tp), :], s0_ref[i % bt, i // bt])
        return carry

    lax.fori_loop(0, nblk, read_state, 0, unroll=4)
    ws = jnp.concatenate([r_scr[pl.ds(2 * tp * i, tp), :] for i in range(nblk)], axis=0)
    qs = jnp.concatenate([r_scr[pl.ds(2 * tp * i + tp, tp), :] for i in range(nblk)], axis=0)
    v_new = u - ws
    vn_scr[...] = v_new
    o = qs + _dot(attn, v_new)
    blockid = lax.broadcasted_iota(jnp.int32, (nrow, 1), 0) // tp

    def write_state(i, carry):
        r0 = pl.multiple_of(i * tp, tp)
        upd = _dot(kdt_scr[...], jnp.where(blockid == i, vn_scr[...], 0.0))
        s_ref[i % bt, i // bt] = s0_ref[i % bt, i // bt] * gl_scr[pl.ds(r0, 1), :] + upd
        return carry

    lax.fori_loop(0, nblk, write_state, 0, unroll=4)

    og = og_ref[...]
    o = o * lax.rsqrt(jnp.mean(o * o, axis=-1, keepdims=True) + NORM_EPS) * og
    for b in range(bt):
        for h in range(DN_HEADS):
            zbh = z_ref[pl.ds(b * lt, lt), h * DN_DV:(h + 1) * DN_DV]
            blk = o[(h * bt + b) * tp:(h * bt + b) * tp + lt, :]
            o_ref[pl.ds(b * lt, lt), h * DN_DV:(h + 1) * DN_DV] = blk * _silu(zbh)


def _gdn_sample(proj, row0, hist, s0, wconv, alog, dtb, og, batch, col_z, col_ab):
    bt, tp, lt = SAMPLE_BT, SAMPLE_TP, DN_CONV
    rows = batch * lt
    nrow = DN_HEADS * bt * tp
    blk = bt * lt
    assert row0 % blk == 0 and batch % bt == 0
    b0 = row0 // blk
    return pl.pallas_call(
        _gdn_sample_kernel,
        grid=(batch // bt,),
        in_specs=[pl.BlockSpec((blk, QKV_W), lambda i: (b0 + i, 0)),
                  pl.BlockSpec((blk, V_W), lambda i: (b0 + i, col_z // V_W)),
                  pl.BlockSpec((blk, LANES), lambda i: (b0 + i, col_ab // LANES)),
                  pl.BlockSpec((bt, DN_CONV - 1, QKV_W), lambda i: (i, 0, 0)),
                  pl.BlockSpec((bt, DN_HEADS, DN_DK, DN_DV), lambda i: (i, 0, 0, 0)),
                  pl.BlockSpec((DN_CONV, QKV_W), lambda i: (0, 0)),
                  pl.BlockSpec((1, LANES), lambda i: (0, 0)),
                  pl.BlockSpec((1, LANES), lambda i: (0, 0)),
                  pl.BlockSpec((1, DN_DV), lambda i: (0, 0))],
        out_specs=[pl.BlockSpec((blk, V_W), lambda i: (i, 0)),
                   pl.BlockSpec((bt, DN_CONV - 1, QKV_W), lambda i: (i, 0, 0)),
                   pl.BlockSpec((bt, DN_HEADS, DN_DK, DN_DV), lambda i: (i, 0, 0, 0))],
        out_shape=[jax.ShapeDtypeStruct((rows, V_W), F32),
                   jax.ShapeDtypeStruct((batch, DN_CONV - 1, QKV_W), F32),
                   jax.ShapeDtypeStruct((batch, DN_HEADS, DN_DK, DN_DV), F32)],
        scratch_shapes=[pltpu.VMEM((bt, SUBLANES + tp, QKV_W), F32),
                        pltpu.VMEM((bt, tp, LANES), F32),
                        pltpu.VMEM((2 * nrow, DN_DK), F32),
                        pltpu.VMEM((2 * nrow, DN_DV), F32),
                        pltpu.VMEM((DN_DK, nrow), F32),
                        pltpu.VMEM((nrow, DN_DV), F32),
                        pltpu.VMEM((nrow, LANES), F32)],
        compiler_params=_params("arbitrary"),
        name="gdn_sample",
    )(proj, proj, proj, hist, s0, wconv, alog, dtb, og)


def _sgu_kernel(n_p, u_ref, v_ref, g_ref, b_ref, wsp_ref, wss_ref, bp_ref, bs_ref, o_ref, vn_ref):
    is_s = pl.program_id(0) >= n_p
    u = jax.nn.gelu(u_ref[...])
    vn = _layer_norm(jax.nn.gelu(v_ref[...]), g_ref[...], b_ref[...])
    vn_ref[...] = vn
    bias = _pick(is_s, bp_ref, bs_ref)
    for g in range(SG_GROUPS):
        sl = slice(g * SG_CH, (g + 1) * SG_CH)
        ws = jnp.where(is_s, wss_ref[g], wsp_ref[g])
        mixed = _dot(ws, vn[:, sl]) + bias[:, sl]
        o_ref[:, sl] = u[:, sl] * mixed


def _sgu(rt, proj, ln_g, ln_b, ws_p, ws_s, bias_p, bias_s, col_u, col_v):
    t = rt.tm
    const = lambda shape: pl.BlockSpec(shape, lambda i: (0,) * len(shape))
    return pl.pallas_call(
        functools.partial(_sgu_kernel, rt.n_p),
        grid=(rt.n,),
        in_specs=[rt.joint(SG_W, col_u // SG_W), rt.joint(SG_W, col_v // SG_W),
                  const((1, SG_W)), const((1, SG_W)),
                  const((SG_GROUPS, t, t)), const((SG_GROUPS, t, t)), const((t, SG_W)), const((t, SG_W))],
        out_specs=[rt.joint(SG_W), rt.sample(SG_W, single=False)],
        out_shape=[jax.ShapeDtypeStruct((rt.n * t, SG_W), F32),
                   jax.ShapeDtypeStruct((rt.n_s * t, SG_W), F32)],
        compiler_params=_params("arbitrary"),
        name="sgu",
    )(proj, proj, ln_g, ln_b, ws_p, ws_s, bias_p, bias_s)


def _merge_kernel(alpha, n_experts, rt, oap_ref, oas_ref, ob_ref, ga_ref, gb_ref, xp_ref, xs_ref,
                  gtp_ref, gts_ref, scp_ref, scs_ref, shp_ref, shs_ref,
                  pa_ref, pb_ref, wo_ref, lg_ref, lb_ref, rw_ref, rb_ref,
                  x1_ref, h2_ref, ti_ref, tg_ref):
    i = pl.program_id(0)
    is_s = i >= rt.n_p
    oa = _pick(is_s, oap_ref, oas_ref)
    merged = (_sigmoid(ga_ref[...]) * _dot(oa, pa_ref[...])
              + _sigmoid(gb_ref[...]) * _dot(ob_ref[...], pb_ref[...]))
    y = _dot(merged, wo_ref[...])
    x = _pick(is_s, xp_ref, xs_ref)
    x1 = _layer_norm(alpha * x + rt.pick_vec(i, gtp_ref, gts_ref) * y, lg_ref[...], lb_ref[...])
    x1_ref[...] = x1
    h2 = x1 * (1.0 + rt.pick_vec(i, scp_ref, scs_ref)) + rt.pick_vec(i, shp_ref, shs_ref)
    h2_ref[...] = h2
    hs = _split(h2)
    logits = jnp.dot(jnp.concatenate([hs[0], hs[1], hs[0]], axis=1), rw_ref[...],
                     preferred_element_type=F32) + rb_ref[...]
    lane = lax.broadcasted_iota(jnp.int32, logits.shape, 1)
    logits = jnp.where(lane < n_experts, logits, -jnp.inf)
    ti = jnp.zeros(logits.shape, jnp.int32)
    tv = jnp.zeros(logits.shape, F32)
    top = None
    for kk in range(TOP_K):
        m = jnp.max(logits, axis=-1, keepdims=True)
        idx = jnp.min(jnp.where(logits == m, lane.astype(F32), float(LANES)), axis=-1,
                      keepdims=True).astype(jnp.int32)
        if kk == 0:
            top = m
        ti = jnp.where(lane == kk, idx, ti)
        tv = jnp.where(lane == kk, jnp.exp(m - top), tv)
        logits = jnp.where(lane == idx, -jnp.inf, logits)
    ti_ref[...] = ti
    tg_ref[...] = tv / jnp.sum(tv, axis=-1, keepdims=True)


def _merge(alpha, n_experts, rt, oa_p, oa_s, ob, proj, xp, xs, mod_seq, mod_row, pa, pb, wo, lg, lb, rw, rb,
           col_ga, col_gb):
    d = xp.shape[1]
    tm = rt.tm
    rows = rt.n * tm
    const = lambda shape: pl.BlockSpec(shape, lambda i: (0,) * len(shape), pipeline_mode=pl.Buffered(1))
    return pl.pallas_call(
        functools.partial(_merge_kernel, alpha, n_experts, rt),
        grid=(rt.n,),
        in_specs=[rt.prompt(V_W), rt.sample(V_W), rt.joint(SG_W),
                  rt.joint(d, col_ga // d), rt.joint(d, col_gb // d),
                  rt.prompt(d), rt.sample(d),
                  rt.seq_vec(d, MOD_GATE1), rt.row_vec(d, MOD_GATE1),
                  rt.seq_vec(d, MOD_SCALE2), rt.row_vec(d, MOD_SCALE2),
                  rt.seq_vec(d, MOD_SHIFT2), rt.row_vec(d, MOD_SHIFT2),
                  const((V_W, d)), const((SG_W, d)), const((d, d)),
                  const((1, d)), const((1, d)), const((3 * d, LANES)), const((1, LANES))],
        out_specs=[rt.joint(d), rt.joint(d), rt.joint(LANES), rt.joint(LANES)],
        out_shape=[jax.ShapeDtypeStruct((rows, d), F32),
                   jax.ShapeDtypeStruct((rows, d), F32),
                   jax.ShapeDtypeStruct((rows, LANES), jnp.int32),
                   jax.ShapeDtypeStruct((rows, LANES), F32)],
        compiler_params=_params("arbitrary"),
        name="merge",
    )(oa_p, oa_s, ob, proj, proj, xp, xs, mod_seq, mod_row, mod_seq, mod_row, mod_seq, mod_row,
      pa, pb, wo, lg, lb, rw, rb)


FILL_ROWS = (128, 64, 32, 16, 8)


def _dispatch_kernel(tm, n_experts, padlo_ref, padlen_ref, nblk_ref, dest_ref, h_ref, xb_hbm, zero_scr, sem, zsem):
    i = pl.program_id(0)
    tq = h_ref.shape[0]
    nb_max = xb_hbm.shape[0] // tm
    zrows = zero_scr.shape[0]

    def fill(act):
        def per_expert(e, carry):
            lo = padlo_ref[e]
            ln = padlen_ref[e]
            head = (SUBLANES - lo % SUBLANES) % SUBLANES
            for r in range(SUBLANES - 1):
                @pl.when(r < head)
                def _(r=r):
                    act(pltpu.make_async_copy(zero_scr.at[pl.ds(0, 1)], xb_hbm.at[pl.ds(lo + r, 1)], zsem))
            off = lo + head
            rem = ln - head
            for b in FILL_ROWS:
                @pl.when((rem & b) != 0)
                def _(off=off, b=b):
                    dst = xb_hbm.at[pl.ds(pl.multiple_of(off, SUBLANES), b)]
                    act(pltpu.make_async_copy(zero_scr.at[pl.ds(0, b)], dst, zsem))
                off = off + (rem & b)
            return carry

        lax.fori_loop(0, n_experts, per_expert, 0)

        def per_block(blk, carry):
            for part in range(tm // zrows):
                row0 = pl.multiple_of(blk * tm + part * zrows, zrows)
                act(pltpu.make_async_copy(zero_scr, xb_hbm.at[pl.ds(row0, zrows)], zsem))
            return carry

        lax.fori_loop(nblk_ref[0], nb_max, per_block, 0)

    @pl.when(i == 0)
    def _():
        zero_scr[...] = jnp.zeros(zero_scr.shape, F32)
        fill(lambda cp: cp.start())
        fill(lambda cp: cp.wait())

    def start(t, carry):
        for kk in range(TOP_K):
            pltpu.make_async_copy(h_ref.at[pl.ds(t, 1)], xb_hbm.at[pl.ds(dest_ref[0, t * TOP_K + kk], 1)],
                                  sem).start()
        return carry

    lax.fori_loop(0, tq, start, 0, unroll=4)
    for kk in range(TOP_K):
        pltpu.make_async_copy(h_ref, xb_hbm.at[pl.ds(0, tq)], sem).wait()


def _dispatch(padlo, padlen, nblk_used, dest, h2, tm, nb_max):
    t, d = h2.shape
    tq = math.gcd(256, t)
    n_experts = padlo.shape[0]
    grid_spec = pltpu.PrefetchScalarGridSpec(
        num_scalar_prefetch=3,
        grid=(t // tq,),
        in_specs=[pl.BlockSpec((None, 1, tq * TOP_K), lambda i, *_: (i, 0, 0), memory_space=pltpu.SMEM),
                  pl.BlockSpec((tq, d), lambda i, *_: (i, 0))],
        out_specs=pl.BlockSpec(memory_space=pl.ANY),
        scratch_shapes=[pltpu.VMEM((FILL_ROWS[0], d), F32), pltpu.SemaphoreType.DMA(()),
                        pltpu.SemaphoreType.DMA(())],
    )
    return pl.pallas_call(
        functools.partial(_dispatch_kernel, tm, n_experts),
        grid_spec=grid_spec,
        out_shape=jax.ShapeDtypeStruct((nb_max * tm, d), F32),
        compiler_params=pltpu.CompilerParams(dimension_semantics=("arbitrary",), vmem_limit_bytes=VMEM_LIMIT,
                                             has_side_effects=True),
        name="dispatch",
    )(padlo, padlen, nblk_used.reshape(1), dest.reshape(t // tq, 1, tq * TOP_K), h2)


def _moe_kernel(nj, e_ref, j_ref, xb_ref, ob_ref, r_ref, flag_ref, wg_code_ref, wu_code_ref,
                x_ref, wg_ref, wu_ref, wd_ref, bg_ref, bu_ref, bd_ref, o_ref,
                wg_scr, wu_scr, wd_scr, acc_scr):
    s = pl.program_id(0)
    flags = flag_ref[s]
    j = j_ref[s]
    r = r_ref[s]

    @pl.when((flags & 2) != 0)
    def _():
        wg_scr[...] = wg_ref[...].astype(BF16)
        wu_scr[...] = wu_ref[...].astype(BF16)
        wd_scr[...] = wd_ref[...].astype(BF16)

    @pl.when(flags == 0)
    def _():
        o_ref[...] = jnp.zeros(o_ref.shape, F32)

    @pl.when((flags & 1) != 0)
    def _():
        x = x_ref[...].astype(BF16)
        gate = jnp.dot(x, wg_scr[...], preferred_element_type=F32) + bg_ref[...]
        up = jnp.dot(x, wu_scr[...], preferred_element_type=F32) + bu_ref[...]
        gate = jnp.minimum(gate, SWIGLU_LIMIT)
        up = jnp.clip(up, -SWIGLU_LIMIT, SWIGLU_LIMIT)
        act = (up + 1.0) * gate * _sigmoid(SWIGLU_ALPHA * gate)
        y = jnp.dot(act.astype(BF16), wd_scr[...], preferred_element_type=F32)

        @pl.when(j == 0)
        def _():
            acc_scr[r] = y

        @pl.when(jnp.logical_and(j > 0, j < nj - 1))
        def _():
            acc_scr[r] = acc_scr[r] + y

        @pl.when(j == nj - 1)
        def _():
            o_ref[...] = acc_scr[r] + y + bd_ref[...]


def _moe(items, xb, w_gu, b_gu, w_dn, b_dn, tm, tf, subs):
    item_e, item_j, item_xb, item_ob, item_r, item_flag, wg_code, wu_code = items
    n_items = item_e.shape[0]
    nslot, d = xb.shape
    n_exp, _, f2 = w_gu.shape
    f = f2 // 2
    nj = f // tf
    assert nj >= 2
    grid_spec = pltpu.PrefetchScalarGridSpec(
        num_scalar_prefetch=8,
        grid=(n_items,),
        in_specs=[pl.BlockSpec((tm, d), lambda s, e, j, xbk, obk, r, fl, cg, cu: (xbk[s], 0)),
                  pl.BlockSpec((None, d, tf), lambda s, e, j, xbk, obk, r, fl, cg, cu: (cg[s] // nj, 0, cg[s] % nj)),
                  pl.BlockSpec((None, d, tf),
                               lambda s, e, j, xbk, obk, r, fl, cg, cu: (cu[s] // nj, 0, nj + cu[s] % nj)),
                  pl.BlockSpec((None, tf, d), lambda s, e, j, xbk, obk, r, fl, cg, cu: (e[s], j[s], 0)),
                  pl.BlockSpec((None, 1, tf), lambda s, e, j, xbk, obk, r, fl, cg, cu: (e[s], 0, j[s])),
                  pl.BlockSpec((None, 1, tf), lambda s, e, j, xbk, obk, r, fl, cg, cu: (e[s], 0, nj + j[s])),
                  pl.BlockSpec((None, 1, d), lambda s, e, j, xbk, obk, r, fl, cg, cu: (e[s], 0, 0))],
        out_specs=pl.BlockSpec((tm, d), lambda s, e, j, xbk, obk, r, fl, cg, cu: (obk[s], 0)),
        scratch_shapes=[pltpu.VMEM((d, tf), BF16), pltpu.VMEM((d, tf), BF16), pltpu.VMEM((tf, d), BF16),
                        pltpu.VMEM((subs, tm, d), F32)],
    )
    return pl.pallas_call(
        functools.partial(_moe_kernel, nj),
        grid_spec=grid_spec,
        out_shape=jax.ShapeDtypeStruct((nslot, d), F32),
        compiler_params=_params("arbitrary"),
        name="moe",
    )(item_e, item_j, item_xb, item_ob, item_r, item_flag, wg_code, wu_code,
      xb, w_gu, w_gu, w_dn, b_gu.reshape(n_exp, 1, f2), b_gu.reshape(n_exp, 1, f2), b_dn.reshape(n_exp, 1, d))


def _combine_kernel(alpha, rt, dest_ref, destn_ref, yb_hbm, tg_ref, x1_ref, gtp_ref, gts_ref, lg_ref, lb_ref,
                    op_ref, os_ref, buf, sem):
    i = pl.program_id(0)
    last = pl.num_programs(0) - 1
    tq = x1_ref.shape[0]
    slot = i % 2

    def start(ids_ref, sl, t):
        for kk in range(TOP_K):
            pltpu.make_async_copy(yb_hbm.at[pl.ds(ids_ref[0, t * TOP_K + kk], 1)],
                                  buf.at[sl, kk, pl.ds(t, 1)], sem.at[sl]).start()

    def wait(sl):
        for kk in range(TOP_K):
            pltpu.make_async_copy(yb_hbm.at[pl.ds(0, tq)], buf.at[sl, kk], sem.at[sl]).wait()

    @pl.when(i == 0)
    def _():
        lax.fori_loop(0, tq, lambda t, c: (start(dest_ref, 0, t), c)[1], 0, unroll=4)

    for t in range(tq):
        start(destn_ref, 1 - slot, t)
    wait(slot)
    tg = tg_ref[...]
    y = jnp.zeros(x1_ref.shape, F32)
    for kk in range(TOP_K):
        y = y + buf[slot, kk] * tg[:, kk:kk + 1]
    is_s = i >= rt.n_p
    out = _layer_norm(alpha * x1_ref[...] + rt.pick_vec(i, gtp_ref, gts_ref) * y, lg_ref[...], lb_ref[...])

    @pl.when(jnp.logical_not(is_s))
    def _():
        op_ref[...] = out

    @pl.when(is_s)
    def _():
        os_ref[...] = out

    @pl.when(i == last)
    def _():
        wait(1 - slot)


def _combine(alpha, rt, dest, yb, tg, x1, mod_seq, mod_row, lg, lb):
    d = x1.shape[1]
    tq = rt.tm
    ids = dest.reshape(rt.n, 1, tq * TOP_K)
    return pl.pallas_call(
        functools.partial(_combine_kernel, alpha, rt),
        grid=(rt.n,),
        in_specs=[pl.BlockSpec((None, 1, tq * TOP_K), lambda i: (i, 0, 0), memory_space=pltpu.SMEM),
                  pl.BlockSpec((None, 1, tq * TOP_K), lambda i: (jnp.minimum(i + 1, rt.n - 1), 0, 0),
                               memory_space=pltpu.SMEM),
                  pl.BlockSpec(memory_space=pl.ANY),
                  rt.joint(LANES), rt.joint(d), rt.seq_vec(d, MOD_GATE2), rt.row_vec(d, MOD_GATE2),
                  pl.BlockSpec((1, d), lambda i: (0, 0)),
                  pl.BlockSpec((1, d), lambda i: (0, 0))],
        out_specs=[rt.prompt(d), rt.sample(d, single=False)],
        out_shape=[jax.ShapeDtypeStruct((rt.n_p * tq, d), F32), jax.ShapeDtypeStruct((rt.n_s * tq, d), F32)],
        scratch_shapes=[pltpu.VMEM((2, TOP_K, tq, d), F32), pltpu.SemaphoreType.DMA((2,))],
        compiler_params=_params("arbitrary"),
        name="combine",
    )(ids, ids, yb, tg, x1, mod_seq, mod_row, lg, lb)


def _take(table, idx):
    hit = idx[:, None] == jnp.arange(table.shape[0], dtype=jnp.int32)[None, :]
    return jnp.sum(jnp.where(hit, table[None, :], 0), axis=1)


def _routing(ti, n_experts, tm, nj, subs):
    t = ti.shape[0]
    n_assign = t * TOP_K
    nb_max = n_assign // tm + n_experts
    onehot = (ti[:, :, None] == jnp.arange(n_experts, dtype=jnp.int32)[None, None, :]).astype(jnp.int32)
    per_tok = jnp.sum(onehot, axis=1)
    cum = jnp.cumsum(per_tok, axis=0)
    counts = cum[-1]
    rank = jnp.sum(onehot * cum[:, None, :], axis=2) - 1
    nblk = (counts + tm - 1) // tm
    blk_end = jnp.cumsum(nblk)
    blk_start = blk_end - nblk
    dest = jnp.sum(onehot * blk_start[None, None, :], axis=2) * tm + rank
    pad_lo = (blk_start * tm + counts).astype(jnp.int32)
    pad_len = (nblk * tm - counts).astype(jnp.int32)
    total_blk = blk_end[-1]
    blocks = jnp.arange(nb_max, dtype=jnp.int32)
    blk_e = jnp.minimum(jnp.sum((blk_end[None, :] <= blocks[:, None]).astype(jnp.int32), axis=1), n_experts - 1)
    start_b, nblk_b = _take(blk_start, blk_e), _take(nblk, blk_e)
    r_in_e = blocks - start_b
    g0 = start_b + (r_in_e // subs) * subs
    nsub = jnp.minimum(subs, nblk_b - (r_in_e // subs) * subs)
    p = jnp.arange(nb_max * nj, dtype=jnp.int32)
    bp = p // nj
    valid = bp < total_blk
    last = jnp.maximum(total_blk - 1, 0)
    bq = jnp.where(valid, bp, last)
    gq, nq, eq = _take(g0, bq), jnp.maximum(_take(nsub, bq), 1), _take(blk_e, bq)
    local = p - nj * gq
    jq = jnp.where(valid, local // nq, nj - 1)
    rq = jnp.where(valid, local % nq, nq - 1)
    item_xb = gq + rq
    item_ob = jnp.where(valid, jnp.where(jq == nj - 1, gq + rq, gq), bp)
    flags = valid.astype(jnp.int32) + 2 * (valid & (rq == 0)).astype(jnp.int32)
    nxt_blk = gq + nq
    has_next = nxt_blk < total_blk
    last_j = jq == nj - 1
    e_n = jnp.where(last_j & has_next, _take(blk_e, jnp.minimum(nxt_blk, nb_max - 1)), eq)
    j_n = jnp.where(last_j, jnp.where(has_next, 0, jq), jq + 1)
    cur, nxt = eq * nj + jq, e_n * nj + j_n
    wg_code = jnp.where(valid & (rq >= 1), nxt, cur)
    wu_code = jnp.where(valid & (rq >= 2), nxt, cur)
    items = (eq.astype(jnp.int32), jq.astype(jnp.int32), item_xb.astype(jnp.int32),
             item_ob.astype(jnp.int32), rq.astype(jnp.int32), flags,
             wg_code.astype(jnp.int32), wu_code.astype(jnp.int32))
    return dest.astype(jnp.int32), pad_lo, pad_len, total_blk.astype(jnp.int32), nb_max, items


def _win_prep_kernel(w_ref, o_ref):
    o_a = QKV_W + V_W
    o_u = o_a + 2 * DN_HEADS
    x = w_ref[...]
    rest = x.shape[1] - o_u
    o_ref[:, 0:o_a] = x[:, 0:o_a].astype(BF16)
    o_ref[:, o_a:o_a + rest] = x[:, o_u:].astype(BF16)
    lane = lax.broadcasted_iota(jnp.int32, (x.shape[0], LANES), 1)
    ab = o_a + rest
    o_ref[:, ab:ab + LANES] = jnp.where(lane < 2 * DN_HEADS, x[:, o_a:o_a + LANES], 0.0).astype(BF16)
    if o_ref.shape[1] > ab + LANES:
        o_ref[:, ab + LANES:] = jnp.zeros((x.shape[0], o_ref.shape[1] - ab - LANES), BF16)


def _rearranged_w_in(w_in, d):
    dm, nw = w_in.shape
    tr = 128
    nw_out = -(-(nw - 2 * DN_HEADS + LANES) // PROJ_TN) * PROJ_TN
    w = pl.pallas_call(
        _win_prep_kernel,
        grid=(dm // tr,),
        in_specs=[pl.BlockSpec((tr, nw), lambda i: (i, 0))],
        out_specs=pl.BlockSpec((tr, nw_out), lambda i: (i, 0)),
        out_shape=jax.ShapeDtypeStruct((dm, nw_out), BF16),
        compiler_params=_params("arbitrary"),
        name="w_in_prep",
    )(w_in)
    cols = dict(z=QKV_W, u=QKV_W + V_W, v=QKV_W + V_W + SG_W, ga=QKV_W + V_W + 2 * SG_W,
                gb=QKV_W + V_W + 2 * SG_W + d, ab=QKV_W + V_W + 2 * SG_W + 2 * d)
    return w, cols


def kernel(x_prompt, x_sample, state_conv_qkv, state_delta, c_prompt, c_sample, w_ada, b_ada, w_in, w_conv, a_log, dt_bias, o_norm_g, sg_ln_g, sg_ln_b, w_s, b_s, p_a, p_b, w_out, ln1_g, ln1_b, router_w, router_b, w_gu, b_gu, w_dn, b_dn, ln2_g, ln2_b):
    depth = w_ada.shape[0]
    alpha = float((2 * depth) ** 0.25)
    bp, seq, d = x_prompt.shape
    bs, lt, _ = x_sample.shape
    assert lt == DN_CONV and seq % SG_CHUNK == 0
    n_experts = router_w.shape[2]
    rows_p, rows_s = bp * seq, bs * lt
    xp = x_prompt.reshape(rows_p, d)
    xs = x_sample.reshape(rows_s, d)
    c_all = jnp.concatenate([c_prompt, c_sample], axis=0)
    tile = lambda cap: _Rows(math.gcd(math.gcd(cap, rows_s), seq), rows_p, rows_s, seq)
    rt_proj, rt_sgu, rt_merge, rt_comb = tile(PROJ_TM), tile(SG_CHUNK), tile(MERGE_TM), tile(COMB_TQ)
    outs = dict(conv_p=[], delta_p=[], conv_s=[], delta_s=[], vrows=[])

    for l in range(depth):
        mod = _adaln(c_all, w_ada[l], b_ada[l])
        mod_row = jnp.repeat(mod[bp:], lt, axis=0)

        w_r, col = _rearranged_w_in(w_in[l], d)
        proj = _inproj(rt_proj, xp, xs, mod, mod_row, w_r)

        alog = jnp.pad(a_log[l:l + 1], ((0, 0), (0, LANES - DN_HEADS)))
        dtb = jnp.pad(dt_bias[l:l + 1], ((0, 0), (0, LANES - DN_HEADS)))
        og = o_norm_g[l:l + 1]
        conv0 = jnp.zeros((bp, DN_CONV - 1, QKV_W), F32)
        s0 = jnp.zeros((bp, DN_HEADS, DN_DK, DN_DV), F32)
        oa_p, conv_p, delta_p = _gdn_prompt(proj, conv0, s0, w_conv[l], alog, dtb, og, bp, seq,
                                            col['z'], col['ab'])
        oa_s, conv_s, delta_s = _gdn_sample(proj, rows_p, state_conv_qkv[l], state_delta[l], w_conv[l],
                                            alog, dtb, og, bs, col['z'], col['ab'])

        ts = rt_sgu.tm
        assert ts == SG_CHUNK
        ws_p = jnp.tril(w_s[l][:, :ts, :ts]).astype(BF16)
        bias_p = jnp.repeat(b_s[l].T[:ts], SG_CH, axis=1)
        eye = jnp.eye(ts // lt, dtype=F32)
        ws_s = jnp.stack([jnp.kron(eye, jnp.tril(w_s[l, g, :lt, :lt])) for g in range(SG_GROUPS)]).astype(BF16)
        bias_s = jnp.tile(jnp.repeat(b_s[l, :, :lt].T, SG_CH, axis=1), (ts // lt, 1))
        ob, vn_s = _sgu(rt_sgu, proj, sg_ln_g[l:l + 1], sg_ln_b[l:l + 1],
                        ws_p, ws_s, bias_p, bias_s, col['u'], col['v'])

        pa, pb, wo = p_a[l].astype(BF16), p_b[l].astype(BF16), w_out[l].astype(BF16)
        rws = _split(jnp.pad(router_w[l], ((0, 0), (0, LANES - n_experts))))
        rw = jnp.concatenate([rws[0], rws[0], rws[1]], axis=0)
        rb = jnp.pad(router_b[l:l + 1], ((0, 0), (0, LANES - n_experts)))
        x1, h2, ti, tg = _merge(alpha, n_experts, rt_merge, oa_p, oa_s, ob, proj, xp, xs,
                                mod, mod_row,
                                pa, pb, wo, ln1_g[l:l + 1], ln1_b[l:l + 1], rw, rb,
                                col['ga'], col['gb'])

        nj = w_dn.shape[2] // MOE_TF
        dest, pad_lo, pad_len, nblk_used, nb_max, items = _routing(ti[:, :TOP_K], n_experts, MOE_TM, nj, MOE_SUBS)
        xb = _dispatch(pad_lo, pad_len, nblk_used, dest, h2, MOE_TM, nb_max)
        yb = _moe(items, xb, w_gu[l], b_gu[l], w_dn[l], b_dn[l], MOE_TM, MOE_TF, MOE_SUBS)
        xp, xs = _combine(alpha, rt_comb, dest, yb, tg, x1, mod, mod_row,
                          ln2_g[l:l + 1], ln2_b[l:l + 1])

        outs['conv_p'].append(conv_p)
        outs['delta_p'].append(delta_p)
        outs['conv_s'].append(conv_s)
        outs['delta_s'].append(delta_s)
        outs['vrows'].append(vn_s.reshape(bs, lt, SG_W))

    return (xp.reshape(bp, seq, d), xs.reshape(bs, lt, d),
            jnp.stack(outs['conv_p']), jnp.stack(outs['delta_p']),
            jnp.stack(outs['conv_s']), jnp.stack(outs['delta_s']), jnp.stack(outs['vrows']))
```

```python
import functools
import math

import jax
import jax.numpy as jnp
from jax import lax
from jax.experimental import pallas as pl
from jax.experimental.pallas import tpu as pltpu

F32 = jnp.float32
BF16 = jnp.bfloat16
HIGHEST = lax.Precision.HIGHEST

DN_HEADS = 8
DN_DK = 128
DN_DV = 128
DN_CONV = 4
DN_CHUNK = 64
SG_GROUPS = 8
SG_CH = 128
SG_CHUNK = 128
TOP_K = 4
SWIGLU_LIMIT = 7.0
SWIGLU_ALPHA = 1.702
LN_EPS = 1e-5
NORM_EPS = 1e-6
QK_W = DN_HEADS * DN_DK
V_W = DN_HEADS * DN_DV
QKV_W = 2 * QK_W + V_W
SG_W = SG_GROUPS * SG_CH

LANES = 128
SUBLANES = 8
VMEM_LIMIT = 56 * 1024 * 1024

PROJ_TM = 512
PROJ_TN = 1536
MERGE_TM = 256
MOE_TM = 256
MOE_TF = 512
MOE_SUBS = 6
COMB_TQ = 128
SAMPLE_BT = 4
SAMPLE_TP = 8
GDN_HG = 4

MOD_SHIFT1, MOD_SCALE1, MOD_GATE1, MOD_SHIFT2, MOD_SCALE2, MOD_GATE2 = range(6)
MOD_PARTS = 6


def _sigmoid(x):
    return 1.0 / (1.0 + jnp.exp(-x))


def _silu(x):
    return x * _sigmoid(x)


def _dot(a, b):
    return jnp.dot(a.astype(BF16), b.astype(BF16), preferred_element_type=F32)


def _dot_nt(a, b):
    return lax.dot_general(a.astype(BF16), b.astype(BF16), (((1,), (1,)), ((), ())),
                           preferred_element_type=F32)


def _dot_hi(a, b):
    return jnp.dot(a, b, precision=HIGHEST, preferred_element_type=F32)


def _layer_norm(x, g, b):
    mu = jnp.mean(x, axis=-1, keepdims=True)
    xc = x - mu
    var = jnp.mean(xc * xc, axis=-1, keepdims=True)
    return xc * lax.rsqrt(var + LN_EPS) * g + b


def _params(*sem):
    return pltpu.CompilerParams(dimension_semantics=sem, vmem_limit_bytes=VMEM_LIMIT)


class _Rows:
    def __init__(self, tm, rows_p, rows_s, seq):
        assert rows_p % tm == 0 and rows_s % tm == 0 and seq % tm == 0, (tm, rows_p, rows_s, seq)
        self.tm, self.n_p, self.n_s = tm, rows_p // tm, rows_s // tm
        self.tiles_per_seq = seq // tm
        self.bp = rows_p // seq
        assert rows_s % SUBLANES == 0
        self.rows_s = rows_s

    @property
    def n(self):
        return self.n_p + self.n_s

    def prompt(self, width, col=0):
        return pl.BlockSpec((self.tm, width), lambda i, *_: (jnp.minimum(i, self.n_p - 1), col))

    def sample(self, width, col=0, single=True):
        mode = dict(pipeline_mode=pl.Buffered(1)) if single else {}
        return pl.BlockSpec((self.tm, width), lambda i, *_: (jnp.maximum(i - self.n_p, 0), col), **mode)

    def joint(self, width, col=0):
        return pl.BlockSpec((self.tm, width), lambda i, *_: (i, col))

    def _seq(self, i):
        return jnp.minimum(i // self.tiles_per_seq, self.bp - 1)

    def seq_vec(self, d, part):
        return pl.BlockSpec((SUBLANES, d), lambda i, *_: ((self.rows_s + self._seq(i)) // SUBLANES, part))

    def row_vec(self, d, part):
        return pl.BlockSpec((self.tm, d), lambda i, *_: (jnp.maximum(i - self.n_p, 0), part),
                            pipeline_mode=pl.Buffered(1))

    def pick_vec(self, i, seq_ref, row_ref):
        prompt = seq_ref[pl.ds(self._seq(i) % SUBLANES, 1), :]
        return jnp.where(i >= self.n_p, row_ref[...], prompt)


def _pick(is_sample, prompt_ref, sample_ref):
    return jnp.where(is_sample, sample_ref[...], prompt_ref[...])


def _adaln_kernel(c_ref, w_ref, b_ref, o_ref):
    o_ref[...] = _dot(_silu(c_ref[...]), w_ref[...]) + b_ref[...]


def _adaln(c, w, b):
    rows, d = c.shape
    n = w.shape[1]
    tn = 1024
    return pl.pallas_call(
        _adaln_kernel,
        grid=(n // tn,),
        in_specs=[pl.BlockSpec((rows, d), lambda j: (0, 0)),
                  pl.BlockSpec((d, tn), lambda j: (0, j)),
                  pl.BlockSpec((1, tn), lambda j: (0, j))],
        out_specs=pl.BlockSpec((rows, tn), lambda j: (0, j)),
        out_shape=jax.ShapeDtypeStruct((rows, n), F32),
        compiler_params=_params("arbitrary"),
        name="adaln",
    )(c, w, b.reshape(1, n))


def _inproj_kernel(rt, xp_ref, xs_ref, scp_ref, scs_ref, shp_ref, shs_ref, w_ref, o_ref, h_scr):
    @pl.when(pl.program_id(1) == 0)
    def _():
        i = pl.program_id(0)
        x = _pick(i >= rt.n_p, xp_ref, xs_ref)
        h_scr[...] = (x * (1.0 + rt.pick_vec(i, scp_ref, scs_ref)) + rt.pick_vec(i, shp_ref, shs_ref)).astype(BF16)

    o_ref[...] = jnp.dot(h_scr[...], w_ref[...], preferred_element_type=F32)


def _inproj(rt, xp, xs, mod_seq, mod_row, w):
    d = xp.shape[1]
    nw = w.shape[1]
    tm = rt.tm
    return pl.pallas_call(
        functools.partial(_inproj_kernel, rt),
        grid=(rt.n, nw // PROJ_TN),
        in_specs=[rt.prompt(d), rt.sample(d), rt.seq_vec(d, MOD_SCALE1), rt.row_vec(d, MOD_SCALE1),
                  rt.seq_vec(d, MOD_SHIFT1), rt.row_vec(d, MOD_SHIFT1),
                  pl.BlockSpec((d, PROJ_TN), lambda i, j: (0, j))],
        out_specs=pl.BlockSpec((tm, PROJ_TN), lambda i, j: (i, j)),
        out_shape=jax.ShapeDtypeStruct((rt.n * tm, nw), F32),
        scratch_shapes=[pltpu.VMEM((tm, d), BF16)],
        compiler_params=_params("arbitrary", "arbitrary"),
        name="inproj",
    )(xp, xs, mod_seq, mod_row, mod_seq, mod_row, w)


def _softplus(x):
    return jnp.maximum(x, 0.0) + jnp.log1p(jnp.exp(-jnp.abs(x)))


def _split(a):
    hi = a.astype(BF16)
    lo = (a - hi.astype(F32)).astype(BF16)
    return hi, lo


def _dot3(a, b):
    lhs = jnp.concatenate([a[0], a[1], a[0]], axis=1)
    rhs = jnp.concatenate([b[0], b[0], b[1]], axis=0)
    return jnp.dot(lhs, rhs, preferred_element_type=F32)


def _map(f, *lists):
    return [f(*args) for args in zip(*lists)]


def _unit_lower_inverses_minus_eye(mats, rows, cols):
    same = (rows // 16) == (cols // 16)
    n = [jnp.where(same, a, 0.0) for a in mats]
    b = _map(lambda a, x: a - x, mats, n)
    n2 = _map(lambda x: _dot(x, x), n)
    n4 = _map(lambda x: _dot(x, x), n2)
    n8 = _map(lambda x: _dot(x, x), n4)
    r = [-x for x in n]
    r = _map(lambda x, p: x + p + _dot(x, p), r, n2)
    r = _map(lambda x, p: x + p + _dot(x, p), r, n4)
    dm = _map(lambda x, p: x + p + _dot(x, p), r, n8)
    m = _map(lambda x, y: y + _dot(x, y), dm, b)
    m2 = _map(lambda x: _dot(x, x), m)
    xm = _map(lambda x, p: x + p + _dot(p, x), dm, m2)
    return _map(lambda x, p: x - p - _dot(p, x), xm, m)


def _gdn_prompt_kernel(qkv_ref, z_ref, ab_ref, hist_ref, s0_ref, wconv_ref, alog_ref, dt_ref, og_ref,
                       o_ref, conv_ref, s_ref, xp_scr):
    n = pl.program_id(1)
    c = DN_CHUNK
    pad = SUBLANES
    nr = GDN_HG * c
    ngrp = DN_HEADS // GDN_HG

    @pl.when(n == 0)
    def _():
        xp_scr[pl.ds(0, pad), :] = jnp.zeros((pad, QKV_W), F32)
        xp_scr[pl.ds(pad - (DN_CONV - 1), DN_CONV - 1), :] = hist_ref[...]
        s_ref[...] = s0_ref[...]

    x = qkv_ref[...]
    xp_scr[pl.ds(pad, c), :] = x
    y = jnp.zeros((c, QKV_W), F32)
    for j in range(DN_CONV):
        y = y + xp_scr[pl.ds(pad - (DN_CONV - 1) + j, c), :] * wconv_ref[pl.ds(j, 1), :]
    y = _silu(y)
    tail = xp_scr[pl.ds(c + pad - (DN_CONV - 1), DN_CONV - 1), :]
    conv_ref[...] = tail
    xp_scr[pl.ds(pad - (DN_CONV - 1), DN_CONV - 1), :] = tail

    ab = ab_ref[...]
    g = -jnp.exp(alog_ref[...]) * _softplus(ab + dt_ref[...])
    beta_all = _sigmoid(ab)
    r64 = lax.broadcasted_iota(jnp.int32, (c, c), 0)
    c64 = lax.broadcasted_iota(jnp.int32, (c, c), 1)
    gc = _dot_hi((r64 >= c64).astype(F32), g)
    z = z_ref[...]
    og = og_ref[...]

    rows = lax.broadcasted_iota(jnp.int32, (nr, nr), 0)
    cols = lax.broadcasted_iota(jnp.int32, (nr, nr), 1)
    same = (rows // c) == (cols // c)
    causal = same & (rows >= cols)
    strict = same & (rows > cols)
    rowhead = lax.broadcasted_iota(jnp.int32, (nr, 1), 0) // c
    groups = [range(grp * GDN_HG, (grp + 1) * GDN_HG) for grp in range(ngrp)]

    def stack(heads, off, width):
        return jnp.concatenate([y[:, off + h * width:off + (h + 1) * width] for h in heads], axis=0)

    def l2n(t):
        return t * lax.rsqrt(jnp.sum(t * t, axis=-1, keepdims=True) + NORM_EPS)

    q = [l2n(stack(hs, 0, DN_DK)) * (DN_DK ** -0.5) for hs in groups]
    k = [l2n(stack(hs, QK_W, DN_DK)) for hs in groups]
    v = [stack(hs, 2 * QK_W, DN_DV) for hs in groups]
    beta = [jnp.concatenate([beta_all[:, DN_HEADS + h:DN_HEADS + h + 1] for h in hs], axis=0) for hs in groups]
    gcf = [jnp.concatenate([jnp.broadcast_to(gc[:, h:h + 1], (c, LANES)) for h in hs], axis=0) for hs in groups]
    gcc = [t[:, 0:1] for t in gcf]
    grow = [t.T[0:1, :] for t in gcf]
    glast = [[gc[c - 1:c, h:h + 1] for h in hs] for hs in groups]
    gtot = [jnp.concatenate([jnp.broadcast_to(t, (c, 1)) for t in gl], axis=0) for gl in glast]
    decay = _map(lambda a, b: jnp.where(causal, jnp.exp(jnp.where(causal, a - b, 0.0)), 0.0), gcc, grow)
    egc = _map(jnp.exp, gcc)
    kb = _map(lambda a, b: a * b, k, beta)
    a_mat = _map(lambda a, b, dd: jnp.where(strict, _dot_nt(a, b) * dd, 0.0), kb, k, decay)
    tm1 = _unit_lower_inverses_minus_eye(a_mat, rows, cols)
    u = _map(lambda t, a, b: a * b + _dot(t, a * b), tm1, v, beta)
    w = _map(lambda t, a, b: a * b + _dot(t, a * b), tm1, kb, egc)
    attn = _map(lambda a, b, dd: _dot_nt(a, b) * dd, q, k, decay)
    qg = _map(lambda a, b: a * b, q, egc)
    kd_t = _map(lambda a, b, cc: (a * jnp.exp(b - cc)).T, k, gtot, gcc)

    for gi, hs in enumerate(groups):
        v_news, qss = [], []
        for hl, h in enumerate(hs):
            sl = slice(hl * c, (hl + 1) * c)
            rs = _dot(jnp.concatenate([w[gi][sl], qg[gi][sl]], axis=0), s_ref[h])
            v_news.append(u[gi][sl] - rs[:c])
            qss.append(rs[c:])
        v_new = jnp.concatenate(v_news, axis=0)
        o = jnp.concatenate(qss, axis=0) + _dot(attn[gi], v_new)
        for hl, h in enumerate(hs):
            upd = _dot(kd_t[gi], jnp.where(rowhead == hl, v_new, 0.0))
            s_ref[h] = s_ref[h] * jnp.exp(glast[gi][hl]) + upd
        o = o * lax.rsqrt(jnp.mean(o * o, axis=-1, keepdims=True) + NORM_EPS) * og
        for hl, h in enumerate(hs):
            o_ref[:, h * DN_DV:(h + 1) * DN_DV] = o[hl * c:(hl + 1) * c] * _silu(z[:, h * DN_DV:(h + 1) * DN_DV])


def _gdn_prompt(proj, hist, s0, wconv, alog, dtb, og, batch, seq, col_z, col_ab):
    c = DN_CHUNK
    nchunk = seq // c
    rows = batch * seq
    row = lambda b, n: b * nchunk + n
    return pl.pallas_call(
        _gdn_prompt_kernel,
        grid=(batch, nchunk),
        in_specs=[pl.BlockSpec((c, QKV_W), lambda b, n: (row(b, n), 0)),
                  pl.BlockSpec((c, V_W), lambda b, n: (row(b, n), col_z // V_W)),
                  pl.BlockSpec((c, LANES), lambda b, n: (row(b, n), col_ab // LANES)),
                  pl.BlockSpec((None, DN_CONV - 1, QKV_W), lambda b, n: (b, 0, 0)),
                  pl.BlockSpec((None, DN_HEADS, DN_DK, DN_DV), lambda b, n: (b, 0, 0, 0)),
                  pl.BlockSpec((DN_CONV, QKV_W), lambda b, n: (0, 0)),
                  pl.BlockSpec((1, LANES), lambda b, n: (0, 0)),
                  pl.BlockSpec((1, LANES), lambda b, n: (0, 0)),
                  pl.BlockSpec((1, DN_DV), lambda b, n: (0, 0))],
        out_specs=[pl.BlockSpec((c, V_W), lambda b, n: (row(b, n), 0)),
                   pl.BlockSpec((None, DN_CONV - 1, QKV_W), lambda b, n: (b, 0, 0)),
                   pl.BlockSpec((None, DN_HEADS, DN_DK, DN_DV), lambda b, n: (b, 0, 0, 0))],
        out_shape=[jax.ShapeDtypeStruct((rows, V_W), F32),
                   jax.ShapeDtypeStruct((batch, DN_CONV - 1, QKV_W), F32),
                   jax.ShapeDtypeStruct((batch, DN_HEADS, DN_DK, DN_DV), F32)],
        scratch_shapes=[pltpu.VMEM((c + SUBLANES, QKV_W), F32)],
        compiler_params=_params("arbitrary", "arbitrary"),
        name="gdn_prompt",
    )(proj, proj, proj, hist, s0, wconv, alog, dtb, og)


def _gdn_sample_kernel(qkv_ref, z_ref, ab_ref, hist_ref, s0_ref, wconv_ref, alog_ref, dt_ref, og_ref,
                       o_ref, conv_ref, s_ref, xp_scr, ab_scr, wq_scr, r_scr, kdt_scr, vn_scr, gl_scr):
    bt, tp, lt = SAMPLE_BT, SAMPLE_TP, DN_CONV
    nblk = DN_HEADS * bt
    nrow = nblk * tp
    hist_rows = DN_CONV - 1

    xp_scr[...] = jnp.zeros(xp_scr.shape, F32)
    ab_scr[...] = jnp.zeros(ab_scr.shape, F32)
    ys = []
    for b in range(bt):
        xp_scr[b, pl.ds(SUBLANES - hist_rows, hist_rows), :] = hist_ref[b]
        xp_scr[b, pl.ds(SUBLANES, lt), :] = qkv_ref[pl.ds(b * lt, lt), :]
        yb = jnp.zeros((tp, QKV_W), F32)
        for j in range(DN_CONV):
            yb = yb + xp_scr[b, pl.ds(SUBLANES - hist_rows + j, tp), :] * wconv_ref[pl.ds(j, 1), :]
        ys.append(_silu(yb))
        conv_ref[b] = xp_scr[b, pl.ds(SUBLANES + lt - hist_rows, hist_rows), :]
        ab_scr[b, pl.ds(0, lt), :] = ab_ref[pl.ds(b * lt, lt), :]
    y = jnp.concatenate(ys, axis=0)
    ab = jnp.concatenate([ab_scr[b] for b in range(bt)], axis=0)
    tok = lax.broadcasted_iota(jnp.int32, (bt * tp, 1), 0) % tp
    real = tok < lt
    g_all = jnp.where(real, -jnp.exp(alog_ref[...]) * _softplus(ab + dt_ref[...]), 0.0)
    beta_all = jnp.where(real, _sigmoid(ab), 0.0)

    def heads_to_rows(t, off, width):
        return jnp.concatenate([t[:, off + h * width:off + (h + 1) * width] for h in range(DN_HEADS)], axis=0)

    realr = jnp.concatenate([real] * DN_HEADS, axis=0)
    q = jnp.where(realr, heads_to_rows(y, 0, DN_DK), 0.0)
    k = jnp.where(realr, heads_to_rows(y, QK_W, DN_DK), 0.0)
    v = jnp.where(realr, heads_to_rows(y, 2 * QK_W, DN_DV), 0.0)
    gcol = jnp.concatenate([g_all[:, h:h + 1] for h in range(DN_HEADS)], axis=0)
    beta = jnp.concatenate([beta_all[:, DN_HEADS + h:DN_HEADS + h + 1] for h in range(DN_HEADS)], axis=0)
    q = q * lax.rsqrt(jnp.sum(q * q, axis=-1, keepdims=True) + NORM_EPS) * (DN_DK ** -0.5)
    k = k * lax.rsqrt(jnp.sum(k * k, axis=-1, keepdims=True) + NORM_EPS)

    rows = lax.broadcasted_iota(jnp.int32, (nrow, nrow), 0)
    cols = lax.broadcasted_iota(jnp.int32, (nrow, nrow), 1)
    same = (rows // tp) == (cols // tp)
    causal = same & (rows >= cols)
    strict = same & (rows > cols)
    gfull = jnp.broadcast_to(gcol, (nrow, LANES))
    gc = _dot_hi(causal.astype(F32), gfull)
    gtot = _dot_hi(same.astype(F32), gfull)
    gcc = gc[:, 0:1]
    grow = gc.T[0:1, :]
    decay = jnp.where(causal, jnp.exp(jnp.where(causal, gcc - grow, 0.0)), 0.0)
    egc = jnp.exp(gcc)
    kb = k * beta
    a_mat = jnp.where(strict, _dot_nt(kb, k) * decay, 0.0)
    a2 = _dot(a_mat, a_mat)
    tm1 = a2 - a_mat - _dot(a_mat, a2)
    u = v * beta + _dot(tm1, v * beta)
    w = kb * egc + _dot(tm1, kb * egc)
    attn = _dot_nt(q, k) * decay
    qg = q * egc
    for i in range(nblk):
        wq_scr[pl.ds(2 * tp * i, tp), :] = w[i * tp:(i + 1) * tp, :]
        wq_scr[pl.ds(2 * tp * i + tp, tp), :] = qg[i * tp:(i + 1) * tp, :]
    kdt_scr[...] = (k * jnp.exp(gtot[:, 0:1] - gcc)).T
    gl_scr[...] = jnp.exp(gtot)

    def read_state(i, carry):
        r0 = pl.multiple_of(i * 2 * tp, 2 * tp)
        r_scr[pl.ds(r0, 2 * tp), :] = _dot(wq_scr[pl.ds(r0, 2 * tp), :], s0_ref[i % bt, i // bt])
        return carry

    lax.fori_loop(0, nblk, read_state, 0, unroll=4)
    ws = jnp.concatenate([r_scr[pl.ds(2 * tp * i, tp), :] for i in range(nblk)], axis=0)
    qs = jnp.concatenate([r_scr[pl.ds(2 * tp * i + tp, tp), :] for i in range(nblk)], axis=0)
    v_new = u - ws
    vn_scr[...] = v_new
    o = qs + _dot(attn, v_new)
    blockid = lax.broadcasted_iota(jnp.int32, (nrow, 1), 0) // tp

    def write_state(i, carry):
        r0 = pl.multiple_of(i * tp, tp)
        upd = _dot(kdt_scr[...], jnp.where(blockid == i, vn_scr[...], 0.0))
        s_ref[i % bt, i // bt] = s0_ref[i % bt, i // bt] * gl_scr[pl.ds(r0, 1), :] + upd
        return carry

    lax.fori_loop(0, nblk, write_state, 0, unroll=4)

    og = og_ref[...]
    o = o * lax.rsqrt(jnp.mean(o * o, axis=-1, keepdims=True) + NORM_EPS) * og
    for b in range(bt):
        for h in range(DN_HEADS):
            zbh = z_ref[pl.ds(b * lt, lt), h * DN_DV:(h + 1) * DN_DV]
            blk = o[(h * bt + b) * tp:(h * bt + b) * tp + lt, :]
            o_ref[pl.ds(b * lt, lt), h * DN_DV:(h + 1) * DN_DV] = blk * _silu(zbh)


def _gdn_sample(proj, row0, hist, s0, wconv, alog, dtb, og, batch, col_z, col_ab):
    bt, tp, lt = SAMPLE_BT, SAMPLE_TP, DN_CONV
    rows = batch * lt
    nrow = DN_HEADS * bt * tp
    blk = bt * lt
    assert row0 % blk == 0 and batch % bt == 0
    b0 = row0 // blk
    return pl.pallas_call(
        _gdn_sample_kernel,
        grid=(batch // bt,),
        in_specs=[pl.BlockSpec((blk, QKV_W), lambda i: (b0 + i, 0)),
                  pl.BlockSpec((blk, V_W), lambda i: (b0 + i, col_z // V_W)),
                  pl.BlockSpec((blk, LANES), lambda i: (b0 + i, col_ab // LANES)),
                  pl.BlockSpec((bt, DN_CONV - 1, QKV_W), lambda i: (i, 0, 0)),
                  pl.BlockSpec((bt, DN_HEADS, DN_DK, DN_DV), lambda i: (i, 0, 0, 0)),
                  pl.BlockSpec((DN_CONV, QKV_W), lambda i: (0, 0)),
                  pl.BlockSpec((1, LANES), lambda i: (0, 0)),
                  pl.BlockSpec((1, LANES), lambda i: (0, 0)),
                  pl.BlockSpec((1, DN_DV), lambda i: (0, 0))],
        out_specs=[pl.BlockSpec((blk, V_W), lambda i: (i, 0)),
                   pl.BlockSpec((bt, DN_CONV - 1, QKV_W), lambda i: (i, 0, 0)),
                   pl.BlockSpec((bt, DN_HEADS, DN_DK, DN_DV), lambda i: (i, 0, 0, 0))],
        out_shape=[jax.ShapeDtypeStruct((rows, V_W), F32),
                   jax.ShapeDtypeStruct((batch, DN_CONV - 1, QKV_W), F32),
                   jax.ShapeDtypeStruct((batch, DN_HEADS, DN_DK, DN_DV), F32)],
        scratch_shapes=[pltpu.VMEM((bt, SUBLANES + tp, QKV_W), F32),
                        pltpu.VMEM((bt, tp, LANES), F32),
                        pltpu.VMEM((2 * nrow, DN_DK), F32),
                        pltpu.VMEM((2 * nrow, DN_DV), F32),
                        pltpu.VMEM((DN_DK, nrow), F32),
                        pltpu.VMEM((nrow, DN_DV), F32),
                        pltpu.VMEM((nrow, LANES), F32)],
        compiler_params=_params("arbitrary"),
        name="gdn_sample",
    )(proj, proj, proj, hist, s0, wconv, alog, dtb, og)


def _sgu_kernel(n_p, u_ref, v_ref, g_ref, b_ref, wsp_ref, wss_ref, bp_ref, bs_ref, o_ref, vn_ref):
    is_s = pl.program_id(0) >= n_p
    u = jax.nn.gelu(u_ref[...])
    vn = _layer_norm(jax.nn.gelu(v_ref[...]), g_ref[...], b_ref[...])
    vn_ref[...] = vn
    bias = _pick(is_s, bp_ref, bs_ref)
    for g in range(SG_GROUPS):
        sl = slice(g * SG_CH, (g + 1) * SG_CH)
        ws = jnp.where(is_s, wss_ref[g], wsp_ref[g])
        mixed = _dot(ws, vn[:, sl]) + bias[:, sl]
        o_ref[:, sl] = u[:, sl] * mixed


def _sgu(rt, proj, ln_g, ln_b, ws_p, ws_s, bias_p, bias_s, col_u, col_v):
    t = rt.tm
    const = lambda shape: pl.BlockSpec(shape, lambda i: (0,) * len(shape))
    return pl.pallas_call(
        functools.partial(_sgu_kernel, rt.n_p),
        grid=(rt.n,),
        in_specs=[rt.joint(SG_W, col_u // SG_W), rt.joint(SG_W, col_v // SG_W),
                  const((1, SG_W)), const((1, SG_W)),
                  const((SG_GROUPS, t, t)), const((SG_GROUPS, t, t)), const((t, SG_W)), const((t, SG_W))],
        out_specs=[rt.joint(SG_W), rt.sample(SG_W, single=False)],
        out_shape=[jax.ShapeDtypeStruct((rt.n * t, SG_W), F32),
                   jax.ShapeDtypeStruct((rt.n_s * t, SG_W), F32)],
        compiler_params=_params("arbitrary"),
        name="sgu",
    )(proj, proj, ln_g, ln_b, ws_p, ws_s, bias_p, bias_s)


def _merge_kernel(alpha, n_experts, rt, oap_ref, oas_ref, ob_ref, ga_ref, gb_ref, xp_ref, xs_ref,
                  gtp_ref, gts_ref, scp_ref, scs_ref, shp_ref, shs_ref,
                  pa_ref, pb_ref, wo_ref, lg_ref, lb_ref, rw_ref, rb_ref,
                  x1_ref, h2_ref, ti_ref, tg_ref):
    i = pl.program_id(0)
    is_s = i >= rt.n_p
    oa = _pick(is_s, oap_ref, oas_ref)
    merged = (_sigmoid(ga_ref[...]) * _dot(oa, pa_ref[...])
              + _sigmoid(gb_ref[...]) * _dot(ob_ref[...], pb_ref[...]))
    y = _dot(merged, wo_ref[...])
    x = _pick(is_s, xp_ref, xs_ref)
    x1 = _layer_norm(alpha * x + rt.pick_vec(i, gtp_ref, gts_ref) * y, lg_ref[...], lb_ref[...])
    x1_ref[...] = x1
    h2 = x1 * (1.0 + rt.pick_vec(i, scp_ref, scs_ref)) + rt.pick_vec(i, shp_ref, shs_ref)
    h2_ref[...] = h2
    hs = _split(h2)
    logits = jnp.dot(jnp.concatenate([hs[0], hs[1], hs[0]], axis=1), rw_ref[...],
                     preferred_element_type=F32) + rb_ref[...]
    lane = lax.broadcasted_iota(jnp.int32, logits.shape, 1)
    logits = jnp.where(lane < n_experts, logits, -jnp.inf)
    ti = jnp.zeros(logits.shape, jnp.int32)
    tv = jnp.zeros(logits.shape, F32)
    top = None
    for kk in range(TOP_K):
        m = jnp.max(logits, axis=-1, keepdims=True)
        idx = jnp.min(jnp.where(logits == m, lane.astype(F32), float(LANES)), axis=-1,
                      keepdims=True).astype(jnp.int32)
        if kk == 0:
            top = m
        ti = jnp.where(lane == kk, idx, ti)
        tv = jnp.where(lane == kk, jnp.exp(m - top), tv)
        logits = jnp.where(lane == idx, -jnp.inf, logits)
    ti_ref[...] = ti
    tg_ref[...] = tv / jnp.sum(tv, axis=-1, keepdims=True)


def _merge(alpha, n_experts, rt, oa_p, oa_s, ob, proj, xp, xs, mod_seq, mod_row, pa, pb, wo, lg, lb, rw, rb,
           col_ga, col_gb):
    d = xp.shape[1]
    tm = rt.tm
    rows = rt.n * tm
    const = lambda shape: pl.BlockSpec(shape, lambda i: (0,) * len(shape), pipeline_mode=pl.Buffered(1))
    return pl.pallas_call(
        functools.partial(_merge_kernel, alpha, n_experts, rt),
        grid=(rt.n,),
        in_specs=[rt.prompt(V_W), rt.sample(V_W), rt.joint(SG_W),
                  rt.joint(d, col_ga // d), rt.joint(d, col_gb // d),
                  rt.prompt(d), rt.sample(d),
                  rt.seq_vec(d, MOD_GATE1), rt.row_vec(d, MOD_GATE1),
                  rt.seq_vec(d, MOD_SCALE2), rt.row_vec(d, MOD_SCALE2),
                  rt.seq_vec(d, MOD_SHIFT2), rt.row_vec(d, MOD_SHIFT2),
                  const((V_W, d)), const((SG_W, d)), const((d, d)),
                  const((1, d)), const((1, d)), const((3 * d, LANES)), const((1, LANES))],
        out_specs=[rt.joint(d), rt.joint(d), rt.joint(LANES), rt.joint(LANES)],
        out_shape=[jax.ShapeDtypeStruct((rows, d), F32),
                   jax.ShapeDtypeStruct((rows, d), F32),
                   jax.ShapeDtypeStruct((rows, LANES), jnp.int32),
                   jax.ShapeDtypeStruct((rows, LANES), F32)],
        compiler_params=_params("arbitrary"),
        name="merge",
    )(oa_p, oa_s, ob, proj, proj, xp, xs, mod_seq, mod_row, mod_seq, mod_row, mod_seq, mod_row,
      pa, pb, wo, lg, lb, rw, rb)


FILL_ROWS = (128, 64, 32, 16, 8)


def _dispatch_kernel(tm, n_experts, padlo_ref, padlen_ref, nblk_ref, dest_ref, h_ref, xb_hbm, zero_scr, sem, zsem):
    i = pl.program_id(0)
    tq = h_ref.shape[0]
    nb_max = xb_hbm.shape[0] // tm
    zrows = zero_scr.shape[0]

    def fill(act):
        def per_expert(e, carry):
            lo = padlo_ref[e]
            ln = padlen_ref[e]
            head = (SUBLANES - lo % SUBLANES) % SUBLANES
            for r in range(SUBLANES - 1):
                @pl.when(r < head)
                def _(r=r):
                    act(pltpu.make_async_copy(zero_scr.at[pl.ds(0, 1)], xb_hbm.at[pl.ds(lo + r, 1)], zsem))
            off = lo + head
            rem = ln - head
            for b in FILL_ROWS:
                @pl.when((rem & b) != 0)
                def _(off=off, b=b):
                    dst = xb_hbm.at[pl.ds(pl.multiple_of(off, SUBLANES), b)]
                    act(pltpu.make_async_copy(zero_scr.at[pl.ds(0, b)], dst, zsem))
                off = off + (rem & b)
            return carry

        lax.fori_loop(0, n_experts, per_expert, 0)

        def per_block(blk, carry):
            for part in range(tm // zrows):
                row0 = pl.multiple_of(blk * tm + part * zrows, zrows)
                act(pltpu.make_async_copy(zero_scr, xb_hbm.at[pl.ds(row0, zrows)], zsem))
            return carry

        lax.fori_loop(nblk_ref[0], nb_max, per_block, 0)

    @pl.when(i == 0)
    def _():
        zero_scr[...] = jnp.zeros(zero_scr.shape, F32)
        fill(lambda cp: cp.start())
        fill(lambda cp: cp.wait())

    def start(t, carry):
        for kk in range(TOP_K):
            pltpu.make_async_copy(h_ref.at[pl.ds(t, 1)], xb_hbm.at[pl.ds(dest_ref[0, t * TOP_K + kk], 1)],
                                  sem).start()
        return carry

    lax.fori_loop(0, tq, start, 0, unroll=4)
    for kk in range(TOP_K):
        pltpu.make_async_copy(h_ref, xb_hbm.at[pl.ds(0, tq)], sem).wait()


def _dispatch(padlo, padlen, nblk_used, dest, h2, tm, nb_max):
    t, d = h2.shape
    tq = math.gcd(256, t)
    n_experts = padlo.shape[0]
    grid_spec = pltpu.PrefetchScalarGridSpec(
        num_scalar_prefetch=3,
        grid=(t // tq,),
        in_specs=[pl.BlockSpec((None, 1, tq * TOP_K), lambda i, *_: (i, 0, 0), memory_space=pltpu.SMEM),
                  pl.BlockSpec((tq, d), lambda i, *_: (i, 0))],
        out_specs=pl.BlockSpec(memory_space=pl.ANY),
        scratch_shapes=[pltpu.VMEM((FILL_ROWS[0], d), F32), pltpu.SemaphoreType.DMA(()),
                        pltpu.SemaphoreType.DMA(())],
    )
    return pl.pallas_call(
        functools.partial(_dispatch_kernel, tm, n_experts),
        grid_spec=grid_spec,
        out_shape=jax.ShapeDtypeStruct((nb_max * tm, d), F32),
        compiler_params=pltpu.CompilerParams(dimension_semantics=("arbitrary",), vmem_limit_bytes=VMEM_LIMIT,
                                             has_side_effects=True),
        name="dispatch",
    )(padlo, padlen, nblk_used.reshape(1), dest.reshape(t // tq, 1, tq * TOP_K), h2)


def _moe_kernel(nj, e_ref, j_ref, xb_ref, ob_ref, r_ref, flag_ref, wg_code_ref, wu_code_ref,
                x_ref, wg_ref, wu_ref, wd_ref, bg_ref, bu_ref, bd_ref, o_ref,
                wg_scr, wu_scr, wd_scr, acc_scr):
    s = pl.program_id(0)
    flags = flag_ref[s]
    j = j_ref[s]
    r = r_ref[s]

    @pl.when((flags & 2) != 0)
    def _():
        wg_scr[...] = wg_ref[...].astype(BF16)
        wu_scr[...] = wu_ref[...].astype(BF16)
        wd_scr[...] = wd_ref[...].astype(BF16)

    @pl.when(flags == 0)
    def _():
        o_ref[...] = jnp.zeros(o_ref.shape, F32)

    tm = x_ref.shape[0]

    def expert_rows(rows):
        x = x_ref[pl.ds(0, rows), :].astype(BF16)
        gate = jnp.dot(x, wg_scr[...], preferred_element_type=F32) + bg_ref[...]
        up = jnp.dot(x, wu_scr[...], preferred_element_type=F32) + bu_ref[...]
        gate = jnp.minimum(gate, SWIGLU_LIMIT)
        up = jnp.clip(up, -SWIGLU_LIMIT, SWIGLU_LIMIT)
        act = (up + 1.0) * gate * _sigmoid(SWIGLU_ALPHA * gate)
        y = jnp.dot(act.astype(BF16), wd_scr[...], preferred_element_type=F32)

        @pl.when(j == 0)
        def _():
            acc_scr[r, pl.ds(0, rows), :] = y

        @pl.when(jnp.logical_and(j > 0, j < nj - 1))
        def _():
            acc_scr[r, pl.ds(0, rows), :] = acc_scr[r, pl.ds(0, rows), :] + y

        @pl.when(j == nj - 1)
        def _():
            o_ref[pl.ds(0, rows), :] = acc_scr[r, pl.ds(0, rows), :] + y + bd_ref[...]
            if rows < tm:
                o_ref[pl.ds(rows, tm - rows), :] = jnp.zeros((tm - rows, o_ref.shape[1]), F32)

    @pl.when((flags & 5) == 1)
    def _():
        expert_rows(tm)

    @pl.when((flags & 5) == 5)
    def _():
        expert_rows(tm // 2)


def _moe(items, xb, w_gu, b_gu, w_dn, b_dn, tm, tf, subs):
    item_e, item_j, item_xb, item_ob, item_r, item_flag, wg_code, wu_code = items
    n_items = item_e.shape[0]
    nslot, d = xb.shape
    n_exp, _, f2 = w_gu.shape
    f = f2 // 2
    nj = f // tf
    assert nj >= 2
    grid_spec = pltpu.PrefetchScalarGridSpec(
        num_scalar_prefetch=8,
        grid=(n_items,),
        in_specs=[pl.BlockSpec((tm, d), lambda s, e, j, xbk, obk, r, fl, cg, cu: (xbk[s], 0)),
                  pl.BlockSpec((None, d, tf), lambda s, e, j, xbk, obk, r, fl, cg, cu: (cg[s] // nj, 0, cg[s] % nj)),
                  pl.BlockSpec((None, d, tf),
                               lambda s, e, j, xbk, obk, r, fl, cg, cu: (cu[s] // nj, 0, nj + cu[s] % nj)),
                  pl.BlockSpec((None, tf, d), lambda s, e, j, xbk, obk, r, fl, cg, cu: (e[s], j[s], 0)),
                  pl.BlockSpec((None, 1, tf), lambda s, e, j, xbk, obk, r, fl, cg, cu: (e[s], 0, j[s])),
                  pl.BlockSpec((None, 1, tf), lambda s, e, j, xbk, obk, r, fl, cg, cu: (e[s], 0, nj + j[s])),
                  pl.BlockSpec((None, 1, d), lambda s, e, j, xbk, obk, r, fl, cg, cu: (e[s], 0, 0))],
        out_specs=pl.BlockSpec((tm, d), lambda s, e, j, xbk, obk, r, fl, cg, cu: (obk[s], 0)),
        scratch_shapes=[pltpu.VMEM((d, tf), BF16), pltpu.VMEM((d, tf), BF16), pltpu.VMEM((tf, d), BF16),
                        pltpu.VMEM((subs, tm, d), F32)],
    )
    return pl.pallas_call(
        functools.partial(_moe_kernel, nj),
        grid_spec=grid_spec,
        out_shape=jax.ShapeDtypeStruct((nslot, d), F32),
        compiler_params=_params("arbitrary"),
        name="moe",
    )(item_e, item_j, item_xb, item_ob, item_r, item_flag, wg_code, wu_code,
      xb, w_gu, w_gu, w_dn, b_gu.reshape(n_exp, 1, f2), b_gu.reshape(n_exp, 1, f2), b_dn.reshape(n_exp, 1, d))


def _combine_kernel(alpha, rt, dest_ref, destn_ref, yb_hbm, tg_ref, x1_ref, gtp_ref, gts_ref, lg_ref, lb_ref,
                    op_ref, os_ref, buf, sem):
    i = pl.program_id(0)
    last = pl.num_programs(0) - 1
    tq = x1_ref.shape[0]
    slot = i % 2

    def start(ids_ref, sl, t):
        for kk in range(TOP_K):
            pltpu.make_async_copy(yb_hbm.at[pl.ds(ids_ref[0, t * TOP_K + kk], 1)],
                                  buf.at[sl, kk, pl.ds(t, 1)], sem.at[sl]).start()

    def wait(sl):
        for kk in range(TOP_K):
            pltpu.make_async_copy(yb_hbm.at[pl.ds(0, tq)], buf.at[sl, kk], sem.at[sl]).wait()

    @pl.when(i == 0)
    def _():
        lax.fori_loop(0, tq, lambda t, c: (start(dest_ref, 0, t), c)[1], 0, unroll=4)

    for t in range(tq):
        start(destn_ref, 1 - slot, t)
    wait(slot)
    tg = tg_ref[...]
    y = jnp.zeros(x1_ref.shape, F32)
    for kk in range(TOP_K):
        y = y + buf[slot, kk] * tg[:, kk:kk + 1]
    is_s = i >= rt.n_p
    out = _layer_norm(alpha * x1_ref[...] + rt.pick_vec(i, gtp_ref, gts_ref) * y, lg_ref[...], lb_ref[...])

    @pl.when(jnp.logical_not(is_s))
    def _():
        op_ref[...] = out

    @pl.when(is_s)
    def _():
        os_ref[...] = out

    @pl.when(i == last)
    def _():
        wait(1 - slot)


def _combine(alpha, rt, dest, yb, tg, x1, mod_seq, mod_row, lg, lb):
    d = x1.shape[1]
    tq = rt.tm
    ids = dest.reshape(rt.n, 1, tq * TOP_K)
    return pl.pallas_call(
        functools.partial(_combine_kernel, alpha, rt),
        grid=(rt.n,),
        in_specs=[pl.BlockSpec((None, 1, tq * TOP_K), lambda i: (i, 0, 0), memory_space=pltpu.SMEM),
                  pl.BlockSpec((None, 1, tq * TOP_K), lambda i: (jnp.minimum(i + 1, rt.n - 1), 0, 0),
                               memory_space=pltpu.SMEM),
                  pl.BlockSpec(memory_space=pl.ANY),
                  rt.joint(LANES), rt.joint(d), rt.seq_vec(d, MOD_GATE2), rt.row_vec(d, MOD_GATE2),
                  pl.BlockSpec((1, d), lambda i: (0, 0)),
                  pl.BlockSpec((1, d), lambda i: (0, 0))],
        out_specs=[rt.prompt(d), rt.sample(d, single=False)],
        out_shape=[jax.ShapeDtypeStruct((rt.n_p * tq, d), F32), jax.ShapeDtypeStruct((rt.n_s * tq, d), F32)],
        scratch_shapes=[pltpu.VMEM((2, TOP_K, tq, d), F32), pltpu.SemaphoreType.DMA((2,))],
        compiler_params=_params("arbitrary"),
        name="combine",
    )(ids, ids, yb, tg, x1, mod_seq, mod_row, lg, lb)


def _take(table, idx):
    hit = idx[:, None] == jnp.arange(table.shape[0], dtype=jnp.int32)[None, :]
    return jnp.sum(jnp.where(hit, table[None, :], 0), axis=1)


def _routing(ti, n_experts, tm, nj, subs):
    t = ti.shape[0]
    n_assign = t * TOP_K
    nb_max = n_assign // tm + n_experts
    onehot = (ti[:, :, None] == jnp.arange(n_experts, dtype=jnp.int32)[None, None, :]).astype(jnp.int32)
    per_tok = jnp.sum(onehot, axis=1)
    cum = jnp.cumsum(per_tok, axis=0)
    counts = cum[-1]
    rank = jnp.sum(onehot * cum[:, None, :], axis=2) - 1
    nblk = (counts + tm - 1) // tm
    blk_end = jnp.cumsum(nblk)
    blk_start = blk_end - nblk
    dest = jnp.sum(onehot * blk_start[None, None, :], axis=2) * tm + rank
    pad_lo = (blk_start * tm + counts).astype(jnp.int32)
    pad_len = (nblk * tm - counts).astype(jnp.int32)
    total_blk = blk_end[-1]
    blocks = jnp.arange(nb_max, dtype=jnp.int32)
    blk_e = jnp.minimum(jnp.sum((blk_end[None, :] <= blocks[:, None]).astype(jnp.int32), axis=1), n_experts - 1)
    start_b, nblk_b = _take(blk_start, blk_e), _take(nblk, blk_e)
    r_in_e = blocks - start_b
    g0 = start_b + (r_in_e // subs) * subs
    nsub = jnp.minimum(subs, nblk_b - (r_in_e // subs) * subs)
    half_b = (_take(counts, blk_e) - r_in_e * tm) <= tm // 2
    p = jnp.arange(nb_max * nj, dtype=jnp.int32)
    bp = p // nj
    valid = bp < total_blk
    last = jnp.maximum(total_blk - 1, 0)
    bq = jnp.where(valid, bp, last)
    gq, nq, eq = _take(g0, bq), jnp.maximum(_take(nsub, bq), 1), _take(blk_e, bq)
    local = p - nj * gq
    jq = jnp.where(valid, local // nq, nj - 1)
    rq = jnp.where(valid, local % nq, nq - 1)
    item_xb = gq + rq
    item_ob = jnp.where(valid, jnp.where(jq == nj - 1, gq + rq, gq), bp)
    half = _take(half_b.astype(jnp.int32), gq + rq) != 0
    flags = (valid.astype(jnp.int32) + 2 * (valid & (rq == 0)).astype(jnp.int32)
             + 4 * (valid & half).astype(jnp.int32))
    nxt_blk = gq + nq
    has_next = nxt_blk < total_blk
    last_j = jq == nj - 1
    e_n = jnp.where(last_j & has_next, _take(blk_e, jnp.minimum(nxt_blk, nb_max - 1)), eq)
    j_n = jnp.where(last_j, jnp.where(has_next, 0, jq), jq + 1)
    cur, nxt = eq * nj + jq, e_n * nj + j_n
    wg_code = jnp.where(valid & (rq >= 1), nxt, cur)
    wu_code = jnp.where(valid & (rq >= 2), nxt, cur)
    items = (eq.astype(jnp.int32), jq.astype(jnp.int32), item_xb.astype(jnp.int32),
             item_ob.astype(jnp.int32), rq.astype(jnp.int32), flags,
             wg_code.astype(jnp.int32), wu_code.astype(jnp.int32))
    return dest.astype(jnp.int32), pad_lo, pad_len, total_blk.astype(jnp.int32), nb_max, items


def _win_prep_kernel(w_ref, o_ref):
    o_a = QKV_W + V_W
    o_u = o_a + 2 * DN_HEADS
    x = w_ref[...]
    rest = x.shape[1] - o_u
    o_ref[:, 0:o_a] = x[:, 0:o_a].astype(BF16)
    o_ref[:, o_a:o_a + rest] = x[:, o_u:].astype(BF16)
    lane = lax.broadcasted_iota(jnp.int32, (x.shape[0], LANES), 1)
    ab = o_a + rest
    o_ref[:, ab:ab + LANES] = jnp.where(lane < 2 * DN_HEADS, x[:, o_a:o_a + LANES], 0.0).astype(BF16)
    if o_ref.shape[1] > ab + LANES:
        o_ref[:, ab + LANES:] = jnp.zeros((x.shape[0], o_ref.shape[1] - ab - LANES), BF16)


def _rearranged_w_in(w_in, d):
    dm, nw = w_in.shape
    tr = 128
    nw_out = -(-(nw - 2 * DN_HEADS + LANES) // PROJ_TN) * PROJ_TN
    w = pl.pallas_call(
        _win_prep_kernel,
        grid=(dm // tr,),
        in_specs=[pl.BlockSpec((tr, nw), lambda i: (i, 0))],
        out_specs=pl.BlockSpec((tr, nw_out), lambda i: (i, 0)),
        out_shape=jax.ShapeDtypeStruct((dm, nw_out), BF16),
        compiler_params=_params("arbitrary"),
        name="w_in_prep",
    )(w_in)
    cols = dict(z=QKV_W, u=QKV_W + V_W, v=QKV_W + V_W + SG_W, ga=QKV_W + V_W + 2 * SG_W,
                gb=QKV_W + V_W + 2 * SG_W + d, ab=QKV_W + V_W + 2 * SG_W + 2 * d)
    return w, cols


def kernel(x_prompt, x_sample, state_conv_qkv, state_delta, c_prompt, c_sample, w_ada, b_ada, w_in, w_conv, a_log, dt_bias, o_norm_g, sg_ln_g, sg_ln_b, w_s, b_s, p_a, p_b, w_out, ln1_g, ln1_b, router_w, router_b, w_gu, b_gu, w_dn, b_dn, ln2_g, ln2_b):
    depth = w_ada.shape[0]
    alpha = float((2 * depth) ** 0.25)
    bp, seq, d = x_prompt.shape
    bs, lt, _ = x_sample.shape
    assert lt == DN_CONV and seq % SG_CHUNK == 0
    n_experts = router_w.shape[2]
    rows_p, rows_s = bp * seq, bs * lt
    xp = x_prompt.reshape(rows_p, d)
    xs = x_sample.reshape(rows_s, d)
    c_all = jnp.concatenate([jnp.repeat(c_sample, lt, axis=0), c_prompt,
                             jnp.zeros((-(rows_s + bp) % SUBLANES, d), F32)], axis=0)
    tile = lambda cap: _Rows(math.gcd(math.gcd(cap, rows_s), seq), rows_p, rows_s, seq)
    rt_proj, rt_sgu, rt_merge, rt_comb = tile(PROJ_TM), tile(SG_CHUNK), tile(MERGE_TM), tile(COMB_TQ)
    outs = dict(conv_p=[], delta_p=[], conv_s=[], delta_s=[], vrows=[])

    for l in range(depth):
        mod = _adaln(c_all, w_ada[l], b_ada[l])

        w_r, col = _rearranged_w_in(w_in[l], d)
        proj = _inproj(rt_proj, xp, xs, mod, mod, w_r)

        alog = jnp.pad(a_log[l:l + 1], ((0, 0), (0, LANES - DN_HEADS)))
        dtb = jnp.pad(dt_bias[l:l + 1], ((0, 0), (0, LANES - DN_HEADS)))
        og = o_norm_g[l:l + 1]
        conv0 = jnp.zeros((bp, DN_CONV - 1, QKV_W), F32)
        s0 = jnp.zeros((bp, DN_HEADS, DN_DK, DN_DV), F32)
        oa_p, conv_p, delta_p = _gdn_prompt(proj, conv0, s0, w_conv[l], alog, dtb, og, bp, seq,
                                            col['z'], col['ab'])
        oa_s, conv_s, delta_s = _gdn_sample(proj, rows_p, state_conv_qkv[l], state_delta[l], w_conv[l],
                                            alog, dtb, og, bs, col['z'], col['ab'])

        ts = rt_sgu.tm
        assert ts == SG_CHUNK
        ws_p = jnp.tril(w_s[l][:, :ts, :ts]).astype(BF16)
        bias_p = jnp.repeat(b_s[l].T[:ts], SG_CH, axis=1)
        eye = jnp.eye(ts // lt, dtype=F32)
        ws_s = jnp.stack([jnp.kron(eye, jnp.tril(w_s[l, g, :lt, :lt])) for g in range(SG_GROUPS)]).astype(BF16)
        bias_s = jnp.tile(jnp.repeat(b_s[l, :, :lt].T, SG_CH, axis=1), (ts // lt, 1))
        ob, vn_s = _sgu(rt_sgu, proj, sg_ln_g[l:l + 1], sg_ln_b[l:l + 1],
                        ws_p, ws_s, bias_p, bias_s, col['u'], col['v'])

        pa, pb, wo = p_a[l].astype(BF16), p_b[l].astype(BF16), w_out[l].astype(BF16)
        rws = _split(jnp.pad(router_w[l], ((0, 0), (0, LANES - n_experts))))
        rw = jnp.concatenate([rws[0], rws[0], rws[1]], axis=0)
        rb = jnp.pad(router_b[l:l + 1], ((0, 0), (0, LANES - n_experts)))
        x1, h2, ti, tg = _merge(alpha, n_experts, rt_merge, oa_p, oa_s, ob, proj, xp, xs,
                                mod, mod,
                                pa, pb, wo, ln1_g[l:l + 1], ln1_b[l:l + 1], rw, rb,
                                col['ga'], col['gb'])

        nj = w_dn.shape[2] // MOE_TF
        dest, pad_lo, pad_len, nblk_used, nb_max, items = _routing(ti[:, :TOP_K], n_experts, MOE_TM, nj, MOE_SUBS)
        xb = _dispatch(pad_lo, pad_len, nblk_used, dest, h2, MOE_TM, nb_max)
        yb = _moe(items, xb, w_gu[l], b_gu[l], w_dn[l], b_dn[l], MOE_TM, MOE_TF, MOE_SUBS)
        xp, xs = _combine(alpha, rt_comb, dest, yb, tg, x1, mod, mod,
                          ln2_g[l:l + 1], ln2_b[l:l + 1])

        outs['conv_p'].append(conv_p)
        outs['delta_p'].append(delta_p)
        outs['conv_s'].append(conv_s)
        outs['delta_s'].append(delta_s)
        outs['vrows'].append(vn_s.reshape(bs, lt, SG_W))

    return (xp.reshape(bp, seq, d), xs.reshape(bs, lt, d),
            jnp.stack(outs['conv_p']), jnp.stack(outs['delta_p']),
            jnp.stack(outs['conv_s']), jnp.stack(outs['delta_s']), jnp.stack(outs['vrows']))
```

```python
import functools
import math

import jax
import jax.numpy as jnp
from jax import lax
from jax.experimental import pallas as pl
from jax.experimental.pallas import tpu as pltpu

F32 = jnp.float32
BF16 = jnp.bfloat16
HIGHEST = lax.Precision.HIGHEST

DN_HEADS = 8
DN_DK = 128
DN_DV = 128
DN_CONV = 4
DN_CHUNK = 64
SG_GROUPS = 8
SG_CH = 128
SG_CHUNK = 128
TOP_K = 4
SWIGLU_LIMIT = 7.0
SWIGLU_ALPHA = 1.702
LN_EPS = 1e-5
NORM_EPS = 1e-6
QK_W = DN_HEADS * DN_DK
V_W = DN_HEADS * DN_DV
QKV_W = 2 * QK_W + V_W
SG_W = SG_GROUPS * SG_CH

LANES = 128
SUBLANES = 8
VMEM_LIMIT = 56 * 1024 * 1024

PROJ_TM = 512
PROJ_TN = 1536
MERGE_TM = 256
MOE_TM = 256
MOE_TF = 512
MOE_SUBS = 6
COMB_TQ = 128
SAMPLE_BT = 4
SAMPLE_TP = 8
GDN_HG = 4

MOD_SHIFT1, MOD_SCALE1, MOD_GATE1, MOD_SHIFT2, MOD_SCALE2, MOD_GATE2 = range(6)
MOD_PARTS = 6


def _sigmoid(x):
    return 1.0 / (1.0 + jnp.exp(-x))


def _silu(x):
    return x * _sigmoid(x)


def _dot(a, b):
    return jnp.dot(a.astype(BF16), b.astype(BF16), preferred_element_type=F32)


def _dot_nt(a, b):
    return lax.dot_general(a.astype(BF16), b.astype(BF16), (((1,), (1,)), ((), ())),
                           preferred_element_type=F32)


def _dot_hi(a, b):
    return jnp.dot(a, b, precision=HIGHEST, preferred_element_type=F32)


def _layer_norm(x, g, b):
    mu = jnp.mean(x, axis=-1, keepdims=True)
    xc = x - mu
    var = jnp.mean(xc * xc, axis=-1, keepdims=True)
    return xc * lax.rsqrt(var + LN_EPS) * g + b


def _params(*sem):
    return pltpu.CompilerParams(dimension_semantics=sem, vmem_limit_bytes=VMEM_LIMIT)


class _Rows:
    def __init__(self, tm, rows_p, rows_s, seq):
        assert rows_p % tm == 0 and rows_s % tm == 0 and seq % tm == 0, (tm, rows_p, rows_s, seq)
        self.tm, self.n_p, self.n_s = tm, rows_p // tm, rows_s // tm
        self.tiles_per_seq = seq // tm
        self.bp = rows_p // seq
        assert rows_s % SUBLANES == 0
        self.rows_s = rows_s

    @property
    def n(self):
        return self.n_p + self.n_s

    def prompt(self, width, col=0):
        return pl.BlockSpec((self.tm, width), lambda i, *_: (jnp.minimum(i, self.n_p - 1), col))

    def sample(self, width, col=0, single=True):
        mode = dict(pipeline_mode=pl.Buffered(1)) if single else {}
        return pl.BlockSpec((self.tm, width), lambda i, *_: (jnp.maximum(i - self.n_p, 0), col), **mode)

    def joint(self, width, col=0):
        return pl.BlockSpec((self.tm, width), lambda i, *_: (i, col))

    def _seq(self, i):
        return jnp.minimum(i // self.tiles_per_seq, self.bp - 1)

    def seq_vec(self, d, part):
        return pl.BlockSpec((SUBLANES, d), lambda i, *_: ((self.rows_s + self._seq(i)) // SUBLANES, part))

    def row_vec(self, d, part):
        return pl.BlockSpec((self.tm, d), lambda i, *_: (jnp.maximum(i - self.n_p, 0), part),
                            pipeline_mode=pl.Buffered(1))

    def pick_vec(self, i, seq_ref, row_ref):
        prompt = seq_ref[pl.ds(self._seq(i) % SUBLANES, 1), :]
        return jnp.where(i >= self.n_p, row_ref[...], prompt)


def _pick(is_sample, prompt_ref, sample_ref):
    return jnp.where(is_sample, sample_ref[...], prompt_ref[...])


def _adaln_kernel(c_ref, w_ref, b_ref, o_ref):
    o_ref[...] = _dot(_silu(c_ref[...]), w_ref[...]) + b_ref[...]


def _adaln(c, w, b):
    rows, d = c.shape
    n = w.shape[1]
    tn = 1024
    return pl.pallas_call(
        _adaln_kernel,
        grid=(n // tn,),
        in_specs=[pl.BlockSpec((rows, d), lambda j: (0, 0)),
                  pl.BlockSpec((d, tn), lambda j: (0, j)),
                  pl.BlockSpec((1, tn), lambda j: (0, j))],
        out_specs=pl.BlockSpec((rows, tn), lambda j: (0, j)),
        out_shape=jax.ShapeDtypeStruct((rows, n), F32),
        compiler_params=_params("arbitrary"),
        name="adaln",
    )(c, w, b.reshape(1, n))


def _inproj_kernel(rt, xp_ref, xs_ref, scp_ref, scs_ref, shp_ref, shs_ref, w_ref, o_ref, h_scr):
    @pl.when(pl.program_id(1) == 0)
    def _():
        i = pl.program_id(0)
        x = _pick(i >= rt.n_p, xp_ref, xs_ref)
        h_scr[...] = (x * (1.0 + rt.pick_vec(i, scp_ref, scs_ref)) + rt.pick_vec(i, shp_ref, shs_ref)).astype(BF16)

    o_ref[...] = lax.dot_general(h_scr[...], w_ref[...], (((1,), (1,)), ((), ())), preferred_element_type=F32)


def _inproj(rt, xp, xs, mod_seq, mod_row, w):
    d = xp.shape[1]
    nw = w.shape[0]
    tm = rt.tm
    return pl.pallas_call(
        functools.partial(_inproj_kernel, rt),
        grid=(rt.n, nw // PROJ_TN),
        in_specs=[rt.prompt(d), rt.sample(d), rt.seq_vec(d, MOD_SCALE1), rt.row_vec(d, MOD_SCALE1),
                  rt.seq_vec(d, MOD_SHIFT1), rt.row_vec(d, MOD_SHIFT1),
                  pl.BlockSpec((PROJ_TN, d), lambda i, j: (j, 0))],
        out_specs=pl.BlockSpec((tm, PROJ_TN), lambda i, j: (i, j)),
        out_shape=jax.ShapeDtypeStruct((rt.n * tm, nw), F32),
        scratch_shapes=[pltpu.VMEM((tm, d), BF16)],
        compiler_params=_params("arbitrary", "arbitrary"),
        name="inproj",
    )(xp, xs, mod_seq, mod_row, mod_seq, mod_row, w)


def _softplus(x):
    return jnp.maximum(x, 0.0) + jnp.log1p(jnp.exp(-jnp.abs(x)))


def _split(a):
    hi = a.astype(BF16)
    lo = (a - hi.astype(F32)).astype(BF16)
    return hi, lo


def _dot3(a, b):
    lhs = jnp.concatenate([a[0], a[1], a[0]], axis=1)
    rhs = jnp.concatenate([b[0], b[0], b[1]], axis=0)
    return jnp.dot(lhs, rhs, preferred_element_type=F32)


def _map(f, *lists):
    return [f(*args) for args in zip(*lists)]


def _unit_lower_inverses_minus_eye(mats, rows, cols):
    same = (rows // 16) == (cols // 16)
    n = [jnp.where(same, a, 0.0) for a in mats]
    b = _map(lambda a, x: a - x, mats, n)
    n2 = _map(lambda x: _dot(x, x), n)
    n4 = _map(lambda x: _dot(x, x), n2)
    n8 = _map(lambda x: _dot(x, x), n4)
    r = [-x for x in n]
    r = _map(lambda x, p: x + p + _dot(x, p), r, n2)
    r = _map(lambda x, p: x + p + _dot(x, p), r, n4)
    dm = _map(lambda x, p: x + p + _dot(x, p), r, n8)
    m = _map(lambda x, y: y + _dot(x, y), dm, b)
    m2 = _map(lambda x: _dot(x, x), m)
    xm = _map(lambda x, p: x + p + _dot(p, x), dm, m2)
    return _map(lambda x, p: x - p - _dot(p, x), xm, m)


def _gdn_prompt_kernel(qkv_ref, z_ref, ab_ref, hist_ref, s0_ref, wconv_ref, alog_ref, dt_ref, og_ref,
                       o_ref, conv_ref, s_ref, xp_scr):
    n = pl.program_id(1)
    c = DN_CHUNK
    pad = SUBLANES
    nr = GDN_HG * c
    ngrp = DN_HEADS // GDN_HG

    @pl.when(n == 0)
    def _():
        xp_scr[pl.ds(0, pad), :] = jnp.zeros((pad, QKV_W), F32)
        xp_scr[pl.ds(pad - (DN_CONV - 1), DN_CONV - 1), :] = hist_ref[...]
        s_ref[...] = s0_ref[...]

    x = qkv_ref[...]
    xp_scr[pl.ds(pad, c), :] = x
    y = jnp.zeros((c, QKV_W), F32)
    for j in range(DN_CONV):
        y = y + xp_scr[pl.ds(pad - (DN_CONV - 1) + j, c), :] * wconv_ref[pl.ds(j, 1), :]
    y = _silu(y)
    tail = xp_scr[pl.ds(c + pad - (DN_CONV - 1), DN_CONV - 1), :]
    conv_ref[...] = tail
    xp_scr[pl.ds(pad - (DN_CONV - 1), DN_CONV - 1), :] = tail

    ab = ab_ref[...]
    g = -jnp.exp(alog_ref[...]) * _softplus(ab + dt_ref[...])
    beta_all = _sigmoid(ab)
    r64 = lax.broadcasted_iota(jnp.int32, (c, c), 0)
    c64 = lax.broadcasted_iota(jnp.int32, (c, c), 1)
    gc = _dot_hi((r64 >= c64).astype(F32), g)
    z = z_ref[...]
    og = og_ref[...]

    rows = lax.broadcasted_iota(jnp.int32, (nr, nr), 0)
    cols = lax.broadcasted_iota(jnp.int32, (nr, nr), 1)
    same = (rows // c) == (cols // c)
    causal = same & (rows >= cols)
    strict = same & (rows > cols)
    rowhead = lax.broadcasted_iota(jnp.int32, (nr, 1), 0) // c
    groups = [range(grp * GDN_HG, (grp + 1) * GDN_HG) for grp in range(ngrp)]

    def stack(heads, off, width):
        return jnp.concatenate([y[:, off + h * width:off + (h + 1) * width] for h in heads], axis=0)

    def l2n(t):
        return t * lax.rsqrt(jnp.sum(t * t, axis=-1, keepdims=True) + NORM_EPS)

    q = [l2n(stack(hs, 0, DN_DK)) * (DN_DK ** -0.5) for hs in groups]
    k = [l2n(stack(hs, QK_W, DN_DK)) for hs in groups]
    v = [stack(hs, 2 * QK_W, DN_DV) for hs in groups]
    beta = [jnp.concatenate([beta_all[:, DN_HEADS + h:DN_HEADS + h + 1] for h in hs], axis=0) for hs in groups]
    gcf = [jnp.concatenate([jnp.broadcast_to(gc[:, h:h + 1], (c, LANES)) for h in hs], axis=0) for hs in groups]
    gcc = [t[:, 0:1] for t in gcf]
    grow = [t.T[0:1, :] for t in gcf]
    glast = [[gc[c - 1:c, h:h + 1] for h in hs] for hs in groups]
    gtot = [jnp.concatenate([jnp.broadcast_to(t, (c, 1)) for t in gl], axis=0) for gl in glast]
    decay = _map(lambda a, b: jnp.where(causal, jnp.exp(jnp.where(causal, a - b, 0.0)), 0.0), gcc, grow)
    egc = _map(jnp.exp, gcc)
    kb = _map(lambda a, b: a * b, k, beta)
    a_mat = _map(lambda a, b, dd: jnp.where(strict, _dot_nt(a, b) * dd, 0.0), kb, k, decay)
    tm1 = _unit_lower_inverses_minus_eye(a_mat, rows, cols)
    u = _map(lambda t, a, b: a * b + _dot(t, a * b), tm1, v, beta)
    w = _map(lambda t, a, b: a * b + _dot(t, a * b), tm1, kb, egc)
    attn = _map(lambda a, b, dd: _dot_nt(a, b) * dd, q, k, decay)
    qg = _map(lambda a, b: a * b, q, egc)
    kd_t = _map(lambda a, b, cc: (a * jnp.exp(b - cc)).T, k, gtot, gcc)

    for gi, hs in enumerate(groups):
        v_news, qss = [], []
        for hl, h in enumerate(hs):
            sl = slice(hl * c, (hl + 1) * c)
            rs = _dot(jnp.concatenate([w[gi][sl], qg[gi][sl]], axis=0), s_ref[h])
            v_news.append(u[gi][sl] - rs[:c])
            qss.append(rs[c:])
        v_new = jnp.concatenate(v_news, axis=0)
        o = jnp.concatenate(qss, axis=0) + _dot(attn[gi], v_new)
        for hl, h in enumerate(hs):
            upd = _dot(kd_t[gi], jnp.where(rowhead == hl, v_new, 0.0))
            s_ref[h] = s_ref[h] * jnp.exp(glast[gi][hl]) + upd
        o = o * lax.rsqrt(jnp.mean(o * o, axis=-1, keepdims=True) + NORM_EPS) * og
        for hl, h in enumerate(hs):
            o_ref[:, h * DN_DV:(h + 1) * DN_DV] = o[hl * c:(hl + 1) * c] * _silu(z[:, h * DN_DV:(h + 1) * DN_DV])


def _gdn_prompt(proj, hist, s0, wconv, alog, dtb, og, batch, seq, col_z, col_ab):
    c = DN_CHUNK
    nchunk = seq // c
    rows = batch * seq
    row = lambda b, n: b * nchunk + n
    return pl.pallas_call(
        _gdn_prompt_kernel,
        grid=(batch, nchunk),
        in_specs=[pl.BlockSpec((c, QKV_W), lambda b, n: (row(b, n), 0)),
                  pl.BlockSpec((c, V_W), lambda b, n: (row(b, n), col_z // V_W)),
                  pl.BlockSpec((c, LANES), lambda b, n: (row(b, n), col_ab // LANES)),
                  pl.BlockSpec((None, DN_CONV - 1, QKV_W), lambda b, n: (b, 0, 0)),
                  pl.BlockSpec((None, DN_HEADS, DN_DK, DN_DV), lambda b, n: (b, 0, 0, 0)),
                  pl.BlockSpec((DN_CONV, QKV_W), lambda b, n: (0, 0)),
                  pl.BlockSpec((1, LANES), lambda b, n: (0, 0)),
                  pl.BlockSpec((1, LANES), lambda b, n: (0, 0)),
                  pl.BlockSpec((1, DN_DV), lambda b, n: (0, 0))],
        out_specs=[pl.BlockSpec((c, V_W), lambda b, n: (row(b, n), 0)),
                   pl.BlockSpec((None, DN_CONV - 1, QKV_W), lambda b, n: (b, 0, 0)),
                   pl.BlockSpec((None, DN_HEADS, DN_DK, DN_DV), lambda b, n: (b, 0, 0, 0))],
        out_shape=[jax.ShapeDtypeStruct((rows, V_W), F32),
                   jax.ShapeDtypeStruct((batch, DN_CONV - 1, QKV_W), F32),
                   jax.ShapeDtypeStruct((batch, DN_HEADS, DN_DK, DN_DV), F32)],
        scratch_shapes=[pltpu.VMEM((c + SUBLANES, QKV_W), F32)],
        compiler_params=_params("arbitrary", "arbitrary"),
        name="gdn_prompt",
    )(proj, proj, proj, hist, s0, wconv, alog, dtb, og)


def _gdn_sample_kernel(qkv_ref, z_ref, ab_ref, hist_ref, s0_ref, wconv_ref, alog_ref, dt_ref, og_ref,
                       o_ref, conv_ref, s_ref, xp_scr, ab_scr, wq_scr, r_scr, kdt_scr, vn_scr, gl_scr):
    bt, tp, lt = SAMPLE_BT, SAMPLE_TP, DN_CONV
    nblk = DN_HEADS * bt
    nrow = nblk * tp
    hist_rows = DN_CONV - 1

    xp_scr[...] = jnp.zeros(xp_scr.shape, F32)
    ab_scr[...] = jnp.zeros(ab_scr.shape, F32)
    ys = []
    for b in range(bt):
        xp_scr[b, pl.ds(SUBLANES - hist_rows, hist_rows), :] = hist_ref[b]
        xp_scr[b, pl.ds(SUBLANES, lt), :] = qkv_ref[pl.ds(b * lt, lt), :]
        yb = jnp.zeros((tp, QKV_W), F32)
        for j in range(DN_CONV):
            yb = yb + xp_scr[b, pl.ds(SUBLANES - hist_rows + j, tp), :] * wconv_ref[pl.ds(j, 1), :]
        ys.append(_silu(yb))
        conv_ref[b] = xp_scr[b, pl.ds(SUBLANES + lt - hist_rows, hist_rows), :]
        ab_scr[b, pl.ds(0, lt), :] = ab_ref[pl.ds(b * lt, lt), :]
    y = jnp.concatenate(ys, axis=0)
    ab = jnp.concatenate([ab_scr[b] for b in range(bt)], axis=0)
    tok = lax.broadcasted_iota(jnp.int32, (bt * tp, 1), 0) % tp
    real = tok < lt
    g_all = jnp.where(real, -jnp.exp(alog_ref[...]) * _softplus(ab + dt_ref[...]), 0.0)
    beta_all = jnp.where(real, _sigmoid(ab), 0.0)

    def heads_to_rows(t, off, width):
        return jnp.concatenate([t[:, off + h * width:off + (h + 1) * width] for h in range(DN_HEADS)], axis=0)

    realr = jnp.concatenate([real] * DN_HEADS, axis=0)
    q = jnp.where(realr, heads_to_rows(y, 0, DN_DK), 0.0)
    k = jnp.where(realr, heads_to_rows(y, QK_W, DN_DK), 0.0)
    v = jnp.where(realr, heads_to_rows(y, 2 * QK_W, DN_DV), 0.0)
    gcol = jnp.concatenate([g_all[:, h:h + 1] for h in range(DN_HEADS)], axis=0)
    beta = jnp.concatenate([beta_all[:, DN_HEADS + h:DN_HEADS + h + 1] for h in range(DN_HEADS)], axis=0)
    q = q * lax.rsqrt(jnp.sum(q * q, axis=-1, keepdims=True) + NORM_EPS) * (DN_DK ** -0.5)
    k = k * lax.rsqrt(jnp.sum(k * k, axis=-1, keepdims=True) + NORM_EPS)

    rows = lax.broadcasted_iota(jnp.int32, (nrow, nrow), 0)
    cols = lax.broadcasted_iota(jnp.int32, (nrow, nrow), 1)
    same = (rows // tp) == (cols // tp)
    causal = same & (rows >= cols)
    strict = same & (rows > cols)
    gfull = jnp.broadcast_to(gcol, (nrow, LANES))
    gc = _dot_hi(causal.astype(F32), gfull)
    gtot = _dot_hi(same.astype(F32), gfull)
    gcc = gc[:, 0:1]
    grow = gc.T[0:1, :]
    decay = jnp.where(causal, jnp.exp(jnp.where(causal, gcc - grow, 0.0)), 0.0)
    egc = jnp.exp(gcc)
    kb = k * beta
    a_mat = jnp.where(strict, _dot_nt(kb, k) * decay, 0.0)
    a2 = _dot(a_mat, a_mat)
    tm1 = a2 - a_mat - _dot(a_mat, a2)
    u = v * beta + _dot(tm1, v * beta)
    w = kb * egc + _dot(tm1, kb * egc)
    attn = _dot_nt(q, k) * decay
    qg = q * egc
    for i in range(nblk):
        wq_scr[pl.ds(2 * tp * i, tp), :] = w[i * tp:(i + 1) * tp, :]
        wq_scr[pl.ds(2 * tp * i + tp, tp), :] = qg[i * tp:(i + 1) * tp, :]
    kdt_scr[...] = (k * jnp.exp(gtot[:, 0:1] - gcc)).T
    gl_scr[...] = jnp.exp(gtot)

    def read_state(i, carry):
        r0 = pl.multiple_of(i * 2 * tp, 2 * tp)
        r_scr[pl.ds(r0, 2 * tp), :] = _dot(wq_scr[pl.ds(r0, 2 * tp), :], s0_ref[i % bt, i // bt])
        return carry

    lax.fori_loop(0, nblk, read_state, 0, unroll=4)
    ws = jnp.concatenate([r_scr[pl.ds(2 * tp * i, tp), :] for i in range(nblk)], axis=0)
    qs = jnp.concatenate([r_scr[pl.ds(2 * tp * i + tp, tp), :] for i in range(nblk)], axis=0)
    v_new = u - ws
    vn_scr[...] = v_new
    o = qs + _dot(attn, v_new)
    blockid = lax.broadcasted_iota(jnp.int32, (nrow, 1), 0) // tp

    def write_state(i, carry):
        r0 = pl.multiple_of(i * tp, tp)
        upd = _dot(kdt_scr[...], jnp.where(blockid == i, vn_scr[...], 0.0))
        s_ref[i % bt, i // bt] = s0_ref[i % bt, i // bt] * gl_scr[pl.ds(r0, 1), :] + upd
        return carry

    lax.fori_loop(0, nblk, write_state, 0, unroll=4)

    og = og_ref[...]
    o = o * lax.rsqrt(jnp.mean(o * o, axis=-1, keepdims=True) + NORM_EPS) * og
    for b in range(bt):
        for h in range(DN_HEADS):
            zbh = z_ref[pl.ds(b * lt, lt), h * DN_DV:(h + 1) * DN_DV]
            blk = o[(h * bt + b) * tp:(h * bt + b) * tp + lt, :]
            o_ref[pl.ds(b * lt, lt), h * DN_DV:(h + 1) * DN_DV] = blk * _silu(zbh)


def _gdn_sample(proj, row0, hist, s0, wconv, alog, dtb, og, batch, col_z, col_ab):
    bt, tp, lt = SAMPLE_BT, SAMPLE_TP, DN_CONV
    rows = batch * lt
    nrow = DN_HEADS * bt * tp
    blk = bt * lt
    assert row0 % blk == 0 and batch % bt == 0
    b0 = row0 // blk
    return pl.pallas_call(
        _gdn_sample_kernel,
        grid=(batch // bt,),
        in_specs=[pl.BlockSpec((blk, QKV_W), lambda i: (b0 + i, 0)),
                  pl.BlockSpec((blk, V_W), lambda i: (b0 + i, col_z // V_W)),
                  pl.BlockSpec((blk, LANES), lambda i: (b0 + i, col_ab // LANES)),
                  pl.BlockSpec((bt, DN_CONV - 1, QKV_W), lambda i: (i, 0, 0)),
                  pl.BlockSpec((bt, DN_HEADS, DN_DK, DN_DV), lambda i: (i, 0, 0, 0)),
                  pl.BlockSpec((DN_CONV, QKV_W), lambda i: (0, 0)),
                  pl.BlockSpec((1, LANES), lambda i: (0, 0)),
                  pl.BlockSpec((1, LANES), lambda i: (0, 0)),
                  pl.BlockSpec((1, DN_DV), lambda i: (0, 0))],
        out_specs=[pl.BlockSpec((blk, V_W), lambda i: (i, 0)),
                   pl.BlockSpec((bt, DN_CONV - 1, QKV_W), lambda i: (i, 0, 0)),
                   pl.BlockSpec((bt, DN_HEADS, DN_DK, DN_DV), lambda i: (i, 0, 0, 0))],
        out_shape=[jax.ShapeDtypeStruct((rows, V_W), F32),
                   jax.ShapeDtypeStruct((batch, DN_CONV - 1, QKV_W), F32),
                   jax.ShapeDtypeStruct((batch, DN_HEADS, DN_DK, DN_DV), F32)],
        scratch_shapes=[pltpu.VMEM((bt, SUBLANES + tp, QKV_W), F32),
                        pltpu.VMEM((bt, tp, LANES), F32),
                        pltpu.VMEM((2 * nrow, DN_DK), F32),
                        pltpu.VMEM((2 * nrow, DN_DV), F32),
                        pltpu.VMEM((DN_DK, nrow), F32),
                        pltpu.VMEM((nrow, DN_DV), F32),
                        pltpu.VMEM((nrow, LANES), F32)],
        compiler_params=_params("arbitrary"),
        name="gdn_sample",
    )(proj, proj, proj, hist, s0, wconv, alog, dtb, og)


def _sgu_kernel(n_p, u_ref, v_ref, g_ref, b_ref, wsp_ref, wss_ref, bp_ref, bs_ref, o_ref, vn_ref):
    is_s = pl.program_id(0) >= n_p
    u = jax.nn.gelu(u_ref[...])
    vn = _layer_norm(jax.nn.gelu(v_ref[...]), g_ref[...], b_ref[...])
    vn_ref[...] = vn
    bias = _pick(is_s, bp_ref, bs_ref)
    for g in range(SG_GROUPS):
        sl = slice(g * SG_CH, (g + 1) * SG_CH)
        ws = jnp.where(is_s, wss_ref[g], wsp_ref[g])
        mixed = _dot(ws, vn[:, sl]) + bias[:, sl]
        o_ref[:, sl] = u[:, sl] * mixed


def _sgu(rt, proj, ln_g, ln_b, ws_p, ws_s, bias_p, bias_s, col_u, col_v):
    t = rt.tm
    const = lambda shape: pl.BlockSpec(shape, lambda i: (0,) * len(shape))
    return pl.pallas_call(
        functools.partial(_sgu_kernel, rt.n_p),
        grid=(rt.n,),
        in_specs=[rt.joint(SG_W, col_u // SG_W), rt.joint(SG_W, col_v // SG_W),
                  const((1, SG_W)), const((1, SG_W)),
                  const((SG_GROUPS, t, t)), const((SG_GROUPS, t, t)), const((t, SG_W)), const((t, SG_W))],
        out_specs=[rt.joint(SG_W), rt.sample(SG_W, single=False)],
        out_shape=[jax.ShapeDtypeStruct((rt.n * t, SG_W), F32),
                   jax.ShapeDtypeStruct((rt.n_s * t, SG_W), F32)],
        compiler_params=_params("arbitrary"),
        name="sgu",
    )(proj, proj, ln_g, ln_b, ws_p, ws_s, bias_p, bias_s)


def _merge_kernel(alpha, n_experts, rt, oap_ref, oas_ref, ob_ref, ga_ref, gb_ref, xp_ref, xs_ref,
                  gtp_ref, gts_ref, scp_ref, scs_ref, shp_ref, shs_ref,
                  pa_ref, pb_ref, wo_ref, lg_ref, lb_ref, rw_ref, rb_ref,
                  x1_ref, h2_ref, ti_ref, tg_ref):
    i = pl.program_id(0)
    is_s = i >= rt.n_p
    oa = _pick(is_s, oap_ref, oas_ref)
    merged = (_sigmoid(ga_ref[...]) * _dot(oa, pa_ref[...])
              + _sigmoid(gb_ref[...]) * _dot(ob_ref[...], pb_ref[...]))
    y = _dot(merged, wo_ref[...])
    x = _pick(is_s, xp_ref, xs_ref)
    x1 = _layer_norm(alpha * x + rt.pick_vec(i, gtp_ref, gts_ref) * y, lg_ref[...], lb_ref[...])
    x1_ref[...] = x1
    h2 = x1 * (1.0 + rt.pick_vec(i, scp_ref, scs_ref)) + rt.pick_vec(i, shp_ref, shs_ref)
    h2_ref[...] = h2
    hs = _split(h2)
    logits = jnp.dot(jnp.concatenate([hs[0], hs[1], hs[0]], axis=1), rw_ref[...],
                     preferred_element_type=F32) + rb_ref[...]
    lane = lax.broadcasted_iota(jnp.int32, logits.shape, 1)
    logits = jnp.where(lane < n_experts, logits, -jnp.inf)
    ti = jnp.zeros(logits.shape, jnp.int32)
    tv = jnp.zeros(logits.shape, F32)
    top = None
    for kk in range(TOP_K):
        m = jnp.max(logits, axis=-1, keepdims=True)
        idx = jnp.min(jnp.where(logits == m, lane.astype(F32), float(LANES)), axis=-1,
                      keepdims=True).astype(jnp.int32)
        if kk == 0:
            top = m
        ti = jnp.where(lane == kk, idx, ti)
        tv = jnp.where(lane == kk, jnp.exp(m - top), tv)
        logits = jnp.where(lane == idx, -jnp.inf, logits)
    ti_ref[...] = ti
    tg_ref[...] = tv / jnp.sum(tv, axis=-1, keepdims=True)


def _merge(alpha, n_experts, rt, oa_p, oa_s, ob, proj, xp, xs, mod_seq, mod_row, pa, pb, wo, lg, lb, rw, rb,
           col_ga, col_gb):
    d = xp.shape[1]
    tm = rt.tm
    rows = rt.n * tm
    const = lambda shape: pl.BlockSpec(shape, lambda i: (0,) * len(shape), pipeline_mode=pl.Buffered(1))
    return pl.pallas_call(
        functools.partial(_merge_kernel, alpha, n_experts, rt),
        grid=(rt.n,),
        in_specs=[rt.prompt(V_W), rt.sample(V_W), rt.joint(SG_W),
                  rt.joint(d, col_ga // d), rt.joint(d, col_gb // d),
                  rt.prompt(d), rt.sample(d),
                  rt.seq_vec(d, MOD_GATE1), rt.row_vec(d, MOD_GATE1),
                  rt.seq_vec(d, MOD_SCALE2), rt.row_vec(d, MOD_SCALE2),
                  rt.seq_vec(d, MOD_SHIFT2), rt.row_vec(d, MOD_SHIFT2),
                  const((V_W, d)), const((SG_W, d)), const((d, d)),
                  const((1, d)), const((1, d)), const((3 * d, LANES)), const((1, LANES))],
        out_specs=[rt.joint(d), rt.joint(d), rt.joint(LANES), rt.joint(LANES)],
        out_shape=[jax.ShapeDtypeStruct((rows, d), F32),
                   jax.ShapeDtypeStruct((rows, d), F32),
                   jax.ShapeDtypeStruct((rows, LANES), jnp.int32),
                   jax.ShapeDtypeStruct((rows, LANES), F32)],
        compiler_params=_params("arbitrary"),
        name="merge",
    )(oa_p, oa_s, ob, proj, proj, xp, xs, mod_seq, mod_row, mod_seq, mod_row, mod_seq, mod_row,
      pa, pb, wo, lg, lb, rw, rb)


FILL_ROWS = (128, 64, 32, 16, 8)


def _dispatch_kernel(tm, n_experts, padlo_ref, padlen_ref, nblk_ref, dest_ref, h_ref, xb_hbm, zero_scr, sem, zsem):
    i = pl.program_id(0)
    tq = h_ref.shape[0]
    nb_max = xb_hbm.shape[0] // tm
    zrows = zero_scr.shape[0]

    def fill(act):
        def per_expert(e, carry):
            lo = padlo_ref[e]
            ln = padlen_ref[e]
            head = (SUBLANES - lo % SUBLANES) % SUBLANES
            for r in range(SUBLANES - 1):
                @pl.when(r < head)
                def _(r=r):
                    act(pltpu.make_async_copy(zero_scr.at[pl.ds(0, 1)], xb_hbm.at[pl.ds(lo + r, 1)], zsem))
            off = lo + head
            rem = ln - head
            for b in FILL_ROWS:
                @pl.when((rem & b) != 0)
                def _(off=off, b=b):
                    dst = xb_hbm.at[pl.ds(pl.multiple_of(off, SUBLANES), b)]
                    act(pltpu.make_async_copy(zero_scr.at[pl.ds(0, b)], dst, zsem))
                off = off + (rem & b)
            return carry

        lax.fori_loop(0, n_experts, per_expert, 0)

        def per_block(blk, carry):
            for part in range(tm // zrows):
                row0 = pl.multiple_of(blk * tm + part * zrows, zrows)
                act(pltpu.make_async_copy(zero_scr, xb_hbm.at[pl.ds(row0, zrows)], zsem))
            return carry

        lax.fori_loop(nblk_ref[0], nb_max, per_block, 0)

    @pl.when(i == 0)
    def _():
        zero_scr[...] = jnp.zeros(zero_scr.shape, F32)
        fill(lambda cp: cp.start())
        fill(lambda cp: cp.wait())

    def start(t, carry):
        for kk in range(TOP_K):
            pltpu.make_async_copy(h_ref.at[pl.ds(t, 1)], xb_hbm.at[pl.ds(dest_ref[0, t * TOP_K + kk], 1)],
                                  sem).start()
        return carry

    lax.fori_loop(0, tq, start, 0, unroll=4)
    for kk in range(TOP_K):
        pltpu.make_async_copy(h_ref, xb_hbm.at[pl.ds(0, tq)], sem).wait()


def _dispatch(padlo, padlen, nblk_used, dest, h2, tm, nb_max):
    t, d = h2.shape
    tq = math.gcd(256, t)
    n_experts = padlo.shape[0]
    grid_spec = pltpu.PrefetchScalarGridSpec(
        num_scalar_prefetch=3,
        grid=(t // tq,),
        in_specs=[pl.BlockSpec((None, 1, tq * TOP_K), lambda i, *_: (i, 0, 0), memory_space=pltpu.SMEM),
                  pl.BlockSpec((tq, d), lambda i, *_: (i, 0))],
        out_specs=pl.BlockSpec(memory_space=pl.ANY),
        scratch_shapes=[pltpu.VMEM((FILL_ROWS[0], d), F32), pltpu.SemaphoreType.DMA(()),
                        pltpu.SemaphoreType.DMA(())],
    )
    return pl.pallas_call(
        functools.partial(_dispatch_kernel, tm, n_experts),
        grid_spec=grid_spec,
        out_shape=jax.ShapeDtypeStruct((nb_max * tm, d), F32),
        compiler_params=pltpu.CompilerParams(dimension_semantics=("arbitrary",), vmem_limit_bytes=VMEM_LIMIT,
                                             has_side_effects=True),
        name="dispatch",
    )(padlo, padlen, nblk_used.reshape(1), dest.reshape(t // tq, 1, tq * TOP_K), h2)


def _moe_kernel(nj, e_ref, j_ref, xb_ref, ob_ref, r_ref, flag_ref, wg_code_ref, wu_code_ref,
                x_ref, wg_ref, wu_ref, wd_ref, bg_ref, bu_ref, bd_ref, o_ref,
                wg_scr, wu_scr, wd_scr, acc_scr):
    s = pl.program_id(0)
    flags = flag_ref[s]
    j = j_ref[s]
    r = r_ref[s]

    @pl.when((flags & 2) != 0)
    def _():
        wg_scr[...] = wg_ref[...].astype(BF16)
        wu_scr[...] = wu_ref[...].astype(BF16)
        wd_scr[...] = wd_ref[...].astype(BF16)

    @pl.when(flags == 0)
    def _():
        o_ref[...] = jnp.zeros(o_ref.shape, F32)

    @pl.when((flags & 1) != 0)
    def _():
        x = x_ref[...].astype(BF16)
        gate = jnp.dot(x, wg_scr[...], preferred_element_type=F32) + bg_ref[...]
        up = jnp.dot(x, wu_scr[...], preferred_element_type=F32) + bu_ref[...]
        gate = jnp.minimum(gate, SWIGLU_LIMIT)
        up = jnp.clip(up, -SWIGLU_LIMIT, SWIGLU_LIMIT)
        act = (up + 1.0) * gate * _sigmoid(SWIGLU_ALPHA * gate)
        y = jnp.dot(act.astype(BF16), wd_scr[...], preferred_element_type=F32)

        @pl.when(j == 0)
        def _():
            acc_scr[r] = y

        @pl.when(jnp.logical_and(j > 0, j < nj - 1))
        def _():
            acc_scr[r] = acc_scr[r] + y

        @pl.when(j == nj - 1)
        def _():
            o_ref[...] = acc_scr[r] + y + bd_ref[...]


def _moe(items, xb, w_gu, b_gu, w_dn, b_dn, tm, tf, subs):
    item_e, item_j, item_xb, item_ob, item_r, item_flag, wg_code, wu_code = items
    n_items = item_e.shape[0]
    nslot, d = xb.shape
    n_exp, _, f2 = w_gu.shape
    f = f2 // 2
    nj = f // tf
    assert nj >= 2
    grid_spec = pltpu.PrefetchScalarGridSpec(
        num_scalar_prefetch=8,
        grid=(n_items,),
        in_specs=[pl.BlockSpec((tm, d), lambda s, e, j, xbk, obk, r, fl, cg, cu: (xbk[s], 0)),
                  pl.BlockSpec((None, d, tf), lambda s, e, j, xbk, obk, r, fl, cg, cu: (cg[s] // nj, 0, cg[s] % nj)),
                  pl.BlockSpec((None, d, tf),
                               lambda s, e, j, xbk, obk, r, fl, cg, cu: (cu[s] // nj, 0, nj + cu[s] % nj)),
                  pl.BlockSpec((None, tf, d), lambda s, e, j, xbk, obk, r, fl, cg, cu: (e[s], j[s], 0)),
                  pl.BlockSpec((None, 1, tf), lambda s, e, j, xbk, obk, r, fl, cg, cu: (e[s], 0, j[s])),
                  pl.BlockSpec((None, 1, tf), lambda s, e, j, xbk, obk, r, fl, cg, cu: (e[s], 0, nj + j[s])),
                  pl.BlockSpec((None, 1, d), lambda s, e, j, xbk, obk, r, fl, cg, cu: (e[s], 0, 0))],
        out_specs=pl.BlockSpec((tm, d), lambda s, e, j, xbk, obk, r, fl, cg, cu: (obk[s], 0)),
        scratch_shapes=[pltpu.VMEM((d, tf), BF16), pltpu.VMEM((d, tf), BF16), pltpu.VMEM((tf, d), BF16),
                        pltpu.VMEM((subs, tm, d), F32)],
    )
    return pl.pallas_call(
        functools.partial(_moe_kernel, nj),
        grid_spec=grid_spec,
        out_shape=jax.ShapeDtypeStruct((nslot, d), F32),
        compiler_params=_params("arbitrary"),
        name="moe",
    )(item_e, item_j, item_xb, item_ob, item_r, item_flag, wg_code, wu_code,
      xb, w_gu, w_gu, w_dn, b_gu.reshape(n_exp, 1, f2), b_gu.reshape(n_exp, 1, f2), b_dn.reshape(n_exp, 1, d))


def _combine_kernel(alpha, rt, dest_ref, destn_ref, yb_hbm, tg_ref, x1_ref, gtp_ref, gts_ref, lg_ref, lb_ref,
                    op_ref, os_ref, buf, sem):
    i = pl.program_id(0)
    last = pl.num_programs(0) - 1
    tq = x1_ref.shape[0]
    slot = i % 2

    def start(ids_ref, sl, t):
        for kk in range(TOP_K):
            pltpu.make_async_copy(yb_hbm.at[pl.ds(ids_ref[0, t * TOP_K + kk], 1)],
                                  buf.at[sl, kk, pl.ds(t, 1)], sem.at[sl]).start()

    def wait(sl):
        for kk in range(TOP_K):
            pltpu.make_async_copy(yb_hbm.at[pl.ds(0, tq)], buf.at[sl, kk], sem.at[sl]).wait()

    @pl.when(i == 0)
    def _():
        lax.fori_loop(0, tq, lambda t, c: (start(dest_ref, 0, t), c)[1], 0, unroll=4)

    for t in range(tq):
        start(destn_ref, 1 - slot, t)
    wait(slot)
    tg = tg_ref[...]
    y = jnp.zeros(x1_ref.shape, F32)
    for kk in range(TOP_K):
        y = y + buf[slot, kk] * tg[:, kk:kk + 1]
    is_s = i >= rt.n_p
    out = _layer_norm(alpha * x1_ref[...] + rt.pick_vec(i, gtp_ref, gts_ref) * y, lg_ref[...], lb_ref[...])

    @pl.when(jnp.logical_not(is_s))
    def _():
        op_ref[...] = out

    @pl.when(is_s)
    def _():
        os_ref[...] = out

    @pl.when(i == last)
    def _():
        wait(1 - slot)


def _combine(alpha, rt, dest, yb, tg, x1, mod_seq, mod_row, lg, lb):
    d = x1.shape[1]
    tq = rt.tm
    ids = dest.reshape(rt.n, 1, tq * TOP_K)
    return pl.pallas_call(
        functools.partial(_combine_kernel, alpha, rt),
        grid=(rt.n,),
        in_specs=[pl.BlockSpec((None, 1, tq * TOP_K), lambda i: (i, 0, 0), memory_space=pltpu.SMEM),
                  pl.BlockSpec((None, 1, tq * TOP_K), lambda i: (jnp.minimum(i + 1, rt.n - 1), 0, 0),
                               memory_space=pltpu.SMEM),
                  pl.BlockSpec(memory_space=pl.ANY),
                  rt.joint(LANES), rt.joint(d), rt.seq_vec(d, MOD_GATE2), rt.row_vec(d, MOD_GATE2),
                  pl.BlockSpec((1, d), lambda i: (0, 0)),
                  pl.BlockSpec((1, d), lambda i: (0, 0))],
        out_specs=[rt.prompt(d), rt.sample(d, single=False)],
        out_shape=[jax.ShapeDtypeStruct((rt.n_p * tq, d), F32), jax.ShapeDtypeStruct((rt.n_s * tq, d), F32)],
        scratch_shapes=[pltpu.VMEM((2, TOP_K, tq, d), F32), pltpu.SemaphoreType.DMA((2,))],
        compiler_params=_params("arbitrary"),
        name="combine",
    )(ids, ids, yb, tg, x1, mod_seq, mod_row, lg, lb)


def _take(table, idx):
    hit = idx[:, None] == jnp.arange(table.shape[0], dtype=jnp.int32)[None, :]
    return jnp.sum(jnp.where(hit, table[None, :], 0), axis=1)


def _routing(ti, n_experts, tm, nj, subs):
    t = ti.shape[0]
    n_assign = t * TOP_K
    nb_max = n_assign // tm + n_experts
    onehot = (ti[:, :, None] == jnp.arange(n_experts, dtype=jnp.int32)[None, None, :]).astype(jnp.int32)
    per_tok = jnp.sum(onehot, axis=1)
    cum = jnp.cumsum(per_tok, axis=0)
    counts = cum[-1]
    rank = jnp.sum(onehot * cum[:, None, :], axis=2) - 1
    nblk = (counts + tm - 1) // tm
    blk_end = jnp.cumsum(nblk)
    blk_start = blk_end - nblk
    dest = jnp.sum(onehot * blk_start[None, None, :], axis=2) * tm + rank
    pad_lo = (blk_start * tm + counts).astype(jnp.int32)
    pad_len = (nblk * tm - counts).astype(jnp.int32)
    total_blk = blk_end[-1]
    blocks = jnp.arange(nb_max, dtype=jnp.int32)
    blk_e = jnp.minimum(jnp.sum((blk_end[None, :] <= blocks[:, None]).astype(jnp.int32), axis=1), n_experts - 1)
    start_b, nblk_b = _take(blk_start, blk_e), _take(nblk, blk_e)
    r_in_e = blocks - start_b
    g0 = start_b + (r_in_e // subs) * subs
    nsub = jnp.minimum(subs, nblk_b - (r_in_e // subs) * subs)
    p = jnp.arange(nb_max * nj, dtype=jnp.int32)
    bp = p // nj
    valid = bp < total_blk
    last = jnp.maximum(total_blk - 1, 0)
    bq = jnp.where(valid, bp, last)
    gq, nq, eq = _take(g0, bq), jnp.maximum(_take(nsub, bq), 1), _take(blk_e, bq)
    local = p - nj * gq
    jq = jnp.where(valid, local // nq, nj - 1)
    rq = jnp.where(valid, local % nq, nq - 1)
    item_xb = gq + rq
    item_ob = jnp.where(valid, jnp.where(jq == nj - 1, gq + rq, gq), bp)
    flags = valid.astype(jnp.int32) + 2 * (valid & (rq == 0)).astype(jnp.int32)
    nxt_blk = gq + nq
    has_next = nxt_blk < total_blk
    last_j = jq == nj - 1
    e_n = jnp.where(last_j & has_next, _take(blk_e, jnp.minimum(nxt_blk, nb_max - 1)), eq)
    j_n = jnp.where(last_j, jnp.where(has_next, 0, jq), jq + 1)
    cur, nxt = eq * nj + jq, e_n * nj + j_n
    wg_code = jnp.where(valid & (rq >= 1), nxt, cur)
    wu_code = jnp.where(valid & (rq >= 2), nxt, cur)
    items = (eq.astype(jnp.int32), jq.astype(jnp.int32), item_xb.astype(jnp.int32),
             item_ob.astype(jnp.int32), rq.astype(jnp.int32), flags,
             wg_code.astype(jnp.int32), wu_code.astype(jnp.int32))
    return dest.astype(jnp.int32), pad_lo, pad_len, total_blk.astype(jnp.int32), nb_max, items


WPREP_TR = 256


def _win_prep_kernel(c_ab, w_hbm, o_ref, buf, sem):
    c = pl.program_id(0)
    tr = o_ref.shape[0]
    o_a = QKV_W + V_W
    c_shift = o_a // tr

    def copy(cc):
        row = jnp.where(cc < c_shift, cc * tr, jnp.where(cc < c_ab, cc * tr + 2 * DN_HEADS, o_a))
        src = w_hbm.at[pl.ds(pl.multiple_of(row, 2 * DN_HEADS), tr)]
        return pltpu.make_async_copy(src, buf.at[cc % 2], sem.at[cc % 2])

    @pl.when(c == 0)
    def _():
        copy(c).start()

    @pl.when(c + 1 <= c_ab)
    def _():
        copy(c + 1).start()

    @pl.when(c <= c_ab)
    def _():
        copy(c).wait()

    x = buf[c % 2]
    rows = lax.broadcasted_iota(jnp.int32, x.shape, 0)
    keep = jnp.logical_or(c < c_ab, jnp.logical_and(c == c_ab, rows < 2 * DN_HEADS))
    o_ref[...] = jnp.where(keep, x, 0.0).astype(BF16)


def _rearranged_w_in(w_in, d):
    dm, nw = w_in.shape
    tr = WPREP_TR
    o_a = QKV_W + V_W
    rest = nw - o_a - 2 * DN_HEADS
    assert o_a % tr == 0 and rest % tr == 0
    nw_out = -(-(nw - 2 * DN_HEADS + LANES) // PROJ_TN) * PROJ_TN
    assert nw_out % tr == 0
    w = pl.pallas_call(
        functools.partial(_win_prep_kernel, (o_a + rest) // tr),
        grid=(nw_out // tr,),
        in_specs=[pl.BlockSpec(memory_space=pl.ANY)],
        out_specs=pl.BlockSpec((tr, dm), lambda c: (c, 0)),
        out_shape=jax.ShapeDtypeStruct((nw_out, dm), BF16),
        scratch_shapes=[pltpu.VMEM((2, tr, dm), F32), pltpu.SemaphoreType.DMA((2,))],
        compiler_params=_params("arbitrary"),
        name="w_in_prep",
    )(w_in.T)
    cols = dict(z=QKV_W, u=QKV_W + V_W, v=QKV_W + V_W + SG_W, ga=QKV_W + V_W + 2 * SG_W,
                gb=QKV_W + V_W + 2 * SG_W + d, ab=QKV_W + V_W + 2 * SG_W + 2 * d)
    return w, cols


def kernel(x_prompt, x_sample, state_conv_qkv, state_delta, c_prompt, c_sample, w_ada, b_ada, w_in, w_conv, a_log, dt_bias, o_norm_g, sg_ln_g, sg_ln_b, w_s, b_s, p_a, p_b, w_out, ln1_g, ln1_b, router_w, router_b, w_gu, b_gu, w_dn, b_dn, ln2_g, ln2_b):
    depth = w_ada.shape[0]
    alpha = float((2 * depth) ** 0.25)
    bp, seq, d = x_prompt.shape
    bs, lt, _ = x_sample.shape
    assert lt == DN_CONV and seq % SG_CHUNK == 0
    n_experts = router_w.shape[2]
    rows_p, rows_s = bp * seq, bs * lt
    xp = x_prompt.reshape(rows_p, d)
    xs = x_sample.reshape(rows_s, d)
    c_all = jnp.concatenate([jnp.repeat(c_sample, lt, axis=0), c_prompt,
                             jnp.zeros((-(rows_s + bp) % SUBLANES, d), F32)], axis=0)
    tile = lambda cap: _Rows(math.gcd(math.gcd(cap, rows_s), seq), rows_p, rows_s, seq)
    rt_proj, rt_sgu, rt_merge, rt_comb = tile(PROJ_TM), tile(SG_CHUNK), tile(MERGE_TM), tile(COMB_TQ)
    outs = dict(conv_p=[], delta_p=[], conv_s=[], delta_s=[], vrows=[])

    for l in range(depth):
        mod = _adaln(c_all, w_ada[l], b_ada[l])

        w_r, col = _rearranged_w_in(w_in[l], d)
        proj = _inproj(rt_proj, xp, xs, mod, mod, w_r)

        alog = jnp.pad(a_log[l:l + 1], ((0, 0), (0, LANES - DN_HEADS)))
        dtb = jnp.pad(dt_bias[l:l + 1], ((0, 0), (0, LANES - DN_HEADS)))
        og = o_norm_g[l:l + 1]
        conv0 = jnp.zeros((bp, DN_CONV - 1, QKV_W), F32)
        s0 = jnp.zeros((bp, DN_HEADS, DN_DK, DN_DV), F32)
        oa_p, conv_p, delta_p = _gdn_prompt(proj, conv0, s0, w_conv[l], alog, dtb, og, bp, seq,
                                            col['z'], col['ab'])
        oa_s, conv_s, delta_s = _gdn_sample(proj, rows_p, state_conv_qkv[l], state_delta[l], w_conv[l],
                                            alog, dtb, og, bs, col['z'], col['ab'])

        ts = rt_sgu.tm
        assert ts == SG_CHUNK
        ws_p = jnp.tril(w_s[l][:, :ts, :ts]).astype(BF16)
        bias_p = jnp.repeat(b_s[l].T[:ts], SG_CH, axis=1)
        eye = jnp.eye(ts // lt, dtype=F32)
        ws_s = jnp.stack([jnp.kron(eye, jnp.tril(w_s[l, g, :lt, :lt])) for g in range(SG_GROUPS)]).astype(BF16)
        bias_s = jnp.tile(jnp.repeat(b_s[l, :, :lt].T, SG_CH, axis=1), (ts // lt, 1))
        ob, vn_s = _sgu(rt_sgu, proj, sg_ln_g[l:l + 1], sg_ln_b[l:l + 1],
                        ws_p, ws_s, bias_p, bias_s, col['u'], col['v'])

        pa, pb, wo = p_a[l].astype(BF16), p_b[l].astype(BF16), w_out[l].astype(BF16)
        rws = _split(jnp.pad(router_w[l], ((0, 0), (0, LANES - n_experts))))
        rw = jnp.concatenate([rws[0], rws[0], rws[1]], axis=0)
        rb = jnp.pad(router_b[l:l + 1], ((0, 0), (0, LANES - n_experts)))
        x1, h2, ti, tg = _merge(alpha, n_experts, rt_merge, oa_p, oa_s, ob, proj, xp, xs,
                                mod, mod,
                                pa, pb, wo, ln1_g[l:l + 1], ln1_b[l:l + 1], rw, rb,
                                col['ga'], col['gb'])

        nj = w_dn.shape[2] // MOE_TF
        dest, pad_lo, pad_len, nblk_used, nb_max, items = _routing(ti[:, :TOP_K], n_experts, MOE_TM, nj, MOE_SUBS)
        xb = _dispatch(pad_lo, pad_len, nblk_used, dest, h2, MOE_TM, nb_max)
        yb = _moe(items, xb, w_gu[l], b_gu[l], w_dn[l], b_dn[l], MOE_TM, MOE_TF, MOE_SUBS)
        xp, xs = _combine(alpha, rt_comb, dest, yb, tg, x1, mod, mod,
                          ln2_g[l:l + 1], ln2_b[l:l + 1])

        outs['conv_p'].append(conv_p)
        outs['delta_p'].append(delta_p)
        outs['conv_s'].append(conv_s)
        outs['delta_s'].append(delta_s)
        outs['vrows'].append(vn_s.reshape(bs, lt, SG_W))

    return (xp.reshape(bp, seq, d), xs.reshape(bs, lt, d),
            jnp.stack(outs['conv_p']), jnp.stack(outs['delta_p']),
            jnp.stack(outs['conv_s']), jnp.stack(outs['delta_s']), jnp.stack(outs['vrows']))
```

```python
import functools
import math

import jax
import jax.numpy as jnp
from jax import lax
from jax.experimental import pallas as pl
from jax.experimental.pallas import tpu as pltpu

F32 = jnp.float32
BF16 = jnp.bfloat16

DN_HEADS = 8
DN_DK = 128
DN_DV = 128
DN_CONV = 4
DN_CHUNK = 64
SG_GROUPS = 8
SG_CH = 128
SG_CHUNK = 128
TOP_K = 4
SWIGLU_LIMIT = 7.0
SWIGLU_ALPHA = 1.702
LN_EPS = 1e-5
NORM_EPS = 1e-6
QK_W = DN_HEADS * DN_DK
V_W = DN_HEADS * DN_DV
QKV_W = 2 * QK_W + V_W
SG_W = SG_GROUPS * SG_CH

LANES = 128
SUBLANES = 8
VMEM_LIMIT = 56 * 1024 * 1024

PROJ_TM = 512
PROJ_TN = 1536
MERGE_TM = 256
MOE_TM = 256
MOE_TF = 512
MOE_SUBS = 6
COMB_TQ = 256
SAMPLE_BT = 4
SAMPLE_TP = 8
GDN_HG = 4

MOD_SHIFT1, MOD_SCALE1, MOD_GATE1, MOD_SHIFT2, MOD_SCALE2, MOD_GATE2 = range(6)
MOD_PARTS = 6


def _sigmoid(x):
    return 0.5 * (jnp.tanh(0.5 * x) + 1.0)


def _silu(x):
    return x * _sigmoid(x)


def _dot(a, b):
    return jnp.dot(a.astype(BF16), b.astype(BF16), preferred_element_type=F32)


def _dot_nt(a, b):
    return lax.dot_general(a.astype(BF16), b.astype(BF16), (((1,), (1,)), ((), ())),
                           preferred_element_type=F32)


def _split3(x):
    hi = x.astype(BF16)
    r1 = x - hi.astype(F32)
    mid = r1.astype(BF16)
    return hi, mid, (r1 - mid.astype(F32)).astype(BF16)


def _mask_sums(masks, col, n):
    lane = lax.broadcasted_iota(jnp.int32, (n, LANES), 1)
    hi, mid, lo = [p.astype(F32) for p in _split3(col)]
    parts = jnp.where(lane == 0, hi, jnp.where(lane == 1, mid, jnp.where(lane == 2, lo, 0.0)))
    y = jnp.dot(masks, parts.astype(BF16), preferred_element_type=F32)
    y = y[:, 0:1] + y[:, 1:2] + y[:, 2:3]
    return [y[i * n:(i + 1) * n] for i in range(masks.shape[0] // n)]


def _layer_norm(x, g, b):
    mu = jnp.mean(x, axis=-1, keepdims=True)
    xc = x - mu
    var = jnp.mean(xc * xc, axis=-1, keepdims=True)
    return xc * lax.rsqrt(var + LN_EPS) * g + b


def _params(*sem):
    return pltpu.CompilerParams(dimension_semantics=sem, vmem_limit_bytes=VMEM_LIMIT)


class _Rows:
    def __init__(self, tm, rows_p, rows_s, seq):
        assert rows_p % tm == 0 and rows_s % tm == 0 and seq % tm == 0, (tm, rows_p, rows_s, seq)
        self.tm, self.n_p, self.n_s = tm, rows_p // tm, rows_s // tm
        self.tiles_per_seq = seq // tm
        self.bp = rows_p // seq
        assert rows_s % SUBLANES == 0
        self.rows_s = rows_s

    @property
    def n(self):
        return self.n_p + self.n_s

    def prompt(self, width, col=0):
        return pl.BlockSpec((self.tm, width), lambda i, *_: (jnp.minimum(i, self.n_p - 1), col))

    def sample(self, width, col=0, single=True):
        mode = dict(pipeline_mode=pl.Buffered(1)) if single else {}
        return pl.BlockSpec((self.tm, width), lambda i, *_: (jnp.maximum(i - self.n_p, 0), col), **mode)

    def joint(self, width, col=0):
        return pl.BlockSpec((self.tm, width), lambda i, *_: (i, col))

    def _seq(self, i):
        return jnp.minimum(i // self.tiles_per_seq, self.bp - 1)

    def seq_vec(self, d, part):
        return pl.BlockSpec((SUBLANES, d), lambda i, *_: ((self.rows_s + self._seq(i)) // SUBLANES, part))

    def row_vec(self, d, part):
        return pl.BlockSpec((self.tm, d), lambda i, *_: (jnp.maximum(i - self.n_p, 0), part),
                            pipeline_mode=pl.Buffered(1))

    def pick_vec(self, i, seq_ref, row_ref):
        prompt = seq_ref[pl.ds(self._seq(i) % SUBLANES, 1), :]
        return jnp.where(i >= self.n_p, row_ref[...], prompt)


def _pick(is_sample, prompt_ref, sample_ref):
    return jnp.where(is_sample, sample_ref[...], prompt_ref[...])


def _adaln_kernel(c_ref, w_ref, b_ref, o_ref):
    o_ref[...] = _dot(_silu(c_ref[...]), w_ref[...]) + b_ref[...]


def _adaln(c, w, b):
    rows, d = c.shape
    n = w.shape[1]
    tn = 1024
    return pl.pallas_call(
        _adaln_kernel,
        grid=(n // tn,),
        in_specs=[pl.BlockSpec((rows, d), lambda j: (0, 0)),
                  pl.BlockSpec((d, tn), lambda j: (0, j)),
                  pl.BlockSpec((1, tn), lambda j: (0, j))],
        out_specs=pl.BlockSpec((rows, tn), lambda j: (0, j)),
        out_shape=jax.ShapeDtypeStruct((rows, n), F32),
        compiler_params=_params("arbitrary"),
        name="adaln",
    )(c, w, b.reshape(1, n))


def _inproj_kernel(rt, xp_ref, xs_ref, scp_ref, scs_ref, shp_ref, shs_ref, w_ref, o_ref, h_scr):
    @pl.when(pl.program_id(1) == 0)
    def _():
        i = pl.program_id(0)
        x = _pick(i >= rt.n_p, xp_ref, xs_ref)
        h_scr[...] = (x * (1.0 + rt.pick_vec(i, scp_ref, scs_ref)) + rt.pick_vec(i, shp_ref, shs_ref)).astype(BF16)

    o_ref[...] = lax.dot_general(h_scr[...], w_ref[...], (((1,), (1,)), ((), ())), preferred_element_type=F32)


def _inproj(rt, xp, xs, mod_seq, mod_row, w):
    d = xp.shape[1]
    nw = w.shape[0]
    tm = rt.tm
    return pl.pallas_call(
        functools.partial(_inproj_kernel, rt),
        grid=(rt.n, nw // PROJ_TN),
        in_specs=[rt.prompt(d), rt.sample(d), rt.seq_vec(d, MOD_SCALE1), rt.row_vec(d, MOD_SCALE1),
                  rt.seq_vec(d, MOD_SHIFT1), rt.row_vec(d, MOD_SHIFT1),
                  pl.BlockSpec((PROJ_TN, d), lambda i, j: (j, 0))],
        out_specs=pl.BlockSpec((tm, PROJ_TN), lambda i, j: (i, j)),
        out_shape=jax.ShapeDtypeStruct((rt.n * tm, nw), F32),
        scratch_shapes=[pltpu.VMEM((tm, d), BF16)],
        compiler_params=_params("arbitrary", "arbitrary"),
        name="inproj",
    )(xp, xs, mod_seq, mod_row, mod_seq, mod_row, w)


def _softplus(x):
    return jnp.maximum(x, 0.0) + jnp.log1p(jnp.exp(-jnp.abs(x)))


def _split(a):
    hi = a.astype(BF16)
    lo = (a - hi.astype(F32)).astype(BF16)
    return hi, lo


def _map(f, *lists):
    return [f(*args) for args in zip(*lists)]


def _unit_lower_inverses_minus_eye(mats, rows, cols):
    same = (rows // 16) == (cols // 16)
    n = [jnp.where(same, a, 0.0) for a in mats]
    b = _map(lambda a, x: a - x, mats, n)
    n2 = _map(lambda x: _dot(x, x), n)
    n4 = _map(lambda x: _dot(x, x), n2)
    n8 = _map(lambda x: _dot(x, x), n4)
    r = [-x for x in n]
    r = _map(lambda x, p: x + p + _dot(x, p), r, n2)
    r = _map(lambda x, p: x + p + _dot(x, p), r, n4)
    dm = _map(lambda x, p: x + p + _dot(x, p), r, n8)
    m = _map(lambda x, y: y + _dot(x, y), dm, b)
    m2 = _map(lambda x: _dot(x, x), m)
    xm = _map(lambda x, p: x + p + _dot(p, x), dm, m2)
    return _map(lambda x, p: x - p - _dot(p, x), xm, m)


def _gdn_prompt_kernel(qkv_ref, z_ref, ab_ref, hist_ref, s0_ref, wconv_ref, alog_ref, dt_ref, og_ref,
                       o_ref, conv_ref, s_ref, xp_scr):
    n = pl.program_id(1)
    c = DN_CHUNK
    pad = SUBLANES
    nr = GDN_HG * c
    ngrp = DN_HEADS // GDN_HG

    @pl.when(n == 0)
    def _():
        xp_scr[pl.ds(0, pad), :] = jnp.zeros((pad, QKV_W), F32)
        xp_scr[pl.ds(pad - (DN_CONV - 1), DN_CONV - 1), :] = hist_ref[...]
        s_ref[...] = s0_ref[...]

    x = qkv_ref[...]
    xp_scr[pl.ds(pad, c), :] = x
    y = jnp.zeros((c, QKV_W), F32)
    for j in range(DN_CONV):
        y = y + xp_scr[pl.ds(pad - (DN_CONV - 1) + j, c), :] * wconv_ref[pl.ds(j, 1), :]
    y = _silu(y)
    tail = xp_scr[pl.ds(c + pad - (DN_CONV - 1), DN_CONV - 1), :]
    conv_ref[...] = tail
    xp_scr[pl.ds(pad - (DN_CONV - 1), DN_CONV - 1), :] = tail

    ab = ab_ref[...]
    g = -jnp.exp(alog_ref[...]) * _softplus(ab + dt_ref[...])
    beta_all = _sigmoid(ab)
    r64 = lax.broadcasted_iota(jnp.int32, (c, c), 0)
    c64 = lax.broadcasted_iota(jnp.int32, (c, c), 1)
    tri = (r64 >= c64).astype(BF16)
    gc = sum(jnp.dot(tri, part, preferred_element_type=F32) for part in _split3(g))
    z = z_ref[...]
    og = og_ref[...]

    rows = lax.broadcasted_iota(jnp.int32, (nr, nr), 0)
    cols = lax.broadcasted_iota(jnp.int32, (nr, nr), 1)
    same = (rows // c) == (cols // c)
    causal = same & (rows >= cols)
    strict = same & (rows > cols)
    rowhead = lax.broadcasted_iota(jnp.int32, (nr, 1), 0) // c
    groups = [range(grp * GDN_HG, (grp + 1) * GDN_HG) for grp in range(ngrp)]

    def stack(heads, off, width):
        return jnp.concatenate([y[:, off + h * width:off + (h + 1) * width] for h in heads], axis=0)

    def l2n(t):
        return t * lax.rsqrt(jnp.sum(t * t, axis=-1, keepdims=True) + NORM_EPS)

    q = [l2n(stack(hs, 0, DN_DK)) * (DN_DK ** -0.5) for hs in groups]
    k = [l2n(stack(hs, QK_W, DN_DK)) for hs in groups]
    v = [stack(hs, 2 * QK_W, DN_DV) for hs in groups]
    beta = [jnp.concatenate([beta_all[:, DN_HEADS + h:DN_HEADS + h + 1] for h in hs], axis=0) for hs in groups]
    gcf = [jnp.concatenate([jnp.broadcast_to(gc[:, h:h + 1], (c, LANES)) for h in hs], axis=0) for hs in groups]
    gcc = [t[:, 0:1] for t in gcf]
    grow = [t.T[0:1, :] for t in gcf]
    glast = [[gc[c - 1:c, h:h + 1] for h in hs] for hs in groups]
    gtot = [jnp.concatenate([jnp.broadcast_to(t, (c, 1)) for t in gl], axis=0) for gl in glast]
    decay = _map(lambda a, b: jnp.where(causal, jnp.exp(jnp.where(causal, a - b, 0.0)), 0.0), gcc, grow)
    egc = _map(jnp.exp, gcc)
    kb = _map(lambda a, b: a * b, k, beta)
    a_mat = _map(lambda a, b, dd: jnp.where(strict, _dot_nt(a, b) * dd, 0.0), kb, k, decay)
    tm1 = _unit_lower_inverses_minus_eye(a_mat, rows, cols)
    u = _map(lambda t, a, b: a * b + _dot(t, a * b), tm1, v, beta)
    w = _map(lambda t, a, b: a * b + _dot(t, a * b), tm1, kb, egc)
    attn = _map(lambda a, b, dd: _dot_nt(a, b) * dd, q, k, decay)
    qg = _map(lambda a, b: a * b, q, egc)
    kd_t = _map(lambda a, b, cc: (a * jnp.exp(b - cc)).T, k, gtot, gcc)

    for gi, hs in enumerate(groups):
        v_news, qss = [], []
        for hl, h in enumerate(hs):
            sl = slice(hl * c, (hl + 1) * c)
            rs = _dot(jnp.concatenate([w[gi][sl], qg[gi][sl]], axis=0), s_ref[h])
            v_news.append(u[gi][sl] - rs[:c])
            qss.append(rs[c:])
        v_new = jnp.concatenate(v_news, axis=0)
        o = jnp.concatenate(qss, axis=0) + _dot(attn[gi], v_new)
        for hl, h in enumerate(hs):
            upd = _dot(kd_t[gi], jnp.where(rowhead == hl, v_new, 0.0))
            s_ref[h] = s_ref[h] * jnp.exp(glast[gi][hl]) + upd
        o = o * lax.rsqrt(jnp.mean(o * o, axis=-1, keepdims=True) + NORM_EPS) * og
        for hl, h in enumerate(hs):
            o_ref[:, h * DN_DV:(h + 1) * DN_DV] = o[hl * c:(hl + 1) * c] * _silu(z[:, h * DN_DV:(h + 1) * DN_DV])


def _gdn_prompt(proj, hist, s0, wconv, alog, dtb, og, batch, seq, col_z, col_ab):
    c = DN_CHUNK
    nchunk = seq // c
    rows = batch * seq
    row = lambda b, n: b * nchunk + n
    return pl.pallas_call(
        _gdn_prompt_kernel,
        grid=(batch, nchunk),
        in_specs=[pl.BlockSpec((c, QKV_W), lambda b, n: (row(b, n), 0)),
                  pl.BlockSpec((c, V_W), lambda b, n: (row(b, n), col_z // V_W)),
                  pl.BlockSpec((c, LANES), lambda b, n: (row(b, n), col_ab // LANES)),
                  pl.BlockSpec((None, DN_CONV - 1, QKV_W), lambda b, n: (b, 0, 0)),
                  pl.BlockSpec((None, DN_HEADS, DN_DK, DN_DV), lambda b, n: (b, 0, 0, 0)),
                  pl.BlockSpec((DN_CONV, QKV_W), lambda b, n: (0, 0)),
                  pl.BlockSpec((1, LANES), lambda b, n: (0, 0)),
                  pl.BlockSpec((1, LANES), lambda b, n: (0, 0)),
                  pl.BlockSpec((1, DN_DV), lambda b, n: (0, 0))],
        out_specs=[pl.BlockSpec((c, V_W), lambda b, n: (row(b, n), 0)),
                   pl.BlockSpec((None, DN_CONV - 1, QKV_W), lambda b, n: (b, 0, 0)),
                   pl.BlockSpec((None, DN_HEADS, DN_DK, DN_DV), lambda b, n: (b, 0, 0, 0))],
        out_shape=[jax.ShapeDtypeStruct((rows, V_W), F32),
                   jax.ShapeDtypeStruct((batch, DN_CONV - 1, QKV_W), F32),
                   jax.ShapeDtypeStruct((batch, DN_HEADS, DN_DK, DN_DV), F32)],
        scratch_shapes=[pltpu.VMEM((c + SUBLANES, QKV_W), F32)],
        compiler_params=_params("arbitrary", "arbitrary"),
        name="gdn_prompt",
    )(proj, proj, proj, hist, s0, wconv, alog, dtb, og)


def _gdn_sample_kernel(qkv_ref, z_ref, ab_ref, hist_ref, s0_ref, wconv_ref, alog_ref, dt_ref, og_ref,
                       o_ref, conv_ref, s_ref, xp_scr, ab_scr, wq_scr, r_scr, kdt_scr, vn_scr, gl_scr):
    bt, tp, lt = SAMPLE_BT, SAMPLE_TP, DN_CONV
    nblk = DN_HEADS * bt
    nrow = nblk * tp
    hist_rows = DN_CONV - 1

    xp_scr[...] = jnp.zeros(xp_scr.shape, F32)
    ab_scr[...] = jnp.zeros(ab_scr.shape, F32)
    ys = []
    for b in range(bt):
        xp_scr[b, pl.ds(SUBLANES - hist_rows, hist_rows), :] = hist_ref[b]
        xp_scr[b, pl.ds(SUBLANES, lt), :] = qkv_ref[pl.ds(b * lt, lt), :]
        yb = jnp.zeros((tp, QKV_W), F32)
        for j in range(DN_CONV):
            yb = yb + xp_scr[b, pl.ds(SUBLANES - hist_rows + j, tp), :] * wconv_ref[pl.ds(j, 1), :]
        ys.append(_silu(yb))
        conv_ref[b] = xp_scr[b, pl.ds(SUBLANES + lt - hist_rows, hist_rows), :]
        ab_scr[b, pl.ds(0, lt), :] = ab_ref[pl.ds(b * lt, lt), :]
    y = jnp.concatenate(ys, axis=0)
    ab = jnp.concatenate([ab_scr[b] for b in range(bt)], axis=0)
    tok = lax.broadcasted_iota(jnp.int32, (bt * tp, 1), 0) % tp
    real = tok < lt
    g_all = jnp.where(real, -jnp.exp(alog_ref[...]) * _softplus(ab + dt_ref[...]), 0.0)
    beta_all = jnp.where(real, _sigmoid(ab), 0.0)

    def heads_to_rows(t, off, width):
        return jnp.concatenate([t[:, off + h * width:off + (h + 1) * width] for h in range(DN_HEADS)], axis=0)

    realr = jnp.concatenate([real] * DN_HEADS, axis=0)
    q = jnp.where(realr, heads_to_rows(y, 0, DN_DK), 0.0)
    k = jnp.where(realr, heads_to_rows(y, QK_W, DN_DK), 0.0)
    v = jnp.where(realr, heads_to_rows(y, 2 * QK_W, DN_DV), 0.0)
    gcol = jnp.concatenate([g_all[:, h:h + 1] for h in range(DN_HEADS)], axis=0)
    beta = jnp.concatenate([beta_all[:, DN_HEADS + h:DN_HEADS + h + 1] for h in range(DN_HEADS)], axis=0)
    q = q * lax.rsqrt(jnp.sum(q * q, axis=-1, keepdims=True) + NORM_EPS) * (DN_DK ** -0.5)
    k = k * lax.rsqrt(jnp.sum(k * k, axis=-1, keepdims=True) + NORM_EPS)

    rows = lax.broadcasted_iota(jnp.int32, (nrow, nrow), 0)
    cols = lax.broadcasted_iota(jnp.int32, (nrow, nrow), 1)
    same = (rows // tp) == (cols // tp)
    causal = same & (rows >= cols)
    strict = same & (rows > cols)
    gcc, gtot = _mask_sums(jnp.concatenate([causal.astype(BF16), same.astype(BF16)], axis=0), gcol, nrow)
    gtot = jnp.broadcast_to(gtot, (nrow, LANES))
    grow = jnp.broadcast_to(gcc, (nrow, LANES)).T[0:1, :]
    decay = jnp.where(causal, jnp.exp(jnp.where(causal, gcc - grow, 0.0)), 0.0)
    egc = jnp.exp(gcc)
    kb = k * beta
    a_mat = jnp.where(strict, _dot_nt(kb, k) * decay, 0.0)
    a2 = _dot(a_mat, a_mat)
    tm1 = a2 - a_mat - _dot(a_mat, a2)
    u = v * beta + _dot(tm1, v * beta)
    w = kb * egc + _dot(tm1, kb * egc)
    attn = _dot_nt(q, k) * decay
    qg = q * egc
    for i in range(nblk):
        wq_scr[pl.ds(2 * tp * i, tp), :] = w[i * tp:(i + 1) * tp, :]
        wq_scr[pl.ds(2 * tp * i + tp, tp), :] = qg[i * tp:(i + 1) * tp, :]
    kdt_scr[...] = (k * jnp.exp(gtot[:, 0:1] - gcc)).T
    gl_scr[...] = jnp.exp(gtot)

    def read_state(i, carry):
        r0 = pl.multiple_of(i * 2 * tp, 2 * tp)
        r_scr[pl.ds(r0, 2 * tp), :] = _dot(wq_scr[pl.ds(r0, 2 * tp), :], s0_ref[i % bt, i // bt])
        return carry

    lax.fori_loop(0, nblk, read_state, 0, unroll=4)
    ws = jnp.concatenate([r_scr[pl.ds(2 * tp * i, tp), :] for i in range(nblk)], axis=0)
    qs = jnp.concatenate([r_scr[pl.ds(2 * tp * i + tp, tp), :] for i in range(nblk)], axis=0)
    v_new = u - ws
    vn_scr[...] = v_new
    o = qs + _dot(attn, v_new)
    blockid = lax.broadcasted_iota(jnp.int32, (nrow, 1), 0) // tp

    def write_state(i, carry):
        r0 = pl.multiple_of(i * tp, tp)
        upd = _dot(kdt_scr[...], jnp.where(blockid == i, vn_scr[...], 0.0))
        s_ref[i % bt, i // bt] = s0_ref[i % bt, i // bt] * gl_scr[pl.ds(r0, 1), :] + upd
        return carry

    lax.fori_loop(0, nblk, write_state, 0, unroll=4)

    og = og_ref[...]
    o = o * lax.rsqrt(jnp.mean(o * o, axis=-1, keepdims=True) + NORM_EPS) * og
    for b in range(bt):
        for h in range(DN_HEADS):
            zbh = z_ref[pl.ds(b * lt, lt), h * DN_DV:(h + 1) * DN_DV]
            blk = o[(h * bt + b) * tp:(h * bt + b) * tp + lt, :]
            o_ref[pl.ds(b * lt, lt), h * DN_DV:(h + 1) * DN_DV] = blk * _silu(zbh)


def _gdn_sample(proj, row0, hist, s0, wconv, alog, dtb, og, batch, col_z, col_ab):
    bt, tp, lt = SAMPLE_BT, SAMPLE_TP, DN_CONV
    rows = batch * lt
    nrow = DN_HEADS * bt * tp
    blk = bt * lt
    assert row0 % blk == 0 and batch % bt == 0
    b0 = row0 // blk
    return pl.pallas_call(
        _gdn_sample_kernel,
        grid=(batch // bt,),
        in_specs=[pl.BlockSpec((blk, QKV_W), lambda i: (b0 + i, 0)),
                  pl.BlockSpec((blk, V_W), lambda i: (b0 + i, col_z // V_W)),
                  pl.BlockSpec((blk, LANES), lambda i: (b0 + i, col_ab // LANES)),
                  pl.BlockSpec((bt, DN_CONV - 1, QKV_W), lambda i: (i, 0, 0)),
                  pl.BlockSpec((bt, DN_HEADS, DN_DK, DN_DV), lambda i: (i, 0, 0, 0)),
                  pl.BlockSpec((DN_CONV, QKV_W), lambda i: (0, 0)),
                  pl.BlockSpec((1, LANES), lambda i: (0, 0)),
                  pl.BlockSpec((1, LANES), lambda i: (0, 0)),
                  pl.BlockSpec((1, DN_DV), lambda i: (0, 0))],
        out_specs=[pl.BlockSpec((blk, V_W), lambda i: (i, 0)),
                   pl.BlockSpec((bt, DN_CONV - 1, QKV_W), lambda i: (i, 0, 0)),
                   pl.BlockSpec((bt, DN_HEADS, DN_DK, DN_DV), lambda i: (i, 0, 0, 0))],
        out_shape=[jax.ShapeDtypeStruct((rows, V_W), F32),
                   jax.ShapeDtypeStruct((batch, DN_CONV - 1, QKV_W), F32),
                   jax.ShapeDtypeStruct((batch, DN_HEADS, DN_DK, DN_DV), F32)],
        scratch_shapes=[pltpu.VMEM((bt, SUBLANES + tp, QKV_W), F32),
                        pltpu.VMEM((bt, tp, LANES), F32),
                        pltpu.VMEM((2 * nrow, DN_DK), F32),
                        pltpu.VMEM((2 * nrow, DN_DV), F32),
                        pltpu.VMEM((DN_DK, nrow), F32),
                        pltpu.VMEM((nrow, DN_DV), F32),
                        pltpu.VMEM((nrow, LANES), F32)],
        compiler_params=_params("arbitrary"),
        name="gdn_sample",
    )(proj, proj, proj, hist, s0, wconv, alog, dtb, og)


def _sgu_kernel(n_p, u_ref, v_ref, g_ref, b_ref, wsp_ref, wss_ref, bp_ref, bs_ref, o_ref, vn_ref):
    is_s = pl.program_id(0) >= n_p
    u = jax.nn.gelu(u_ref[...])
    vn = _layer_norm(jax.nn.gelu(v_ref[...]), g_ref[...], b_ref[...])
    vn_ref[...] = vn
    bias = _pick(is_s, bp_ref, bs_ref)
    for g in range(SG_GROUPS):
        sl = slice(g * SG_CH, (g + 1) * SG_CH)
        ws = jnp.where(is_s, wss_ref[g], wsp_ref[g])
        mixed = _dot(ws, vn[:, sl]) + bias[:, sl]
        o_ref[:, sl] = u[:, sl] * mixed


def _sgu(rt, proj, ln_g, ln_b, ws_p, ws_s, bias_p, bias_s, col_u, col_v):
    t = rt.tm
    const = lambda shape: pl.BlockSpec(shape, lambda i: (0,) * len(shape))
    return pl.pallas_call(
        functools.partial(_sgu_kernel, rt.n_p),
        grid=(rt.n,),
        in_specs=[rt.joint(SG_W, col_u // SG_W), rt.joint(SG_W, col_v // SG_W),
                  const((1, SG_W)), const((1, SG_W)),
                  const((SG_GROUPS, t, t)), const((SG_GROUPS, t, t)), const((t, SG_W)), const((t, SG_W))],
        out_specs=[rt.joint(SG_W), rt.sample(SG_W, single=False)],
        out_shape=[jax.ShapeDtypeStruct((rt.n * t, SG_W), F32),
                   jax.ShapeDtypeStruct((rt.n_s * t, SG_W), F32)],
        compiler_params=_params("arbitrary"),
        name="sgu",
    )(proj, proj, ln_g, ln_b, ws_p, ws_s, bias_p, bias_s)


def _merge_kernel(alpha, n_experts, rt, oap_ref, oas_ref, ob_ref, ga_ref, gb_ref, xp_ref, xs_ref,
                  gtp_ref, gts_ref, scp_ref, scs_ref, shp_ref, shs_ref,
                  pa_ref, pb_ref, wo_ref, lg_ref, lb_ref, rw_ref, rb_ref,
                  x1_ref, h2_ref, ti_ref, tg_ref):
    i = pl.program_id(0)
    is_s = i >= rt.n_p
    oa = _pick(is_s, oap_ref, oas_ref)
    merged = (_sigmoid(ga_ref[...]) * _dot(oa, pa_ref[...])
              + _sigmoid(gb_ref[...]) * _dot(ob_ref[...], pb_ref[...]))
    y = _dot(merged, wo_ref[...])
    x = _pick(is_s, xp_ref, xs_ref)
    x1 = _layer_norm(alpha * x + rt.pick_vec(i, gtp_ref, gts_ref) * y, lg_ref[...], lb_ref[...])
    x1_ref[...] = x1
    h2 = x1 * (1.0 + rt.pick_vec(i, scp_ref, scs_ref)) + rt.pick_vec(i, shp_ref, shs_ref)
    h2_ref[...] = h2
    hs = _split(h2)
    logits = jnp.dot(jnp.concatenate([hs[0], hs[1], hs[0]], axis=1), rw_ref[...],
                     preferred_element_type=F32) + rb_ref[...]
    lane = lax.broadcasted_iota(jnp.int32, logits.shape, 1)
    logits = jnp.where(lane < n_experts, logits, -jnp.inf)
    ti = jnp.zeros(logits.shape, jnp.int32)
    tv = jnp.zeros(logits.shape, F32)
    top = None
    for kk in range(TOP_K):
        m = jnp.max(logits, axis=-1, keepdims=True)
        idx = jnp.min(jnp.where(logits == m, lane.astype(F32), float(LANES)), axis=-1,
                      keepdims=True).astype(jnp.int32)
        if kk == 0:
            top = m
        ti = jnp.where(lane == kk, idx, ti)
        tv = jnp.where(lane == kk, jnp.exp(m - top), tv)
        logits = jnp.where(lane == idx, -jnp.inf, logits)
    ti_ref[...] = ti
    tg_ref[...] = tv / jnp.sum(tv, axis=-1, keepdims=True)


def _merge(alpha, n_experts, rt, oa_p, oa_s, ob, proj, xp, xs, mod_seq, mod_row, pa, pb, wo, lg, lb, rw, rb,
           col_ga, col_gb):
    d = xp.shape[1]
    tm = rt.tm
    rows = rt.n * tm
    const = lambda shape: pl.BlockSpec(shape, lambda i: (0,) * len(shape), pipeline_mode=pl.Buffered(1))
    return pl.pallas_call(
        functools.partial(_merge_kernel, alpha, n_experts, rt),
        grid=(rt.n,),
        in_specs=[rt.prompt(V_W), rt.sample(V_W), rt.joint(SG_W),
                  rt.joint(d, col_ga // d), rt.joint(d, col_gb // d),
                  rt.prompt(d), rt.sample(d),
                  rt.seq_vec(d, MOD_GATE1), rt.row_vec(d, MOD_GATE1),
                  rt.seq_vec(d, MOD_SCALE2), rt.row_vec(d, MOD_SCALE2),
                  rt.seq_vec(d, MOD_SHIFT2), rt.row_vec(d, MOD_SHIFT2),
                  const((V_W, d)), const((SG_W, d)), const((d, d)),
                  const((1, d)), const((1, d)), const((3 * d, LANES)), const((1, LANES))],
        out_specs=[rt.joint(d), rt.joint(d), rt.joint(LANES), rt.joint(LANES)],
        out_shape=[jax.ShapeDtypeStruct((rows, d), F32),
                   jax.ShapeDtypeStruct((rows, d), F32),
                   jax.ShapeDtypeStruct((rows, LANES), jnp.int32),
                   jax.ShapeDtypeStruct((rows, LANES), F32)],
        compiler_params=_params("arbitrary"),
        name="merge",
    )(oa_p, oa_s, ob, proj, proj, xp, xs, mod_seq, mod_row, mod_seq, mod_row, mod_seq, mod_row,
      pa, pb, wo, lg, lb, rw, rb)


FILL_ROWS = (128, 64, 32, 16, 8)


def _dispatch_kernel(tm, n_experts, padlo_ref, padlen_ref, nblk_ref, dest_ref, h_ref, xb_hbm, zero_scr, sem, zsem):
    i = pl.program_id(0)
    tq = h_ref.shape[0]
    nb_max = xb_hbm.shape[0] // tm
    zrows = zero_scr.shape[0]

    def fill(act):
        def per_expert(e, carry):
            lo = padlo_ref[e]
            ln = padlen_ref[e]
            head = (SUBLANES - lo % SUBLANES) % SUBLANES
            for r in range(SUBLANES - 1):
                @pl.when(r < head)
                def _(r=r):
                    act(pltpu.make_async_copy(zero_scr.at[pl.ds(0, 1)], xb_hbm.at[pl.ds(lo + r, 1)], zsem))
            off = lo + head
            rem = ln - head
            for b in FILL_ROWS:
                @pl.when((rem & b) != 0)
                def _(off=off, b=b):
                    dst = xb_hbm.at[pl.ds(pl.multiple_of(off, SUBLANES), b)]
                    act(pltpu.make_async_copy(zero_scr.at[pl.ds(0, b)], dst, zsem))
                off = off + (rem & b)
            return carry

        lax.fori_loop(0, n_experts, per_expert, 0)

        def per_block(blk, carry):
            for part in range(tm // zrows):
                row0 = pl.multiple_of(blk * tm + part * zrows, zrows)
                act(pltpu.make_async_copy(zero_scr, xb_hbm.at[pl.ds(row0, zrows)], zsem))
            return carry

        lax.fori_loop(nblk_ref[0], nb_max, per_block, 0)

    @pl.when(i == 0)
    def _():
        zero_scr[...] = jnp.zeros(zero_scr.shape, F32)
        fill(lambda cp: cp.start())
        fill(lambda cp: cp.wait())

    def start(t, carry):
        for kk in range(TOP_K):
            pltpu.make_async_copy(h_ref.at[pl.ds(t, 1)], xb_hbm.at[pl.ds(dest_ref[0, t * TOP_K + kk], 1)],
                                  sem).start()
        return carry

    lax.fori_loop(0, tq, start, 0, unroll=8)
    for kk in range(TOP_K):
        pltpu.make_async_copy(h_ref, xb_hbm.at[pl.ds(0, tq)], sem).wait()


def _dispatch(padlo, padlen, nblk_used, dest, h2, tm, nb_max):
    t, d = h2.shape
    tq = math.gcd(256, t)
    n_experts = padlo.shape[0]
    grid_spec = pltpu.PrefetchScalarGridSpec(
        num_scalar_prefetch=3,
        grid=(t // tq,),
        in_specs=[pl.BlockSpec((None, 1, tq * TOP_K), lambda i, *_: (i, 0, 0), memory_space=pltpu.SMEM),
                  pl.BlockSpec((tq, d), lambda i, *_: (i, 0))],
        out_specs=pl.BlockSpec(memory_space=pl.ANY),
        scratch_shapes=[pltpu.VMEM((FILL_ROWS[0], d), F32), pltpu.SemaphoreType.DMA(()),
                        pltpu.SemaphoreType.DMA(())],
    )
    return pl.pallas_call(
        functools.partial(_dispatch_kernel, tm, n_experts),
        grid_spec=grid_spec,
        out_shape=jax.ShapeDtypeStruct((nb_max * tm, d), F32),
        compiler_params=pltpu.CompilerParams(dimension_semantics=("arbitrary",), vmem_limit_bytes=VMEM_LIMIT,
                                             has_side_effects=True),
        name="dispatch",
    )(padlo, padlen, nblk_used.reshape(1), dest.reshape(t // tq, 1, tq * TOP_K), h2)


def _moe_kernel(nj, e_ref, j_ref, xb_ref, ob_ref, r_ref, flag_ref, wg_code_ref, wu_code_ref,
                x_ref, wg_ref, wu_ref, wd_ref, bg_ref, bu_ref, bd_ref, o_ref,
                wg_scr, wu_scr, wd_scr, acc_scr):
    s = pl.program_id(0)
    flags = flag_ref[s]
    j = j_ref[s]
    r = r_ref[s]

    @pl.when((flags & 2) != 0)
    def _():
        wg_scr[...] = wg_ref[...].astype(BF16)
        wu_scr[...] = wu_ref[...].astype(BF16)
        wd_scr[...] = wd_ref[...].astype(BF16)

    @pl.when(flags == 0)
    def _():
        o_ref[...] = jnp.zeros(o_ref.shape, F32)

    @pl.when((flags & 1) != 0)
    def _():
        x = x_ref[...].astype(BF16)
        gate = jnp.dot(x, wg_scr[...], preferred_element_type=F32) + bg_ref[...]
        up = jnp.dot(x, wu_scr[...], preferred_element_type=F32) + bu_ref[...]
        gate = jnp.minimum(gate, SWIGLU_LIMIT)
        up = jnp.clip(up, -SWIGLU_LIMIT, SWIGLU_LIMIT)
        act = (up + 1.0) * gate * _sigmoid(SWIGLU_ALPHA * gate)
        y = jnp.dot(act.astype(BF16), wd_scr[...], preferred_element_type=F32)

        @pl.when(j == 0)
        def _():
            acc_scr[r] = y

        @pl.when(jnp.logical_and(j > 0, j < nj - 1))
        def _():
            acc_scr[r] = acc_scr[r] + y

        @pl.when(j == nj - 1)
        def _():
            o_ref[...] = acc_scr[r] + y + bd_ref[...]


def _moe(items, xb, w_gu, b_gu, w_dn, b_dn, tm, tf, subs):
    item_e, item_j, item_xb, item_ob, item_r, item_flag, wg_code, wu_code = items
    n_items = item_e.shape[0]
    nslot, d = xb.shape
    n_exp, _, f2 = w_gu.shape
    f = f2 // 2
    nj = f // tf
    assert nj >= 2
    grid_spec = pltpu.PrefetchScalarGridSpec(
        num_scalar_prefetch=8,
        grid=(n_items,),
        in_specs=[pl.BlockSpec((tm, d), lambda s, e, j, xbk, obk, r, fl, cg, cu: (xbk[s], 0)),
                  pl.BlockSpec((None, d, tf), lambda s, e, j, xbk, obk, r, fl, cg, cu: (cg[s] // nj, 0, cg[s] % nj)),
                  pl.BlockSpec((None, d, tf),
                               lambda s, e, j, xbk, obk, r, fl, cg, cu: (cu[s] // nj, 0, nj + cu[s] % nj)),
                  pl.BlockSpec((None, tf, d), lambda s, e, j, xbk, obk, r, fl, cg, cu: (e[s], j[s], 0)),
                  pl.BlockSpec((None, 1, tf), lambda s, e, j, xbk, obk, r, fl, cg, cu: (e[s], 0, j[s])),
                  pl.BlockSpec((None, 1, tf), lambda s, e, j, xbk, obk, r, fl, cg, cu: (e[s], 0, nj + j[s])),
                  pl.BlockSpec((None, 1, d), lambda s, e, j, xbk, obk, r, fl, cg, cu: (e[s], 0, 0))],
        out_specs=pl.BlockSpec((tm, d), lambda s, e, j, xbk, obk, r, fl, cg, cu: (obk[s], 0)),
        scratch_shapes=[pltpu.VMEM((d, tf), BF16), pltpu.VMEM((d, tf), BF16), pltpu.VMEM((tf, d), BF16),
                        pltpu.VMEM((subs, tm, d), F32)],
    )
    return pl.pallas_call(
        functools.partial(_moe_kernel, nj),
        grid_spec=grid_spec,
        out_shape=jax.ShapeDtypeStruct((nslot, d), F32),
        compiler_params=_params("arbitrary"),
        name="moe",
    )(item_e, item_j, item_xb, item_ob, item_r, item_flag, wg_code, wu_code,
      xb, w_gu, w_gu, w_dn, b_gu.reshape(n_exp, 1, f2), b_gu.reshape(n_exp, 1, f2), b_dn.reshape(n_exp, 1, d))


def _combine_kernel(alpha, rt, dest_ref, destn_ref, yb_hbm, tg_ref, x1_ref, gtp_ref, gts_ref, lg_ref, lb_ref,
                    op_ref, os_ref, buf, sem):
    i = pl.program_id(0)
    last = pl.num_programs(0) - 1
    tq = x1_ref.shape[0]
    slot = i % 2

    def start(ids_ref, sl, t):
        for kk in range(TOP_K):
            pltpu.make_async_copy(yb_hbm.at[pl.ds(ids_ref[0, t * TOP_K + kk], 1)],
                                  buf.at[sl, kk, pl.ds(t, 1)], sem.at[sl]).start()

    def wait(sl):
        for kk in range(TOP_K):
            pltpu.make_async_copy(yb_hbm.at[pl.ds(0, tq)], buf.at[sl, kk], sem.at[sl]).wait()

    @pl.when(i == 0)
    def _():
        lax.fori_loop(0, tq, lambda t, c: (start(dest_ref, 0, t), c)[1], 0, unroll=4)

    for t in range(tq):
        start(destn_ref, 1 - slot, t)
    wait(slot)
    tg = tg_ref[...]
    y = jnp.zeros(x1_ref.shape, F32)
    for kk in range(TOP_K):
        y = y + buf[slot, kk] * tg[:, kk:kk + 1]
    is_s = i >= rt.n_p
    out = _layer_norm(alpha * x1_ref[...] + rt.pick_vec(i, gtp_ref, gts_ref) * y, lg_ref[...], lb_ref[...])

    @pl.when(jnp.logical_not(is_s))
    def _():
        op_ref[...] = out

    @pl.when(is_s)
    def _():
        os_ref[...] = out

    @pl.when(i == last)
    def _():
        wait(1 - slot)


def _combine(alpha, rt, dest, yb, tg, x1, mod_seq, mod_row, lg, lb):
    d = x1.shape[1]
    tq = rt.tm
    ids = dest.reshape(rt.n, 1, tq * TOP_K)
    return pl.pallas_call(
        functools.partial(_combine_kernel, alpha, rt),
        grid=(rt.n,),
        in_specs=[pl.BlockSpec((None, 1, tq * TOP_K), lambda i: (i, 0, 0), memory_space=pltpu.SMEM),
                  pl.BlockSpec((None, 1, tq * TOP_K), lambda i: (jnp.minimum(i + 1, rt.n - 1), 0, 0),
                               memory_space=pltpu.SMEM),
                  pl.BlockSpec(memory_space=pl.ANY),
                  rt.joint(LANES), rt.joint(d), rt.seq_vec(d, MOD_GATE2), rt.row_vec(d, MOD_GATE2),
                  pl.BlockSpec((1, d), lambda i: (0, 0)),
                  pl.BlockSpec((1, d), lambda i: (0, 0))],
        out_specs=[rt.prompt(d), rt.sample(d, single=False)],
        out_shape=[jax.ShapeDtypeStruct((rt.n_p * tq, d), F32), jax.ShapeDtypeStruct((rt.n_s * tq, d), F32)],
        scratch_shapes=[pltpu.VMEM((2, TOP_K, tq, d), F32), pltpu.SemaphoreType.DMA((2,))],
        compiler_params=_params("arbitrary"),
        name="combine",
    )(ids, ids, yb, tg, x1, mod_seq, mod_row, lg, lb)


def _take(table, idx):
    hit = idx[:, None] == jnp.arange(table.shape[0], dtype=jnp.int32)[None, :]
    return jnp.sum(jnp.where(hit, table[None, :], 0), axis=1)


def _routing(ti, n_experts, tm, nj, subs):
    t = ti.shape[0]
    n_assign = t * TOP_K
    nb_max = n_assign // tm + n_experts
    onehot = (ti[:, :, None] == jnp.arange(n_experts, dtype=jnp.int32)[None, None, :]).astype(jnp.int32)
    per_tok = jnp.sum(onehot, axis=1)
    cum = jnp.cumsum(per_tok, axis=0)
    counts = cum[-1]
    rank = jnp.sum(onehot * cum[:, None, :], axis=2) - 1
    nblk = (counts + tm - 1) // tm
    blk_end = jnp.cumsum(nblk)
    blk_start = blk_end - nblk
    dest = jnp.sum(onehot * blk_start[None, None, :], axis=2) * tm + rank
    pad_lo = (blk_start * tm + counts).astype(jnp.int32)
    pad_len = (nblk * tm - counts).astype(jnp.int32)
    total_blk = blk_end[-1]
    blocks = jnp.arange(nb_max, dtype=jnp.int32)
    blk_e = jnp.minimum(jnp.sum((blk_end[None, :] <= blocks[:, None]).astype(jnp.int32), axis=1), n_experts - 1)
    start_b, nblk_b = _take(blk_start, blk_e), _take(nblk, blk_e)
    r_in_e = blocks - start_b
    g0 = start_b + (r_in_e // subs) * subs
    nsub = jnp.minimum(subs, nblk_b - (r_in_e // subs) * subs)
    p = jnp.arange(nb_max * nj, dtype=jnp.int32)
    bp = p // nj
    valid = bp < total_blk
    last = jnp.maximum(total_blk - 1, 0)
    bq = jnp.where(valid, bp, last)
    gq, nq, eq = _take(g0, bq), jnp.maximum(_take(nsub, bq), 1), _take(blk_e, bq)
    local = p - nj * gq
    jq = jnp.where(valid, local // nq, nj - 1)
    rq = jnp.where(valid, local % nq, nq - 1)
    item_xb = gq + rq
    item_ob = jnp.where(valid, jnp.where(jq == nj - 1, gq + rq, gq), bp)
    flags = valid.astype(jnp.int32) + 2 * (valid & (rq == 0)).astype(jnp.int32)
    nxt_blk = gq + nq
    has_next = nxt_blk < total_blk
    last_j = jq == nj - 1
    e_n = jnp.where(last_j & has_next, _take(blk_e, jnp.minimum(nxt_blk, nb_max - 1)), eq)
    j_n = jnp.where(last_j, jnp.where(has_next, 0, jq), jq + 1)
    cur, nxt = eq * nj + jq, e_n * nj + j_n
    wg_code = jnp.where(valid & (rq >= 1), nxt, cur)
    wu_code = jnp.where(valid & (rq >= 2), nxt, cur)
    items = (eq.astype(jnp.int32), jq.astype(jnp.int32), item_xb.astype(jnp.int32),
             item_ob.astype(jnp.int32), rq.astype(jnp.int32), flags,
             wg_code.astype(jnp.int32), wu_code.astype(jnp.int32))
    return dest.astype(jnp.int32), pad_lo, pad_len, total_blk.astype(jnp.int32), nb_max, items


WPREP_TR = 256


def _win_prep_kernel(c_ab, w_hbm, o_ref, buf, sem):
    c = pl.program_id(0)
    tr = o_ref.shape[0]
    o_a = QKV_W + V_W
    c_shift = o_a // tr

    def copy(cc):
        row = jnp.where(cc < c_shift, cc * tr, jnp.where(cc < c_ab, cc * tr + 2 * DN_HEADS, o_a))
        src = w_hbm.at[pl.ds(pl.multiple_of(row, 2 * DN_HEADS), tr)]
        return pltpu.make_async_copy(src, buf.at[cc % 2], sem.at[cc % 2])

    @pl.when(c == 0)
    def _():
        copy(c).start()

    @pl.when(c + 1 <= c_ab)
    def _():
        copy(c + 1).start()

    @pl.when(c <= c_ab)
    def _():
        copy(c).wait()

    x = buf[c % 2]
    rows = lax.broadcasted_iota(jnp.int32, x.shape, 0)
    keep = jnp.logical_or(c < c_ab, jnp.logical_and(c == c_ab, rows < 2 * DN_HEADS))
    o_ref[...] = jnp.where(keep, x, 0.0).astype(BF16)


def _rearranged_w_in(w_in, d):
    dm, nw = w_in.shape
    tr = WPREP_TR
    o_a = QKV_W + V_W
    rest = nw - o_a - 2 * DN_HEADS
    assert o_a % tr == 0 and rest % tr == 0
    nw_out = -(-(nw - 2 * DN_HEADS + LANES) // PROJ_TN) * PROJ_TN
    assert nw_out % tr == 0
    w = pl.pallas_call(
        functools.partial(_win_prep_kernel, (o_a + rest) // tr),
        grid=(nw_out // tr,),
        in_specs=[pl.BlockSpec(memory_space=pl.ANY)],
        out_specs=pl.BlockSpec((tr, dm), lambda c: (c, 0)),
        out_shape=jax.ShapeDtypeStruct((nw_out, dm), BF16),
        scratch_shapes=[pltpu.VMEM((2, tr, dm), F32), pltpu.SemaphoreType.DMA((2,))],
        compiler_params=_params("arbitrary"),
        name="w_in_prep",
    )(w_in.T)
    cols = dict(z=QKV_W, u=QKV_W + V_W, v=QKV_W + V_W + SG_W, ga=QKV_W + V_W + 2 * SG_W,
                gb=QKV_W + V_W + 2 * SG_W + d, ab=QKV_W + V_W + 2 * SG_W + 2 * d)
    return w, cols


def kernel(x_prompt, x_sample, state_conv_qkv, state_delta, c_prompt, c_sample, w_ada, b_ada, w_in, w_conv, a_log, dt_bias, o_norm_g, sg_ln_g, sg_ln_b, w_s, b_s, p_a, p_b, w_out, ln1_g, ln1_b, router_w, router_b, w_gu, b_gu, w_dn, b_dn, ln2_g, ln2_b):
    depth = w_ada.shape[0]
    alpha = float((2 * depth) ** 0.25)
    bp, seq, d = x_prompt.shape
    bs, lt, _ = x_sample.shape
    assert lt == DN_CONV and seq % SG_CHUNK == 0
    n_experts = router_w.shape[2]
    rows_p, rows_s = bp * seq, bs * lt
    xp = x_prompt.reshape(rows_p, d)
    xs = x_sample.reshape(rows_s, d)
    c_all = jnp.concatenate([jnp.repeat(c_sample, lt, axis=0), c_prompt,
                             jnp.zeros((-(rows_s + bp) % SUBLANES, d), F32)], axis=0)
    tile = lambda cap: _Rows(math.gcd(math.gcd(cap, rows_s), seq), rows_p, rows_s, seq)
    rt_proj, rt_sgu, rt_merge, rt_comb = tile(PROJ_TM), tile(SG_CHUNK), tile(MERGE_TM), tile(COMB_TQ)
    outs = dict(conv_p=[], delta_p=[], conv_s=[], delta_s=[], vrows=[])

    for l in range(depth):
        mod = _adaln(c_all, w_ada[l], b_ada[l])

        w_r, col = _rearranged_w_in(w_in[l], d)
        proj = _inproj(rt_proj, xp, xs, mod, mod, w_r)

        alog = jnp.pad(a_log[l:l + 1], ((0, 0), (0, LANES - DN_HEADS)))
        dtb = jnp.pad(dt_bias[l:l + 1], ((0, 0), (0, LANES - DN_HEADS)))
        og = o_norm_g[l:l + 1]
        conv0 = jnp.zeros((bp, DN_CONV - 1, QKV_W), F32)
        s0 = jnp.zeros((bp, DN_HEADS, DN_DK, DN_DV), F32)
        oa_p, conv_p, delta_p = _gdn_prompt(proj, conv0, s0, w_conv[l], alog, dtb, og, bp, seq,
                                            col['z'], col['ab'])
        oa_s, conv_s, delta_s = _gdn_sample(proj, rows_p, state_conv_qkv[l], state_delta[l], w_conv[l],
                                            alog, dtb, og, bs, col['z'], col['ab'])

        ts = rt_sgu.tm
        assert ts == SG_CHUNK
        ws_p = jnp.tril(w_s[l][:, :ts, :ts]).astype(BF16)
        bias_p = jnp.repeat(b_s[l].T[:ts], SG_CH, axis=1)
        eye = jnp.eye(ts // lt, dtype=F32)
        ws_s = jnp.stack([jnp.kron(eye, jnp.tril(w_s[l, g, :lt, :lt])) for g in range(SG_GROUPS)]).astype(BF16)
        bias_s = jnp.tile(jnp.repeat(b_s[l, :, :lt].T, SG_CH, axis=1), (ts // lt, 1))
        ob, vn_s = _sgu(rt_sgu, proj, sg_ln_g[l:l + 1], sg_ln_b[l:l + 1],
                        ws_p, ws_s, bias_p, bias_s, col['u'], col['v'])

        pa, pb, wo = p_a[l].astype(BF16), p_b[l].astype(BF16), w_out[l].astype(BF16)
        rws = _split(jnp.pad(router_w[l], ((0, 0), (0, LANES - n_experts))))
        rw = jnp.concatenate([rws[0], rws[0], rws[1]], axis=0)
        rb = jnp.pad(router_b[l:l + 1], ((0, 0), (0, LANES - n_experts)))
        x1, h2, ti, tg = _merge(alpha, n_experts, rt_merge, oa_p, oa_s, ob, proj, xp, xs,
                                mod, mod,
                                pa, pb, wo, ln1_g[l:l + 1], ln1_b[l:l + 1], rw, rb,
                                col['ga'], col['gb'])

        nj = w_dn.shape[2] // MOE_TF
        dest, pad_lo, pad_len, nblk_used, nb_max, items = _routing(ti[:, :TOP_K], n_experts, MOE_TM, nj, MOE_SUBS)
        xb = _dispatch(pad_lo, pad_len, nblk_used, dest, h2, MOE_TM, nb_max)
        yb = _moe(items, xb, w_gu[l], b_gu[l], w_dn[l], b_dn[l], MOE_TM, MOE_TF, MOE_SUBS)
        xp, xs = _combine(alpha, rt_comb, dest, yb, tg, x1, mod, mod,
                          ln2_g[l:l + 1], ln2_b[l:l + 1])

        outs['conv_p'].append(conv_p)
        outs['delta_p'].append(delta_p)
        outs['conv_s'].append(conv_s)
        outs['delta_s'].append(delta_s)
        outs['vrows'].append(vn_s.reshape(bs, lt, SG_W))

    return (xp.reshape(bp, seq, d), xs.reshape(bs, lt, d),
            jnp.stack(outs['conv_p']), jnp.stack(outs['delta_p']),
            jnp.stack(outs['conv_s']), jnp.stack(outs['delta_s']), jnp.stack(outs['vrows']))
```

```python
import functools
import math

import jax
import jax.numpy as jnp
from jax import lax
from jax.experimental import pallas as pl
from jax.experimental.pallas import tpu as pltpu

F32 = jnp.float32
BF16 = jnp.bfloat16

DN_HEADS = 8
DN_DK = 128
DN_DV = 128
DN_CONV = 4
DN_CHUNK = 64
SG_GROUPS = 8
SG_CH = 128
SG_CHUNK = 128
TOP_K = 4
SWIGLU_LIMIT = 7.0
SWIGLU_ALPHA = 1.702
LN_EPS = 1e-5
NORM_EPS = 1e-6
QK_W = DN_HEADS * DN_DK
V_W = DN_HEADS * DN_DV
QKV_W = 2 * QK_W + V_W
SG_W = SG_GROUPS * SG_CH

LANES = 128
SUBLANES = 8
VMEM_LIMIT = 56 * 1024 * 1024

PROJ_TM = 512
PROJ_TN = 1536
MERGE_TM = 256
MOE_TM = 256
MOE_TF = 512
MOE_SUBS = 6
COMB_TQ = 256
SAMPLE_BT = 4
SAMPLE_TP = 8
GDN_HG = 4

MOD_SHIFT1, MOD_SCALE1, MOD_GATE1, MOD_SHIFT2, MOD_SCALE2, MOD_GATE2 = range(6)
MOD_PARTS = 6


def _sigmoid(x):
    return 0.5 * (jnp.tanh(0.5 * x) + 1.0)


def _silu(x):
    return x * _sigmoid(x)


def _dot(a, b):
    return jnp.dot(a.astype(BF16), b.astype(BF16), preferred_element_type=F32)


def _dot_nt(a, b):
    return lax.dot_general(a.astype(BF16), b.astype(BF16), (((1,), (1,)), ((), ())),
                           preferred_element_type=F32)


def _split3(x):
    hi = x.astype(BF16)
    r1 = x - hi.astype(F32)
    mid = r1.astype(BF16)
    return hi, mid, (r1 - mid.astype(F32)).astype(BF16)


def _mask_sums(masks, col, n):
    lane = lax.broadcasted_iota(jnp.int32, (n, LANES), 1)
    hi, mid, lo = [p.astype(F32) for p in _split3(col)]
    parts = jnp.where(lane == 0, hi, jnp.where(lane == 1, mid, jnp.where(lane == 2, lo, 0.0)))
    y = jnp.dot(masks, parts.astype(BF16), preferred_element_type=F32)
    y = y[:, 0:1] + y[:, 1:2] + y[:, 2:3]
    return [y[i * n:(i + 1) * n] for i in range(masks.shape[0] // n)]


def _layer_norm(x, g, b):
    mu = jnp.mean(x, axis=-1, keepdims=True)
    xc = x - mu
    var = jnp.mean(xc * xc, axis=-1, keepdims=True)
    return xc * lax.rsqrt(var + LN_EPS) * g + b


def _params(*sem):
    return pltpu.CompilerParams(dimension_semantics=sem, vmem_limit_bytes=VMEM_LIMIT)


class _Rows:
    def __init__(self, tm, rows_p, rows_s, seq):
        assert rows_p % tm == 0 and rows_s % tm == 0 and seq % tm == 0, (tm, rows_p, rows_s, seq)
        self.tm, self.n_p, self.n_s = tm, rows_p // tm, rows_s // tm
        self.tiles_per_seq = seq // tm
        self.bp = rows_p // seq
        assert rows_s % SUBLANES == 0
        self.rows_s = rows_s

    @property
    def n(self):
        return self.n_p + self.n_s

    def prompt(self, width, col=0):
        return pl.BlockSpec((self.tm, width), lambda i, *_: (jnp.minimum(i, self.n_p - 1), col))

    def sample(self, width, col=0, single=True):
        mode = dict(pipeline_mode=pl.Buffered(1)) if single else {}
        return pl.BlockSpec((self.tm, width), lambda i, *_: (jnp.maximum(i - self.n_p, 0), col), **mode)

    def joint(self, width, col=0):
        return pl.BlockSpec((self.tm, width), lambda i, *_: (i, col))

    def _seq(self, i):
        return jnp.minimum(i // self.tiles_per_seq, self.bp - 1)

    def seq_vec(self, d, part):
        return pl.BlockSpec((SUBLANES, d), lambda i, *_: ((self.rows_s + self._seq(i)) // SUBLANES, part))

    def row_vec(self, d, part):
        return pl.BlockSpec((self.tm, d), lambda i, *_: (jnp.maximum(i - self.n_p, 0), part),
                            pipeline_mode=pl.Buffered(1))

    def pick_vec(self, i, seq_ref, row_ref):
        prompt = seq_ref[pl.ds(self._seq(i) % SUBLANES, 1), :]
        return jnp.where(i >= self.n_p, row_ref[...], prompt)


def _pick(is_sample, prompt_ref, sample_ref):
    return jnp.where(is_sample, sample_ref[...], prompt_ref[...])


def _adaln_kernel(c_ref, w_ref, b_ref, o_ref):
    o_ref[...] = _dot(_silu(c_ref[...]), w_ref[...]) + b_ref[...]


def _adaln(c, w, b):
    rows, d = c.shape
    n = w.shape[1]
    tn = 1024
    return pl.pallas_call(
        _adaln_kernel,
        grid=(n // tn,),
        in_specs=[pl.BlockSpec((rows, d), lambda j: (0, 0)),
                  pl.BlockSpec((d, tn), lambda j: (0, j)),
                  pl.BlockSpec((1, tn), lambda j: (0, j))],
        out_specs=pl.BlockSpec((rows, tn), lambda j: (0, j)),
        out_shape=jax.ShapeDtypeStruct((rows, n), F32),
        compiler_params=_params("arbitrary"),
        name="adaln",
    )(c, w, b.reshape(1, n))


def _inproj_kernel(rt, xp_ref, xs_ref, scp_ref, scs_ref, shp_ref, shs_ref, w_ref, o_ref, h_scr):
    @pl.when(pl.program_id(1) == 0)
    def _():
        i = pl.program_id(0)
        x = _pick(i >= rt.n_p, xp_ref, xs_ref)
        h_scr[...] = (x * (1.0 + rt.pick_vec(i, scp_ref, scs_ref)) + rt.pick_vec(i, shp_ref, shs_ref)).astype(BF16)

    o_ref[...] = lax.dot_general(h_scr[...], w_ref[...], (((1,), (1,)), ((), ())), preferred_element_type=F32)


def _inproj(rt, xp, xs, mod_seq, mod_row, w):
    d = xp.shape[1]
    nw = w.shape[0]
    tm = rt.tm
    return pl.pallas_call(
        functools.partial(_inproj_kernel, rt),
        grid=(rt.n, nw // PROJ_TN),
        in_specs=[rt.prompt(d), rt.sample(d), rt.seq_vec(d, MOD_SCALE1), rt.row_vec(d, MOD_SCALE1),
                  rt.seq_vec(d, MOD_SHIFT1), rt.row_vec(d, MOD_SHIFT1),
                  pl.BlockSpec((PROJ_TN, d), lambda i, j: (j, 0))],
        out_specs=pl.BlockSpec((tm, PROJ_TN), lambda i, j: (i, j)),
        out_shape=jax.ShapeDtypeStruct((rt.n * tm, nw), F32),
        scratch_shapes=[pltpu.VMEM((tm, d), BF16)],
        compiler_params=_params("arbitrary", "arbitrary"),
        name="inproj",
    )(xp, xs, mod_seq, mod_row, mod_seq, mod_row, w)


def _softplus(x):
    return jnp.maximum(x, 0.0) + jnp.log1p(jnp.exp(-jnp.abs(x)))


def _split(a):
    hi = a.astype(BF16)
    lo = (a - hi.astype(F32)).astype(BF16)
    return hi, lo


def _map(f, *lists):
    return [f(*args) for args in zip(*lists)]


def _unit_lower_inverses_minus_eye(mats, rows, cols):
    same = (rows // 16) == (cols // 16)
    n = [jnp.where(same, a, 0.0) for a in mats]
    b = _map(lambda a, x: a - x, mats, n)
    n2 = _map(lambda x: _dot(x, x), n)
    n4 = _map(lambda x: _dot(x, x), n2)
    n8 = _map(lambda x: _dot(x, x), n4)
    r = [-x for x in n]
    r = _map(lambda x, p: x + p + _dot(x, p), r, n2)
    r = _map(lambda x, p: x + p + _dot(x, p), r, n4)
    dm = _map(lambda x, p: x + p + _dot(x, p), r, n8)
    m = _map(lambda x, y: y + _dot(x, y), dm, b)
    m2 = _map(lambda x: _dot(x, x), m)
    xm = _map(lambda x, p: x + p + _dot(p, x), dm, m2)
    return _map(lambda x, p: x - p - _dot(p, x), xm, m)


def _gdn_prompt_kernel(qkv_ref, z_ref, ab_ref, hist_ref, s0_ref, wconv_ref, alog_ref, dt_ref, og_ref,
                       o_ref, conv_ref, s_ref, xp_scr):
    n = pl.program_id(1)
    c = DN_CHUNK
    pad = SUBLANES
    nr = GDN_HG * c
    ngrp = DN_HEADS // GDN_HG

    @pl.when(n == 0)
    def _():
        xp_scr[pl.ds(0, pad), :] = jnp.zeros((pad, QKV_W), F32)
        xp_scr[pl.ds(pad - (DN_CONV - 1), DN_CONV - 1), :] = hist_ref[...]
        s_ref[...] = s0_ref[...]

    x = qkv_ref[...]
    xp_scr[pl.ds(pad, c), :] = x
    y = jnp.zeros((c, QKV_W), F32)
    for j in range(DN_CONV):
        y = y + xp_scr[pl.ds(pad - (DN_CONV - 1) + j, c), :] * wconv_ref[pl.ds(j, 1), :]
    y = _silu(y)
    tail = xp_scr[pl.ds(c + pad - (DN_CONV - 1), DN_CONV - 1), :]
    conv_ref[...] = tail
    xp_scr[pl.ds(pad - (DN_CONV - 1), DN_CONV - 1), :] = tail

    ab = ab_ref[...]
    g = -jnp.exp(alog_ref[...]) * _softplus(ab + dt_ref[...])
    beta_all = _sigmoid(ab)
    r64 = lax.broadcasted_iota(jnp.int32, (c, c), 0)
    c64 = lax.broadcasted_iota(jnp.int32, (c, c), 1)
    tri = (r64 >= c64).astype(BF16)
    gc = sum(jnp.dot(tri, part, preferred_element_type=F32) for part in _split3(g))
    z = z_ref[...]
    og = og_ref[...]

    rows = lax.broadcasted_iota(jnp.int32, (nr, nr), 0)
    cols = lax.broadcasted_iota(jnp.int32, (nr, nr), 1)
    same = (rows // c) == (cols // c)
    causal = same & (rows >= cols)
    strict = same & (rows > cols)
    rowhead = lax.broadcasted_iota(jnp.int32, (nr, 1), 0) // c
    groups = [range(grp * GDN_HG, (grp + 1) * GDN_HG) for grp in range(ngrp)]

    def stack(heads, off, width):
        return jnp.concatenate([y[:, off + h * width:off + (h + 1) * width] for h in heads], axis=0)

    def l2n(t):
        return t * lax.rsqrt(jnp.sum(t * t, axis=-1, keepdims=True) + NORM_EPS)

    q = [l2n(stack(hs, 0, DN_DK)) * (DN_DK ** -0.5) for hs in groups]
    k = [l2n(stack(hs, QK_W, DN_DK)) for hs in groups]
    v = [stack(hs, 2 * QK_W, DN_DV) for hs in groups]
    beta = [jnp.concatenate([beta_all[:, DN_HEADS + h:DN_HEADS + h + 1] for h in hs], axis=0) for hs in groups]
    gcf = [jnp.concatenate([jnp.broadcast_to(gc[:, h:h + 1], (c, LANES)) for h in hs], axis=0) for hs in groups]
    gcc = [t[:, 0:1] for t in gcf]
    grow = [t.T[0:1, :] for t in gcf]
    glast = [[gc[c - 1:c, h:h + 1] for h in hs] for hs in groups]
    gtot = [jnp.concatenate([jnp.broadcast_to(t, (c, 1)) for t in gl], axis=0) for gl in glast]
    decay = _map(lambda a, b: jnp.where(causal, jnp.exp(jnp.where(causal, a - b, 0.0)), 0.0), gcc, grow)
    egc = _map(jnp.exp, gcc)
    kb = _map(lambda a, b: a * b, k, beta)
    a_mat = _map(lambda a, b, dd: jnp.where(strict, _dot_nt(a, b) * dd, 0.0), kb, k, decay)
    tm1 = _unit_lower_inverses_minus_eye(a_mat, rows, cols)
    u = _map(lambda t, a, b: a * b + _dot(t, a * b), tm1, v, beta)
    w = _map(lambda t, a, b: a * b + _dot(t, a * b), tm1, kb, egc)
    attn = _map(lambda a, b, dd: _dot_nt(a, b) * dd, q, k, decay)
    qg = _map(lambda a, b: a * b, q, egc)
    kd_t = _map(lambda a, b, cc: (a * jnp.exp(b - cc)).T, k, gtot, gcc)

    for gi, hs in enumerate(groups):
        v_news, qss = [], []
        for hl, h in enumerate(hs):
            sl = slice(hl * c, (hl + 1) * c)
            rs = _dot(jnp.concatenate([w[gi][sl], qg[gi][sl]], axis=0), s_ref[h])
            v_news.append(u[gi][sl] - rs[:c])
            qss.append(rs[c:])
        v_new = jnp.concatenate(v_news, axis=0)
        o = jnp.concatenate(qss, axis=0) + _dot(attn[gi], v_new)
        for hl, h in enumerate(hs):
            upd = _dot(kd_t[gi], jnp.where(rowhead == hl, v_new, 0.0))
            s_ref[h] = s_ref[h] * jnp.exp(glast[gi][hl]) + upd
        o = o * lax.rsqrt(jnp.mean(o * o, axis=-1, keepdims=True) + NORM_EPS) * og
        for hl, h in enumerate(hs):
            o_ref[:, h * DN_DV:(h + 1) * DN_DV] = o[hl * c:(hl + 1) * c] * _silu(z[:, h * DN_DV:(h + 1) * DN_DV])


def _gdn_prompt(proj, hist, s0, wconv, alog, dtb, og, batch, seq, col_z, col_ab):
    c = DN_CHUNK
    nchunk = seq // c
    rows = batch * seq
    row = lambda b, n: b * nchunk + n
    return pl.pallas_call(
        _gdn_prompt_kernel,
        grid=(batch, nchunk),
        in_specs=[pl.BlockSpec((c, QKV_W), lambda b, n: (row(b, n), 0)),
                  pl.BlockSpec((c, V_W), lambda b, n: (row(b, n), col_z // V_W)),
                  pl.BlockSpec((c, LANES), lambda b, n: (row(b, n), col_ab // LANES)),
                  pl.BlockSpec((None, DN_CONV - 1, QKV_W), lambda b, n: (b, 0, 0)),
                  pl.BlockSpec((None, DN_HEADS, DN_DK, DN_DV), lambda b, n: (b, 0, 0, 0)),
                  pl.BlockSpec((DN_CONV, QKV_W), lambda b, n: (0, 0)),
                  pl.BlockSpec((1, LANES), lambda b, n: (0, 0)),
                  pl.BlockSpec((1, LANES), lambda b, n: (0, 0)),
                  pl.BlockSpec((1, DN_DV), lambda b, n: (0, 0))],
        out_specs=[pl.BlockSpec((c, V_W), lambda b, n: (row(b, n), 0)),
                   pl.BlockSpec((None, DN_CONV - 1, QKV_W), lambda b, n: (b, 0, 0)),
                   pl.BlockSpec((None, DN_HEADS, DN_DK, DN_DV), lambda b, n: (b, 0, 0, 0))],
        out_shape=[jax.ShapeDtypeStruct((rows, V_W), F32),
                   jax.ShapeDtypeStruct((batch, DN_CONV - 1, QKV_W), F32),
                   jax.ShapeDtypeStruct((batch, DN_HEADS, DN_DK, DN_DV), F32)],
        scratch_shapes=[pltpu.VMEM((c + SUBLANES, QKV_W), F32)],
        compiler_params=_params("arbitrary", "arbitrary"),
        name="gdn_prompt",
    )(proj, proj, proj, hist, s0, wconv, alog, dtb, og)


def _gdn_sample_kernel(qkv_ref, z_ref, ab_ref, hist_ref, s0_ref, wconv_ref, alog_ref, dt_ref, og_ref,
                       o_ref, conv_ref, s_ref, xp_scr, ab_scr, wq_scr, r_scr, kdt_scr, vn_scr, gl_scr):
    bt, tp, lt = SAMPLE_BT, SAMPLE_TP, DN_CONV
    nblk = DN_HEADS * bt
    nrow = nblk * tp
    hist_rows = DN_CONV - 1

    xp_scr[...] = jnp.zeros(xp_scr.shape, F32)
    ab_scr[...] = jnp.zeros(ab_scr.shape, F32)
    ys = []
    for b in range(bt):
        xp_scr[b, pl.ds(SUBLANES - hist_rows, hist_rows), :] = hist_ref[b]
        xp_scr[b, pl.ds(SUBLANES, lt), :] = qkv_ref[pl.ds(b * lt, lt), :]
        yb = jnp.zeros((tp, QKV_W), F32)
        for j in range(DN_CONV):
            yb = yb + xp_scr[b, pl.ds(SUBLANES - hist_rows + j, tp), :] * wconv_ref[pl.ds(j, 1), :]
        ys.append(_silu(yb))
        conv_ref[b] = xp_scr[b, pl.ds(SUBLANES + lt - hist_rows, hist_rows), :]
        ab_scr[b, pl.ds(0, lt), :] = ab_ref[pl.ds(b * lt, lt), :]
    y = jnp.concatenate(ys, axis=0)
    ab = jnp.concatenate([ab_scr[b] for b in range(bt)], axis=0)
    tok = lax.broadcasted_iota(jnp.int32, (bt * tp, 1), 0) % tp
    real = tok < lt
    g_all = jnp.where(real, -jnp.exp(alog_ref[...]) * _softplus(ab + dt_ref[...]), 0.0)
    beta_all = jnp.where(real, _sigmoid(ab), 0.0)

    def heads_to_rows(t, off, width):
        return jnp.concatenate([t[:, off + h * width:off + (h + 1) * width] for h in range(DN_HEADS)], axis=0)

    realr = jnp.concatenate([real] * DN_HEADS, axis=0)
    q = jnp.where(realr, heads_to_rows(y, 0, DN_DK), 0.0)
    k = jnp.where(realr, heads_to_rows(y, QK_W, DN_DK), 0.0)
    v = jnp.where(realr, heads_to_rows(y, 2 * QK_W, DN_DV), 0.0)
    gcol = jnp.concatenate([g_all[:, h:h + 1] for h in range(DN_HEADS)], axis=0)
    beta = jnp.concatenate([beta_all[:, DN_HEADS + h:DN_HEADS + h + 1] for h in range(DN_HEADS)], axis=0)
    q = q * lax.rsqrt(jnp.sum(q * q, axis=-1, keepdims=True) + NORM_EPS) * (DN_DK ** -0.5)
    k = k * lax.rsqrt(jnp.sum(k * k, axis=-1, keepdims=True) + NORM_EPS)

    rows = lax.broadcasted_iota(jnp.int32, (nrow, nrow), 0)
    cols = lax.broadcasted_iota(jnp.int32, (nrow, nrow), 1)
    same = (rows // tp) == (cols // tp)
    causal = same & (rows >= cols)
    strict = same & (rows > cols)
    gcc, gtot = _mask_sums(jnp.concatenate([causal.astype(BF16), same.astype(BF16)], axis=0), gcol, nrow)
    gtot = jnp.broadcast_to(gtot, (nrow, LANES))
    grow = jnp.broadcast_to(gcc, (nrow, LANES)).T[0:1, :]
    decay = jnp.where(causal, jnp.exp(jnp.where(causal, gcc - grow, 0.0)), 0.0)
    egc = jnp.exp(gcc)
    kb = k * beta
    a_mat = jnp.where(strict, _dot_nt(kb, k) * decay, 0.0)
    a2 = _dot(a_mat, a_mat)
    tm1 = a2 - a_mat - _dot(a_mat, a2)
    u = v * beta + _dot(tm1, v * beta)
    w = kb * egc + _dot(tm1, kb * egc)
    attn = _dot_nt(q, k) * decay
    qg = q * egc
    for i in range(nblk):
        wq_scr[pl.ds(2 * tp * i, tp), :] = w[i * tp:(i + 1) * tp, :]
        wq_scr[pl.ds(2 * tp * i + tp, tp), :] = qg[i * tp:(i + 1) * tp, :]
    kdt_scr[...] = (k * jnp.exp(gtot[:, 0:1] - gcc)).T
    gl_scr[...] = jnp.exp(gtot)

    def read_state(i, carry):
        r0 = pl.multiple_of(i * 2 * tp, 2 * tp)
        r_scr[pl.ds(r0, 2 * tp), :] = _dot(wq_scr[pl.ds(r0, 2 * tp), :], s0_ref[i % bt, i // bt])
        return carry

    lax.fori_loop(0, nblk, read_state, 0, unroll=4)
    ws = jnp.concatenate([r_scr[pl.ds(2 * tp * i, tp), :] for i in range(nblk)], axis=0)
    qs = jnp.concatenate([r_scr[pl.ds(2 * tp * i + tp, tp), :] for i in range(nblk)], axis=0)
    v_new = u - ws
    vn_scr[...] = v_new
    o = qs + _dot(attn, v_new)
    blockid = lax.broadcasted_iota(jnp.int32, (nrow, 1), 0) // tp

    def write_state(i, carry):
        r0 = pl.multiple_of(i * tp, tp)
        upd = _dot(kdt_scr[...], jnp.where(blockid == i, vn_scr[...], 0.0))
        s_ref[i % bt, i // bt] = s0_ref[i % bt, i // bt] * gl_scr[pl.ds(r0, 1), :] + upd
        return carry

    lax.fori_loop(0, nblk, write_state, 0, unroll=4)

    og = og_ref[...]
    o = o * lax.rsqrt(jnp.mean(o * o, axis=-1, keepdims=True) + NORM_EPS) * og
    for b in range(bt):
        for h in range(DN_HEADS):
            zbh = z_ref[pl.ds(b * lt, lt), h * DN_DV:(h + 1) * DN_DV]
            blk = o[(h * bt + b) * tp:(h * bt + b) * tp + lt, :]
            o_ref[pl.ds(b * lt, lt), h * DN_DV:(h + 1) * DN_DV] = blk * _silu(zbh)


def _gdn_sample(proj, row0, hist, s0, wconv, alog, dtb, og, batch, col_z, col_ab):
    bt, tp, lt = SAMPLE_BT, SAMPLE_TP, DN_CONV
    rows = batch * lt
    nrow = DN_HEADS * bt * tp
    blk = bt * lt
    assert row0 % blk == 0 and batch % bt == 0
    b0 = row0 // blk
    return pl.pallas_call(
        _gdn_sample_kernel,
        grid=(batch // bt,),
        in_specs=[pl.BlockSpec((blk, QKV_W), lambda i: (b0 + i, 0)),
                  pl.BlockSpec((blk, V_W), lambda i: (b0 + i, col_z // V_W)),
                  pl.BlockSpec((blk, LANES), lambda i: (b0 + i, col_ab // LANES)),
                  pl.BlockSpec((bt, DN_CONV - 1, QKV_W), lambda i: (i, 0, 0)),
                  pl.BlockSpec((bt, DN_HEADS, DN_DK, DN_DV), lambda i: (i, 0, 0, 0)),
                  pl.BlockSpec((DN_CONV, QKV_W), lambda i: (0, 0)),
                  pl.BlockSpec((1, LANES), lambda i: (0, 0)),
                  pl.BlockSpec((1, LANES), lambda i: (0, 0)),
                  pl.BlockSpec((1, DN_DV), lambda i: (0, 0))],
        out_specs=[pl.BlockSpec((blk, V_W), lambda i: (i, 0)),
                   pl.BlockSpec((bt, DN_CONV - 1, QKV_W), lambda i: (i, 0, 0)),
                   pl.BlockSpec((bt, DN_HEADS, DN_DK, DN_DV), lambda i: (i, 0, 0, 0))],
        out_shape=[jax.ShapeDtypeStruct((rows, V_W), F32),
                   jax.ShapeDtypeStruct((batch, DN_CONV - 1, QKV_W), F32),
                   jax.ShapeDtypeStruct((batch, DN_HEADS, DN_DK, DN_DV), F32)],
        scratch_shapes=[pltpu.VMEM((bt, SUBLANES + tp, QKV_W), F32),
                        pltpu.VMEM((bt, tp, LANES), F32),
                        pltpu.VMEM((2 * nrow, DN_DK), F32),
                        pltpu.VMEM((2 * nrow, DN_DV), F32),
                        pltpu.VMEM((DN_DK, nrow), F32),
                        pltpu.VMEM((nrow, DN_DV), F32),
                        pltpu.VMEM((nrow, LANES), F32)],
        compiler_params=_params("arbitrary"),
        name="gdn_sample",
    )(proj, proj, proj, hist, s0, wconv, alog, dtb, og)


def _sgu_kernel(n_p, u_ref, v_ref, g_ref, b_ref, wsp_ref, wss_ref, bp_ref, bs_ref, o_ref, vn_ref):
    is_s = pl.program_id(0) >= n_p
    u = jax.nn.gelu(u_ref[...])
    vn = _layer_norm(jax.nn.gelu(v_ref[...]), g_ref[...], b_ref[...])
    vn_ref[...] = vn
    bias = _pick(is_s, bp_ref, bs_ref)
    for g in range(SG_GROUPS):
        sl = slice(g * SG_CH, (g + 1) * SG_CH)
        ws = jnp.where(is_s, wss_ref[g], wsp_ref[g])
        mixed = _dot(ws, vn[:, sl]) + bias[:, sl]
        o_ref[:, sl] = u[:, sl] * mixed


def _sgu(rt, proj, ln_g, ln_b, ws_p, ws_s, bias_p, bias_s, col_u, col_v):
    t = rt.tm
    const = lambda shape: pl.BlockSpec(shape, lambda i: (0,) * len(shape))
    return pl.pallas_call(
        functools.partial(_sgu_kernel, rt.n_p),
        grid=(rt.n,),
        in_specs=[rt.joint(SG_W, col_u // SG_W), rt.joint(SG_W, col_v // SG_W),
                  const((1, SG_W)), const((1, SG_W)),
                  const((SG_GROUPS, t, t)), const((SG_GROUPS, t, t)), const((t, SG_W)), const((t, SG_W))],
        out_specs=[rt.joint(SG_W), rt.sample(SG_W, single=False)],
        out_shape=[jax.ShapeDtypeStruct((rt.n * t, SG_W), F32),
                   jax.ShapeDtypeStruct((rt.n_s * t, SG_W), F32)],
        compiler_params=_params("arbitrary"),
        name="sgu",
    )(proj, proj, ln_g, ln_b, ws_p, ws_s, bias_p, bias_s)


def _merge_kernel(alpha, n_experts, rt, oap_ref, oas_ref, ob_ref, ga_ref, gb_ref, xp_ref, xs_ref,
                  gtp_ref, gts_ref, scp_ref, scs_ref, shp_ref, shs_ref,
                  pa_ref, pb_ref, wo_ref, lg_ref, lb_ref, rw_ref, rb_ref,
                  x1_ref, h2_ref, ti_ref, tg_ref):
    i = pl.program_id(0)
    is_s = i >= rt.n_p
    oa = _pick(is_s, oap_ref, oas_ref)
    merged = (_sigmoid(ga_ref[...]) * _dot(oa, pa_ref[...])
              + _sigmoid(gb_ref[...]) * _dot(ob_ref[...], pb_ref[...]))
    y = _dot(merged, wo_ref[...])
    x = _pick(is_s, xp_ref, xs_ref)
    x1 = _layer_norm(alpha * x + rt.pick_vec(i, gtp_ref, gts_ref) * y, lg_ref[...], lb_ref[...])
    x1_ref[...] = x1
    h2 = x1 * (1.0 + rt.pick_vec(i, scp_ref, scs_ref)) + rt.pick_vec(i, shp_ref, shs_ref)
    h2_ref[...] = h2
    hs = _split(h2)
    logits = jnp.dot(jnp.concatenate([hs[0], hs[1], hs[0]], axis=1), rw_ref[...],
                     preferred_element_type=F32) + rb_ref[...]
    lane = lax.broadcasted_iota(jnp.int32, logits.shape, 1)
    logits = jnp.where(lane < n_experts, logits, -jnp.inf)
    ti = jnp.zeros(logits.shape, jnp.int32)
    tv = jnp.zeros(logits.shape, F32)
    top = None
    for kk in range(TOP_K):
        m = jnp.max(logits, axis=-1, keepdims=True)
        idx = jnp.min(jnp.where(logits == m, lane.astype(F32), float(LANES)), axis=-1,
                      keepdims=True).astype(jnp.int32)
        if kk == 0:
            top = m
        ti = jnp.where(lane == kk, idx, ti)
        tv = jnp.where(lane == kk, jnp.exp(m - top), tv)
        logits = jnp.where(lane == idx, -jnp.inf, logits)
    ti_ref[...] = ti
    tg_ref[...] = tv / jnp.sum(tv, axis=-1, keepdims=True)


def _merge(alpha, n_experts, rt, oa_p, oa_s, ob, proj, xp, xs, mod_seq, mod_row, pa, pb, wo, lg, lb, rw, rb,
           col_ga, col_gb):
    d = xp.shape[1]
    tm = rt.tm
    rows = rt.n * tm
    const = lambda shape: pl.BlockSpec(shape, lambda i: (0,) * len(shape), pipeline_mode=pl.Buffered(1))
    return pl.pallas_call(
        functools.partial(_merge_kernel, alpha, n_experts, rt),
        grid=(rt.n,),
        in_specs=[rt.prompt(V_W), rt.sample(V_W), rt.joint(SG_W),
                  rt.joint(d, col_ga // d), rt.joint(d, col_gb // d),
                  rt.prompt(d), rt.sample(d),
                  rt.seq_vec(d, MOD_GATE1), rt.row_vec(d, MOD_GATE1),
                  rt.seq_vec(d, MOD_SCALE2), rt.row_vec(d, MOD_SCALE2),
                  rt.seq_vec(d, MOD_SHIFT2), rt.row_vec(d, MOD_SHIFT2),
                  const((V_W, d)), const((SG_W, d)), const((d, d)),
                  const((1, d)), const((1, d)), const((3 * d, LANES)), const((1, LANES))],
        out_specs=[rt.joint(d), rt.joint(d), rt.joint(LANES), rt.joint(LANES)],
        out_shape=[jax.ShapeDtypeStruct((rows, d), F32),
                   jax.ShapeDtypeStruct((rows, d), F32),
                   jax.ShapeDtypeStruct((rows, LANES), jnp.int32),
                   jax.ShapeDtypeStruct((rows, LANES), F32)],
        compiler_params=_params("arbitrary"),
        name="merge",
    )(oa_p, oa_s, ob, proj, proj, xp, xs, mod_seq, mod_row, mod_seq, mod_row, mod_seq, mod_row,
      pa, pb, wo, lg, lb, rw, rb)


FILL_TILES = (16, 8, 4, 2, 1)


def _row_at(ref3, row):
    return ref3.at[lax.shift_right_logical(row, 3), pl.ds(row & (SUBLANES - 1), 1)]


def _dispatch_kernel(tm, n_experts, padlo_ref, padlen_ref, nblk_ref, dest_ref, h_ref, xb_hbm, zero_scr, sem, zsem):
    i = pl.program_id(0)
    tiles = h_ref.shape[0]
    nb_max = xb_hbm.shape[0] * SUBLANES // tm
    ztiles = zero_scr.shape[0]

    def fill(act):
        def per_expert(e, carry):
            lo = padlo_ref[e]
            ln = padlen_ref[e]
            head = (SUBLANES - (lo & (SUBLANES - 1))) & (SUBLANES - 1)
            for r in range(SUBLANES - 1):
                @pl.when(r < head)
                def _(r=r):
                    act(pltpu.make_async_copy(zero_scr.at[0, pl.ds(0, 1)], _row_at(xb_hbm, lo + r), zsem))
            off = lax.shift_right_logical(lo + head, 3)
            rem = lax.shift_right_logical(ln - head, 3)
            for b in FILL_TILES:
                @pl.when((rem & b) != 0)
                def _(off=off, b=b):
                    act(pltpu.make_async_copy(zero_scr.at[pl.ds(0, b)], xb_hbm.at[pl.ds(off, b)], zsem))
                off = off + (rem & b)
            return carry

        lax.fori_loop(0, n_experts, per_expert, 0)

        def per_block(blk, carry):
            for part in range(tm // SUBLANES // ztiles):
                t0 = blk * (tm // SUBLANES) + part * ztiles
                act(pltpu.make_async_copy(zero_scr, xb_hbm.at[pl.ds(t0, ztiles)], zsem))
            return carry

        lax.fori_loop(nblk_ref[0], nb_max, per_block, 0)

    @pl.when(i == 0)
    def _():
        zero_scr[...] = jnp.zeros(zero_scr.shape, F32)
        fill(lambda cp: cp.start())
        fill(lambda cp: cp.wait())

    def start(tile, carry):
        for u in range(SUBLANES):
            for kk in range(TOP_K):
                row = dest_ref[0, tile * (SUBLANES * TOP_K) + u * TOP_K + kk]
                pltpu.make_async_copy(h_ref.at[tile, pl.ds(u, 1)], _row_at(xb_hbm, row), sem).start(priority=kk % 2)
        return carry

    lax.fori_loop(0, tiles, start, 0)
    for kk in range(TOP_K):
        pltpu.make_async_copy(h_ref, xb_hbm.at[pl.ds(0, tiles)], sem).wait()


def _dispatch(padlo, padlen, nblk_used, dest, h2, tm, nb_max):
    t, d = h2.shape
    tq = math.gcd(256, t)
    n_experts = padlo.shape[0]
    grid_spec = pltpu.PrefetchScalarGridSpec(
        num_scalar_prefetch=3,
        grid=(t // tq,),
        in_specs=[pl.BlockSpec((None, 1, tq * TOP_K), lambda i, *_: (i, 0, 0), memory_space=pltpu.SMEM),
                  pl.BlockSpec((tq // SUBLANES, SUBLANES, d), lambda i, *_: (i, 0, 0))],
        out_specs=pl.BlockSpec(memory_space=pl.ANY),
        scratch_shapes=[pltpu.VMEM((FILL_TILES[0], SUBLANES, d), F32), pltpu.SemaphoreType.DMA(()),
                        pltpu.SemaphoreType.DMA(())],
    )
    xb = pl.pallas_call(
        functools.partial(_dispatch_kernel, tm, n_experts),
        grid_spec=grid_spec,
        out_shape=jax.ShapeDtypeStruct((nb_max * tm // SUBLANES, SUBLANES, d), F32),
        compiler_params=pltpu.CompilerParams(dimension_semantics=("arbitrary",), vmem_limit_bytes=VMEM_LIMIT,
                                             has_side_effects=True),
        name="dispatch",
    )(padlo, padlen, nblk_used.reshape(1), dest.reshape(t // tq, 1, tq * TOP_K),
      h2.reshape(t // SUBLANES, SUBLANES, d))
    return xb.reshape(nb_max * tm, d)


def _moe_kernel(nj, e_ref, j_ref, xb_ref, ob_ref, r_ref, flag_ref, wg_code_ref, wu_code_ref,
                x_ref, wg_ref, wu_ref, wd_ref, bg_ref, bu_ref, bd_ref, o_ref,
                wg_scr, wu_scr, wd_scr, acc_scr):
    s = pl.program_id(0)
    flags = flag_ref[s]
    j = j_ref[s]
    r = r_ref[s]

    @pl.when((flags & 2) != 0)
    def _():
        wg_scr[...] = wg_ref[...].astype(BF16)
        wu_scr[...] = wu_ref[...].astype(BF16)
        wd_scr[...] = wd_ref[...].astype(BF16)

    @pl.when(flags == 0)
    def _():
        o_ref[...] = jnp.zeros(o_ref.shape, F32)

    @pl.when((flags & 1) != 0)
    def _():
        x = x_ref[...].astype(BF16)
        gate = jnp.dot(x, wg_scr[...], preferred_element_type=F32) + bg_ref[...]
        up = jnp.dot(x, wu_scr[...], preferred_element_type=F32) + bu_ref[...]
        gate = jnp.minimum(gate, SWIGLU_LIMIT)
        up = jnp.clip(up, -SWIGLU_LIMIT, SWIGLU_LIMIT)
        act = (up + 1.0) * gate * _sigmoid(SWIGLU_ALPHA * gate)
        y = jnp.dot(act.astype(BF16), wd_scr[...], preferred_element_type=F32)

        @pl.when(j == 0)
        def _():
            acc_scr[r] = y

        @pl.when(jnp.logical_and(j > 0, j < nj - 1))
        def _():
            acc_scr[r] = acc_scr[r] + y

        @pl.when(j == nj - 1)
        def _():
            o_ref[...] = acc_scr[r] + y + bd_ref[...]


def _moe(items, xb, w_gu, b_gu, w_dn, b_dn, tm, tf, subs):
    item_e, item_j, item_xb, item_ob, item_r, item_flag, wg_code, wu_code = items
    n_items = item_e.shape[0]
    nslot, d = xb.shape
    n_exp, _, f2 = w_gu.shape
    f = f2 // 2
    nj = f // tf
    assert nj >= 2
    grid_spec = pltpu.PrefetchScalarGridSpec(
        num_scalar_prefetch=8,
        grid=(n_items,),
        in_specs=[pl.BlockSpec((tm, d), lambda s, e, j, xbk, obk, r, fl, cg, cu: (xbk[s], 0)),
                  pl.BlockSpec((None, d, tf), lambda s, e, j, xbk, obk, r, fl, cg, cu: (cg[s] // nj, 0, cg[s] % nj)),
                  pl.BlockSpec((None, d, tf),
                               lambda s, e, j, xbk, obk, r, fl, cg, cu: (cu[s] // nj, 0, nj + cu[s] % nj)),
                  pl.BlockSpec((None, tf, d), lambda s, e, j, xbk, obk, r, fl, cg, cu: (e[s], j[s], 0)),
                  pl.BlockSpec((None, 1, tf), lambda s, e, j, xbk, obk, r, fl, cg, cu: (e[s], 0, j[s])),
                  pl.BlockSpec((None, 1, tf), lambda s, e, j, xbk, obk, r, fl, cg, cu: (e[s], 0, nj + j[s])),
                  pl.BlockSpec((None, 1, d), lambda s, e, j, xbk, obk, r, fl, cg, cu: (e[s], 0, 0))],
        out_specs=pl.BlockSpec((tm, d), lambda s, e, j, xbk, obk, r, fl, cg, cu: (obk[s], 0)),
        scratch_shapes=[pltpu.VMEM((d, tf), BF16), pltpu.VMEM((d, tf), BF16), pltpu.VMEM((tf, d), BF16),
                        pltpu.VMEM((subs, tm, d), F32)],
    )
    return pl.pallas_call(
        functools.partial(_moe_kernel, nj),
        grid_spec=grid_spec,
        out_shape=jax.ShapeDtypeStruct((nslot, d), F32),
        compiler_params=_params("arbitrary"),
        name="moe",
    )(item_e, item_j, item_xb, item_ob, item_r, item_flag, wg_code, wu_code,
      xb, w_gu, w_gu, w_dn, b_gu.reshape(n_exp, 1, f2), b_gu.reshape(n_exp, 1, f2), b_dn.reshape(n_exp, 1, d))


def _combine_kernel(alpha, rt, dest_ref, destn_ref, yb_hbm, tg_ref, x1_ref, gtp_ref, gts_ref, lg_ref, lb_ref,
                    op_ref, os_ref, buf, sem):
    i = pl.program_id(0)
    last = pl.num_programs(0) - 1
    tq = x1_ref.shape[0]
    slot = i % 2

    def start(ids_ref, sl, t):
        for kk in range(TOP_K):
            pltpu.make_async_copy(yb_hbm.at[pl.ds(ids_ref[0, t * TOP_K + kk], 1)],
                                  buf.at[sl, kk, pl.ds(t, 1)], sem.at[sl]).start()

    def wait(sl):
        for kk in range(TOP_K):
            pltpu.make_async_copy(yb_hbm.at[pl.ds(0, tq)], buf.at[sl, kk], sem.at[sl]).wait()

    @pl.when(i == 0)
    def _():
        lax.fori_loop(0, tq, lambda t, c: (start(dest_ref, 0, t), c)[1], 0, unroll=4)

    for t in range(tq):
        start(destn_ref, 1 - slot, t)
    wait(slot)
    tg = tg_ref[...]
    y = jnp.zeros(x1_ref.shape, F32)
    for kk in range(TOP_K):
        y = y + buf[slot, kk] * tg[:, kk:kk + 1]
    is_s = i >= rt.n_p
    out = _layer_norm(alpha * x1_ref[...] + rt.pick_vec(i, gtp_ref, gts_ref) * y, lg_ref[...], lb_ref[...])

    @pl.when(jnp.logical_not(is_s))
    def _():
        op_ref[...] = out

    @pl.when(is_s)
    def _():
        os_ref[...] = out

    @pl.when(i == last)
    def _():
        wait(1 - slot)


def _combine(alpha, rt, dest, yb, tg, x1, mod_seq, mod_row, lg, lb):
    d = x1.shape[1]
    tq = rt.tm
    ids = dest.reshape(rt.n, 1, tq * TOP_K)
    return pl.pallas_call(
        functools.partial(_combine_kernel, alpha, rt),
        grid=(rt.n,),
        in_specs=[pl.BlockSpec((None, 1, tq * TOP_K), lambda i: (i, 0, 0), memory_space=pltpu.SMEM),
                  pl.BlockSpec((None, 1, tq * TOP_K), lambda i: (jnp.minimum(i + 1, rt.n - 1), 0, 0),
                               memory_space=pltpu.SMEM),
                  pl.BlockSpec(memory_space=pl.ANY),
                  rt.joint(LANES), rt.joint(d), rt.seq_vec(d, MOD_GATE2), rt.row_vec(d, MOD_GATE2),
                  pl.BlockSpec((1, d), lambda i: (0, 0)),
                  pl.BlockSpec((1, d), lambda i: (0, 0))],
        out_specs=[rt.prompt(d), rt.sample(d, single=False)],
        out_shape=[jax.ShapeDtypeStruct((rt.n_p * tq, d), F32), jax.ShapeDtypeStruct((rt.n_s * tq, d), F32)],
        scratch_shapes=[pltpu.VMEM((2, TOP_K, tq, d), F32), pltpu.SemaphoreType.DMA((2,))],
        compiler_params=_params("arbitrary"),
        name="combine",
    )(ids, ids, yb, tg, x1, mod_seq, mod_row, lg, lb)


def _take(table, idx):
    hit = idx[:, None] == jnp.arange(table.shape[0], dtype=jnp.int32)[None, :]
    return jnp.sum(jnp.where(hit, table[None, :], 0), axis=1)


def _routing(ti, n_experts, tm, nj, subs):
    t = ti.shape[0]
    n_assign = t * TOP_K
    nb_max = n_assign // tm + n_experts
    onehot = (ti[:, :, None] == jnp.arange(n_experts, dtype=jnp.int32)[None, None, :]).astype(jnp.int32)
    per_tok = jnp.sum(onehot, axis=1)
    cum = jnp.cumsum(per_tok, axis=0)
    counts = cum[-1]
    rank = jnp.sum(onehot * cum[:, None, :], axis=2) - 1
    nblk = (counts + tm - 1) // tm
    blk_end = jnp.cumsum(nblk)
    blk_start = blk_end - nblk
    dest = jnp.sum(onehot * blk_start[None, None, :], axis=2) * tm + rank
    pad_lo = (blk_start * tm + counts).astype(jnp.int32)
    pad_len = (nblk * tm - counts).astype(jnp.int32)
    total_blk = blk_end[-1]
    blocks = jnp.arange(nb_max, dtype=jnp.int32)
    blk_e = jnp.minimum(jnp.sum((blk_end[None, :] <= blocks[:, None]).astype(jnp.int32), axis=1), n_experts - 1)
    start_b, nblk_b = _take(blk_start, blk_e), _take(nblk, blk_e)
    r_in_e = blocks - start_b
    g0 = start_b + (r_in_e // subs) * subs
    nsub = jnp.minimum(subs, nblk_b - (r_in_e // subs) * subs)
    p = jnp.arange(nb_max * nj, dtype=jnp.int32)
    bp = p // nj
    valid = bp < total_blk
    last = jnp.maximum(total_blk - 1, 0)
    bq = jnp.where(valid, bp, last)
    gq, nq, eq = _take(g0, bq), jnp.maximum(_take(nsub, bq), 1), _take(blk_e, bq)
    local = p - nj * gq
    jq = jnp.where(valid, local // nq, nj - 1)
    rq = jnp.where(valid, local % nq, nq - 1)
    item_xb = gq + rq
    item_ob = jnp.where(valid, jnp.where(jq == nj - 1, gq + rq, gq), bp)
    flags = valid.astype(jnp.int32) + 2 * (valid & (rq == 0)).astype(jnp.int32)
    nxt_blk = gq + nq
    has_next = nxt_blk < total_blk
    last_j = jq == nj - 1
    e_n = jnp.where(last_j & has_next, _take(blk_e, jnp.minimum(nxt_blk, nb_max - 1)), eq)
    j_n = jnp.where(last_j, jnp.where(has_next, 0, jq), jq + 1)
    cur, nxt = eq * nj + jq, e_n * nj + j_n
    wg_code = jnp.where(valid & (rq >= 1), nxt, cur)
    wu_code = jnp.where(valid & (rq >= 2), nxt, cur)
    items = (eq.astype(jnp.int32), jq.astype(jnp.int32), item_xb.astype(jnp.int32),
             item_ob.astype(jnp.int32), rq.astype(jnp.int32), flags,
             wg_code.astype(jnp.int32), wu_code.astype(jnp.int32))
    return dest.astype(jnp.int32), pad_lo, pad_len, total_blk.astype(jnp.int32), nb_max, items


WPREP_TR = 256


def _win_prep_kernel(c_ab, w_hbm, o_ref, buf, sem):
    c = pl.program_id(0)
    tr = o_ref.shape[0]
    o_a = QKV_W + V_W
    c_shift = o_a // tr

    def copy(cc):
        row = jnp.where(cc < c_shift, cc * tr, jnp.where(cc < c_ab, cc * tr + 2 * DN_HEADS, o_a))
        src = w_hbm.at[pl.ds(pl.multiple_of(row, 2 * DN_HEADS), tr)]
        return pltpu.make_async_copy(src, buf.at[cc % 2], sem.at[cc % 2])

    @pl.when(c == 0)
    def _():
        copy(c).start()

    @pl.when(c + 1 <= c_ab)
    def _():
        copy(c + 1).start()

    @pl.when(c <= c_ab)
    def _():
        copy(c).wait()

    x = buf[c % 2]
    rows = lax.broadcasted_iota(jnp.int32, x.shape, 0)
    keep = jnp.logical_or(c < c_ab, jnp.logical_and(c == c_ab, rows < 2 * DN_HEADS))
    o_ref[...] = jnp.where(keep, x, 0.0).astype(BF16)


def _rearranged_w_in(w_in, d):
    dm, nw = w_in.shape
    tr = WPREP_TR
    o_a = QKV_W + V_W
    rest = nw - o_a - 2 * DN_HEADS
    assert o_a % tr == 0 and rest % tr == 0
    nw_out = -(-(nw - 2 * DN_HEADS + LANES) // PROJ_TN) * PROJ_TN
    assert nw_out % tr == 0
    w = pl.pallas_call(
        functools.partial(_win_prep_kernel, (o_a + rest) // tr),
        grid=(nw_out // tr,),
        in_specs=[pl.BlockSpec(memory_space=pl.ANY)],
        out_specs=pl.BlockSpec((tr, dm), lambda c: (c, 0)),
        out_shape=jax.ShapeDtypeStruct((nw_out, dm), BF16),
        scratch_shapes=[pltpu.VMEM((2, tr, dm), F32), pltpu.SemaphoreType.DMA((2,))],
        compiler_params=_params("arbitrary"),
        name="w_in_prep",
    )(w_in.T)
    cols = dict(z=QKV_W, u=QKV_W + V_W, v=QKV_W + V_W + SG_W, ga=QKV_W + V_W + 2 * SG_W,
                gb=QKV_W + V_W + 2 * SG_W + d, ab=QKV_W + V_W + 2 * SG_W + 2 * d)
    return w, cols


def kernel(x_prompt, x_sample, state_conv_qkv, state_delta, c_prompt, c_sample, w_ada, b_ada, w_in, w_conv, a_log, dt_bias, o_norm_g, sg_ln_g, sg_ln_b, w_s, b_s, p_a, p_b, w_out, ln1_g, ln1_b, router_w, router_b, w_gu, b_gu, w_dn, b_dn, ln2_g, ln2_b):
    depth = w_ada.shape[0]
    alpha = float((2 * depth) ** 0.25)
    bp, seq, d = x_prompt.shape
    bs, lt, _ = x_sample.shape
    assert lt == DN_CONV and seq % SG_CHUNK == 0
    n_experts = router_w.shape[2]
    rows_p, rows_s = bp * seq, bs * lt
    xp = x_prompt.reshape(rows_p, d)
    xs = x_sample.reshape(rows_s, d)
    c_all = jnp.concatenate([jnp.repeat(c_sample, lt, axis=0), c_prompt,
                             jnp.zeros((-(rows_s + bp) % SUBLANES, d), F32)], axis=0)
    tile = lambda cap: _Rows(math.gcd(math.gcd(cap, rows_s), seq), rows_p, rows_s, seq)
    rt_proj, rt_sgu, rt_merge, rt_comb = tile(PROJ_TM), tile(SG_CHUNK), tile(MERGE_TM), tile(COMB_TQ)
    outs = dict(conv_p=[], delta_p=[], conv_s=[], delta_s=[], vrows=[])

    for l in range(depth):
        mod = _adaln(c_all, w_ada[l], b_ada[l])

        w_r, col = _rearranged_w_in(w_in[l], d)
        proj = _inproj(rt_proj, xp, xs, mod, mod, w_r)

        alog = jnp.pad(a_log[l:l + 1], ((0, 0), (0, LANES - DN_HEADS)))
        dtb = jnp.pad(dt_bias[l:l + 1], ((0, 0), (0, LANES - DN_HEADS)))
        og = o_norm_g[l:l + 1]
        conv0 = jnp.zeros((bp, DN_CONV - 1, QKV_W), F32)
        s0 = jnp.zeros((bp, DN_HEADS, DN_DK, DN_DV), F32)
        oa_p, conv_p, delta_p = _gdn_prompt(proj, conv0, s0, w_conv[l], alog, dtb, og, bp, seq,
                                            col['z'], col['ab'])
        oa_s, conv_s, delta_s = _gdn_sample(proj, rows_p, state_conv_qkv[l], state_delta[l], w_conv[l],
                                            alog, dtb, og, bs, col['z'], col['ab'])

        ts = rt_sgu.tm
        assert ts == SG_CHUNK
        ws_p = jnp.tril(w_s[l][:, :ts, :ts]).astype(BF16)
        bias_p = jnp.repeat(b_s[l].T[:ts], SG_CH, axis=1)
        eye = jnp.eye(ts // lt, dtype=F32)
        ws_s = jnp.stack([jnp.kron(eye, jnp.tril(w_s[l, g, :lt, :lt])) for g in range(SG_GROUPS)]).astype(BF16)
        bias_s = jnp.tile(jnp.repeat(b_s[l, :, :lt].T, SG_CH, axis=1), (ts // lt, 1))
        ob, vn_s = _sgu(rt_sgu, proj, sg_ln_g[l:l + 1], sg_ln_b[l:l + 1],
                        ws_p, ws_s, bias_p, bias_s, col['u'], col['v'])

        pa, pb, wo = p_a[l].astype(BF16), p_b[l].astype(BF16), w_out[l].astype(BF16)
        rws = _split(jnp.pad(router_w[l], ((0, 0), (0, LANES - n_experts))))
        rw = jnp.concatenate([rws[0], rws[0], rws[1]], axis=0)
        rb = jnp.pad(router_b[l:l + 1], ((0, 0), (0, LANES - n_experts)))
        x1, h2, ti, tg = _merge(alpha, n_experts, rt_merge, oa_p, oa_s, ob, proj, xp, xs,
                                mod, mod,
                                pa, pb, wo, ln1_g[l:l + 1], ln1_b[l:l + 1], rw, rb,
                                col['ga'], col['gb'])

        nj = w_dn.shape[2] // MOE_TF
        dest, pad_lo, pad_len, nblk_used, nb_max, items = _routing(ti[:, :TOP_K], n_experts, MOE_TM, nj, MOE_SUBS)
        xb = _dispatch(pad_lo, pad_len, nblk_used, dest, h2, MOE_TM, nb_max)
        yb = _moe(items, xb, w_gu[l], b_gu[l], w_dn[l], b_dn[l], MOE_TM, MOE_TF, MOE_SUBS)
        xp, xs = _combine(alpha, rt_comb, dest, yb, tg, x1, mod, mod,
                          ln2_g[l:l + 1], ln2_b[l:l + 1])

        outs['conv_p'].append(conv_p)
        outs['delta_p'].append(delta_p)
        outs['conv_s'].append(conv_s)
        outs['delta_s'].append(delta_s)
        outs['vrows'].append(vn_s.reshape(bs, lt, SG_W))

    return (xp.reshape(bp, seq, d), xs.reshape(bs, lt, d),
            jnp.stack(outs['conv_p']), jnp.stack(outs['delta_p']),
            jnp.stack(outs['conv_s']), jnp.stack(outs['delta_s']), jnp.stack(outs['vrows']))
```

```python
import functools
import math

import jax
import jax.numpy as jnp
from jax import lax
from jax.experimental import pallas as pl
from jax.experimental.pallas import tpu as pltpu

F32 = jnp.float32
BF16 = jnp.bfloat16

DN_HEADS = 8
DN_DK = 128
DN_DV = 128
DN_CONV = 4
DN_CHUNK = 64
SG_GROUPS = 8
SG_CH = 128
SG_CHUNK = 128
TOP_K = 4
SWIGLU_LIMIT = 7.0
SWIGLU_ALPHA = 1.702
LN_EPS = 1e-5
NORM_EPS = 1e-6
QK_W = DN_HEADS * DN_DK
V_W = DN_HEADS * DN_DV
QKV_W = 2 * QK_W + V_W
SG_W = SG_GROUPS * SG_CH

LANES = 128
SUBLANES = 8
VMEM_LIMIT = 56 * 1024 * 1024

PROJ_TM = 512
PROJ_TN = 1536
MERGE_TM = 256
MOE_TM = 256
MOE_TF = 512
MOE_SUBS = 6
COMB_TQ = 256
DISPATCH_TQ = 256
SGU_TM = 256
ADALN_TN = 1024
INV_BLOCK = 16
SAMPLE_BT = 4
SAMPLE_TP = 8
GDN_HG = 4

MOD_SHIFT1, MOD_SCALE1, MOD_GATE1, MOD_SHIFT2, MOD_SCALE2, MOD_GATE2 = range(6)
MOD_PARTS = 6


def _sigmoid(x):
    return 0.5 * (jnp.tanh(0.5 * x) + 1.0)


def _silu(x):
    return x * _sigmoid(x)


def _dot(a, b):
    return jnp.dot(a.astype(BF16), b.astype(BF16), preferred_element_type=F32)


def _dot_nt(a, b):
    return lax.dot_general(a.astype(BF16), b.astype(BF16), (((1,), (1,)), ((), ())),
                           preferred_element_type=F32)


def _split3(x):
    hi = x.astype(BF16)
    r1 = x - hi.astype(F32)
    mid = r1.astype(BF16)
    return hi, mid, (r1 - mid.astype(F32)).astype(BF16)


def _mask_sums(masks, col, n):
    lane = lax.broadcasted_iota(jnp.int32, (n, LANES), 1)
    hi, mid, lo = [p.astype(F32) for p in _split3(col)]
    parts = jnp.where(lane == 0, hi, jnp.where(lane == 1, mid, jnp.where(lane == 2, lo, 0.0)))
    y = jnp.dot(masks, parts.astype(BF16), preferred_element_type=F32)
    y = y[:, 0:1] + y[:, 1:2] + y[:, 2:3]
    return [y[i * n:(i + 1) * n] for i in range(masks.shape[0] // n)]


def _layer_norm(x, g, b):
    mu = jnp.mean(x, axis=-1, keepdims=True)
    xc = x - mu
    var = jnp.mean(xc * xc, axis=-1, keepdims=True)
    return xc * lax.rsqrt(var + LN_EPS) * g + b


def _params(*sem):
    return pltpu.CompilerParams(dimension_semantics=sem, vmem_limit_bytes=VMEM_LIMIT)


class _Rows:
    def __init__(self, tm, rows_p, rows_s, seq):
        assert rows_p % tm == 0 and rows_s % tm == 0 and seq % tm == 0, (tm, rows_p, rows_s, seq)
        self.tm, self.n_p, self.n_s = tm, rows_p // tm, rows_s // tm
        self.tiles_per_seq = seq // tm
        self.bp = rows_p // seq
        assert rows_s % SUBLANES == 0
        self.rows_s = rows_s

    @property
    def n(self):
        return self.n_p + self.n_s

    def prompt(self, width, col=0):
        return pl.BlockSpec((self.tm, width), lambda i, *_: (jnp.minimum(i, self.n_p - 1), col))

    def sample(self, width, col=0, single=True):
        mode = dict(pipeline_mode=pl.Buffered(1)) if single else {}
        return pl.BlockSpec((self.tm, width), lambda i, *_: (jnp.maximum(i - self.n_p, 0), col), **mode)

    def joint(self, width, col=0):
        return pl.BlockSpec((self.tm, width), lambda i, *_: (i, col))

    def _seq(self, i):
        return jnp.minimum(i // self.tiles_per_seq, self.bp - 1)

    def seq_vec(self, d, part):
        return pl.BlockSpec((SUBLANES, d), lambda i, *_: ((self.rows_s + self._seq(i)) // SUBLANES, part))

    def row_vec(self, d, part):
        return pl.BlockSpec((self.tm, d), lambda i, *_: (jnp.maximum(i - self.n_p, 0), part),
                            pipeline_mode=pl.Buffered(1))

    def pick_vec(self, i, seq_ref, row_ref):
        prompt = seq_ref[pl.ds(self._seq(i) % SUBLANES, 1), :]
        return jnp.where(i >= self.n_p, row_ref[...], prompt)


def _pick(is_sample, prompt_ref, sample_ref):
    return jnp.where(is_sample, sample_ref[...], prompt_ref[...])


def _adaln_kernel(c_ref, w_ref, b_ref, o_ref):
    o_ref[...] = _dot(_silu(c_ref[...]), w_ref[...]) + b_ref[...]


def _adaln(c, w, b):
    rows, d = c.shape
    n = w.shape[1]
    tn = ADALN_TN
    return pl.pallas_call(
        _adaln_kernel,
        grid=(n // tn,),
        in_specs=[pl.BlockSpec((rows, d), lambda j: (0, 0)),
                  pl.BlockSpec((d, tn), lambda j: (0, j)),
                  pl.BlockSpec((1, tn), lambda j: (0, j))],
        out_specs=pl.BlockSpec((rows, tn), lambda j: (0, j)),
        out_shape=jax.ShapeDtypeStruct((rows, n), F32),
        compiler_params=_params("arbitrary"),
        name="adaln",
    )(c, w, b.reshape(1, n))


def _inproj_kernel(rt, xp_ref, xs_ref, scp_ref, scs_ref, shp_ref, shs_ref, w_ref, o_ref, h_scr):
    @pl.when(pl.program_id(1) == 0)
    def _():
        i = pl.program_id(0)
        x = _pick(i >= rt.n_p, xp_ref, xs_ref)
        h_scr[...] = (x * (1.0 + rt.pick_vec(i, scp_ref, scs_ref)) + rt.pick_vec(i, shp_ref, shs_ref)).astype(BF16)

    o_ref[...] = lax.dot_general(h_scr[...], w_ref[...], (((1,), (1,)), ((), ())), preferred_element_type=F32)


def _inproj(rt, xp, xs, mod_seq, mod_row, w):
    d = xp.shape[1]
    nw = w.shape[0]
    tm = rt.tm
    return pl.pallas_call(
        functools.partial(_inproj_kernel, rt),
        grid=(rt.n, nw // PROJ_TN),
        in_specs=[rt.prompt(d), rt.sample(d), rt.seq_vec(d, MOD_SCALE1), rt.row_vec(d, MOD_SCALE1),
                  rt.seq_vec(d, MOD_SHIFT1), rt.row_vec(d, MOD_SHIFT1),
                  pl.BlockSpec((PROJ_TN, d), lambda i, j: (j, 0))],
        out_specs=pl.BlockSpec((tm, PROJ_TN), lambda i, j: (i, j)),
        out_shape=jax.ShapeDtypeStruct((rt.n * tm, nw), F32),
        scratch_shapes=[pltpu.VMEM((tm, d), BF16)],
        compiler_params=_params("arbitrary", "arbitrary"),
        name="inproj",
    )(xp, xs, mod_seq, mod_row, mod_seq, mod_row, w)


def _softplus(x):
    return jnp.maximum(x, 0.0) + jnp.log1p(jnp.exp(-jnp.abs(x)))


def _split(a):
    hi = a.astype(BF16)
    lo = (a - hi.astype(F32)).astype(BF16)
    return hi, lo


def _map(f, *lists):
    return [f(*args) for args in zip(*lists)]


def _unit_lower_inverses_minus_eye(mats, rows, cols):
    same = (rows // INV_BLOCK) == (cols // INV_BLOCK)
    n = [jnp.where(same, a, 0.0) for a in mats]
    b = _map(lambda a, x: a - x, mats, n)
    n2 = _map(lambda x: _dot(x, x), n)
    n4 = _map(lambda x: _dot(x, x), n2)
    n8 = _map(lambda x: _dot(x, x), n4)
    r = [-x for x in n]
    r = _map(lambda x, p: x + p + _dot(x, p), r, n2)
    r = _map(lambda x, p: x + p + _dot(x, p), r, n4)
    dm = _map(lambda x, p: x + p + _dot(x, p), r, n8)
    m = _map(lambda x, y: y + _dot(x, y), dm, b)
    m2 = _map(lambda x: _dot(x, x), m)
    xm = _map(lambda x, p: x + p + _dot(p, x), dm, m2)
    return _map(lambda x, p: x - p - _dot(p, x), xm, m)


def _gdn_prompt_kernel(qkv_ref, z_ref, ab_ref, hist_ref, s0_ref, wconv_ref, alog_ref, dt_ref, og_ref,
                       o_ref, conv_ref, s_ref, xp_scr):
    n = pl.program_id(1)
    c = DN_CHUNK
    pad = SUBLANES
    nr = GDN_HG * c
    ngrp = DN_HEADS // GDN_HG

    @pl.when(n == 0)
    def _():
        xp_scr[pl.ds(0, pad), :] = jnp.zeros((pad, QKV_W), F32)
        xp_scr[pl.ds(pad - (DN_CONV - 1), DN_CONV - 1), :] = hist_ref[...]
        s_ref[...] = s0_ref[...]

    x = qkv_ref[...]
    xp_scr[pl.ds(pad, c), :] = x
    y = jnp.zeros((c, QKV_W), F32)
    for j in range(DN_CONV):
        y = y + xp_scr[pl.ds(pad - (DN_CONV - 1) + j, c), :] * wconv_ref[pl.ds(j, 1), :]
    y = _silu(y)
    tail = xp_scr[pl.ds(c + pad - (DN_CONV - 1), DN_CONV - 1), :]
    conv_ref[...] = tail
    xp_scr[pl.ds(pad - (DN_CONV - 1), DN_CONV - 1), :] = tail

    ab = ab_ref[...]
    g = -jnp.exp(alog_ref[...]) * _softplus(ab + dt_ref[...])
    beta_all = _sigmoid(ab)
    r64 = lax.broadcasted_iota(jnp.int32, (c, c), 0)
    c64 = lax.broadcasted_iota(jnp.int32, (c, c), 1)
    tri = (r64 >= c64).astype(BF16)
    gc = sum(jnp.dot(tri, part, preferred_element_type=F32) for part in _split3(g))
    z = z_ref[...]
    og = og_ref[...]

    rows = lax.broadcasted_iota(jnp.int32, (nr, nr), 0)
    cols = lax.broadcasted_iota(jnp.int32, (nr, nr), 1)
    same = (rows // c) == (cols // c)
    causal = same & (rows >= cols)
    strict = same & (rows > cols)
    rowhead = lax.broadcasted_iota(jnp.int32, (nr, 1), 0) // c
    groups = [range(grp * GDN_HG, (grp + 1) * GDN_HG) for grp in range(ngrp)]

    def stack(heads, off, width):
        return jnp.concatenate([y[:, off + h * width:off + (h + 1) * width] for h in heads], axis=0)

    def l2n(t):
        return t * lax.rsqrt(jnp.sum(t * t, axis=-1, keepdims=True) + NORM_EPS)

    q = [l2n(stack(hs, 0, DN_DK)) * (DN_DK ** -0.5) for hs in groups]
    k = [l2n(stack(hs, QK_W, DN_DK)) for hs in groups]
    v = [stack(hs, 2 * QK_W, DN_DV) for hs in groups]
    beta = [jnp.concatenate([beta_all[:, DN_HEADS + h:DN_HEADS + h + 1] for h in hs], axis=0) for hs in groups]
    gcf = [jnp.concatenate([jnp.broadcast_to(gc[:, h:h + 1], (c, LANES)) for h in hs], axis=0) for hs in groups]
    gcc = [t[:, 0:1] for t in gcf]
    grow = [t.T[0:1, :] for t in gcf]
    glast = [[gc[c - 1:c, h:h + 1] for h in hs] for hs in groups]
    gtot = [jnp.concatenate([jnp.broadcast_to(t, (c, 1)) for t in gl], axis=0) for gl in glast]
    decay = _map(lambda a, b: jnp.where(causal, jnp.exp(jnp.where(causal, a - b, 0.0)), 0.0), gcc, grow)
    egc = _map(jnp.exp, gcc)
    kb = _map(lambda a, b: a * b, k, beta)
    a_mat = _map(lambda a, b, dd: jnp.where(strict, _dot_nt(a, b) * dd, 0.0), kb, k, decay)
    tm1 = _unit_lower_inverses_minus_eye(a_mat, rows, cols)
    u = _map(lambda t, a, b: a * b + _dot(t, a * b), tm1, v, beta)
    w = _map(lambda t, a, b: a * b + _dot(t, a * b), tm1, kb, egc)
    attn = _map(lambda a, b, dd: _dot_nt(a, b) * dd, q, k, decay)
    qg = _map(lambda a, b: a * b, q, egc)
    kd_t = _map(lambda a, b, cc: (a * jnp.exp(b - cc)).T, k, gtot, gcc)

    for gi, hs in enumerate(groups):
        v_news, qss = [], []
        for hl, h in enumerate(hs):
            sl = slice(hl * c, (hl + 1) * c)
            rs = _dot(jnp.concatenate([w[gi][sl], qg[gi][sl]], axis=0), s_ref[h])
            v_news.append(u[gi][sl] - rs[:c])
            qss.append(rs[c:])
        v_new = jnp.concatenate(v_news, axis=0)
        o = jnp.concatenate(qss, axis=0) + _dot(attn[gi], v_new)
        for hl, h in enumerate(hs):
            upd = _dot(kd_t[gi], jnp.where(rowhead == hl, v_new, 0.0))
            s_ref[h] = s_ref[h] * jnp.exp(glast[gi][hl]) + upd
        o = o * lax.rsqrt(jnp.mean(o * o, axis=-1, keepdims=True) + NORM_EPS) * og
        for hl, h in enumerate(hs):
            o_ref[:, h * DN_DV:(h + 1) * DN_DV] = o[hl * c:(hl + 1) * c] * _silu(z[:, h * DN_DV:(h + 1) * DN_DV])


def _gdn_prompt(proj, hist, s0, wconv, alog, dtb, og, batch, seq, col_z, col_ab):
    c = DN_CHUNK
    nchunk = seq // c
    rows = batch * seq
    row = lambda b, n: b * nchunk + n
    return pl.pallas_call(
        _gdn_prompt_kernel,
        grid=(batch, nchunk),
        in_specs=[pl.BlockSpec((c, QKV_W), lambda b, n: (row(b, n), 0)),
                  pl.BlockSpec((c, V_W), lambda b, n: (row(b, n), col_z // V_W)),
                  pl.BlockSpec((c, LANES), lambda b, n: (row(b, n), col_ab // LANES)),
                  pl.BlockSpec((None, DN_CONV - 1, QKV_W), lambda b, n: (b, 0, 0)),
                  pl.BlockSpec((None, DN_HEADS, DN_DK, DN_DV), lambda b, n: (b, 0, 0, 0)),
                  pl.BlockSpec((DN_CONV, QKV_W), lambda b, n: (0, 0)),
                  pl.BlockSpec((1, LANES), lambda b, n: (0, 0)),
                  pl.BlockSpec((1, LANES), lambda b, n: (0, 0)),
                  pl.BlockSpec((1, DN_DV), lambda b, n: (0, 0))],
        out_specs=[pl.BlockSpec((c, V_W), lambda b, n: (row(b, n), 0)),
                   pl.BlockSpec((None, DN_CONV - 1, QKV_W), lambda b, n: (b, 0, 0)),
                   pl.BlockSpec((None, DN_HEADS, DN_DK, DN_DV), lambda b, n: (b, 0, 0, 0))],
        out_shape=[jax.ShapeDtypeStruct((rows, V_W), F32),
                   jax.ShapeDtypeStruct((batch, DN_CONV - 1, QKV_W), F32),
                   jax.ShapeDtypeStruct((batch, DN_HEADS, DN_DK, DN_DV), F32)],
        scratch_shapes=[pltpu.VMEM((c + SUBLANES, QKV_W), F32)],
        compiler_params=_params("arbitrary", "arbitrary"),
        name="gdn_prompt",
    )(proj, proj, proj, hist, s0, wconv, alog, dtb, og)


def _gdn_sample_kernel(qkv_ref, z_ref, ab_ref, hist_ref, s0_ref, wconv_ref, alog_ref, dt_ref, og_ref,
                       o_ref, conv_ref, s_ref, xp_scr, ab_scr, wq_scr, r_scr, kdt_scr, vn_scr, gl_scr):
    bt, tp, lt = SAMPLE_BT, SAMPLE_TP, DN_CONV
    nblk = DN_HEADS * bt
    nrow = nblk * tp
    hist_rows = DN_CONV - 1

    xp_scr[...] = jnp.zeros(xp_scr.shape, F32)
    ab_scr[...] = jnp.zeros(ab_scr.shape, F32)
    ys = []
    for b in range(bt):
        xp_scr[b, pl.ds(SUBLANES - hist_rows, hist_rows), :] = hist_ref[b]
        xp_scr[b, pl.ds(SUBLANES, lt), :] = qkv_ref[pl.ds(b * lt, lt), :]
        yb = jnp.zeros((tp, QKV_W), F32)
        for j in range(DN_CONV):
            yb = yb + xp_scr[b, pl.ds(SUBLANES - hist_rows + j, tp), :] * wconv_ref[pl.ds(j, 1), :]
        ys.append(_silu(yb))
        conv_ref[b] = xp_scr[b, pl.ds(SUBLANES + lt - hist_rows, hist_rows), :]
        ab_scr[b, pl.ds(0, lt), :] = ab_ref[pl.ds(b * lt, lt), :]
    y = jnp.concatenate(ys, axis=0)
    ab = jnp.concatenate([ab_scr[b] for b in range(bt)], axis=0)
    tok = lax.broadcasted_iota(jnp.int32, (bt * tp, 1), 0) % tp
    real = tok < lt
    g_all = jnp.where(real, -jnp.exp(alog_ref[...]) * _softplus(ab + dt_ref[...]), 0.0)
    beta_all = jnp.where(real, _sigmoid(ab), 0.0)

    def heads_to_rows(t, off, width):
        return jnp.concatenate([t[:, off + h * width:off + (h + 1) * width] for h in range(DN_HEADS)], axis=0)

    realr = jnp.concatenate([real] * DN_HEADS, axis=0)
    q = jnp.where(realr, heads_to_rows(y, 0, DN_DK), 0.0)
    k = jnp.where(realr, heads_to_rows(y, QK_W, DN_DK), 0.0)
    v = jnp.where(realr, heads_to_rows(y, 2 * QK_W, DN_DV), 0.0)
    gcol = jnp.concatenate([g_all[:, h:h + 1] for h in range(DN_HEADS)], axis=0)
    beta = jnp.concatenate([beta_all[:, DN_HEADS + h:DN_HEADS + h + 1] for h in range(DN_HEADS)], axis=0)
    q = q * lax.rsqrt(jnp.sum(q * q, axis=-1, keepdims=True) + NORM_EPS) * (DN_DK ** -0.5)
    k = k * lax.rsqrt(jnp.sum(k * k, axis=-1, keepdims=True) + NORM_EPS)

    rows = lax.broadcasted_iota(jnp.int32, (nrow, nrow), 0)
    cols = lax.broadcasted_iota(jnp.int32, (nrow, nrow), 1)
    same = (rows // tp) == (cols // tp)
    causal = same & (rows >= cols)
    strict = same & (rows > cols)
    gcc, gtot = _mask_sums(jnp.concatenate([causal.astype(BF16), same.astype(BF16)], axis=0), gcol, nrow)
    gtot = jnp.broadcast_to(gtot, (nrow, LANES))
    grow = jnp.broadcast_to(gcc, (nrow, LANES)).T[0:1, :]
    decay = jnp.where(causal, jnp.exp(jnp.where(causal, gcc - grow, 0.0)), 0.0)
    egc = jnp.exp(gcc)
    kb = k * beta
    a_mat = jnp.where(strict, _dot_nt(kb, k) * decay, 0.0)
    a2 = _dot(a_mat, a_mat)
    tm1 = a2 - a_mat - _dot(a_mat, a2)
    u = v * beta + _dot(tm1, v * beta)
    w = kb * egc + _dot(tm1, kb * egc)
    attn = _dot_nt(q, k) * decay
    qg = q * egc
    for i in range(nblk):
        wq_scr[pl.ds(2 * tp * i, tp), :] = w[i * tp:(i + 1) * tp, :]
        wq_scr[pl.ds(2 * tp * i + tp, tp), :] = qg[i * tp:(i + 1) * tp, :]
    kdt_scr[...] = (k * jnp.exp(gtot[:, 0:1] - gcc)).T
    gl_scr[...] = jnp.exp(gtot)

    def read_state(i, carry):
        r0 = pl.multiple_of(i * 2 * tp, 2 * tp)
        r_scr[pl.ds(r0, 2 * tp), :] = _dot(wq_scr[pl.ds(r0, 2 * tp), :], s0_ref[i % bt, i // bt])
        return carry

    lax.fori_loop(0, nblk, read_state, 0, unroll=4)
    ws = jnp.concatenate([r_scr[pl.ds(2 * tp * i, tp), :] for i in range(nblk)], axis=0)
    qs = jnp.concatenate([r_scr[pl.ds(2 * tp * i + tp, tp), :] for i in range(nblk)], axis=0)
    v_new = u - ws
    vn_scr[...] = v_new
    o = qs + _dot(attn, v_new)
    blockid = lax.broadcasted_iota(jnp.int32, (nrow, 1), 0) // tp

    def write_state(i, carry):
        r0 = pl.multiple_of(i * tp, tp)
        upd = _dot(kdt_scr[...], jnp.where(blockid == i, vn_scr[...], 0.0))
        s_ref[i % bt, i // bt] = s0_ref[i % bt, i // bt] * gl_scr[pl.ds(r0, 1), :] + upd
        return carry

    lax.fori_loop(0, nblk, write_state, 0, unroll=4)

    og = og_ref[...]
    o = o * lax.rsqrt(jnp.mean(o * o, axis=-1, keepdims=True) + NORM_EPS) * og
    for b in range(bt):
        for h in range(DN_HEADS):
            zbh = z_ref[pl.ds(b * lt, lt), h * DN_DV:(h + 1) * DN_DV]
            blk = o[(h * bt + b) * tp:(h * bt + b) * tp + lt, :]
            o_ref[pl.ds(b * lt, lt), h * DN_DV:(h + 1) * DN_DV] = blk * _silu(zbh)


def _gdn_sample(proj, row0, hist, s0, wconv, alog, dtb, og, batch, col_z, col_ab):
    bt, tp, lt = SAMPLE_BT, SAMPLE_TP, DN_CONV
    rows = batch * lt
    nrow = DN_HEADS * bt * tp
    blk = bt * lt
    assert row0 % blk == 0 and batch % bt == 0
    b0 = row0 // blk
    return pl.pallas_call(
        _gdn_sample_kernel,
        grid=(batch // bt,),
        in_specs=[pl.BlockSpec((blk, QKV_W), lambda i: (b0 + i, 0)),
                  pl.BlockSpec((blk, V_W), lambda i: (b0 + i, col_z // V_W)),
                  pl.BlockSpec((blk, LANES), lambda i: (b0 + i, col_ab // LANES)),
                  pl.BlockSpec((bt, DN_CONV - 1, QKV_W), lambda i: (i, 0, 0)),
                  pl.BlockSpec((bt, DN_HEADS, DN_DK, DN_DV), lambda i: (i, 0, 0, 0)),
                  pl.BlockSpec((DN_CONV, QKV_W), lambda i: (0, 0)),
                  pl.BlockSpec((1, LANES), lambda i: (0, 0)),
                  pl.BlockSpec((1, LANES), lambda i: (0, 0)),
                  pl.BlockSpec((1, DN_DV), lambda i: (0, 0))],
        out_specs=[pl.BlockSpec((blk, V_W), lambda i: (i, 0)),
                   pl.BlockSpec((bt, DN_CONV - 1, QKV_W), lambda i: (i, 0, 0)),
                   pl.BlockSpec((bt, DN_HEADS, DN_DK, DN_DV), lambda i: (i, 0, 0, 0))],
        out_shape=[jax.ShapeDtypeStruct((rows, V_W), F32),
                   jax.ShapeDtypeStruct((batch, DN_CONV - 1, QKV_W), F32),
                   jax.ShapeDtypeStruct((batch, DN_HEADS, DN_DK, DN_DV), F32)],
        scratch_shapes=[pltpu.VMEM((bt, SUBLANES + tp, QKV_W), F32),
                        pltpu.VMEM((bt, tp, LANES), F32),
                        pltpu.VMEM((2 * nrow, DN_DK), F32),
                        pltpu.VMEM((2 * nrow, DN_DV), F32),
                        pltpu.VMEM((DN_DK, nrow), F32),
                        pltpu.VMEM((nrow, DN_DV), F32),
                        pltpu.VMEM((nrow, LANES), F32)],
        compiler_params=_params("arbitrary"),
        name="gdn_sample",
    )(proj, proj, proj, hist, s0, wconv, alog, dtb, og)


def _sgu_kernel(n_p, u_ref, v_ref, g_ref, b_ref, wsp_ref, wss_ref, bp_ref, bs_ref, o_ref, vn_ref):
    is_s = pl.program_id(0) >= n_p
    u = jax.nn.gelu(u_ref[...])
    vn = _layer_norm(jax.nn.gelu(v_ref[...]), g_ref[...], b_ref[...])
    vn_ref[...] = vn
    bias = _pick(is_s, bp_ref, bs_ref)
    for g in range(SG_GROUPS):
        sl = slice(g * SG_CH, (g + 1) * SG_CH)
        ws = jnp.where(is_s, wss_ref[g], wsp_ref[g])
        for c in range(u.shape[0] // SG_CHUNK):
            rows = slice(c * SG_CHUNK, (c + 1) * SG_CHUNK)
            mixed = _dot(ws, vn[rows, sl]) + bias[:, sl]
            o_ref[rows, sl] = u[rows, sl] * mixed


def _sgu(rt, proj, ln_g, ln_b, ws_p, ws_s, bias_p, bias_s, col_u, col_v):
    t = rt.tm
    const = lambda shape: pl.BlockSpec(shape, lambda i: (0,) * len(shape))
    return pl.pallas_call(
        functools.partial(_sgu_kernel, rt.n_p),
        grid=(rt.n,),
        in_specs=[rt.joint(SG_W, col_u // SG_W), rt.joint(SG_W, col_v // SG_W),
                  const((1, SG_W)), const((1, SG_W)),
                  const((SG_GROUPS, SG_CHUNK, SG_CHUNK)), const((SG_GROUPS, SG_CHUNK, SG_CHUNK)),
                  const((SG_CHUNK, SG_W)), const((SG_CHUNK, SG_W))],
        out_specs=[rt.joint(SG_W), rt.sample(SG_W, single=False)],
        out_shape=[jax.ShapeDtypeStruct((rt.n * t, SG_W), F32),
                   jax.ShapeDtypeStruct((rt.n_s * t, SG_W), F32)],
        compiler_params=_params("arbitrary"),
        name="sgu",
    )(proj, proj, ln_g, ln_b, ws_p, ws_s, bias_p, bias_s)


def _merge_kernel(alpha, n_experts, rt, oap_ref, oas_ref, ob_ref, ga_ref, gb_ref, xp_ref, xs_ref,
                  gtp_ref, gts_ref, scp_ref, scs_ref, shp_ref, shs_ref,
                  pa_ref, pb_ref, wo_ref, lg_ref, lb_ref, rw_ref, rb_ref,
                  x1_ref, h2_ref, ti_ref, tg_ref):
    i = pl.program_id(0)
    is_s = i >= rt.n_p
    oa = _pick(is_s, oap_ref, oas_ref)
    merged = (_sigmoid(ga_ref[...]) * _dot(oa, pa_ref[...])
              + _sigmoid(gb_ref[...]) * _dot(ob_ref[...], pb_ref[...]))
    y = _dot(merged, wo_ref[...])
    x = _pick(is_s, xp_ref, xs_ref)
    x1 = _layer_norm(alpha * x + rt.pick_vec(i, gtp_ref, gts_ref) * y, lg_ref[...], lb_ref[...])
    x1_ref[...] = x1
    h2 = x1 * (1.0 + rt.pick_vec(i, scp_ref, scs_ref)) + rt.pick_vec(i, shp_ref, shs_ref)
    h2_ref[...] = h2
    hs = _split(h2)
    logits = jnp.dot(jnp.concatenate([hs[0], hs[1], hs[0]], axis=1), rw_ref[...],
                     preferred_element_type=F32) + rb_ref[...]
    lane = lax.broadcasted_iota(jnp.int32, logits.shape, 1)
    logits = jnp.where(lane < n_experts, logits, -jnp.inf)
    ti = jnp.zeros(logits.shape, jnp.int32)
    tv = jnp.zeros(logits.shape, F32)
    top = None
    for kk in range(TOP_K):
        m = jnp.max(logits, axis=-1, keepdims=True)
        idx = jnp.min(jnp.where(logits == m, lane.astype(F32), float(LANES)), axis=-1,
                      keepdims=True).astype(jnp.int32)
        if kk == 0:
            top = m
        ti = jnp.where(lane == kk, idx, ti)
        tv = jnp.where(lane == kk, jnp.exp(m - top), tv)
        logits = jnp.where(lane == idx, -jnp.inf, logits)
    ti_ref[...] = ti
    tg_ref[...] = tv / jnp.sum(tv, axis=-1, keepdims=True)


def _merge(alpha, n_experts, rt, oa_p, oa_s, ob, proj, xp, xs, mod_seq, mod_row, pa, pb, wo, lg, lb, rw, rb,
           col_ga, col_gb):
    d = xp.shape[1]
    tm = rt.tm
    rows = rt.n * tm
    const = lambda shape: pl.BlockSpec(shape, lambda i: (0,) * len(shape), pipeline_mode=pl.Buffered(1))
    return pl.pallas_call(
        functools.partial(_merge_kernel, alpha, n_experts, rt),
        grid=(rt.n,),
        in_specs=[rt.prompt(V_W), rt.sample(V_W), rt.joint(SG_W),
                  rt.joint(d, col_ga // d), rt.joint(d, col_gb // d),
                  rt.prompt(d), rt.sample(d),
                  rt.seq_vec(d, MOD_GATE1), rt.row_vec(d, MOD_GATE1),
                  rt.seq_vec(d, MOD_SCALE2), rt.row_vec(d, MOD_SCALE2),
                  rt.seq_vec(d, MOD_SHIFT2), rt.row_vec(d, MOD_SHIFT2),
                  const((V_W, d)), const((SG_W, d)), const((d, d)),
                  const((1, d)), const((1, d)), const((3 * d, LANES)), const((1, LANES))],
        out_specs=[rt.joint(d), rt.joint(d), rt.joint(LANES), rt.joint(LANES)],
        out_shape=[jax.ShapeDtypeStruct((rows, d), F32),
                   jax.ShapeDtypeStruct((rows, d), F32),
                   jax.ShapeDtypeStruct((rows, LANES), jnp.int32),
                   jax.ShapeDtypeStruct((rows, LANES), F32)],
        compiler_params=_params("arbitrary"),
        name="merge",
    )(oa_p, oa_s, ob, proj, proj, xp, xs, mod_seq, mod_row, mod_seq, mod_row, mod_seq, mod_row,
      pa, pb, wo, lg, lb, rw, rb)


FILL_ROWS = (128, 64, 32, 16, 8)


def _dispatch_kernel(tm, n_experts, padlo_ref, padlen_ref, nblk_ref, dest_ref, h_ref, xb_hbm, zero_scr, sem, zsem):
    i = pl.program_id(0)
    tq = h_ref.shape[0]
    nb_max = xb_hbm.shape[0] // tm
    zrows = zero_scr.shape[0]

    def fill(act):
        def per_expert(e, carry):
            lo = padlo_ref[e]
            ln = padlen_ref[e]
            head = (SUBLANES - lo % SUBLANES) % SUBLANES
            for r in range(SUBLANES - 1):
                @pl.when(r < head)
                def _(r=r):
                    act(pltpu.make_async_copy(zero_scr.at[pl.ds(0, 1)], xb_hbm.at[pl.ds(lo + r, 1)], zsem))
            off = lo + head
            rem = ln - head
            for b in FILL_ROWS:
                @pl.when((rem & b) != 0)
                def _(off=off, b=b):
                    dst = xb_hbm.at[pl.ds(pl.multiple_of(off, SUBLANES), b)]
                    act(pltpu.make_async_copy(zero_scr.at[pl.ds(0, b)], dst, zsem))
                off = off + (rem & b)
            return carry

        lax.fori_loop(0, n_experts, per_expert, 0)

        def per_block(blk, carry):
            for part in range(tm // zrows):
                row0 = pl.multiple_of(blk * tm + part * zrows, zrows)
                act(pltpu.make_async_copy(zero_scr, xb_hbm.at[pl.ds(row0, zrows)], zsem))
            return carry

        lax.fori_loop(nblk_ref[0], nb_max, per_block, 0)

    @pl.when(i == 0)
    def _():
        zero_scr[...] = jnp.zeros(zero_scr.shape, F32)
        fill(lambda cp: cp.start())
        fill(lambda cp: cp.wait())

    def start(t, carry):
        for kk in range(TOP_K):
            pltpu.make_async_copy(h_ref.at[pl.ds(t, 1)], xb_hbm.at[pl.ds(dest_ref[0, t * TOP_K + kk], 1)],
                                  sem).start()
        return carry

    lax.fori_loop(0, tq, start, 0, unroll=8)
    for kk in range(TOP_K):
        pltpu.make_async_copy(h_ref, xb_hbm.at[pl.ds(0, tq)], sem).wait()


def _dispatch(padlo, padlen, nblk_used, dest, h2, tm, nb_max):
    t, d = h2.shape
    tq = math.gcd(DISPATCH_TQ, t)
    n_experts = padlo.shape[0]
    grid_spec = pltpu.PrefetchScalarGridSpec(
        num_scalar_prefetch=3,
        grid=(t // tq,),
        in_specs=[pl.BlockSpec((None, 1, tq * TOP_K), lambda i, *_: (i, 0, 0), memory_space=pltpu.SMEM),
                  pl.BlockSpec((tq, d), lambda i, *_: (i, 0))],
        out_specs=pl.BlockSpec(memory_space=pl.ANY),
        scratch_shapes=[pltpu.VMEM((FILL_ROWS[0], d), F32), pltpu.SemaphoreType.DMA(()),
                        pltpu.SemaphoreType.DMA(())],
    )
    return pl.pallas_call(
        functools.partial(_dispatch_kernel, tm, n_experts),
        grid_spec=grid_spec,
        out_shape=jax.ShapeDtypeStruct((nb_max * tm, d), F32),
        compiler_params=pltpu.CompilerParams(dimension_semantics=("arbitrary",), vmem_limit_bytes=VMEM_LIMIT,
                                             has_side_effects=True),
        name="dispatch",
    )(padlo, padlen, nblk_used.reshape(1), dest.reshape(t // tq, 1, tq * TOP_K), h2)


def _moe_kernel(nj, e_ref, j_ref, xb_ref, ob_ref, r_ref, flag_ref, wg_code_ref, wu_code_ref,
                x_ref, wg_ref, wu_ref, wd_ref, bg_ref, bu_ref, bd_ref, o_ref,
                wg_scr, wu_scr, wd_scr, acc_scr):
    s = pl.program_id(0)
    flags = flag_ref[s]
    j = j_ref[s]
    r = r_ref[s]

    @pl.when((flags & 2) != 0)
    def _():
        wg_scr[...] = wg_ref[...].astype(BF16)
        wu_scr[...] = wu_ref[...].astype(BF16)
        wd_scr[...] = wd_ref[...].astype(BF16)

    @pl.when(flags == 0)
    def _():
        o_ref[...] = jnp.zeros(o_ref.shape, F32)

    @pl.when((flags & 1) != 0)
    def _():
        x = x_ref[...].astype(BF16)
        gate = jnp.dot(x, wg_scr[...], preferred_element_type=F32) + bg_ref[...]
        up = jnp.dot(x, wu_scr[...], preferred_element_type=F32) + bu_ref[...]
        gate = jnp.minimum(gate, SWIGLU_LIMIT)
        up = jnp.clip(up, -SWIGLU_LIMIT, SWIGLU_LIMIT)
        act = (up + 1.0) * gate * _sigmoid(SWIGLU_ALPHA * gate)
        y = jnp.dot(act.astype(BF16), wd_scr[...], preferred_element_type=F32)

        @pl.when(j == 0)
        def _():
            acc_scr[r] = y

        @pl.when(jnp.logical_and(j > 0, j < nj - 1))
        def _():
            acc_scr[r] = acc_scr[r] + y

        @pl.when(j == nj - 1)
        def _():
            o_ref[...] = acc_scr[r] + y + bd_ref[...]


def _moe(items, xb, w_gu, b_gu, w_dn, b_dn, tm, tf, subs):
    item_e, item_j, item_xb, item_ob, item_r, item_flag, wg_code, wu_code = items
    n_items = item_e.shape[0]
    nslot, d = xb.shape
    n_exp, _, f2 = w_gu.shape
    f = f2 // 2
    nj = f // tf
    assert nj >= 2
    grid_spec = pltpu.PrefetchScalarGridSpec(
        num_scalar_prefetch=8,
        grid=(n_items,),
        in_specs=[pl.BlockSpec((tm, d), lambda s, e, j, xbk, obk, r, fl, cg, cu: (xbk[s], 0)),
                  pl.BlockSpec((None, d, tf), lambda s, e, j, xbk, obk, r, fl, cg, cu: (cg[s] // nj, 0, cg[s] % nj)),
                  pl.BlockSpec((None, d, tf),
                               lambda s, e, j, xbk, obk, r, fl, cg, cu: (cu[s] // nj, 0, nj + cu[s] % nj)),
                  pl.BlockSpec((None, tf, d), lambda s, e, j, xbk, obk, r, fl, cg, cu: (e[s], j[s], 0)),
                  pl.BlockSpec((None, 1, tf), lambda s, e, j, xbk, obk, r, fl, cg, cu: (e[s], 0, j[s])),
                  pl.BlockSpec((None, 1, tf), lambda s, e, j, xbk, obk, r, fl, cg, cu: (e[s], 0, nj + j[s])),
                  pl.BlockSpec((None, 1, d), lambda s, e, j, xbk, obk, r, fl, cg, cu: (e[s], 0, 0))],
        out_specs=pl.BlockSpec((tm, d), lambda s, e, j, xbk, obk, r, fl, cg, cu: (obk[s], 0)),
        scratch_shapes=[pltpu.VMEM((d, tf), BF16), pltpu.VMEM((d, tf), BF16), pltpu.VMEM((tf, d), BF16),
                        pltpu.VMEM((subs, tm, d), F32)],
    )
    return pl.pallas_call(
        functools.partial(_moe_kernel, nj),
        grid_spec=grid_spec,
        out_shape=jax.ShapeDtypeStruct((nslot, d), F32),
        compiler_params=_params("arbitrary"),
        name="moe",
    )(item_e, item_j, item_xb, item_ob, item_r, item_flag, wg_code, wu_code,
      xb, w_gu, w_gu, w_dn, b_gu.reshape(n_exp, 1, f2), b_gu.reshape(n_exp, 1, f2), b_dn.reshape(n_exp, 1, d))


def _combine_kernel(alpha, rt, dest_ref, destn_ref, yb_hbm, tg_ref, x1_ref, gtp_ref, gts_ref, lg_ref, lb_ref,
                    op_ref, os_ref, buf, sem):
    i = pl.program_id(0)
    last = pl.num_programs(0) - 1
    tq = x1_ref.shape[0]
    slot = i % 2

    def start(ids_ref, sl, t):
        for kk in range(TOP_K):
            pltpu.make_async_copy(yb_hbm.at[pl.ds(ids_ref[0, t * TOP_K + kk], 1)],
                                  buf.at[sl, kk, pl.ds(t, 1)], sem.at[sl]).start()

    def wait(sl):
        for kk in range(TOP_K):
            pltpu.make_async_copy(yb_hbm.at[pl.ds(0, tq)], buf.at[sl, kk], sem.at[sl]).wait()

    @pl.when(i == 0)
    def _():
        lax.fori_loop(0, tq, lambda t, c: (start(dest_ref, 0, t), c)[1], 0, unroll=4)

    for t in range(tq):
        start(destn_ref, 1 - slot, t)
    wait(slot)
    tg = tg_ref[...]
    y = jnp.zeros(x1_ref.shape, F32)
    for kk in range(TOP_K):
        y = y + buf[slot, kk] * tg[:, kk:kk + 1]
    is_s = i >= rt.n_p
    out = _layer_norm(alpha * x1_ref[...] + rt.pick_vec(i, gtp_ref, gts_ref) * y, lg_ref[...], lb_ref[...])

    @pl.when(jnp.logical_not(is_s))
    def _():
        op_ref[...] = out

    @pl.when(is_s)
    def _():
        os_ref[...] = out

    @pl.when(i == last)
    def _():
        wait(1 - slot)


def _combine(alpha, rt, dest, yb, tg, x1, mod_seq, mod_row, lg, lb):
    d = x1.shape[1]
    tq = rt.tm
    ids = dest.reshape(rt.n, 1, tq * TOP_K)
    return pl.pallas_call(
        functools.partial(_combine_kernel, alpha, rt),
        grid=(rt.n,),
        in_specs=[pl.BlockSpec((None, 1, tq * TOP_K), lambda i: (i, 0, 0), memory_space=pltpu.SMEM),
                  pl.BlockSpec((None, 1, tq * TOP_K), lambda i: (jnp.minimum(i + 1, rt.n - 1), 0, 0),
                               memory_space=pltpu.SMEM),
                  pl.BlockSpec(memory_space=pl.ANY),
                  rt.joint(LANES), rt.joint(d), rt.seq_vec(d, MOD_GATE2), rt.row_vec(d, MOD_GATE2),
                  pl.BlockSpec((1, d), lambda i: (0, 0)),
                  pl.BlockSpec((1, d), lambda i: (0, 0))],
        out_specs=[rt.prompt(d), rt.sample(d, single=False)],
        out_shape=[jax.ShapeDtypeStruct((rt.n_p * tq, d), F32), jax.ShapeDtypeStruct((rt.n_s * tq, d), F32)],
        scratch_shapes=[pltpu.VMEM((2, TOP_K, tq, d), F32), pltpu.SemaphoreType.DMA((2,))],
        compiler_params=_params("arbitrary"),
        name="combine",
    )(ids, ids, yb, tg, x1, mod_seq, mod_row, lg, lb)


def _take(table, idx):
    hit = idx[:, None] == jnp.arange(table.shape[0], dtype=jnp.int32)[None, :]
    return jnp.sum(jnp.where(hit, table[None, :], 0), axis=1)


def _routing(ti, n_experts, tm, nj, subs):
    t = ti.shape[0]
    n_assign = t * TOP_K
    nb_max = n_assign // tm + n_experts
    onehot = (ti[:, :, None] == jnp.arange(n_experts, dtype=jnp.int32)[None, None, :]).astype(jnp.int32)
    per_tok = jnp.sum(onehot, axis=1)
    cum = jnp.cumsum(per_tok, axis=0)
    counts = cum[-1]
    rank = jnp.sum(onehot * cum[:, None, :], axis=2) - 1
    nblk = (counts + tm - 1) // tm
    blk_end = jnp.cumsum(nblk)
    blk_start = blk_end - nblk
    dest = jnp.sum(onehot * blk_start[None, None, :], axis=2) * tm + rank
    pad_lo = (blk_start * tm + counts).astype(jnp.int32)
    pad_len = (nblk * tm - counts).astype(jnp.int32)
    total_blk = blk_end[-1]
    blocks = jnp.arange(nb_max, dtype=jnp.int32)
    blk_e = jnp.minimum(jnp.sum((blk_end[None, :] <= blocks[:, None]).astype(jnp.int32), axis=1), n_experts - 1)
    start_b, nblk_b = _take(blk_start, blk_e), _take(nblk, blk_e)
    r_in_e = blocks - start_b
    g0 = start_b + (r_in_e // subs) * subs
    nsub = jnp.minimum(subs, nblk_b - (r_in_e // subs) * subs)
    p = jnp.arange(nb_max * nj, dtype=jnp.int32)
    bp = p // nj
    valid = bp < total_blk
    last = jnp.maximum(total_blk - 1, 0)
    bq = jnp.where(valid, bp, last)
    gq, nq, eq = _take(g0, bq), jnp.maximum(_take(nsub, bq), 1), _take(blk_e, bq)
    local = p - nj * gq
    jq = jnp.where(valid, local // nq, nj - 1)
    rq = jnp.where(valid, local % nq, nq - 1)
    item_xb = gq + rq
    item_ob = jnp.where(valid, jnp.where(jq == nj - 1, gq + rq, gq), bp)
    flags = valid.astype(jnp.int32) + 2 * (valid & (rq == 0)).astype(jnp.int32)
    nxt_blk = gq + nq
    has_next = nxt_blk < total_blk
    last_j = jq == nj - 1
    e_n = jnp.where(last_j & has_next, _take(blk_e, jnp.minimum(nxt_blk, nb_max - 1)), eq)
    j_n = jnp.where(last_j, jnp.where(has_next, 0, jq), jq + 1)
    cur, nxt = eq * nj + jq, e_n * nj + j_n
    wg_code = jnp.where(valid & (rq >= 1), nxt, cur)
    wu_code = jnp.where(valid & (rq >= 2), nxt, cur)
    items = (eq.astype(jnp.int32), jq.astype(jnp.int32), item_xb.astype(jnp.int32),
             item_ob.astype(jnp.int32), rq.astype(jnp.int32), flags,
             wg_code.astype(jnp.int32), wu_code.astype(jnp.int32))
    return dest.astype(jnp.int32), pad_lo, pad_len, total_blk.astype(jnp.int32), nb_max, items


WPREP_TR = 256


def _win_prep_kernel(c_ab, w_hbm, o_ref, buf, sem):
    c = pl.program_id(0)
    tr = o_ref.shape[0]
    o_a = QKV_W + V_W
    c_shift = o_a // tr

    def copy(cc):
        row = jnp.where(cc < c_shift, cc * tr, jnp.where(cc < c_ab, cc * tr + 2 * DN_HEADS, o_a))
        src = w_hbm.at[pl.ds(pl.multiple_of(row, 2 * DN_HEADS), tr)]
        return pltpu.make_async_copy(src, buf.at[cc % 2], sem.at[cc % 2])

    @pl.when(c == 0)
    def _():
        copy(c).start()

    @pl.when(c + 1 <= c_ab)
    def _():
        copy(c + 1).start()

    @pl.when(c <= c_ab)
    def _():
        copy(c).wait()

    x = buf[c % 2]
    rows = lax.broadcasted_iota(jnp.int32, x.shape, 0)
    keep = jnp.logical_or(c < c_ab, jnp.logical_and(c == c_ab, rows < 2 * DN_HEADS))
    o_ref[...] = jnp.where(keep, x, 0.0).astype(BF16)


def _rearranged_w_in(w_in, d):
    dm, nw = w_in.shape
    tr = WPREP_TR
    o_a = QKV_W + V_W
    rest = nw - o_a - 2 * DN_HEADS
    assert o_a % tr == 0 and rest % tr == 0
    nw_out = -(-(nw - 2 * DN_HEADS + LANES) // PROJ_TN) * PROJ_TN
    assert nw_out % tr == 0
    w = pl.pallas_call(
        functools.partial(_win_prep_kernel, (o_a + rest) // tr),
        grid=(nw_out // tr,),
        in_specs=[pl.BlockSpec(memory_space=pl.ANY)],
        out_specs=pl.BlockSpec((tr, dm), lambda c: (c, 0)),
        out_shape=jax.ShapeDtypeStruct((nw_out, dm), BF16),
        scratch_shapes=[pltpu.VMEM((2, tr, dm), F32), pltpu.SemaphoreType.DMA((2,))],
        compiler_params=_params("arbitrary"),
        name="w_in_prep",
    )(w_in.T)
    cols = dict(z=QKV_W, u=QKV_W + V_W, v=QKV_W + V_W + SG_W, ga=QKV_W + V_W + 2 * SG_W,
                gb=QKV_W + V_W + 2 * SG_W + d, ab=QKV_W + V_W + 2 * SG_W + 2 * d)
    return w, cols


def kernel(x_prompt, x_sample, state_conv_qkv, state_delta, c_prompt, c_sample, w_ada, b_ada, w_in, w_conv, a_log, dt_bias, o_norm_g, sg_ln_g, sg_ln_b, w_s, b_s, p_a, p_b, w_out, ln1_g, ln1_b, router_w, router_b, w_gu, b_gu, w_dn, b_dn, ln2_g, ln2_b):
    depth = w_ada.shape[0]
    alpha = float((2 * depth) ** 0.25)
    bp, seq, d = x_prompt.shape
    bs, lt, _ = x_sample.shape
    assert lt == DN_CONV and seq % SG_CHUNK == 0
    n_experts = router_w.shape[2]
    rows_p, rows_s = bp * seq, bs * lt
    xp = x_prompt.reshape(rows_p, d)
    xs = x_sample.reshape(rows_s, d)
    c_all = jnp.concatenate([jnp.repeat(c_sample, lt, axis=0), c_prompt,
                             jnp.zeros((-(rows_s + bp) % SUBLANES, d), F32)], axis=0)
    tile = lambda cap: _Rows(math.gcd(math.gcd(cap, rows_s), seq), rows_p, rows_s, seq)
    rt_proj, rt_sgu, rt_merge, rt_comb = tile(PROJ_TM), tile(SGU_TM), tile(MERGE_TM), tile(COMB_TQ)
    outs = dict(conv_p=[], delta_p=[], conv_s=[], delta_s=[], vrows=[])

    for l in range(depth):
        mod = _adaln(c_all, w_ada[l], b_ada[l])

        w_r, col = _rearranged_w_in(w_in[l], d)
        proj = _inproj(rt_proj, xp, xs, mod, mod, w_r)

        alog = jnp.pad(a_log[l:l + 1], ((0, 0), (0, LANES - DN_HEADS)))
        dtb = jnp.pad(dt_bias[l:l + 1], ((0, 0), (0, LANES - DN_HEADS)))
        og = o_norm_g[l:l + 1]
        conv0 = jnp.zeros((bp, DN_CONV - 1, QKV_W), F32)
        s0 = jnp.zeros((bp, DN_HEADS, DN_DK, DN_DV), F32)
        oa_p, conv_p, delta_p = _gdn_prompt(proj, conv0, s0, w_conv[l], alog, dtb, og, bp, seq,
                                            col['z'], col['ab'])
        oa_s, conv_s, delta_s = _gdn_sample(proj, rows_p, state_conv_qkv[l], state_delta[l], w_conv[l],
                                            alog, dtb, og, bs, col['z'], col['ab'])

        assert rt_sgu.tm % SG_CHUNK == 0 and SG_CHUNK % lt == 0
        ws_p = jnp.tril(w_s[l]).astype(BF16)
        bias_p = jnp.repeat(b_s[l].T, SG_CH, axis=1)
        eye = jnp.eye(SG_CHUNK // lt, dtype=F32)
        ws_s = jnp.stack([jnp.kron(eye, jnp.tril(w_s[l, g, :lt, :lt])) for g in range(SG_GROUPS)]).astype(BF16)
        bias_s = jnp.tile(jnp.repeat(b_s[l, :, :lt].T, SG_CH, axis=1), (SG_CHUNK // lt, 1))
        ob, vn_s = _sgu(rt_sgu, proj, sg_ln_g[l:l + 1], sg_ln_b[l:l + 1],
                        ws_p, ws_s, bias_p, bias_s, col['u'], col['v'])

        pa, pb, wo = p_a[l].astype(BF16), p_b[l].astype(BF16), w_out[l].astype(BF16)
        rws = _split(jnp.pad(router_w[l], ((0, 0), (0, LANES - n_experts))))
        rw = jnp.concatenate([rws[0], rws[0], rws[1]], axis=0)
        rb = jnp.pad(router_b[l:l + 1], ((0, 0), (0, LANES - n_experts)))
        x1, h2, ti, tg = _merge(alpha, n_experts, rt_merge, oa_p, oa_s, ob, proj, xp, xs,
                                mod, mod,
                                pa, pb, wo, ln1_g[l:l + 1], ln1_b[l:l + 1], rw, rb,
                                col['ga'], col['gb'])

        nj = w_dn.shape[2] // MOE_TF
        dest, pad_lo, pad_len, nblk_used, nb_max, items = _routing(ti[:, :TOP_K], n_experts, MOE_TM, nj, MOE_SUBS)
        xb = _dispatch(pad_lo, pad_len, nblk_used, dest, h2, MOE_TM, nb_max)
        yb = _moe(items, xb, w_gu[l], b_gu[l], w_dn[l], b_dn[l], MOE_TM, MOE_TF, MOE_SUBS)
        xp, xs = _combine(alpha, rt_comb, dest, yb, tg, x1, mod, mod,
                          ln2_g[l:l + 1], ln2_b[l:l + 1])

        outs['conv_p'].append(conv_p)
        outs['delta_p'].append(delta_p)
        outs['conv_s'].append(conv_s)
        outs['delta_s'].append(delta_s)
        outs['vrows'].append(vn_s.reshape(bs, lt, SG_W))

    return (xp.reshape(bp, seq, d), xs.reshape(bs, lt, d),
            jnp.stack(outs['conv_p']), jnp.stack(outs['delta_p']),
            jnp.stack(outs['conv_s']), jnp.stack(outs['delta_s']), jnp.stack(outs['vrows']))
```

```python
import functools
import math

import jax
import jax.numpy as jnp
from jax import lax
from jax.experimental import pallas as pl
from jax.experimental.pallas import tpu as pltpu

F32 = jnp.float32
BF16 = jnp.bfloat16

DN_HEADS = 8
DN_DK = 128
DN_DV = 128
DN_CONV = 4
DN_CHUNK = 64
SG_GROUPS = 8
SG_CH = 128
SG_CHUNK = 128
TOP_K = 4
SWIGLU_LIMIT = 7.0
SWIGLU_ALPHA = 1.702
LN_EPS = 1e-5
NORM_EPS = 1e-6
QK_W = DN_HEADS * DN_DK
V_W = DN_HEADS * DN_DV
QKV_W = 2 * QK_W + V_W
SG_W = SG_GROUPS * SG_CH

LANES = 128
SUBLANES = 8
VMEM_LIMIT = 56 * 1024 * 1024

PROJ_TM = 512
PROJ_TN = 1536
MERGE_TM = 256
MOE_TM = 256
MOE_TF = 512
MOE_SUBS = 6
COMB_TQ = 256
DISPATCH_TQ = 256
SGU_TM = 512
ADALN_TN = 1024
INV_BLOCK = 16
SAMPLE_BT = 4
SAMPLE_TP = 8
GDN_HG = 4

MOD_SHIFT1, MOD_SCALE1, MOD_GATE1, MOD_SHIFT2, MOD_SCALE2, MOD_GATE2 = range(6)
MOD_PARTS = 6


def _sigmoid(x):
    return 0.5 * (jnp.tanh(0.5 * x) + 1.0)


def _silu(x):
    return x * _sigmoid(x)


def _dot(a, b):
    return jnp.dot(a.astype(BF16), b.astype(BF16), preferred_element_type=F32)


def _dot_nt(a, b):
    return lax.dot_general(a.astype(BF16), b.astype(BF16), (((1,), (1,)), ((), ())),
                           preferred_element_type=F32)


def _split3(x):
    hi = x.astype(BF16)
    r1 = x - hi.astype(F32)
    mid = r1.astype(BF16)
    return hi, mid, (r1 - mid.astype(F32)).astype(BF16)


def _mask_sums(masks, col, n):
    lane = lax.broadcasted_iota(jnp.int32, (n, LANES), 1)
    hi, mid, lo = [p.astype(F32) for p in _split3(col)]
    parts = jnp.where(lane == 0, hi, jnp.where(lane == 1, mid, jnp.where(lane == 2, lo, 0.0)))
    y = jnp.dot(masks, parts.astype(BF16), preferred_element_type=F32)
    y = y[:, 0:1] + y[:, 1:2] + y[:, 2:3]
    return [y[i * n:(i + 1) * n] for i in range(masks.shape[0] // n)]


def _layer_norm(x, g, b):
    mu = jnp.mean(x, axis=-1, keepdims=True)
    xc = x - mu
    var = jnp.mean(xc * xc, axis=-1, keepdims=True)
    return xc * lax.rsqrt(var + LN_EPS) * g + b


def _params(*sem):
    return pltpu.CompilerParams(dimension_semantics=sem, vmem_limit_bytes=VMEM_LIMIT)


class _Rows:
    def __init__(self, tm, rows_p, rows_s, seq):
        assert rows_p % tm == 0 and rows_s % tm == 0 and seq % tm == 0, (tm, rows_p, rows_s, seq)
        self.tm, self.n_p, self.n_s = tm, rows_p // tm, rows_s // tm
        self.tiles_per_seq = seq // tm
        self.bp = rows_p // seq
        assert rows_s % SUBLANES == 0
        self.rows_s = rows_s

    @property
    def n(self):
        return self.n_p + self.n_s

    def prompt(self, width, col=0):
        return pl.BlockSpec((self.tm, width), lambda i, *_: (jnp.minimum(i, self.n_p - 1), col))

    def sample(self, width, col=0, single=True):
        mode = dict(pipeline_mode=pl.Buffered(1)) if single else {}
        return pl.BlockSpec((self.tm, width), lambda i, *_: (jnp.maximum(i - self.n_p, 0), col), **mode)

    def joint(self, width, col=0):
        return pl.BlockSpec((self.tm, width), lambda i, *_: (i, col))

    def _seq(self, i):
        return jnp.minimum(i // self.tiles_per_seq, self.bp - 1)

    def seq_vec(self, d, part):
        return pl.BlockSpec((SUBLANES, d), lambda i, *_: ((self.rows_s + self._seq(i)) // SUBLANES, part))

    def row_vec(self, d, part):
        return pl.BlockSpec((self.tm, d), lambda i, *_: (jnp.maximum(i - self.n_p, 0), part),
                            pipeline_mode=pl.Buffered(1))

    def pick_vec(self, i, seq_ref, row_ref):
        prompt = seq_ref[pl.ds(self._seq(i) % SUBLANES, 1), :]
        return jnp.where(i >= self.n_p, row_ref[...], prompt)


def _pick(is_sample, prompt_ref, sample_ref):
    return jnp.where(is_sample, sample_ref[...], prompt_ref[...])


def _adaln_kernel(c_ref, w_ref, b_ref, o_ref):
    o_ref[...] = _dot(_silu(c_ref[...]), w_ref[...]) + b_ref[...]


def _adaln(c, w, b):
    rows, d = c.shape
    n = w.shape[1]
    tn = ADALN_TN
    return pl.pallas_call(
        _adaln_kernel,
        grid=(n // tn,),
        in_specs=[pl.BlockSpec((rows, d), lambda j: (0, 0)),
                  pl.BlockSpec((d, tn), lambda j: (0, j)),
                  pl.BlockSpec((1, tn), lambda j: (0, j))],
        out_specs=pl.BlockSpec((rows, tn), lambda j: (0, j)),
        out_shape=jax.ShapeDtypeStruct((rows, n), F32),
        compiler_params=_params("arbitrary"),
        name="adaln",
    )(c, w, b.reshape(1, n))


def _inproj_kernel(rt, xp_ref, xs_ref, scp_ref, scs_ref, shp_ref, shs_ref, w_ref, o_ref, h_scr):
    @pl.when(pl.program_id(1) == 0)
    def _():
        i = pl.program_id(0)
        x = _pick(i >= rt.n_p, xp_ref, xs_ref)
        h_scr[...] = (x * (1.0 + rt.pick_vec(i, scp_ref, scs_ref)) + rt.pick_vec(i, shp_ref, shs_ref)).astype(BF16)

    o_ref[...] = lax.dot_general(h_scr[...], w_ref[...], (((1,), (1,)), ((), ())), preferred_element_type=F32)


def _inproj(rt, xp, xs, mod_seq, mod_row, w):
    d = xp.shape[1]
    nw = w.shape[0]
    tm = rt.tm
    return pl.pallas_call(
        functools.partial(_inproj_kernel, rt),
        grid=(rt.n, nw // PROJ_TN),
        in_specs=[rt.prompt(d), rt.sample(d), rt.seq_vec(d, MOD_SCALE1), rt.row_vec(d, MOD_SCALE1),
                  rt.seq_vec(d, MOD_SHIFT1), rt.row_vec(d, MOD_SHIFT1),
                  pl.BlockSpec((PROJ_TN, d), lambda i, j: (j, 0))],
        out_specs=pl.BlockSpec((tm, PROJ_TN), lambda i, j: (i, j)),
        out_shape=jax.ShapeDtypeStruct((rt.n * tm, nw), F32),
        scratch_shapes=[pltpu.VMEM((tm, d), BF16)],
        compiler_params=_params("arbitrary", "arbitrary"),
        name="inproj",
    )(xp, xs, mod_seq, mod_row, mod_seq, mod_row, w)


def _softplus(x):
    return jnp.maximum(x, 0.0) + jnp.log1p(jnp.exp(-jnp.abs(x)))


def _split(a):
    hi = a.astype(BF16)
    lo = (a - hi.astype(F32)).astype(BF16)
    return hi, lo


def _map(f, *lists):
    return [f(*args) for args in zip(*lists)]


def _unit_lower_inverses_minus_eye(mats, rows, cols):
    same = (rows // INV_BLOCK) == (cols // INV_BLOCK)
    n = [jnp.where(same, a, 0.0) for a in mats]
    b = _map(lambda a, x: a - x, mats, n)
    n2 = _map(lambda x: _dot(x, x), n)
    n4 = _map(lambda x: _dot(x, x), n2)
    n8 = _map(lambda x: _dot(x, x), n4)
    r = [-x for x in n]
    r = _map(lambda x, p: x + p + _dot(x, p), r, n2)
    r = _map(lambda x, p: x + p + _dot(x, p), r, n4)
    dm = _map(lambda x, p: x + p + _dot(x, p), r, n8)
    m = _map(lambda x, y: y + _dot(x, y), dm, b)
    m2 = _map(lambda x: _dot(x, x), m)
    xm = _map(lambda x, p: x + p + _dot(p, x), dm, m2)
    return _map(lambda x, p: x - p - _dot(p, x), xm, m)


def _gdn_prompt_kernel(qkv_ref, z_ref, ab_ref, hist_ref, s0_ref, wconv_ref, alog_ref, dt_ref, og_ref,
                       o_ref, conv_ref, s_ref, xp_scr):
    n = pl.program_id(1)
    c = DN_CHUNK
    pad = SUBLANES
    nr = GDN_HG * c
    ngrp = DN_HEADS // GDN_HG

    @pl.when(n == 0)
    def _():
        xp_scr[pl.ds(0, pad), :] = jnp.zeros((pad, QKV_W), F32)
        xp_scr[pl.ds(pad - (DN_CONV - 1), DN_CONV - 1), :] = hist_ref[...]
        s_ref[...] = s0_ref[...]

    x = qkv_ref[...]
    xp_scr[pl.ds(pad, c), :] = x
    y = jnp.zeros((c, QKV_W), F32)
    for j in range(DN_CONV):
        y = y + xp_scr[pl.ds(pad - (DN_CONV - 1) + j, c), :] * wconv_ref[pl.ds(j, 1), :]
    y = _silu(y)
    tail = xp_scr[pl.ds(c + pad - (DN_CONV - 1), DN_CONV - 1), :]
    conv_ref[...] = tail
    xp_scr[pl.ds(pad - (DN_CONV - 1), DN_CONV - 1), :] = tail

    ab = ab_ref[...]
    g = -jnp.exp(alog_ref[...]) * _softplus(ab + dt_ref[...])
    beta_all = _sigmoid(ab)
    r64 = lax.broadcasted_iota(jnp.int32, (c, c), 0)
    c64 = lax.broadcasted_iota(jnp.int32, (c, c), 1)
    tri = (r64 >= c64).astype(BF16)
    gc = sum(jnp.dot(tri, part, preferred_element_type=F32) for part in _split3(g))
    z = z_ref[...]
    og = og_ref[...]

    rows = lax.broadcasted_iota(jnp.int32, (nr, nr), 0)
    cols = lax.broadcasted_iota(jnp.int32, (nr, nr), 1)
    same = (rows // c) == (cols // c)
    causal = same & (rows >= cols)
    strict = same & (rows > cols)
    rowhead = lax.broadcasted_iota(jnp.int32, (nr, 1), 0) // c
    groups = [range(grp * GDN_HG, (grp + 1) * GDN_HG) for grp in range(ngrp)]

    def stack(heads, off, width):
        return jnp.concatenate([y[:, off + h * width:off + (h + 1) * width] for h in heads], axis=0)

    def l2n(t):
        return t * lax.rsqrt(jnp.sum(t * t, axis=-1, keepdims=True) + NORM_EPS)

    q = [l2n(stack(hs, 0, DN_DK)) * (DN_DK ** -0.5) for hs in groups]
    k = [l2n(stack(hs, QK_W, DN_DK)) for hs in groups]
    v = [stack(hs, 2 * QK_W, DN_DV) for hs in groups]
    beta = [jnp.concatenate([beta_all[:, DN_HEADS + h:DN_HEADS + h + 1] for h in hs], axis=0) for hs in groups]
    gcf = [jnp.concatenate([jnp.broadcast_to(gc[:, h:h + 1], (c, LANES)) for h in hs], axis=0) for hs in groups]
    gcc = [t[:, 0:1] for t in gcf]
    grow = [t.T[0:1, :] for t in gcf]
    glast = [[gc[c - 1:c, h:h + 1] for h in hs] for hs in groups]
    gtot = [jnp.concatenate([jnp.broadcast_to(t, (c, 1)) for t in gl], axis=0) for gl in glast]
    decay = _map(lambda a, b: jnp.where(causal, jnp.exp(jnp.where(causal, a - b, 0.0)), 0.0), gcc, grow)
    egc = _map(jnp.exp, gcc)
    kb = _map(lambda a, b: a * b, k, beta)
    a_mat = _map(lambda a, b, dd: jnp.where(strict, _dot_nt(a, b) * dd, 0.0), kb, k, decay)
    tm1 = _unit_lower_inverses_minus_eye(a_mat, rows, cols)
    u = _map(lambda t, a, b: a * b + _dot(t, a * b), tm1, v, beta)
    w = _map(lambda t, a, b: a * b + _dot(t, a * b), tm1, kb, egc)
    attn = _map(lambda a, b, dd: _dot_nt(a, b) * dd, q, k, decay)
    qg = _map(lambda a, b: a * b, q, egc)
    kd_t = _map(lambda a, b, cc: (a * jnp.exp(b - cc)).T, k, gtot, gcc)

    for gi, hs in enumerate(groups):
        v_news, qss = [], []
        for hl, h in enumerate(hs):
            sl = slice(hl * c, (hl + 1) * c)
            rs = _dot(jnp.concatenate([w[gi][sl], qg[gi][sl]], axis=0), s_ref[h])
            v_news.append(u[gi][sl] - rs[:c])
            qss.append(rs[c:])
        v_new = jnp.concatenate(v_news, axis=0)
        o = jnp.concatenate(qss, axis=0) + _dot(attn[gi], v_new)
        for hl, h in enumerate(hs):
            upd = _dot(kd_t[gi], jnp.where(rowhead == hl, v_new, 0.0))
            s_ref[h] = s_ref[h] * jnp.exp(glast[gi][hl]) + upd
        o = o * lax.rsqrt(jnp.mean(o * o, axis=-1, keepdims=True) + NORM_EPS) * og
        for hl, h in enumerate(hs):
            o_ref[:, h * DN_DV:(h + 1) * DN_DV] = o[hl * c:(hl + 1) * c] * _silu(z[:, h * DN_DV:(h + 1) * DN_DV])


def _gdn_prompt(proj, hist, s0, wconv, alog, dtb, og, batch, seq, col_z, col_ab):
    c = DN_CHUNK
    nchunk = seq // c
    rows = batch * seq
    row = lambda b, n: b * nchunk + n
    return pl.pallas_call(
        _gdn_prompt_kernel,
        grid=(batch, nchunk),
        in_specs=[pl.BlockSpec((c, QKV_W), lambda b, n: (row(b, n), 0)),
                  pl.BlockSpec((c, V_W), lambda b, n: (row(b, n), col_z // V_W)),
                  pl.BlockSpec((c, LANES), lambda b, n: (row(b, n), col_ab // LANES)),
                  pl.BlockSpec((None, DN_CONV - 1, QKV_W), lambda b, n: (b, 0, 0)),
                  pl.BlockSpec((None, DN_HEADS, DN_DK, DN_DV), lambda b, n: (b, 0, 0, 0)),
                  pl.BlockSpec((DN_CONV, QKV_W), lambda b, n: (0, 0)),
                  pl.BlockSpec((1, LANES), lambda b, n: (0, 0)),
                  pl.BlockSpec((1, LANES), lambda b, n: (0, 0)),
                  pl.BlockSpec((1, DN_DV), lambda b, n: (0, 0))],
        out_specs=[pl.BlockSpec((c, V_W), lambda b, n: (row(b, n), 0)),
                   pl.BlockSpec((None, DN_CONV - 1, QKV_W), lambda b, n: (b, 0, 0)),
                   pl.BlockSpec((None, DN_HEADS, DN_DK, DN_DV), lambda b, n: (b, 0, 0, 0))],
        out_shape=[jax.ShapeDtypeStruct((rows, V_W), F32),
                   jax.ShapeDtypeStruct((batch, DN_CONV - 1, QKV_W), F32),
                   jax.ShapeDtypeStruct((batch, DN_HEADS, DN_DK, DN_DV), F32)],
        scratch_shapes=[pltpu.VMEM((c + SUBLANES, QKV_W), F32)],
        compiler_params=_params("arbitrary", "arbitrary"),
        name="gdn_prompt",
    )(proj, proj, proj, hist, s0, wconv, alog, dtb, og)


def _gdn_sample_kernel(qkv_ref, z_ref, ab_ref, hist_ref, s0_ref, wconv_ref, alog_ref, dt_ref, og_ref,
                       o_ref, conv_ref, s_ref, xp_scr, ab_scr, wq_scr, r_scr, kdt_scr, vn_scr, gl_scr):
    bt, tp, lt = SAMPLE_BT, SAMPLE_TP, DN_CONV
    nblk = DN_HEADS * bt
    nrow = nblk * tp
    hist_rows = DN_CONV - 1

    xp_scr[...] = jnp.zeros(xp_scr.shape, F32)
    ab_scr[...] = jnp.zeros(ab_scr.shape, F32)
    ys = []
    for b in range(bt):
        xp_scr[b, pl.ds(SUBLANES - hist_rows, hist_rows), :] = hist_ref[b]
        xp_scr[b, pl.ds(SUBLANES, lt), :] = qkv_ref[pl.ds(b * lt, lt), :]
        yb = jnp.zeros((tp, QKV_W), F32)
        for j in range(DN_CONV):
            yb = yb + xp_scr[b, pl.ds(SUBLANES - hist_rows + j, tp), :] * wconv_ref[pl.ds(j, 1), :]
        ys.append(_silu(yb))
        conv_ref[b] = xp_scr[b, pl.ds(SUBLANES + lt - hist_rows, hist_rows), :]
        ab_scr[b, pl.ds(0, lt), :] = ab_ref[pl.ds(b * lt, lt), :]
    y = jnp.concatenate(ys, axis=0)
    ab = jnp.concatenate([ab_scr[b] for b in range(bt)], axis=0)
    tok = lax.broadcasted_iota(jnp.int32, (bt * tp, 1), 0) % tp
    real = tok < lt
    g_all = jnp.where(real, -jnp.exp(alog_ref[...]) * _softplus(ab + dt_ref[...]), 0.0)
    beta_all = jnp.where(real, _sigmoid(ab), 0.0)

    def heads_to_rows(t, off, width):
        return jnp.concatenate([t[:, off + h * width:off + (h + 1) * width] for h in range(DN_HEADS)], axis=0)

    realr = jnp.concatenate([real] * DN_HEADS, axis=0)
    q = jnp.where(realr, heads_to_rows(y, 0, DN_DK), 0.0)
    k = jnp.where(realr, heads_to_rows(y, QK_W, DN_DK), 0.0)
    v = jnp.where(realr, heads_to_rows(y, 2 * QK_W, DN_DV), 0.0)
    gcol = jnp.concatenate([g_all[:, h:h + 1] for h in range(DN_HEADS)], axis=0)
    beta = jnp.concatenate([beta_all[:, DN_HEADS + h:DN_HEADS + h + 1] for h in range(DN_HEADS)], axis=0)
    q = q * lax.rsqrt(jnp.sum(q * q, axis=-1, keepdims=True) + NORM_EPS) * (DN_DK ** -0.5)
    k = k * lax.rsqrt(jnp.sum(k * k, axis=-1, keepdims=True) + NORM_EPS)

    rows = lax.broadcasted_iota(jnp.int32, (nrow, nrow), 0)
    cols = lax.broadcasted_iota(jnp.int32, (nrow, nrow), 1)
    same = (rows // tp) == (cols // tp)
    causal = same & (rows >= cols)
    strict = same & (rows > cols)
    gcc, gtot = _mask_sums(jnp.concatenate([causal.astype(BF16), same.astype(BF16)], axis=0), gcol, nrow)
    gtot = jnp.broadcast_to(gtot, (nrow, LANES))
    grow = jnp.broadcast_to(gcc, (nrow, LANES)).T[0:1, :]
    decay = jnp.where(causal, jnp.exp(jnp.where(causal, gcc - grow, 0.0)), 0.0)
    egc = jnp.exp(gcc)
    kb = k * beta
    a_mat = jnp.where(strict, _dot_nt(kb, k) * decay, 0.0)
    a2 = _dot(a_mat, a_mat)
    tm1 = a2 - a_mat - _dot(a_mat, a2)
    u = v * beta + _dot(tm1, v * beta)
    w = kb * egc + _dot(tm1, kb * egc)
    attn = _dot_nt(q, k) * decay
    qg = q * egc
    for i in range(nblk):
        wq_scr[pl.ds(2 * tp * i, tp), :] = w[i * tp:(i + 1) * tp, :]
        wq_scr[pl.ds(2 * tp * i + tp, tp), :] = qg[i * tp:(i + 1) * tp, :]
    kdt_scr[...] = (k * jnp.exp(gtot[:, 0:1] - gcc)).T
    gl_scr[...] = jnp.exp(gtot)

    def read_state(i, carry):
        r0 = pl.multiple_of(i * 2 * tp, 2 * tp)
        r_scr[pl.ds(r0, 2 * tp), :] = _dot(wq_scr[pl.ds(r0, 2 * tp), :], s0_ref[i % bt, i // bt])
        return carry

    lax.fori_loop(0, nblk, read_state, 0, unroll=4)
    ws = jnp.concatenate([r_scr[pl.ds(2 * tp * i, tp), :] for i in range(nblk)], axis=0)
    qs = jnp.concatenate([r_scr[pl.ds(2 * tp * i + tp, tp), :] for i in range(nblk)], axis=0)
    v_new = u - ws
    vn_scr[...] = v_new
    o = qs + _dot(attn, v_new)
    blockid = lax.broadcasted_iota(jnp.int32, (nrow, 1), 0) // tp

    def write_state(i, carry):
        r0 = pl.multiple_of(i * tp, tp)
        upd = _dot(kdt_scr[...], jnp.where(blockid == i, vn_scr[...], 0.0))
        s_ref[i % bt, i // bt] = s0_ref[i % bt, i // bt] * gl_scr[pl.ds(r0, 1), :] + upd
        return carry

    lax.fori_loop(0, nblk, write_state, 0, unroll=4)

    og = og_ref[...]
    o = o * lax.rsqrt(jnp.mean(o * o, axis=-1, keepdims=True) + NORM_EPS) * og
    for b in range(bt):
        for h in range(DN_HEADS):
            zbh = z_ref[pl.ds(b * lt, lt), h * DN_DV:(h + 1) * DN_DV]
            blk = o[(h * bt + b) * tp:(h * bt + b) * tp + lt, :]
            o_ref[pl.ds(b * lt, lt), h * DN_DV:(h + 1) * DN_DV] = blk * _silu(zbh)


def _gdn_sample(proj, row0, hist, s0, wconv, alog, dtb, og, batch, col_z, col_ab):
    bt, tp, lt = SAMPLE_BT, SAMPLE_TP, DN_CONV
    rows = batch * lt
    nrow = DN_HEADS * bt * tp
    blk = bt * lt
    assert row0 % blk == 0 and batch % bt == 0
    b0 = row0 // blk
    return pl.pallas_call(
        _gdn_sample_kernel,
        grid=(batch // bt,),
        in_specs=[pl.BlockSpec((blk, QKV_W), lambda i: (b0 + i, 0)),
                  pl.BlockSpec((blk, V_W), lambda i: (b0 + i, col_z // V_W)),
                  pl.BlockSpec((blk, LANES), lambda i: (b0 + i, col_ab // LANES)),
                  pl.BlockSpec((bt, DN_CONV - 1, QKV_W), lambda i: (i, 0, 0)),
                  pl.BlockSpec((bt, DN_HEADS, DN_DK, DN_DV), lambda i: (i, 0, 0, 0)),
                  pl.BlockSpec((DN_CONV, QKV_W), lambda i: (0, 0)),
                  pl.BlockSpec((1, LANES), lambda i: (0, 0)),
                  pl.BlockSpec((1, LANES), lambda i: (0, 0)),
                  pl.BlockSpec((1, DN_DV), lambda i: (0, 0))],
        out_specs=[pl.BlockSpec((blk, V_W), lambda i: (i, 0)),
                   pl.BlockSpec((bt, DN_CONV - 1, QKV_W), lambda i: (i, 0, 0)),
                   pl.BlockSpec((bt, DN_HEADS, DN_DK, DN_DV), lambda i: (i, 0, 0, 0))],
        out_shape=[jax.ShapeDtypeStruct((rows, V_W), F32),
                   jax.ShapeDtypeStruct((batch, DN_CONV - 1, QKV_W), F32),
                   jax.ShapeDtypeStruct((batch, DN_HEADS, DN_DK, DN_DV), F32)],
        scratch_shapes=[pltpu.VMEM((bt, SUBLANES + tp, QKV_W), F32),
                        pltpu.VMEM((bt, tp, LANES), F32),
                        pltpu.VMEM((2 * nrow, DN_DK), F32),
                        pltpu.VMEM((2 * nrow, DN_DV), F32),
                        pltpu.VMEM((DN_DK, nrow), F32),
                        pltpu.VMEM((nrow, DN_DV), F32),
                        pltpu.VMEM((nrow, LANES), F32)],
        compiler_params=_params("arbitrary"),
        name="gdn_sample",
    )(proj, proj, proj, hist, s0, wconv, alog, dtb, og)


def _sgu_kernel(n_p, u_ref, v_ref, g_ref, b_ref, wsp_ref, wss_ref, bp_ref, bs_ref, o_ref, vn_ref):
    is_s = pl.program_id(0) >= n_p
    u = jax.nn.gelu(u_ref[...])
    vn = _layer_norm(jax.nn.gelu(v_ref[...]), g_ref[...], b_ref[...])
    vn_ref[...] = vn
    bias = _pick(is_s, bp_ref, bs_ref)
    for g in range(SG_GROUPS):
        sl = slice(g * SG_CH, (g + 1) * SG_CH)
        ws = jnp.where(is_s, wss_ref[g], wsp_ref[g])
        for c in range(u.shape[0] // SG_CHUNK):
            rows = slice(c * SG_CHUNK, (c + 1) * SG_CHUNK)
            mixed = _dot(ws, vn[rows, sl]) + bias[:, sl]
            o_ref[rows, sl] = u[rows, sl] * mixed


def _sgu(rt, proj, ln_g, ln_b, ws_p, ws_s, bias_p, bias_s, col_u, col_v):
    t = rt.tm
    const = lambda shape: pl.BlockSpec(shape, lambda i: (0,) * len(shape))
    return pl.pallas_call(
        functools.partial(_sgu_kernel, rt.n_p),
        grid=(rt.n,),
        in_specs=[rt.joint(SG_W, col_u // SG_W), rt.joint(SG_W, col_v // SG_W),
                  const((1, SG_W)), const((1, SG_W)),
                  const((SG_GROUPS, SG_CHUNK, SG_CHUNK)), const((SG_GROUPS, SG_CHUNK, SG_CHUNK)),
                  const((SG_CHUNK, SG_W)), const((SG_CHUNK, SG_W))],
        out_specs=[rt.joint(SG_W), rt.sample(SG_W, single=False)],
        out_shape=[jax.ShapeDtypeStruct((rt.n * t, SG_W), F32),
                   jax.ShapeDtypeStruct((rt.n_s * t, SG_W), F32)],
        compiler_params=_params("arbitrary"),
        name="sgu",
    )(proj, proj, ln_g, ln_b, ws_p, ws_s, bias_p, bias_s)


def _merge_kernel(alpha, n_experts, rt, oap_ref, oas_ref, ob_ref, ga_ref, gb_ref, xp_ref, xs_ref,
                  gtp_ref, gts_ref, scp_ref, scs_ref, shp_ref, shs_ref,
                  pa_ref, pb_ref, wo_ref, lg_ref, lb_ref, rw_ref, rb_ref,
                  x1_ref, h2_ref, ti_ref, tg_ref):
    i = pl.program_id(0)
    is_s = i >= rt.n_p
    oa = _pick(is_s, oap_ref, oas_ref)
    merged = (_sigmoid(ga_ref[...]) * _dot(oa, pa_ref[...])
              + _sigmoid(gb_ref[...]) * _dot(ob_ref[...], pb_ref[...]))
    y = _dot(merged, wo_ref[...])
    x = _pick(is_s, xp_ref, xs_ref)
    x1 = _layer_norm(alpha * x + rt.pick_vec(i, gtp_ref, gts_ref) * y, lg_ref[...], lb_ref[...])
    x1_ref[...] = x1
    h2 = x1 * (1.0 + rt.pick_vec(i, scp_ref, scs_ref)) + rt.pick_vec(i, shp_ref, shs_ref)
    h2_ref[...] = h2
    hs = _split(h2)
    logits = jnp.dot(jnp.concatenate([hs[0], hs[1], hs[0]], axis=1), rw_ref[...],
                     preferred_element_type=F32) + rb_ref[...]
    lane = lax.broadcasted_iota(jnp.int32, logits.shape, 1)
    logits = jnp.where(lane < n_experts, logits, -jnp.inf)
    ti = jnp.zeros(logits.shape, jnp.int32)
    tv = jnp.zeros(logits.shape, F32)
    top = None
    for kk in range(TOP_K):
        m = jnp.max(logits, axis=-1, keepdims=True)
        idx = jnp.min(jnp.where(logits == m, lane.astype(F32), float(LANES)), axis=-1,
                      keepdims=True).astype(jnp.int32)
        if kk == 0:
            top = m
        ti = jnp.where(lane == kk, idx, ti)
        tv = jnp.where(lane == kk, jnp.exp(m - top), tv)
        logits = jnp.where(lane == idx, -jnp.inf, logits)
    ti_ref[...] = ti
    tg_ref[...] = tv / jnp.sum(tv, axis=-1, keepdims=True)


def _merge(alpha, n_experts, rt, oa_p, oa_s, ob, proj, xp, xs, mod_seq, mod_row, pa, pb, wo, lg, lb, rw, rb,
           col_ga, col_gb):
    d = xp.shape[1]
    tm = rt.tm
    rows = rt.n * tm
    const = lambda shape: pl.BlockSpec(shape, lambda i: (0,) * len(shape), pipeline_mode=pl.Buffered(1))
    return pl.pallas_call(
        functools.partial(_merge_kernel, alpha, n_experts, rt),
        grid=(rt.n,),
        in_specs=[rt.prompt(V_W), rt.sample(V_W), rt.joint(SG_W),
                  rt.joint(d, col_ga // d), rt.joint(d, col_gb // d),
                  rt.prompt(d), rt.sample(d),
                  rt.seq_vec(d, MOD_GATE1), rt.row_vec(d, MOD_GATE1),
                  rt.seq_vec(d, MOD_SCALE2), rt.row_vec(d, MOD_SCALE2),
                  rt.seq_vec(d, MOD_SHIFT2), rt.row_vec(d, MOD_SHIFT2),
                  const((V_W, d)), const((SG_W, d)), const((d, d)),
                  const((1, d)), const((1, d)), const((3 * d, LANES)), const((1, LANES))],
        out_specs=[rt.joint(d), rt.joint(d), rt.joint(LANES), rt.joint(LANES)],
        out_shape=[jax.ShapeDtypeStruct((rows, d), F32),
                   jax.ShapeDtypeStruct((rows, d), F32),
                   jax.ShapeDtypeStruct((rows, LANES), jnp.int32),
                   jax.ShapeDtypeStruct((rows, LANES), F32)],
        compiler_params=_params("arbitrary"),
        name="merge",
    )(oa_p, oa_s, ob, proj, proj, xp, xs, mod_seq, mod_row, mod_seq, mod_row, mod_seq, mod_row,
      pa, pb, wo, lg, lb, rw, rb)


FILL_ROWS = (128, 64, 32, 16, 8)


def _dispatch_kernel(tm, n_experts, padlo_ref, padlen_ref, nblk_ref, dest_ref, h_ref, xb_hbm, zero_scr, sem, zsem):
    i = pl.program_id(0)
    tq = h_ref.shape[0]
    nb_max = xb_hbm.shape[0] // tm
    zrows = zero_scr.shape[0]

    def fill(act):
        def per_expert(e, carry):
            lo = padlo_ref[e]
            ln = padlen_ref[e]
            head = (SUBLANES - lo % SUBLANES) % SUBLANES
            for r in range(SUBLANES - 1):
                @pl.when(r < head)
                def _(r=r):
                    act(pltpu.make_async_copy(zero_scr.at[pl.ds(0, 1)], xb_hbm.at[pl.ds(lo + r, 1)], zsem))
            off = lo + head
            rem = ln - head
            for b in FILL_ROWS:
                @pl.when((rem & b) != 0)
                def _(off=off, b=b):
                    dst = xb_hbm.at[pl.ds(pl.multiple_of(off, SUBLANES), b)]
                    act(pltpu.make_async_copy(zero_scr.at[pl.ds(0, b)], dst, zsem))
                off = off + (rem & b)
            return carry

        lax.fori_loop(0, n_experts, per_expert, 0)

        def per_block(blk, carry):
            for part in range(tm // zrows):
                row0 = pl.multiple_of(blk * tm + part * zrows, zrows)
                act(pltpu.make_async_copy(zero_scr, xb_hbm.at[pl.ds(row0, zrows)], zsem))
            return carry

        lax.fori_loop(nblk_ref[0], nb_max, per_block, 0)

    @pl.when(i == 0)
    def _():
        zero_scr[...] = jnp.zeros(zero_scr.shape, F32)
        fill(lambda cp: cp.start())
        fill(lambda cp: cp.wait())

    def start(t, carry):
        for kk in range(TOP_K):
            pltpu.make_async_copy(h_ref.at[pl.ds(t, 1)], xb_hbm.at[pl.ds(dest_ref[0, t * TOP_K + kk], 1)],
                                  sem).start()
        return carry

    lax.fori_loop(0, tq, start, 0, unroll=8)
    for kk in range(TOP_K):
        pltpu.make_async_copy(h_ref, xb_hbm.at[pl.ds(0, tq)], sem).wait()


def _dispatch(padlo, padlen, nblk_used, dest, h2, tm, nb_max):
    t, d = h2.shape
    tq = math.gcd(DISPATCH_TQ, t)
    n_experts = padlo.shape[0]
    grid_spec = pltpu.PrefetchScalarGridSpec(
        num_scalar_prefetch=3,
        grid=(t // tq,),
        in_specs=[pl.BlockSpec((None, 1, tq * TOP_K), lambda i, *_: (i, 0, 0), memory_space=pltpu.SMEM),
                  pl.BlockSpec((tq, d), lambda i, *_: (i, 0))],
        out_specs=pl.BlockSpec(memory_space=pl.ANY),
        scratch_shapes=[pltpu.VMEM((FILL_ROWS[0], d), F32), pltpu.SemaphoreType.DMA(()),
                        pltpu.SemaphoreType.DMA(())],
    )
    return pl.pallas_call(
        functools.partial(_dispatch_kernel, tm, n_experts),
        grid_spec=grid_spec,
        out_shape=jax.ShapeDtypeStruct((nb_max * tm, d), F32),
        compiler_params=pltpu.CompilerParams(dimension_semantics=("arbitrary",), vmem_limit_bytes=VMEM_LIMIT,
                                             has_side_effects=True),
        name="dispatch",
    )(padlo, padlen, nblk_used.reshape(1), dest.reshape(t // tq, 1, tq * TOP_K), h2)


def _moe_kernel(nj, e_ref, j_ref, xb_ref, ob_ref, r_ref, flag_ref, wg_code_ref, wu_code_ref,
                x_ref, wg_ref, wu_ref, wd_ref, bg_ref, bu_ref, bd_ref, o_ref,
                wg_scr, wu_scr, wd_scr, acc_scr):
    s = pl.program_id(0)
    flags = flag_ref[s]
    j = j_ref[s]
    r = r_ref[s]

    @pl.when((flags & 2) != 0)
    def _():
        wg_scr[...] = wg_ref[...].astype(BF16)
        wu_scr[...] = wu_ref[...].astype(BF16)
        wd_scr[...] = wd_ref[...].astype(BF16)

    @pl.when(flags == 0)
    def _():
        o_ref[...] = jnp.zeros(o_ref.shape, F32)

    @pl.when((flags & 1) != 0)
    def _():
        x = x_ref[...].astype(BF16)
        gate = jnp.dot(x, wg_scr[...], preferred_element_type=F32) + bg_ref[...]
        up = jnp.dot(x, wu_scr[...], preferred_element_type=F32) + bu_ref[...]
        gate = jnp.minimum(gate, SWIGLU_LIMIT)
        up = jnp.clip(up, -SWIGLU_LIMIT, SWIGLU_LIMIT)
        act = (up + 1.0) * gate * _sigmoid(SWIGLU_ALPHA * gate)
        y = jnp.dot(act.astype(BF16), wd_scr[...], preferred_element_type=F32)

        @pl.when(j == 0)
        def _():
            acc_scr[r] = y

        @pl.when(jnp.logical_and(j > 0, j < nj - 1))
        def _():
            acc_scr[r] = acc_scr[r] + y

        @pl.when(j == nj - 1)
        def _():
            o_ref[...] = acc_scr[r] + y + bd_ref[...]


def _moe(items, xb, w_gu, b_gu, w_dn, b_dn, tm, tf, subs):
    item_e, item_j, item_xb, item_ob, item_r, item_flag, wg_code, wu_code = items
    n_items = item_e.shape[0]
    nslot, d = xb.shape
    n_exp, _, f2 = w_gu.shape
    f = f2 // 2
    nj = f // tf
    assert nj >= 2
    grid_spec = pltpu.PrefetchScalarGridSpec(
        num_scalar_prefetch=8,
        grid=(n_items,),
        in_specs=[pl.BlockSpec((tm, d), lambda s, e, j, xbk, obk, r, fl, cg, cu: (xbk[s], 0)),
                  pl.BlockSpec((None, d, tf), lambda s, e, j, xbk, obk, r, fl, cg, cu: (cg[s] // nj, 0, cg[s] % nj)),
                  pl.BlockSpec((None, d, tf),
                               lambda s, e, j, xbk, obk, r, fl, cg, cu: (cu[s] // nj, 0, nj + cu[s] % nj)),
                  pl.BlockSpec((None, tf, d), lambda s, e, j, xbk, obk, r, fl, cg, cu: (e[s], j[s], 0)),
                  pl.BlockSpec((None, 1, tf), lambda s, e, j, xbk, obk, r, fl, cg, cu: (e[s], 0, j[s])),
                  pl.BlockSpec((None, 1, tf), lambda s, e, j, xbk, obk, r, fl, cg, cu: (e[s], 0, nj + j[s])),
                  pl.BlockSpec((None, 1, d), lambda s, e, j, xbk, obk, r, fl, cg, cu: (e[s], 0, 0))],
        out_specs=pl.BlockSpec((tm, d), lambda s, e, j, xbk, obk, r, fl, cg, cu: (obk[s], 0)),
        scratch_shapes=[pltpu.VMEM((d, tf), BF16), pltpu.VMEM((d, tf), BF16), pltpu.VMEM((tf, d), BF16),
                        pltpu.VMEM((subs, tm, d), F32)],
    )
    return pl.pallas_call(
        functools.partial(_moe_kernel, nj),
        grid_spec=grid_spec,
        out_shape=jax.ShapeDtypeStruct((nslot, d), F32),
        compiler_params=_params("arbitrary"),
        name="moe",
    )(item_e, item_j, item_xb, item_ob, item_r, item_flag, wg_code, wu_code,
      xb, w_gu, w_gu, w_dn, b_gu.reshape(n_exp, 1, f2), b_gu.reshape(n_exp, 1, f2), b_dn.reshape(n_exp, 1, d))


def _combine_kernel(alpha, rt, dest_ref, destn_ref, yb_hbm, tg_ref, x1_ref, gtp_ref, gts_ref, lg_ref, lb_ref,
                    op_ref, os_ref, buf, sem):
    i = pl.program_id(0)
    last = pl.num_programs(0) - 1
    tq = x1_ref.shape[0]
    slot = i % 2

    def start(ids_ref, sl, t):
        for kk in range(TOP_K):
            pltpu.make_async_copy(yb_hbm.at[pl.ds(ids_ref[0, t * TOP_K + kk], 1)],
                                  buf.at[sl, kk, pl.ds(t, 1)], sem.at[sl]).start()

    def wait(sl):
        for kk in range(TOP_K):
            pltpu.make_async_copy(yb_hbm.at[pl.ds(0, tq)], buf.at[sl, kk], sem.at[sl]).wait()

    @pl.when(i == 0)
    def _():
        lax.fori_loop(0, tq, lambda t, c: (start(dest_ref, 0, t), c)[1], 0, unroll=4)

    for t in range(tq):
        start(destn_ref, 1 - slot, t)
    wait(slot)
    tg = tg_ref[...]
    y = jnp.zeros(x1_ref.shape, F32)
    for kk in range(TOP_K):
        y = y + buf[slot, kk] * tg[:, kk:kk + 1]
    is_s = i >= rt.n_p
    out = _layer_norm(alpha * x1_ref[...] + rt.pick_vec(i, gtp_ref, gts_ref) * y, lg_ref[...], lb_ref[...])

    @pl.when(jnp.logical_not(is_s))
    def _():
        op_ref[...] = out

    @pl.when(is_s)
    def _():
        os_ref[...] = out

    @pl.when(i == last)
    def _():
        wait(1 - slot)


def _combine(alpha, rt, dest, yb, tg, x1, mod_seq, mod_row, lg, lb):
    d = x1.shape[1]
    tq = rt.tm
    ids = dest.reshape(rt.n, 1, tq * TOP_K)
    return pl.pallas_call(
        functools.partial(_combine_kernel, alpha, rt),
        grid=(rt.n,),
        in_specs=[pl.BlockSpec((None, 1, tq * TOP_K), lambda i: (i, 0, 0), memory_space=pltpu.SMEM),
                  pl.BlockSpec((None, 1, tq * TOP_K), lambda i: (jnp.minimum(i + 1, rt.n - 1), 0, 0),
                               memory_space=pltpu.SMEM),
                  pl.BlockSpec(memory_space=pl.ANY),
                  rt.joint(LANES), rt.joint(d), rt.seq_vec(d, MOD_GATE2), rt.row_vec(d, MOD_GATE2),
                  pl.BlockSpec((1, d), lambda i: (0, 0)),
                  pl.BlockSpec((1, d), lambda i: (0, 0))],
        out_specs=[rt.prompt(d), rt.sample(d, single=False)],
        out_shape=[jax.ShapeDtypeStruct((rt.n_p * tq, d), F32), jax.ShapeDtypeStruct((rt.n_s * tq, d), F32)],
        scratch_shapes=[pltpu.VMEM((2, TOP_K, tq, d), F32), pltpu.SemaphoreType.DMA((2,))],
        compiler_params=_params("arbitrary"),
        name="combine",
    )(ids, ids, yb, tg, x1, mod_seq, mod_row, lg, lb)


def _take(table, idx):
    hit = idx[:, None] == jnp.arange(table.shape[0], dtype=jnp.int32)[None, :]
    return jnp.sum(jnp.where(hit, table[None, :], 0), axis=1)


def _routing(ti, n_experts, tm, nj, subs):
    t = ti.shape[0]
    n_assign = t * TOP_K
    nb_max = n_assign // tm + n_experts
    onehot = (ti[:, :, None] == jnp.arange(n_experts, dtype=jnp.int32)[None, None, :]).astype(jnp.int32)
    per_tok = jnp.sum(onehot, axis=1)
    cum = jnp.cumsum(per_tok, axis=0)
    counts = cum[-1]
    rank = jnp.sum(onehot * cum[:, None, :], axis=2) - 1
    nblk = (counts + tm - 1) // tm
    blk_end = jnp.cumsum(nblk)
    blk_start = blk_end - nblk
    dest = jnp.sum(onehot * blk_start[None, None, :], axis=2) * tm + rank
    pad_lo = (blk_start * tm + counts).astype(jnp.int32)
    pad_len = (nblk * tm - counts).astype(jnp.int32)
    total_blk = blk_end[-1]
    blocks = jnp.arange(nb_max, dtype=jnp.int32)
    blk_e = jnp.minimum(jnp.sum((blk_end[None, :] <= blocks[:, None]).astype(jnp.int32), axis=1), n_experts - 1)
    start_b, nblk_b = _take(blk_start, blk_e), _take(nblk, blk_e)
    r_in_e = blocks - start_b
    g0 = start_b + (r_in_e // subs) * subs
    nsub = jnp.minimum(subs, nblk_b - (r_in_e // subs) * subs)
    p = jnp.arange(nb_max * nj, dtype=jnp.int32)
    bp = p // nj
    valid = bp < total_blk
    last = jnp.maximum(total_blk - 1, 0)
    bq = jnp.where(valid, bp, last)
    gq, nq, eq = _take(g0, bq), jnp.maximum(_take(nsub, bq), 1), _take(blk_e, bq)
    local = p - nj * gq
    jq = jnp.where(valid, local // nq, nj - 1)
    rq = jnp.where(valid, local % nq, nq - 1)
    item_xb = gq + rq
    item_ob = jnp.where(valid, jnp.where(jq == nj - 1, gq + rq, gq), bp)
    flags = valid.astype(jnp.int32) + 2 * (valid & (rq == 0)).astype(jnp.int32)
    nxt_blk = gq + nq
    has_next = nxt_blk < total_blk
    last_j = jq == nj - 1
    e_n = jnp.where(last_j & has_next, _take(blk_e, jnp.minimum(nxt_blk, nb_max - 1)), eq)
    j_n = jnp.where(last_j, jnp.where(has_next, 0, jq), jq + 1)
    cur, nxt = eq * nj + jq, e_n * nj + j_n
    wg_code = jnp.where(valid & (rq >= 1), nxt, cur)
    wu_code = jnp.where(valid & (rq >= 2), nxt, cur)
    items = (eq.astype(jnp.int32), jq.astype(jnp.int32), item_xb.astype(jnp.int32),
             item_ob.astype(jnp.int32), rq.astype(jnp.int32), flags,
             wg_code.astype(jnp.int32), wu_code.astype(jnp.int32))
    return dest.astype(jnp.int32), pad_lo, pad_len, total_blk.astype(jnp.int32), nb_max, items


WPREP_TR = 256


def _win_prep_kernel(c_ab, w_hbm, o_ref, buf, sem):
    c = pl.program_id(0)
    tr = o_ref.shape[0]
    o_a = QKV_W + V_W
    c_shift = o_a // tr

    def copy(cc):
        row = jnp.where(cc < c_shift, cc * tr, jnp.where(cc < c_ab, cc * tr + 2 * DN_HEADS, o_a))
        src = w_hbm.at[pl.ds(pl.multiple_of(row, 2 * DN_HEADS), tr)]
        return pltpu.make_async_copy(src, buf.at[cc % 2], sem.at[cc % 2])

    @pl.when(c == 0)
    def _():
        copy(c).start()

    @pl.when(c + 1 <= c_ab)
    def _():
        copy(c + 1).start()

    @pl.when(c <= c_ab)
    def _():
        copy(c).wait()

    x = buf[c % 2]
    rows = lax.broadcasted_iota(jnp.int32, x.shape, 0)
    keep = jnp.logical_or(c < c_ab, jnp.logical_and(c == c_ab, rows < 2 * DN_HEADS))
    o_ref[...] = jnp.where(keep, x, 0.0).astype(BF16)


def _rearranged_w_in(w_in, d):
    dm, nw = w_in.shape
    tr = WPREP_TR
    o_a = QKV_W + V_W
    rest = nw - o_a - 2 * DN_HEADS
    assert o_a % tr == 0 and rest % tr == 0
    nw_out = -(-(nw - 2 * DN_HEADS + LANES) // PROJ_TN) * PROJ_TN
    assert nw_out % tr == 0
    w = pl.pallas_call(
        functools.partial(_win_prep_kernel, (o_a + rest) // tr),
        grid=(nw_out // tr,),
        in_specs=[pl.BlockSpec(memory_space=pl.ANY)],
        out_specs=pl.BlockSpec((tr, dm), lambda c: (c, 0)),
        out_shape=jax.ShapeDtypeStruct((nw_out, dm), BF16),
        scratch_shapes=[pltpu.VMEM((2, tr, dm), F32), pltpu.SemaphoreType.DMA((2,))],
        compiler_params=_params("arbitrary"),
        name="w_in_prep",
    )(w_in.T)
    cols = dict(z=QKV_W, u=QKV_W + V_W, v=QKV_W + V_W + SG_W, ga=QKV_W + V_W + 2 * SG_W,
                gb=QKV_W + V_W + 2 * SG_W + d, ab=QKV_W + V_W + 2 * SG_W + 2 * d)
    return w, cols


def kernel(x_prompt, x_sample, state_conv_qkv, state_delta, c_prompt, c_sample, w_ada, b_ada, w_in, w_conv, a_log, dt_bias, o_norm_g, sg_ln_g, sg_ln_b, w_s, b_s, p_a, p_b, w_out, ln1_g, ln1_b, router_w, router_b, w_gu, b_gu, w_dn, b_dn, ln2_g, ln2_b):
    depth = w_ada.shape[0]
    alpha = float((2 * depth) ** 0.25)
    bp, seq, d = x_prompt.shape
    bs, lt, _ = x_sample.shape
    assert lt == DN_CONV and seq % SG_CHUNK == 0
    n_experts = router_w.shape[2]
    rows_p, rows_s = bp * seq, bs * lt
    xp = x_prompt.reshape(rows_p, d)
    xs = x_sample.reshape(rows_s, d)
    c_all = jnp.concatenate([jnp.repeat(c_sample, lt, axis=0), c_prompt,
                             jnp.zeros((-(rows_s + bp) % SUBLANES, d), F32)], axis=0)
    tile = lambda cap: _Rows(math.gcd(math.gcd(cap, rows_s), seq), rows_p, rows_s, seq)
    rt_proj, rt_sgu, rt_merge, rt_comb = tile(PROJ_TM), tile(SGU_TM), tile(MERGE_TM), tile(COMB_TQ)
    outs = dict(conv_p=[], delta_p=[], conv_s=[], delta_s=[], vrows=[])

    for l in range(depth):
        mod = _adaln(c_all, w_ada[l], b_ada[l])

        w_r, col = _rearranged_w_in(w_in[l], d)
        proj = _inproj(rt_proj, xp, xs, mod, mod, w_r)

        alog = jnp.pad(a_log[l:l + 1], ((0, 0), (0, LANES - DN_HEADS)))
        dtb = jnp.pad(dt_bias[l:l + 1], ((0, 0), (0, LANES - DN_HEADS)))
        og = o_norm_g[l:l + 1]
        conv0 = jnp.zeros((bp, DN_CONV - 1, QKV_W), F32)
        s0 = jnp.zeros((bp, DN_HEADS, DN_DK, DN_DV), F32)
        oa_p, conv_p, delta_p = _gdn_prompt(proj, conv0, s0, w_conv[l], alog, dtb, og, bp, seq,
                                            col['z'], col['ab'])
        oa_s, conv_s, delta_s = _gdn_sample(proj, rows_p, state_conv_qkv[l], state_delta[l], w_conv[l],
                                            alog, dtb, og, bs, col['z'], col['ab'])

        assert rt_sgu.tm % SG_CHUNK == 0 and SG_CHUNK % lt == 0
        ws_p = jnp.tril(w_s[l]).astype(BF16)
        bias_p = jnp.repeat(b_s[l].T, SG_CH, axis=1)
        eye = jnp.eye(SG_CHUNK // lt, dtype=F32)
        ws_s = jnp.stack([jnp.kron(eye, jnp.tril(w_s[l, g, :lt, :lt])) for g in range(SG_GROUPS)]).astype(BF16)
        bias_s = jnp.tile(jnp.repeat(b_s[l, :, :lt].T, SG_CH, axis=1), (SG_CHUNK // lt, 1))
        ob, vn_s = _sgu(rt_sgu, proj, sg_ln_g[l:l + 1], sg_ln_b[l:l + 1],
                        ws_p, ws_s, bias_p, bias_s, col['u'], col['v'])

        pa, pb, wo = p_a[l].astype(BF16), p_b[l].astype(BF16), w_out[l].astype(BF16)
        rws = _split(jnp.pad(router_w[l], ((0, 0), (0, LANES - n_experts))))
        rw = jnp.concatenate([rws[0], rws[0], rws[1]], axis=0)
        rb = jnp.pad(router_b[l:l + 1], ((0, 0), (0, LANES - n_experts)))
        x1, h2, ti, tg = _merge(alpha, n_experts, rt_merge, oa_p, oa_s, ob, proj, xp, xs,
                                mod, mod,
                                pa, pb, wo, ln1_g[l:l + 1], ln1_b[l:l + 1], rw, rb,
                                col['ga'], col['gb'])

        nj = w_dn.shape[2] // MOE_TF
        dest, pad_lo, pad_len, nblk_used, nb_max, items = _routing(ti[:, :TOP_K], n_experts, MOE_TM, nj, MOE_SUBS)
        xb = _dispatch(pad_lo, pad_len, nblk_used, dest, h2, MOE_TM, nb_max)
        yb = _moe(items, xb, w_gu[l], b_gu[l], w_dn[l], b_dn[l], MOE_TM, MOE_TF, MOE_SUBS)
        xp, xs = _combine(alpha, rt_comb, dest, yb, tg, x1, mod, mod,
                          ln2_g[l:l + 1], ln2_b[l:l + 1])

        outs['conv_p'].append(conv_p)
        outs['delta_p'].append(delta_p)
        outs['conv_s'].append(conv_s)
        outs['delta_s'].append(delta_s)
        outs['vrows'].append(vn_s.reshape(bs, lt, SG_W))

    return (xp.reshape(bp, seq, d), xs.reshape(bs, lt, d),
            jnp.stack(outs['conv_p']), jnp.stack(outs['delta_p']),
            jnp.stack(outs['conv_s']), jnp.stack(outs['delta_s']), jnp.stack(outs['vrows']))
```

```python
import functools
import math

import jax
import jax.numpy as jnp
from jax import lax
from jax.experimental import pallas as pl
from jax.experimental.pallas import tpu as pltpu

F32 = jnp.float32
BF16 = jnp.bfloat16

DN_HEADS = 8
DN_DK = 128
DN_DV = 128
DN_CONV = 4
DN_CHUNK = 64
SG_GROUPS = 8
SG_CH = 128
SG_CHUNK = 128
TOP_K = 4
SWIGLU_LIMIT = 7.0
SWIGLU_ALPHA = 1.702
LN_EPS = 1e-5
NORM_EPS = 1e-6
QK_W = DN_HEADS * DN_DK
V_W = DN_HEADS * DN_DV
QKV_W = 2 * QK_W + V_W
SG_W = SG_GROUPS * SG_CH

LANES = 128
SUBLANES = 8
VMEM_LIMIT = 56 * 1024 * 1024

PROJ_TM = 512
PROJ_TN = 1536
MERGE_TM = 256
MOE_TM = 256
MOE_TF = 512
MOE_SUBS = 6
COMB_TQ = 256
DISPATCH_TQ = 256
SGU_TM = 512
ADALN_TN = 2048
INV_BLOCK = 16
SAMPLE_BT = 4
SAMPLE_TP = 8
GDN_HG = 4

MOD_SHIFT1, MOD_SCALE1, MOD_GATE1, MOD_SHIFT2, MOD_SCALE2, MOD_GATE2 = range(6)
MOD_PARTS = 6


def _sigmoid(x):
    return 0.5 * (jnp.tanh(0.5 * x) + 1.0)


def _silu(x):
    return x * _sigmoid(x)


def _dot(a, b):
    return jnp.dot(a.astype(BF16), b.astype(BF16), preferred_element_type=F32)


def _dot_nt(a, b):
    return lax.dot_general(a.astype(BF16), b.astype(BF16), (((1,), (1,)), ((), ())),
                           preferred_element_type=F32)


def _split3(x):
    hi = x.astype(BF16)
    r1 = x - hi.astype(F32)
    mid = r1.astype(BF16)
    return hi, mid, (r1 - mid.astype(F32)).astype(BF16)


def _mask_sums(masks, col, n):
    lane = lax.broadcasted_iota(jnp.int32, (n, LANES), 1)
    hi, mid, lo = [p.astype(F32) for p in _split3(col)]
    parts = jnp.where(lane == 0, hi, jnp.where(lane == 1, mid, jnp.where(lane == 2, lo, 0.0)))
    y = jnp.dot(masks, parts.astype(BF16), preferred_element_type=F32)
    y = y[:, 0:1] + y[:, 1:2] + y[:, 2:3]
    return [y[i * n:(i + 1) * n] for i in range(masks.shape[0] // n)]


def _layer_norm(x, g, b):
    mu = jnp.mean(x, axis=-1, keepdims=True)
    xc = x - mu
    var = jnp.mean(xc * xc, axis=-1, keepdims=True)
    return xc * lax.rsqrt(var + LN_EPS) * g + b


def _params(*sem):
    return pltpu.CompilerParams(dimension_semantics=sem, vmem_limit_bytes=VMEM_LIMIT)


class _Rows:
    def __init__(self, tm, rows_p, rows_s, seq):
        assert rows_p % tm == 0 and rows_s % tm == 0 and seq % tm == 0, (tm, rows_p, rows_s, seq)
        self.tm, self.n_p, self.n_s = tm, rows_p // tm, rows_s // tm
        self.tiles_per_seq = seq // tm
        self.bp = rows_p // seq
        assert rows_s % SUBLANES == 0
        self.rows_s = rows_s

    @property
    def n(self):
        return self.n_p + self.n_s

    def prompt(self, width, col=0):
        return pl.BlockSpec((self.tm, width), lambda i, *_: (jnp.minimum(i, self.n_p - 1), col))

    def sample(self, width, col=0, single=True):
        mode = dict(pipeline_mode=pl.Buffered(1)) if single else {}
        return pl.BlockSpec((self.tm, width), lambda i, *_: (jnp.maximum(i - self.n_p, 0), col), **mode)

    def joint(self, width, col=0):
        return pl.BlockSpec((self.tm, width), lambda i, *_: (i, col))

    def _seq(self, i):
        return jnp.minimum(i // self.tiles_per_seq, self.bp - 1)

    def seq_vec(self, d, part):
        return pl.BlockSpec((SUBLANES, d), lambda i, *_: ((self.rows_s + self._seq(i)) // SUBLANES, part))

    def row_vec(self, d, part):
        return pl.BlockSpec((self.tm, d), lambda i, *_: (jnp.maximum(i - self.n_p, 0), part),
                            pipeline_mode=pl.Buffered(1))

    def pick_vec(self, i, seq_ref, row_ref):
        prompt = seq_ref[pl.ds(self._seq(i) % SUBLANES, 1), :]
        return jnp.where(i >= self.n_p, row_ref[...], prompt)


def _pick(is_sample, prompt_ref, sample_ref):
    return jnp.where(is_sample, sample_ref[...], prompt_ref[...])


def _adaln_kernel(c_ref, w_ref, b_ref, o_ref):
    o_ref[...] = _dot(_silu(c_ref[...]), w_ref[...]) + b_ref[...]


def _adaln(c, w, b):
    rows, d = c.shape
    n = w.shape[1]
    tn = ADALN_TN
    return pl.pallas_call(
        _adaln_kernel,
        grid=(n // tn,),
        in_specs=[pl.BlockSpec((rows, d), lambda j: (0, 0)),
                  pl.BlockSpec((d, tn), lambda j: (0, j)),
                  pl.BlockSpec((1, tn), lambda j: (0, j))],
        out_specs=pl.BlockSpec((rows, tn), lambda j: (0, j)),
        out_shape=jax.ShapeDtypeStruct((rows, n), F32),
        compiler_params=_params("arbitrary"),
        name="adaln",
    )(c, w, b.reshape(1, n))


def _inproj_kernel(rt, xp_ref, xs_ref, scp_ref, scs_ref, shp_ref, shs_ref, w_ref, o_ref, h_scr):
    @pl.when(pl.program_id(1) == 0)
    def _():
        i = pl.program_id(0)
        x = _pick(i >= rt.n_p, xp_ref, xs_ref)
        h_scr[...] = (x * (1.0 + rt.pick_vec(i, scp_ref, scs_ref)) + rt.pick_vec(i, shp_ref, shs_ref)).astype(BF16)

    o_ref[...] = lax.dot_general(h_scr[...], w_ref[...], (((1,), (1,)), ((), ())), preferred_element_type=F32)


def _inproj(rt, xp, xs, mod_seq, mod_row, w):
    d = xp.shape[1]
    nw = w.shape[0]
    tm = rt.tm
    return pl.pallas_call(
        functools.partial(_inproj_kernel, rt),
        grid=(rt.n, nw // PROJ_TN),
        in_specs=[rt.prompt(d), rt.sample(d), rt.seq_vec(d, MOD_SCALE1), rt.row_vec(d, MOD_SCALE1),
                  rt.seq_vec(d, MOD_SHIFT1), rt.row_vec(d, MOD_SHIFT1),
                  pl.BlockSpec((PROJ_TN, d), lambda i, j: (j, 0))],
        out_specs=pl.BlockSpec((tm, PROJ_TN), lambda i, j: (i, j)),
        out_shape=jax.ShapeDtypeStruct((rt.n * tm, nw), F32),
        scratch_shapes=[pltpu.VMEM((tm, d), BF16)],
        compiler_params=_params("arbitrary", "arbitrary"),
        name="inproj",
    )(xp, xs, mod_seq, mod_row, mod_seq, mod_row, w)


def _softplus(x):
    return jnp.maximum(x, 0.0) + jnp.log1p(jnp.exp(-jnp.abs(x)))


def _split(a):
    hi = a.astype(BF16)
    lo = (a - hi.astype(F32)).astype(BF16)
    return hi, lo


def _map(f, *lists):
    return [f(*args) for args in zip(*lists)]


def _unit_lower_inverses_minus_eye(mats, rows, cols):
    same = (rows // INV_BLOCK) == (cols // INV_BLOCK)
    n = [jnp.where(same, a, 0.0) for a in mats]
    b = _map(lambda a, x: a - x, mats, n)
    n2 = _map(lambda x: _dot(x, x), n)
    n4 = _map(lambda x: _dot(x, x), n2)
    n8 = _map(lambda x: _dot(x, x), n4)
    r = [-x for x in n]
    r = _map(lambda x, p: x + p + _dot(x, p), r, n2)
    r = _map(lambda x, p: x + p + _dot(x, p), r, n4)
    dm = _map(lambda x, p: x + p + _dot(x, p), r, n8)
    m = _map(lambda x, y: y + _dot(x, y), dm, b)
    m2 = _map(lambda x: _dot(x, x), m)
    xm = _map(lambda x, p: x + p + _dot(p, x), dm, m2)
    return _map(lambda x, p: x - p - _dot(p, x), xm, m)


def _gdn_prompt_kernel(qkv_ref, z_ref, ab_ref, hist_ref, s0_ref, wconv_ref, alog_ref, dt_ref, og_ref,
                       o_ref, conv_ref, s_ref, xp_scr):
    n = pl.program_id(1)
    c = DN_CHUNK
    pad = SUBLANES
    nr = GDN_HG * c
    ngrp = DN_HEADS // GDN_HG

    @pl.when(n == 0)
    def _():
        xp_scr[pl.ds(0, pad), :] = jnp.zeros((pad, QKV_W), F32)
        xp_scr[pl.ds(pad - (DN_CONV - 1), DN_CONV - 1), :] = hist_ref[...]
        s_ref[...] = s0_ref[...]

    x = qkv_ref[...]
    xp_scr[pl.ds(pad, c), :] = x
    y = jnp.zeros((c, QKV_W), F32)
    for j in range(DN_CONV):
        y = y + xp_scr[pl.ds(pad - (DN_CONV - 1) + j, c), :] * wconv_ref[pl.ds(j, 1), :]
    y = _silu(y)
    tail = xp_scr[pl.ds(c + pad - (DN_CONV - 1), DN_CONV - 1), :]
    conv_ref[...] = tail
    xp_scr[pl.ds(pad - (DN_CONV - 1), DN_CONV - 1), :] = tail

    ab = ab_ref[...]
    g = -jnp.exp(alog_ref[...]) * _softplus(ab + dt_ref[...])
    beta_all = _sigmoid(ab)
    r64 = lax.broadcasted_iota(jnp.int32, (c, c), 0)
    c64 = lax.broadcasted_iota(jnp.int32, (c, c), 1)
    tri = (r64 >= c64).astype(BF16)
    gc = sum(jnp.dot(tri, part, preferred_element_type=F32) for part in _split3(g))
    z = z_ref[...]
    og = og_ref[...]

    rows = lax.broadcasted_iota(jnp.int32, (nr, nr), 0)
    cols = lax.broadcasted_iota(jnp.int32, (nr, nr), 1)
    same = (rows // c) == (cols // c)
    causal = same & (rows >= cols)
    strict = same & (rows > cols)
    rowhead = lax.broadcasted_iota(jnp.int32, (nr, 1), 0) // c
    groups = [range(grp * GDN_HG, (grp + 1) * GDN_HG) for grp in range(ngrp)]

    def stack(heads, off, width):
        return jnp.concatenate([y[:, off + h * width:off + (h + 1) * width] for h in heads], axis=0)

    def l2n(t):
        return t * lax.rsqrt(jnp.sum(t * t, axis=-1, keepdims=True) + NORM_EPS)

    q = [l2n(stack(hs, 0, DN_DK)) * (DN_DK ** -0.5) for hs in groups]
    k = [l2n(stack(hs, QK_W, DN_DK)) for hs in groups]
    v = [stack(hs, 2 * QK_W, DN_DV) for hs in groups]
    beta = [jnp.concatenate([beta_all[:, DN_HEADS + h:DN_HEADS + h + 1] for h in hs], axis=0) for hs in groups]
    gcf = [jnp.concatenate([jnp.broadcast_to(gc[:, h:h + 1], (c, LANES)) for h in hs], axis=0) for hs in groups]
    gcc = [t[:, 0:1] for t in gcf]
    grow = [t.T[0:1, :] for t in gcf]
    glast = [[gc[c - 1:c, h:h + 1] for h in hs] for hs in groups]
    gtot = [jnp.concatenate([jnp.broadcast_to(t, (c, 1)) for t in gl], axis=0) for gl in glast]
    decay = _map(lambda a, b: jnp.where(causal, jnp.exp(jnp.where(causal, a - b, 0.0)), 0.0), gcc, grow)
    egc = _map(jnp.exp, gcc)
    kb = _map(lambda a, b: a * b, k, beta)
    a_mat = _map(lambda a, b, dd: jnp.where(strict, _dot_nt(a, b) * dd, 0.0), kb, k, decay)
    tm1 = _unit_lower_inverses_minus_eye(a_mat, rows, cols)
    u = _map(lambda t, a, b: a * b + _dot(t, a * b), tm1, v, beta)
    w = _map(lambda t, a, b: a * b + _dot(t, a * b), tm1, kb, egc)
    attn = _map(lambda a, b, dd: _dot_nt(a, b) * dd, q, k, decay)
    qg = _map(lambda a, b: a * b, q, egc)
    kd_t = _map(lambda a, b, cc: (a * jnp.exp(b - cc)).T, k, gtot, gcc)

    for gi, hs in enumerate(groups):
        v_news, qss = [], []
        for hl, h in enumerate(hs):
            sl = slice(hl * c, (hl + 1) * c)
            rs = _dot(jnp.concatenate([w[gi][sl], qg[gi][sl]], axis=0), s_ref[h])
            v_news.append(u[gi][sl] - rs[:c])
            qss.append(rs[c:])
        v_new = jnp.concatenate(v_news, axis=0)
        o = jnp.concatenate(qss, axis=0) + _dot(attn[gi], v_new)
        for hl, h in enumerate(hs):
            upd = _dot(kd_t[gi], jnp.where(rowhead == hl, v_new, 0.0))
            s_ref[h] = s_ref[h] * jnp.exp(glast[gi][hl]) + upd
        o = o * lax.rsqrt(jnp.mean(o * o, axis=-1, keepdims=True) + NORM_EPS) * og
        for hl, h in enumerate(hs):
            o_ref[:, h * DN_DV:(h + 1) * DN_DV] = o[hl * c:(hl + 1) * c] * _silu(z[:, h * DN_DV:(h + 1) * DN_DV])


def _gdn_prompt(proj, hist, s0, wconv, alog, dtb, og, batch, seq, col_z, col_ab):
    c = DN_CHUNK
    nchunk = seq // c
    rows = batch * seq
    row = lambda b, n: b * nchunk + n
    return pl.pallas_call(
        _gdn_prompt_kernel,
        grid=(batch, nchunk),
        in_specs=[pl.BlockSpec((c, QKV_W), lambda b, n: (row(b, n), 0)),
                  pl.BlockSpec((c, V_W), lambda b, n: (row(b, n), col_z // V_W)),
                  pl.BlockSpec((c, LANES), lambda b, n: (row(b, n), col_ab // LANES)),
                  pl.BlockSpec((None, DN_CONV - 1, QKV_W), lambda b, n: (b, 0, 0)),
                  pl.BlockSpec((None, DN_HEADS, DN_DK, DN_DV), lambda b, n: (b, 0, 0, 0)),
                  pl.BlockSpec((DN_CONV, QKV_W), lambda b, n: (0, 0)),
                  pl.BlockSpec((1, LANES), lambda b, n: (0, 0)),
                  pl.BlockSpec((1, LANES), lambda b, n: (0, 0)),
                  pl.BlockSpec((1, DN_DV), lambda b, n: (0, 0))],
        out_specs=[pl.BlockSpec((c, V_W), lambda b, n: (row(b, n), 0)),
                   pl.BlockSpec((None, DN_CONV - 1, QKV_W), lambda b, n: (b, 0, 0)),
                   pl.BlockSpec((None, DN_HEADS, DN_DK, DN_DV), lambda b, n: (b, 0, 0, 0))],
        out_shape=[jax.ShapeDtypeStruct((rows, V_W), F32),
                   jax.ShapeDtypeStruct((batch, DN_CONV - 1, QKV_W), F32),
                   jax.ShapeDtypeStruct((batch, DN_HEADS, DN_DK, DN_DV), F32)],
        scratch_shapes=[pltpu.VMEM((c + SUBLANES, QKV_W), F32)],
        compiler_params=_params("arbitrary", "arbitrary"),
        name="gdn_prompt",
    )(proj, proj, proj, hist, s0, wconv, alog, dtb, og)


def _gdn_sample_kernel(qkv_ref, z_ref, ab_ref, hist_ref, s0_ref, wconv_ref, alog_ref, dt_ref, og_ref,
                       o_ref, conv_ref, s_ref, xp_scr, ab_scr, wq_scr, r_scr, kdt_scr, vn_scr, gl_scr):
    bt, tp, lt = SAMPLE_BT, SAMPLE_TP, DN_CONV
    nblk = DN_HEADS * bt
    nrow = nblk * tp
    hist_rows = DN_CONV - 1

    xp_scr[...] = jnp.zeros(xp_scr.shape, F32)
    ab_scr[...] = jnp.zeros(ab_scr.shape, F32)
    ys = []
    for b in range(bt):
        xp_scr[b, pl.ds(SUBLANES - hist_rows, hist_rows), :] = hist_ref[b]
        xp_scr[b, pl.ds(SUBLANES, lt), :] = qkv_ref[pl.ds(b * lt, lt), :]
        yb = jnp.zeros((tp, QKV_W), F32)
        for j in range(DN_CONV):
            yb = yb + xp_scr[b, pl.ds(SUBLANES - hist_rows + j, tp), :] * wconv_ref[pl.ds(j, 1), :]
        ys.append(_silu(yb))
        conv_ref[b] = xp_scr[b, pl.ds(SUBLANES + lt - hist_rows, hist_rows), :]
        ab_scr[b, pl.ds(0, lt), :] = ab_ref[pl.ds(b * lt, lt), :]
    y = jnp.concatenate(ys, axis=0)
    ab = jnp.concatenate([ab_scr[b] for b in range(bt)], axis=0)
    tok = lax.broadcasted_iota(jnp.int32, (bt * tp, 1), 0) % tp
    real = tok < lt
    g_all = jnp.where(real, -jnp.exp(alog_ref[...]) * _softplus(ab + dt_ref[...]), 0.0)
    beta_all = jnp.where(real, _sigmoid(ab), 0.0)

    def heads_to_rows(t, off, width):
        return jnp.concatenate([t[:, off + h * width:off + (h + 1) * width] for h in range(DN_HEADS)], axis=0)

    realr = jnp.concatenate([real] * DN_HEADS, axis=0)
    q = jnp.where(realr, heads_to_rows(y, 0, DN_DK), 0.0)
    k = jnp.where(realr, heads_to_rows(y, QK_W, DN_DK), 0.0)
    v = jnp.where(realr, heads_to_rows(y, 2 * QK_W, DN_DV), 0.0)
    gcol = jnp.concatenate([g_all[:, h:h + 1] for h in range(DN_HEADS)], axis=0)
    beta = jnp.concatenate([beta_all[:, DN_HEADS + h:DN_HEADS + h + 1] for h in range(DN_HEADS)], axis=0)
    q = q * lax.rsqrt(jnp.sum(q * q, axis=-1, keepdims=True) + NORM_EPS) * (DN_DK ** -0.5)
    k = k * lax.rsqrt(jnp.sum(k * k, axis=-1, keepdims=True) + NORM_EPS)

    rows = lax.broadcasted_iota(jnp.int32, (nrow, nrow), 0)
    cols = lax.broadcasted_iota(jnp.int32, (nrow, nrow), 1)
    same = (rows // tp) == (cols // tp)
    causal = same & (rows >= cols)
    strict = same & (rows > cols)
    gcc, gtot = _mask_sums(jnp.concatenate([causal.astype(BF16), same.astype(BF16)], axis=0), gcol, nrow)
    gtot = jnp.broadcast_to(gtot, (nrow, LANES))
    grow = jnp.broadcast_to(gcc, (nrow, LANES)).T[0:1, :]
    decay = jnp.where(causal, jnp.exp(jnp.where(causal, gcc - grow, 0.0)), 0.0)
    egc = jnp.exp(gcc)
    kb = k * beta
    a_mat = jnp.where(strict, _dot_nt(kb, k) * decay, 0.0)
    a2 = _dot(a_mat, a_mat)
    tm1 = a2 - a_mat - _dot(a_mat, a2)
    u = v * beta + _dot(tm1, v * beta)
    w = kb * egc + _dot(tm1, kb * egc)
    attn = _dot_nt(q, k) * decay
    qg = q * egc
    for i in range(nblk):
        wq_scr[pl.ds(2 * tp * i, tp), :] = w[i * tp:(i + 1) * tp, :]
        wq_scr[pl.ds(2 * tp * i + tp, tp), :] = qg[i * tp:(i + 1) * tp, :]
    kdt_scr[...] = (k * jnp.exp(gtot[:, 0:1] - gcc)).T
    gl_scr[...] = jnp.exp(gtot)

    def read_state(i, carry):
        r0 = pl.multiple_of(i * 2 * tp, 2 * tp)
        r_scr[pl.ds(r0, 2 * tp), :] = _dot(wq_scr[pl.ds(r0, 2 * tp), :], s0_ref[i % bt, i // bt])
        return carry

    lax.fori_loop(0, nblk, read_state, 0, unroll=8)
    ws = jnp.concatenate([r_scr[pl.ds(2 * tp * i, tp), :] for i in range(nblk)], axis=0)
    qs = jnp.concatenate([r_scr[pl.ds(2 * tp * i + tp, tp), :] for i in range(nblk)], axis=0)
    v_new = u - ws
    vn_scr[...] = v_new
    o = qs + _dot(attn, v_new)
    blockid = lax.broadcasted_iota(jnp.int32, (nrow, 1), 0) // tp

    def write_state(i, carry):
        r0 = pl.multiple_of(i * tp, tp)
        upd = _dot(kdt_scr[...], jnp.where(blockid == i, vn_scr[...], 0.0))
        s_ref[i % bt, i // bt] = s0_ref[i % bt, i // bt] * gl_scr[pl.ds(r0, 1), :] + upd
        return carry

    lax.fori_loop(0, nblk, write_state, 0, unroll=8)

    og = og_ref[...]
    o = o * lax.rsqrt(jnp.mean(o * o, axis=-1, keepdims=True) + NORM_EPS) * og
    for b in range(bt):
        for h in range(DN_HEADS):
            zbh = z_ref[pl.ds(b * lt, lt), h * DN_DV:(h + 1) * DN_DV]
            blk = o[(h * bt + b) * tp:(h * bt + b) * tp + lt, :]
            o_ref[pl.ds(b * lt, lt), h * DN_DV:(h + 1) * DN_DV] = blk * _silu(zbh)


def _gdn_sample(proj, row0, hist, s0, wconv, alog, dtb, og, batch, col_z, col_ab):
    bt, tp, lt = SAMPLE_BT, SAMPLE_TP, DN_CONV
    rows = batch * lt
    nrow = DN_HEADS * bt * tp
    blk = bt * lt
    assert row0 % blk == 0 and batch % bt == 0
    b0 = row0 // blk
    return pl.pallas_call(
        _gdn_sample_kernel,
        grid=(batch // bt,),
        in_specs=[pl.BlockSpec((blk, QKV_W), lambda i: (b0 + i, 0)),
                  pl.BlockSpec((blk, V_W), lambda i: (b0 + i, col_z // V_W)),
                  pl.BlockSpec((blk, LANES), lambda i: (b0 + i, col_ab // LANES)),
                  pl.BlockSpec((bt, DN_CONV - 1, QKV_W), lambda i: (i, 0, 0)),
                  pl.BlockSpec((bt, DN_HEADS, DN_DK, DN_DV), lambda i: (i, 0, 0, 0)),
                  pl.BlockSpec((DN_CONV, QKV_W), lambda i: (0, 0)),
                  pl.BlockSpec((1, LANES), lambda i: (0, 0)),
                  pl.BlockSpec((1, LANES), lambda i: (0, 0)),
                  pl.BlockSpec((1, DN_DV), lambda i: (0, 0))],
        out_specs=[pl.BlockSpec((blk, V_W), lambda i: (i, 0)),
                   pl.BlockSpec((bt, DN_CONV - 1, QKV_W), lambda i: (i, 0, 0)),
                   pl.BlockSpec((bt, DN_HEADS, DN_DK, DN_DV), lambda i: (i, 0, 0, 0))],
        out_shape=[jax.ShapeDtypeStruct((rows, V_W), F32),
                   jax.ShapeDtypeStruct((batch, DN_CONV - 1, QKV_W), F32),
                   jax.ShapeDtypeStruct((batch, DN_HEADS, DN_DK, DN_DV), F32)],
        scratch_shapes=[pltpu.VMEM((bt, SUBLANES + tp, QKV_W), F32),
                        pltpu.VMEM((bt, tp, LANES), F32),
                        pltpu.VMEM((2 * nrow, DN_DK), F32),
                        pltpu.VMEM((2 * nrow, DN_DV), F32),
                        pltpu.VMEM((DN_DK, nrow), F32),
                        pltpu.VMEM((nrow, DN_DV), F32),
                        pltpu.VMEM((nrow, LANES), F32)],
        compiler_params=_params("arbitrary"),
        name="gdn_sample",
    )(proj, proj, proj, hist, s0, wconv, alog, dtb, og)


def _sgu_kernel(n_p, u_ref, v_ref, g_ref, b_ref, wsp_ref, wss_ref, bp_ref, bs_ref, o_ref, vn_ref):
    is_s = pl.program_id(0) >= n_p
    u = jax.nn.gelu(u_ref[...])
    vn = _layer_norm(jax.nn.gelu(v_ref[...]), g_ref[...], b_ref[...])
    vn_ref[...] = vn
    bias = _pick(is_s, bp_ref, bs_ref)
    for g in range(SG_GROUPS):
        sl = slice(g * SG_CH, (g + 1) * SG_CH)
        ws = jnp.where(is_s, wss_ref[g], wsp_ref[g])
        for c in range(u.shape[0] // SG_CHUNK):
            rows = slice(c * SG_CHUNK, (c + 1) * SG_CHUNK)
            mixed = _dot(ws, vn[rows, sl]) + bias[:, sl]
            o_ref[rows, sl] = u[rows, sl] * mixed


def _sgu(rt, proj, ln_g, ln_b, ws_p, ws_s, bias_p, bias_s, col_u, col_v):
    t = rt.tm
    const = lambda shape: pl.BlockSpec(shape, lambda i: (0,) * len(shape))
    return pl.pallas_call(
        functools.partial(_sgu_kernel, rt.n_p),
        grid=(rt.n,),
        in_specs=[rt.joint(SG_W, col_u // SG_W), rt.joint(SG_W, col_v // SG_W),
                  const((1, SG_W)), const((1, SG_W)),
                  const((SG_GROUPS, SG_CHUNK, SG_CHUNK)), const((SG_GROUPS, SG_CHUNK, SG_CHUNK)),
                  const((SG_CHUNK, SG_W)), const((SG_CHUNK, SG_W))],
        out_specs=[rt.joint(SG_W), rt.sample(SG_W, single=False)],
        out_shape=[jax.ShapeDtypeStruct((rt.n * t, SG_W), F32),
                   jax.ShapeDtypeStruct((rt.n_s * t, SG_W), F32)],
        compiler_params=_params("arbitrary"),
        name="sgu",
    )(proj, proj, ln_g, ln_b, ws_p, ws_s, bias_p, bias_s)


def _merge_kernel(alpha, n_experts, rt, oap_ref, oas_ref, ob_ref, ga_ref, gb_ref, xp_ref, xs_ref,
                  gtp_ref, gts_ref, scp_ref, scs_ref, shp_ref, shs_ref,
                  pa_ref, pb_ref, wo_ref, lg_ref, lb_ref, rw_ref, rb_ref,
                  x1_ref, h2_ref, ti_ref, tg_ref):
    i = pl.program_id(0)
    is_s = i >= rt.n_p
    oa = _pick(is_s, oap_ref, oas_ref)
    merged = (_sigmoid(ga_ref[...]) * _dot(oa, pa_ref[...])
              + _sigmoid(gb_ref[...]) * _dot(ob_ref[...], pb_ref[...]))
    y = _dot(merged, wo_ref[...])
    x = _pick(is_s, xp_ref, xs_ref)
    x1 = _layer_norm(alpha * x + rt.pick_vec(i, gtp_ref, gts_ref) * y, lg_ref[...], lb_ref[...])
    x1_ref[...] = x1
    h2 = x1 * (1.0 + rt.pick_vec(i, scp_ref, scs_ref)) + rt.pick_vec(i, shp_ref, shs_ref)
    h2_ref[...] = h2
    hs = _split(h2)
    logits = jnp.dot(jnp.concatenate([hs[0], hs[1], hs[0]], axis=1), rw_ref[...],
                     preferred_element_type=F32) + rb_ref[...]
    lane = lax.broadcasted_iota(jnp.int32, logits.shape, 1)
    logits = jnp.where(lane < n_experts, logits, -jnp.inf)
    ti = jnp.zeros(logits.shape, jnp.int32)
    tv = jnp.zeros(logits.shape, F32)
    top = None
    for kk in range(TOP_K):
        m = jnp.max(logits, axis=-1, keepdims=True)
        idx = jnp.min(jnp.where(logits == m, lane.astype(F32), float(LANES)), axis=-1,
                      keepdims=True).astype(jnp.int32)
        if kk == 0:
            top = m
        ti = jnp.where(lane == kk, idx, ti)
        tv = jnp.where(lane == kk, jnp.exp(m - top), tv)
        logits = jnp.where(lane == idx, -jnp.inf, logits)
    ti_ref[...] = ti
    tg_ref[...] = tv / jnp.sum(tv, axis=-1, keepdims=True)


def _merge(alpha, n_experts, rt, oa_p, oa_s, ob, proj, xp, xs, mod_seq, mod_row, pa, pb, wo, lg, lb, rw, rb,
           col_ga, col_gb):
    d = xp.shape[1]
    tm = rt.tm
    rows = rt.n * tm
    const = lambda shape: pl.BlockSpec(shape, lambda i: (0,) * len(shape), pipeline_mode=pl.Buffered(1))
    return pl.pallas_call(
        functools.partial(_merge_kernel, alpha, n_experts, rt),
        grid=(rt.n,),
        in_specs=[rt.prompt(V_W), rt.sample(V_W), rt.joint(SG_W),
                  rt.joint(d, col_ga // d), rt.joint(d, col_gb // d),
                  rt.prompt(d), rt.sample(d),
                  rt.seq_vec(d, MOD_GATE1), rt.row_vec(d, MOD_GATE1),
                  rt.seq_vec(d, MOD_SCALE2), rt.row_vec(d, MOD_SCALE2),
                  rt.seq_vec(d, MOD_SHIFT2), rt.row_vec(d, MOD_SHIFT2),
                  const((V_W, d)), const((SG_W, d)), const((d, d)),
                  const((1, d)), const((1, d)), const((3 * d, LANES)), const((1, LANES))],
        out_specs=[rt.joint(d), rt.joint(d), rt.joint(LANES), rt.joint(LANES)],
        out_shape=[jax.ShapeDtypeStruct((rows, d), F32),
                   jax.ShapeDtypeStruct((rows, d), F32),
                   jax.ShapeDtypeStruct((rows, LANES), jnp.int32),
                   jax.ShapeDtypeStruct((rows, LANES), F32)],
        compiler_params=_params("arbitrary"),
        name="merge",
    )(oa_p, oa_s, ob, proj, proj, xp, xs, mod_seq, mod_row, mod_seq, mod_row, mod_seq, mod_row,
      pa, pb, wo, lg, lb, rw, rb)


FILL_ROWS = (128, 64, 32, 16, 8)


def _dispatch_kernel(tm, n_experts, padlo_ref, padlen_ref, nblk_ref, dest_ref, h_ref, xb_hbm, zero_scr, sem, zsem):
    i = pl.program_id(0)
    tq = h_ref.shape[0]
    nb_max = xb_hbm.shape[0] // tm
    zrows = zero_scr.shape[0]

    def fill(act):
        def per_expert(e, carry):
            lo = padlo_ref[e]
            ln = padlen_ref[e]
            head = (SUBLANES - lo % SUBLANES) % SUBLANES
            for r in range(SUBLANES - 1):
                @pl.when(r < head)
                def _(r=r):
                    act(pltpu.make_async_copy(zero_scr.at[pl.ds(0, 1)], xb_hbm.at[pl.ds(lo + r, 1)], zsem))
            off = lo + head
            rem = ln - head
            for b in FILL_ROWS:
                @pl.when((rem & b) != 0)
                def _(off=off, b=b):
                    dst = xb_hbm.at[pl.ds(pl.multiple_of(off, SUBLANES), b)]
                    act(pltpu.make_async_copy(zero_scr.at[pl.ds(0, b)], dst, zsem))
                off = off + (rem & b)
            return carry

        lax.fori_loop(0, n_experts, per_expert, 0)

        def per_block(blk, carry):
            for part in range(tm // zrows):
                row0 = pl.multiple_of(blk * tm + part * zrows, zrows)
                act(pltpu.make_async_copy(zero_scr, xb_hbm.at[pl.ds(row0, zrows)], zsem))
            return carry

        lax.fori_loop(nblk_ref[0], nb_max, per_block, 0)

    @pl.when(i == 0)
    def _():
        zero_scr[...] = jnp.zeros(zero_scr.shape, F32)
        fill(lambda cp: cp.start())
        fill(lambda cp: cp.wait())

    def start(t, carry):
        for kk in range(TOP_K):
            pltpu.make_async_copy(h_ref.at[pl.ds(t, 1)], xb_hbm.at[pl.ds(dest_ref[0, t * TOP_K + kk], 1)],
                                  sem).start()
        return carry

    lax.fori_loop(0, tq, start, 0, unroll=8)
    for kk in range(TOP_K):
        pltpu.make_async_copy(h_ref, xb_hbm.at[pl.ds(0, tq)], sem).wait()


def _dispatch(padlo, padlen, nblk_used, dest, h2, tm, nb_max):
    t, d = h2.shape
    tq = math.gcd(DISPATCH_TQ, t)
    n_experts = padlo.shape[0]
    grid_spec = pltpu.PrefetchScalarGridSpec(
        num_scalar_prefetch=3,
        grid=(t // tq,),
        in_specs=[pl.BlockSpec((None, 1, tq * TOP_K), lambda i, *_: (i, 0, 0), memory_space=pltpu.SMEM),
                  pl.BlockSpec((tq, d), lambda i, *_: (i, 0))],
        out_specs=pl.BlockSpec(memory_space=pl.ANY),
        scratch_shapes=[pltpu.VMEM((FILL_ROWS[0], d), F32), pltpu.SemaphoreType.DMA(()),
                        pltpu.SemaphoreType.DMA(())],
    )
    return pl.pallas_call(
        functools.partial(_dispatch_kernel, tm, n_experts),
        grid_spec=grid_spec,
        out_shape=jax.ShapeDtypeStruct((nb_max * tm, d), F32),
        compiler_params=pltpu.CompilerParams(dimension_semantics=("arbitrary",), vmem_limit_bytes=VMEM_LIMIT,
                                             has_side_effects=True),
        name="dispatch",
    )(padlo, padlen, nblk_used.reshape(1), dest.reshape(t // tq, 1, tq * TOP_K), h2)


def _moe_kernel(nj, e_ref, j_ref, xb_ref, ob_ref, r_ref, flag_ref, wg_code_ref, wu_code_ref,
                x_ref, wg_ref, wu_ref, wd_ref, bg_ref, bu_ref, bd_ref, o_ref,
                wg_scr, wu_scr, wd_scr, acc_scr):
    s = pl.program_id(0)
    flags = flag_ref[s]
    j = j_ref[s]
    r = r_ref[s]

    @pl.when((flags & 2) != 0)
    def _():
        wg_scr[...] = wg_ref[...].astype(BF16)
        wu_scr[...] = wu_ref[...].astype(BF16)
        wd_scr[...] = wd_ref[...].astype(BF16)

    @pl.when(flags == 0)
    def _():
        o_ref[...] = jnp.zeros(o_ref.shape, F32)

    @pl.when((flags & 1) != 0)
    def _():
        x = x_ref[...].astype(BF16)
        gate = jnp.dot(x, wg_scr[...], preferred_element_type=F32) + bg_ref[...]
        up = jnp.dot(x, wu_scr[...], preferred_element_type=F32) + bu_ref[...]
        gate = jnp.minimum(gate, SWIGLU_LIMIT)
        up = jnp.clip(up, -SWIGLU_LIMIT, SWIGLU_LIMIT)
        act = (up + 1.0) * gate * _sigmoid(SWIGLU_ALPHA * gate)
        y = jnp.dot(act.astype(BF16), wd_scr[...], preferred_element_type=F32)

        @pl.when(j == 0)
        def _():
            acc_scr[r] = y

        @pl.when(jnp.logical_and(j > 0, j < nj - 1))
        def _():
            acc_scr[r] = acc_scr[r] + y

        @pl.when(j == nj - 1)
        def _():
            o_ref[...] = acc_scr[r] + y + bd_ref[...]


def _moe(items, xb, w_gu, b_gu, w_dn, b_dn, tm, tf, subs):
    item_e, item_j, item_xb, item_ob, item_r, item_flag, wg_code, wu_code = items
    n_items = item_e.shape[0]
    nslot, d = xb.shape
    n_exp, _, f2 = w_gu.shape
    f = f2 // 2
    nj = f // tf
    assert nj >= 2
    grid_spec = pltpu.PrefetchScalarGridSpec(
        num_scalar_prefetch=8,
        grid=(n_items,),
        in_specs=[pl.BlockSpec((tm, d), lambda s, e, j, xbk, obk, r, fl, cg, cu: (xbk[s], 0)),
                  pl.BlockSpec((None, d, tf), lambda s, e, j, xbk, obk, r, fl, cg, cu: (cg[s] // nj, 0, cg[s] % nj)),
                  pl.BlockSpec((None, d, tf),
                               lambda s, e, j, xbk, obk, r, fl, cg, cu: (cu[s] // nj, 0, nj + cu[s] % nj)),
                  pl.BlockSpec((None, tf, d), lambda s, e, j, xbk, obk, r, fl, cg, cu: (e[s], j[s], 0)),
                  pl.BlockSpec((None, 1, tf), lambda s, e, j, xbk, obk, r, fl, cg, cu: (e[s], 0, j[s])),
                  pl.BlockSpec((None, 1, tf), lambda s, e, j, xbk, obk, r, fl, cg, cu: (e[s], 0, nj + j[s])),
                  pl.BlockSpec((None, 1, d), lambda s, e, j, xbk, obk, r, fl, cg, cu: (e[s], 0, 0))],
        out_specs=pl.BlockSpec((tm, d), lambda s, e, j, xbk, obk, r, fl, cg, cu: (obk[s], 0)),
        scratch_shapes=[pltpu.VMEM((d, tf), BF16), pltpu.VMEM((d, tf), BF16), pltpu.VMEM((tf, d), BF16),
                        pltpu.VMEM((subs, tm, d), F32)],
    )
    return pl.pallas_call(
        functools.partial(_moe_kernel, nj),
        grid_spec=grid_spec,
        out_shape=jax.ShapeDtypeStruct((nslot, d), F32),
        compiler_params=_params("arbitrary"),
        name="moe",
    )(item_e, item_j, item_xb, item_ob, item_r, item_flag, wg_code, wu_code,
      xb, w_gu, w_gu, w_dn, b_gu.reshape(n_exp, 1, f2), b_gu.reshape(n_exp, 1, f2), b_dn.reshape(n_exp, 1, d))


def _combine_kernel(alpha, rt, dest_ref, destn_ref, yb_hbm, tg_ref, x1_ref, gtp_ref, gts_ref, lg_ref, lb_ref,
                    op_ref, os_ref, buf, sem):
    i = pl.program_id(0)
    last = pl.num_programs(0) - 1
    tq = x1_ref.shape[0]
    slot = i % 2

    def start(ids_ref, sl, t):
        for kk in range(TOP_K):
            pltpu.make_async_copy(yb_hbm.at[pl.ds(ids_ref[0, t * TOP_K + kk], 1)],
                                  buf.at[sl, kk, pl.ds(t, 1)], sem.at[sl]).start()

    def wait(sl):
        for kk in range(TOP_K):
            pltpu.make_async_copy(yb_hbm.at[pl.ds(0, tq)], buf.at[sl, kk], sem.at[sl]).wait()

    @pl.when(i == 0)
    def _():
        lax.fori_loop(0, tq, lambda t, c: (start(dest_ref, 0, t), c)[1], 0, unroll=4)

    for t in range(tq):
        start(destn_ref, 1 - slot, t)
    wait(slot)
    tg = tg_ref[...]
    y = jnp.zeros(x1_ref.shape, F32)
    for kk in range(TOP_K):
        y = y + buf[slot, kk] * tg[:, kk:kk + 1]
    is_s = i >= rt.n_p
    out = _layer_norm(alpha * x1_ref[...] + rt.pick_vec(i, gtp_ref, gts_ref) * y, lg_ref[...], lb_ref[...])

    @pl.when(jnp.logical_not(is_s))
    def _():
        op_ref[...] = out

    @pl.when(is_s)
    def _():
        os_ref[...] = out

    @pl.when(i == last)
    def _():
        wait(1 - slot)


def _combine(alpha, rt, dest, yb, tg, x1, mod_seq, mod_row, lg, lb):
    d = x1.shape[1]
    tq = rt.tm
    ids = dest.reshape(rt.n, 1, tq * TOP_K)
    return pl.pallas_call(
        functools.partial(_combine_kernel, alpha, rt),
        grid=(rt.n,),
        in_specs=[pl.BlockSpec((None, 1, tq * TOP_K), lambda i: (i, 0, 0), memory_space=pltpu.SMEM),
                  pl.BlockSpec((None, 1, tq * TOP_K), lambda i: (jnp.minimum(i + 1, rt.n - 1), 0, 0),
                               memory_space=pltpu.SMEM),
                  pl.BlockSpec(memory_space=pl.ANY),
                  rt.joint(LANES), rt.joint(d), rt.seq_vec(d, MOD_GATE2), rt.row_vec(d, MOD_GATE2),
                  pl.BlockSpec((1, d), lambda i: (0, 0)),
                  pl.BlockSpec((1, d), lambda i: (0, 0))],
        out_specs=[rt.prompt(d), rt.sample(d, single=False)],
        out_shape=[jax.ShapeDtypeStruct((rt.n_p * tq, d), F32), jax.ShapeDtypeStruct((rt.n_s * tq, d), F32)],
        scratch_shapes=[pltpu.VMEM((2, TOP_K, tq, d), F32), pltpu.SemaphoreType.DMA((2,))],
        compiler_params=_params("arbitrary"),
        name="combine",
    )(ids, ids, yb, tg, x1, mod_seq, mod_row, lg, lb)


def _take(table, idx):
    hit = idx[:, None] == jnp.arange(table.shape[0], dtype=jnp.int32)[None, :]
    return jnp.sum(jnp.where(hit, table[None, :], 0), axis=1)


def _routing(ti, n_experts, tm, nj, subs):
    t = ti.shape[0]
    n_assign = t * TOP_K
    nb_max = n_assign // tm + n_experts
    onehot = (ti[:, :, None] == jnp.arange(n_experts, dtype=jnp.int32)[None, None, :]).astype(jnp.int32)
    per_tok = jnp.sum(onehot, axis=1)
    cum = jnp.cumsum(per_tok, axis=0)
    counts = cum[-1]
    rank = jnp.sum(onehot * cum[:, None, :], axis=2) - 1
    nblk = (counts + tm - 1) // tm
    blk_end = jnp.cumsum(nblk)
    blk_start = blk_end - nblk
    dest = jnp.sum(onehot * blk_start[None, None, :], axis=2) * tm + rank
    pad_lo = (blk_start * tm + counts).astype(jnp.int32)
    pad_len = (nblk * tm - counts).astype(jnp.int32)
    total_blk = blk_end[-1]
    blocks = jnp.arange(nb_max, dtype=jnp.int32)
    blk_e = jnp.minimum(jnp.sum((blk_end[None, :] <= blocks[:, None]).astype(jnp.int32), axis=1), n_experts - 1)
    start_b, nblk_b = _take(blk_start, blk_e), _take(nblk, blk_e)
    r_in_e = blocks - start_b
    g0 = start_b + (r_in_e // subs) * subs
    nsub = jnp.minimum(subs, nblk_b - (r_in_e // subs) * subs)
    p = jnp.arange(nb_max * nj, dtype=jnp.int32)
    bp = p // nj
    valid = bp < total_blk
    last = jnp.maximum(total_blk - 1, 0)
    bq = jnp.where(valid, bp, last)
    gq, nq, eq = _take(g0, bq), jnp.maximum(_take(nsub, bq), 1), _take(blk_e, bq)
    local = p - nj * gq
    jq = jnp.where(valid, local // nq, nj - 1)
    rq = jnp.where(valid, local % nq, nq - 1)
    item_xb = gq + rq
    item_ob = jnp.where(valid, jnp.where(jq == nj - 1, gq + rq, gq), bp)
    flags = valid.astype(jnp.int32) + 2 * (valid & (rq == 0)).astype(jnp.int32)
    nxt_blk = gq + nq
    has_next = nxt_blk < total_blk
    last_j = jq == nj - 1
    e_n = jnp.where(last_j & has_next, _take(blk_e, jnp.minimum(nxt_blk, nb_max - 1)), eq)
    j_n = jnp.where(last_j, jnp.where(has_next, 0, jq), jq + 1)
    cur, nxt = eq * nj + jq, e_n * nj + j_n
    wg_code = jnp.where(valid & (rq >= 1), nxt, cur)
    wu_code = jnp.where(valid & (rq >= 2), nxt, cur)
    items = (eq.astype(jnp.int32), jq.astype(jnp.int32), item_xb.astype(jnp.int32),
             item_ob.astype(jnp.int32), rq.astype(jnp.int32), flags,
             wg_code.astype(jnp.int32), wu_code.astype(jnp.int32))
    return dest.astype(jnp.int32), pad_lo, pad_len, total_blk.astype(jnp.int32), nb_max, items


WPREP_TR = 256


def _win_prep_kernel(c_ab, w_hbm, o_ref, buf, sem):
    c = pl.program_id(0)
    tr = o_ref.shape[0]
    o_a = QKV_W + V_W
    c_shift = o_a // tr

    def copy(cc):
        row = jnp.where(cc < c_shift, cc * tr, jnp.where(cc < c_ab, cc * tr + 2 * DN_HEADS, o_a))
        src = w_hbm.at[pl.ds(pl.multiple_of(row, 2 * DN_HEADS), tr)]
        return pltpu.make_async_copy(src, buf.at[cc % 2], sem.at[cc % 2])

    @pl.when(c == 0)
    def _():
        copy(c).start()

    @pl.when(c + 1 <= c_ab)
    def _():
        copy(c + 1).start()

    @pl.when(c <= c_ab)
    def _():
        copy(c).wait()

    x = buf[c % 2]
    rows = lax.broadcasted_iota(jnp.int32, x.shape, 0)
    keep = jnp.logical_or(c < c_ab, jnp.logical_and(c == c_ab, rows < 2 * DN_HEADS))
    o_ref[...] = jnp.where(keep, x, 0.0).astype(BF16)


def _rearranged_w_in(w_in, d):
    dm, nw = w_in.shape
    tr = WPREP_TR
    o_a = QKV_W + V_W
    rest = nw - o_a - 2 * DN_HEADS
    assert o_a % tr == 0 and rest % tr == 0
    nw_out = -(-(nw - 2 * DN_HEADS + LANES) // PROJ_TN) * PROJ_TN
    assert nw_out % tr == 0
    w = pl.pallas_call(
        functools.partial(_win_prep_kernel, (o_a + rest) // tr),
        grid=(nw_out // tr,),
        in_specs=[pl.BlockSpec(memory_space=pl.ANY)],
        out_specs=pl.BlockSpec((tr, dm), lambda c: (c, 0)),
        out_shape=jax.ShapeDtypeStruct((nw_out, dm), BF16),
        scratch_shapes=[pltpu.VMEM((2, tr, dm), F32), pltpu.SemaphoreType.DMA((2,))],
        compiler_params=_params("arbitrary"),
        name="w_in_prep",
    )(w_in.T)
    cols = dict(z=QKV_W, u=QKV_W + V_W, v=QKV_W + V_W + SG_W, ga=QKV_W + V_W + 2 * SG_W,
                gb=QKV_W + V_W + 2 * SG_W + d, ab=QKV_W + V_W + 2 * SG_W + 2 * d)
    return w, cols


def kernel(x_prompt, x_sample, state_conv_qkv, state_delta, c_prompt, c_sample, w_ada, b_ada, w_in, w_conv, a_log, dt_bias, o_norm_g, sg_ln_g, sg_ln_b, w_s, b_s, p_a, p_b, w_out, ln1_g, ln1_b, router_w, router_b, w_gu, b_gu, w_dn, b_dn, ln2_g, ln2_b):
    depth = w_ada.shape[0]
    alpha = float((2 * depth) ** 0.25)
    bp, seq, d = x_prompt.shape
    bs, lt, _ = x_sample.shape
    assert lt == DN_CONV and seq % SG_CHUNK == 0
    n_experts = router_w.shape[2]
    rows_p, rows_s = bp * seq, bs * lt
    xp = x_prompt.reshape(rows_p, d)
    xs = x_sample.reshape(rows_s, d)
    c_all = jnp.concatenate([jnp.repeat(c_sample, lt, axis=0), c_prompt,
                             jnp.zeros((-(rows_s + bp) % SUBLANES, d), F32)], axis=0)
    tile = lambda cap: _Rows(math.gcd(math.gcd(cap, rows_s), seq), rows_p, rows_s, seq)
    rt_proj, rt_sgu, rt_merge, rt_comb = tile(PROJ_TM), tile(SGU_TM), tile(MERGE_TM), tile(COMB_TQ)
    outs = dict(conv_p=[], delta_p=[], conv_s=[], delta_s=[], vrows=[])

    for l in range(depth):
        mod = _adaln(c_all, w_ada[l], b_ada[l])

        w_r, col = _rearranged_w_in(w_in[l], d)
        proj = _inproj(rt_proj, xp, xs, mod, mod, w_r)

        alog = jnp.pad(a_log[l:l + 1], ((0, 0), (0, LANES - DN_HEADS)))
        dtb = jnp.pad(dt_bias[l:l + 1], ((0, 0), (0, LANES - DN_HEADS)))
        og = o_norm_g[l:l + 1]
        conv0 = jnp.zeros((bp, DN_CONV - 1, QKV_W), F32)
        s0 = jnp.zeros((bp, DN_HEADS, DN_DK, DN_DV), F32)
        oa_p, conv_p, delta_p = _gdn_prompt(proj, conv0, s0, w_conv[l], alog, dtb, og, bp, seq,
                                            col['z'], col['ab'])
        oa_s, conv_s, delta_s = _gdn_sample(proj, rows_p, state_conv_qkv[l], state_delta[l], w_conv[l],
                                            alog, dtb, og, bs, col['z'], col['ab'])

        assert rt_sgu.tm % SG_CHUNK == 0 and SG_CHUNK % lt == 0
        ws_p = jnp.tril(w_s[l]).astype(BF16)
        bias_p = jnp.repeat(b_s[l].T, SG_CH, axis=1)
        eye = jnp.eye(SG_CHUNK // lt, dtype=F32)
        ws_s = jnp.stack([jnp.kron(eye, jnp.tril(w_s[l, g, :lt, :lt])) for g in range(SG_GROUPS)]).astype(BF16)
        bias_s = jnp.tile(jnp.repeat(b_s[l, :, :lt].T, SG_CH, axis=1), (SG_CHUNK // lt, 1))
        ob, vn_s = _sgu(rt_sgu, proj, sg_ln_g[l:l + 1], sg_ln_b[l:l + 1],
                        ws_p, ws_s, bias_p, bias_s, col['u'], col['v'])

        pa, pb, wo = p_a[l].astype(BF16), p_b[l].astype(BF16), w_out[l].astype(BF16)
        rws = _split(jnp.pad(router_w[l], ((0, 0), (0, LANES - n_experts))))
        rw = jnp.concatenate([rws[0], rws[0], rws[1]], axis=0)
        rb = jnp.pad(router_b[l:l + 1], ((0, 0), (0, LANES - n_experts)))
        x1, h2, ti, tg = _merge(alpha, n_experts, rt_merge, oa_p, oa_s, ob, proj, xp, xs,
                                mod, mod,
                                pa, pb, wo, ln1_g[l:l + 1], ln1_b[l:l + 1], rw, rb,
                                col['ga'], col['gb'])

        nj = w_dn.shape[2] // MOE_TF
        dest, pad_lo, pad_len, nblk_used, nb_max, items = _routing(ti[:, :TOP_K], n_experts, MOE_TM, nj, MOE_SUBS)
        xb = _dispatch(pad_lo, pad_len, nblk_used, dest, h2, MOE_TM, nb_max)
        yb = _moe(items, xb, w_gu[l], b_gu[l], w_dn[l], b_dn[l], MOE_TM, MOE_TF, MOE_SUBS)
        xp, xs = _combine(alpha, rt_comb, dest, yb, tg, x1, mod, mod,
                          ln2_g[l:l + 1], ln2_b[l:l + 1])

        outs['conv_p'].append(conv_p)
        outs['delta_p'].append(delta_p)
        outs['conv_s'].append(conv_s)
        outs['delta_s'].append(delta_s)
        outs['vrows'].append(vn_s.reshape(bs, lt, SG_W))

    return (xp.reshape(bp, seq, d), xs.reshape(bs, lt, d),
            jnp.stack(outs['conv_p']), jnp.stack(outs['delta_p']),
            jnp.stack(outs['conv_s']), jnp.stack(outs['delta_s']), jnp.stack(outs['vrows']))
```

```python
import functools
import math

import jax
import jax.numpy as jnp
from jax import lax
from jax.experimental import pallas as pl
from jax.experimental.pallas import tpu as pltpu

F32 = jnp.float32
BF16 = jnp.bfloat16

DN_HEADS = 8
DN_DK = 128
DN_DV = 128
DN_CONV = 4
DN_CHUNK = 64
SG_GROUPS = 8
SG_CH = 128
SG_CHUNK = 128
TOP_K = 4
SWIGLU_LIMIT = 7.0
SWIGLU_ALPHA = 1.702
LN_EPS = 1e-5
NORM_EPS = 1e-6
QK_W = DN_HEADS * DN_DK
V_W = DN_HEADS * DN_DV
QKV_W = 2 * QK_W + V_W
SG_W = SG_GROUPS * SG_CH

LANES = 128
SUBLANES = 8
VMEM_LIMIT = 56 * 1024 * 1024

PROJ_TM = 512
PROJ_TN = 1536
MERGE_TM = 256
MOE_TM = 256
MOE_TF = 512
MOE_SUBS = 6
COMB_TQ = 256
DISPATCH_TQ = 256
SGU_TM = 512
ADALN_TN = 1024
INV_BLOCK = 16
SAMPLE_BT = 4
SAMPLE_TP = 8
GDN_HG = 4

MOD_SHIFT1, MOD_SCALE1, MOD_GATE1, MOD_SHIFT2, MOD_SCALE2, MOD_GATE2 = range(6)
MOD_PARTS = 6


def _sigmoid(x):
    return 0.5 * (jnp.tanh(0.5 * x) + 1.0)


def _silu(x):
    return x * _sigmoid(x)


def _dot(a, b):
    return jnp.dot(a.astype(BF16), b.astype(BF16), preferred_element_type=F32)


def _dot_nt(a, b):
    return lax.dot_general(a.astype(BF16), b.astype(BF16), (((1,), (1,)), ((), ())),
                           preferred_element_type=F32)


def _split3(x):
    hi = x.astype(BF16)
    r1 = x - hi.astype(F32)
    mid = r1.astype(BF16)
    return hi, mid, (r1 - mid.astype(F32)).astype(BF16)


def _mask_sums(masks, col, n):
    lane = lax.broadcasted_iota(jnp.int32, (n, LANES), 1)
    hi, mid, lo = [p.astype(F32) for p in _split3(col)]
    parts = jnp.where(lane == 0, hi, jnp.where(lane == 1, mid, jnp.where(lane == 2, lo, 0.0)))
    y = jnp.dot(masks, parts.astype(BF16), preferred_element_type=F32)
    y = y[:, 0:1] + y[:, 1:2] + y[:, 2:3]
    return [y[i * n:(i + 1) * n] for i in range(masks.shape[0] // n)]


def _layer_norm(x, g, b):
    mu = jnp.mean(x, axis=-1, keepdims=True)
    xc = x - mu
    var = jnp.mean(xc * xc, axis=-1, keepdims=True)
    return xc * lax.rsqrt(var + LN_EPS) * g + b


def _params(*sem):
    return pltpu.CompilerParams(dimension_semantics=sem, vmem_limit_bytes=VMEM_LIMIT)


class _Rows:
    def __init__(self, tm, rows_p, rows_s, seq):
        assert rows_p % tm == 0 and rows_s % tm == 0 and seq % tm == 0, (tm, rows_p, rows_s, seq)
        self.tm, self.n_p, self.n_s = tm, rows_p // tm, rows_s // tm
        self.tiles_per_seq = seq // tm
        self.bp = rows_p // seq
        assert rows_s % SUBLANES == 0
        self.rows_s = rows_s

    @property
    def n(self):
        return self.n_p + self.n_s

    def prompt(self, width, col=0):
        return pl.BlockSpec((self.tm, width), lambda i, *_: (jnp.minimum(i, self.n_p - 1), col))

    def sample(self, width, col=0, single=True):
        mode = dict(pipeline_mode=pl.Buffered(1)) if single else {}
        return pl.BlockSpec((self.tm, width), lambda i, *_: (jnp.maximum(i - self.n_p, 0), col), **mode)

    def joint(self, width, col=0):
        return pl.BlockSpec((self.tm, width), lambda i, *_: (i, col))

    def _seq(self, i):
        return jnp.minimum(i // self.tiles_per_seq, self.bp - 1)

    def seq_vec(self, d, part):
        return pl.BlockSpec((SUBLANES, d), lambda i, *_: ((self.rows_s + self._seq(i)) // SUBLANES, part))

    def row_vec(self, d, part):
        return pl.BlockSpec((self.tm, d), lambda i, *_: (jnp.maximum(i - self.n_p, 0), part),
                            pipeline_mode=pl.Buffered(1))

    def pick_vec(self, i, seq_ref, row_ref):
        prompt = seq_ref[pl.ds(self._seq(i) % SUBLANES, 1), :]
        return jnp.where(i >= self.n_p, row_ref[...], prompt)


def _pick(is_sample, prompt_ref, sample_ref):
    return jnp.where(is_sample, sample_ref[...], prompt_ref[...])


def _adaln_kernel(c_ref, w_ref, b_ref, o_ref):
    o_ref[...] = _dot(_silu(c_ref[...]), w_ref[...]) + b_ref[...]


def _adaln(c, w, b):
    rows, d = c.shape
    n = w.shape[1]
    tn = ADALN_TN
    return pl.pallas_call(
        _adaln_kernel,
        grid=(n // tn,),
        in_specs=[pl.BlockSpec((rows, d), lambda j: (0, 0)),
                  pl.BlockSpec((d, tn), lambda j: (0, j)),
                  pl.BlockSpec((1, tn), lambda j: (0, j))],
        out_specs=pl.BlockSpec((rows, tn), lambda j: (0, j)),
        out_shape=jax.ShapeDtypeStruct((rows, n), F32),
        compiler_params=_params("arbitrary"),
        name="adaln",
    )(c, w, b.reshape(1, n))


def _inproj_kernel(rt, xp_ref, xs_ref, scp_ref, scs_ref, shp_ref, shs_ref, w_ref, o_ref, h_scr):
    @pl.when(pl.program_id(1) == 0)
    def _():
        i = pl.program_id(0)
        x = _pick(i >= rt.n_p, xp_ref, xs_ref)
        h_scr[...] = (x * (1.0 + rt.pick_vec(i, scp_ref, scs_ref)) + rt.pick_vec(i, shp_ref, shs_ref)).astype(BF16)

    o_ref[...] = lax.dot_general(h_scr[...], w_ref[...], (((1,), (1,)), ((), ())), preferred_element_type=F32)


def _inproj(rt, xp, xs, mod_seq, mod_row, w):
    d = xp.shape[1]
    nw = w.shape[0]
    tm = rt.tm
    return pl.pallas_call(
        functools.partial(_inproj_kernel, rt),
        grid=(rt.n, nw // PROJ_TN),
        in_specs=[rt.prompt(d), rt.sample(d), rt.seq_vec(d, MOD_SCALE1), rt.row_vec(d, MOD_SCALE1),
                  rt.seq_vec(d, MOD_SHIFT1), rt.row_vec(d, MOD_SHIFT1),
                  pl.BlockSpec((PROJ_TN, d), lambda i, j: (j, 0))],
        out_specs=pl.BlockSpec((tm, PROJ_TN), lambda i, j: (i, j)),
        out_shape=jax.ShapeDtypeStruct((rt.n * tm, nw), F32),
        scratch_shapes=[pltpu.VMEM((tm, d), BF16)],
        compiler_params=_params("arbitrary", "arbitrary"),
        name="inproj",
    )(xp, xs, mod_seq, mod_row, mod_seq, mod_row, w)


def _softplus(x):
    return jnp.maximum(x, 0.0) + jnp.log1p(jnp.exp(-jnp.abs(x)))


def _split(a):
    hi = a.astype(BF16)
    lo = (a - hi.astype(F32)).astype(BF16)
    return hi, lo


def _map(f, *lists):
    return [f(*args) for args in zip(*lists)]


def _unit_lower_inverses_minus_eye(mats, rows, cols):
    same = (rows // INV_BLOCK) == (cols // INV_BLOCK)
    n = [jnp.where(same, a, 0.0) for a in mats]
    b = _map(lambda a, x: a - x, mats, n)
    n2 = _map(lambda x: _dot(x, x), n)
    n4 = _map(lambda x: _dot(x, x), n2)
    n8 = _map(lambda x: _dot(x, x), n4)
    r = [-x for x in n]
    r = _map(lambda x, p: x + p + _dot(x, p), r, n2)
    r = _map(lambda x, p: x + p + _dot(x, p), r, n4)
    dm = _map(lambda x, p: x + p + _dot(x, p), r, n8)
    m = _map(lambda x, y: y + _dot(x, y), dm, b)
    m2 = _map(lambda x: _dot(x, x), m)
    xm = _map(lambda x, p: x + p + _dot(p, x), dm, m2)
    return _map(lambda x, p: x - p - _dot(p, x), xm, m)


def _gdn_prompt_kernel(qkv_ref, z_ref, ab_ref, hist_ref, s0_ref, wconv_ref, alog_ref, dt_ref, og_ref,
                       o_ref, conv_ref, s_ref, xp_scr):
    n = pl.program_id(1)
    c = DN_CHUNK
    pad = SUBLANES
    nr = GDN_HG * c
    ngrp = DN_HEADS // GDN_HG

    @pl.when(n == 0)
    def _():
        xp_scr[pl.ds(0, pad), :] = jnp.zeros((pad, QKV_W), F32)
        xp_scr[pl.ds(pad - (DN_CONV - 1), DN_CONV - 1), :] = hist_ref[...]
        s_ref[...] = s0_ref[...]

    x = qkv_ref[...]
    xp_scr[pl.ds(pad, c), :] = x
    y = jnp.zeros((c, QKV_W), F32)
    for j in range(DN_CONV):
        y = y + xp_scr[pl.ds(pad - (DN_CONV - 1) + j, c), :] * wconv_ref[pl.ds(j, 1), :]
    y = _silu(y)
    tail = xp_scr[pl.ds(c + pad - (DN_CONV - 1), DN_CONV - 1), :]
    conv_ref[...] = tail
    xp_scr[pl.ds(pad - (DN_CONV - 1), DN_CONV - 1), :] = tail

    ab = ab_ref[...]
    g = -jnp.exp(alog_ref[...]) * _softplus(ab + dt_ref[...])
    beta_all = _sigmoid(ab)
    r64 = lax.broadcasted_iota(jnp.int32, (c, c), 0)
    c64 = lax.broadcasted_iota(jnp.int32, (c, c), 1)
    tri = (r64 >= c64).astype(BF16)
    gc = sum(jnp.dot(tri, part, preferred_element_type=F32) for part in _split3(g))
    z = z_ref[...]
    og = og_ref[...]

    rows = lax.broadcasted_iota(jnp.int32, (nr, nr), 0)
    cols = lax.broadcasted_iota(jnp.int32, (nr, nr), 1)
    same = (rows // c) == (cols // c)
    causal = same & (rows >= cols)
    strict = same & (rows > cols)
    rowhead = lax.broadcasted_iota(jnp.int32, (nr, 1), 0) // c
    groups = [range(grp * GDN_HG, (grp + 1) * GDN_HG) for grp in range(ngrp)]

    def stack(heads, off, width):
        return jnp.concatenate([y[:, off + h * width:off + (h + 1) * width] for h in heads], axis=0)

    def l2n(t):
        return t * lax.rsqrt(jnp.sum(t * t, axis=-1, keepdims=True) + NORM_EPS)

    q = [l2n(stack(hs, 0, DN_DK)) * (DN_DK ** -0.5) for hs in groups]
    k = [l2n(stack(hs, QK_W, DN_DK)) for hs in groups]
    v = [stack(hs, 2 * QK_W, DN_DV) for hs in groups]
    beta = [jnp.concatenate([beta_all[:, DN_HEADS + h:DN_HEADS + h + 1] for h in hs], axis=0) for hs in groups]
    gcf = [jnp.concatenate([jnp.broadcast_to(gc[:, h:h + 1], (c, LANES)) for h in hs], axis=0) for hs in groups]
    gcc = [t[:, 0:1] for t in gcf]
    grow = [t.T[0:1, :] for t in gcf]
    glast = [[gc[c - 1:c, h:h + 1] for h in hs] for hs in groups]
    gtot = [jnp.concatenate([jnp.broadcast_to(t, (c, 1)) for t in gl], axis=0) for gl in glast]
    decay = _map(lambda a, b: jnp.where(causal, jnp.exp(jnp.where(causal, a - b, 0.0)), 0.0), gcc, grow)
    egc = _map(jnp.exp, gcc)
    kb = _map(lambda a, b: a * b, k, beta)
    a_mat = _map(lambda a, b, dd: jnp.where(strict, _dot_nt(a, b) * dd, 0.0), kb, k, decay)
    tm1 = _unit_lower_inverses_minus_eye(a_mat, rows, cols)
    u = _map(lambda t, a, b: a * b + _dot(t, a * b), tm1, v, beta)
    w = _map(lambda t, a, b: a * b + _dot(t, a * b), tm1, kb, egc)
    attn = _map(lambda a, b, dd: _dot_nt(a, b) * dd, q, k, decay)
    qg = _map(lambda a, b: a * b, q, egc)
    kd_t = _map(lambda a, b, cc: (a * jnp.exp(b - cc)).T, k, gtot, gcc)

    for gi, hs in enumerate(groups):
        v_news, qss = [], []
        for hl, h in enumerate(hs):
            sl = slice(hl * c, (hl + 1) * c)
            rs = _dot(jnp.concatenate([w[gi][sl], qg[gi][sl]], axis=0), s_ref[h])
            v_news.append(u[gi][sl] - rs[:c])
            qss.append(rs[c:])
        v_new = jnp.concatenate(v_news, axis=0)
        o = jnp.concatenate(qss, axis=0) + _dot(attn[gi], v_new)
        for hl, h in enumerate(hs):
            upd = _dot(kd_t[gi], jnp.where(rowhead == hl, v_new, 0.0))
            s_ref[h] = s_ref[h] * jnp.exp(glast[gi][hl]) + upd
        o = o * lax.rsqrt(jnp.mean(o * o, axis=-1, keepdims=True) + NORM_EPS) * og
        for hl, h in enumerate(hs):
            o_ref[:, h * DN_DV:(h + 1) * DN_DV] = o[hl * c:(hl + 1) * c] * _silu(z[:, h * DN_DV:(h + 1) * DN_DV])


def _gdn_prompt(proj, hist, s0, wconv, alog, dtb, og, batch, seq, col_z, col_ab):
    c = DN_CHUNK
    nchunk = seq // c
    rows = batch * seq
    row = lambda b, n: b * nchunk + n
    return pl.pallas_call(
        _gdn_prompt_kernel,
        grid=(batch, nchunk),
        in_specs=[pl.BlockSpec((c, QKV_W), lambda b, n: (row(b, n), 0)),
                  pl.BlockSpec((c, V_W), lambda b, n: (row(b, n), col_z // V_W)),
                  pl.BlockSpec((c, LANES), lambda b, n: (row(b, n), col_ab // LANES)),
                  pl.BlockSpec((None, DN_CONV - 1, QKV_W), lambda b, n: (b, 0, 0)),
                  pl.BlockSpec((None, DN_HEADS, DN_DK, DN_DV), lambda b, n: (b, 0, 0, 0)),
                  pl.BlockSpec((DN_CONV, QKV_W), lambda b, n: (0, 0)),
                  pl.BlockSpec((1, LANES), lambda b, n: (0, 0)),
                  pl.BlockSpec((1, LANES), lambda b, n: (0, 0)),
                  pl.BlockSpec((1, DN_DV), lambda b, n: (0, 0))],
        out_specs=[pl.BlockSpec((c, V_W), lambda b, n: (row(b, n), 0)),
                   pl.BlockSpec((None, DN_CONV - 1, QKV_W), lambda b, n: (b, 0, 0)),
                   pl.BlockSpec((None, DN_HEADS, DN_DK, DN_DV), lambda b, n: (b, 0, 0, 0))],
        out_shape=[jax.ShapeDtypeStruct((rows, V_W), F32),
                   jax.ShapeDtypeStruct((batch, DN_CONV - 1, QKV_W), F32),
                   jax.ShapeDtypeStruct((batch, DN_HEADS, DN_DK, DN_DV), F32)],
        scratch_shapes=[pltpu.VMEM((c + SUBLANES, QKV_W), F32)],
        compiler_params=_params("arbitrary", "arbitrary"),
        name="gdn_prompt",
    )(proj, proj, proj, hist, s0, wconv, alog, dtb, og)


def _gdn_sample_kernel(qkv_ref, z_ref, ab_ref, hist_ref, s0_ref, wconv_ref, alog_ref, dt_ref, og_ref,
                       o_ref, conv_ref, s_ref, xp_scr, ab_scr, wq_scr, r_scr, kdt_scr, vn_scr, gl_scr):
    bt, tp, lt = SAMPLE_BT, SAMPLE_TP, DN_CONV
    nblk = DN_HEADS * bt
    nrow = nblk * tp
    hist_rows = DN_CONV - 1

    xp_scr[...] = jnp.zeros(xp_scr.shape, F32)
    ab_scr[...] = jnp.zeros(ab_scr.shape, F32)
    ys = []
    for b in range(bt):
        xp_scr[b, pl.ds(SUBLANES - hist_rows, hist_rows), :] = hist_ref[b]
        xp_scr[b, pl.ds(SUBLANES, lt), :] = qkv_ref[pl.ds(b * lt, lt), :]
        yb = jnp.zeros((tp, QKV_W), F32)
        for j in range(DN_CONV):
            yb = yb + xp_scr[b, pl.ds(SUBLANES - hist_rows + j, tp), :] * wconv_ref[pl.ds(j, 1), :]
        ys.append(_silu(yb))
        conv_ref[b] = xp_scr[b, pl.ds(SUBLANES + lt - hist_rows, hist_rows), :]
        ab_scr[b, pl.ds(0, lt), :] = ab_ref[pl.ds(b * lt, lt), :]
    y = jnp.concatenate(ys, axis=0)
    ab = jnp.concatenate([ab_scr[b] for b in range(bt)], axis=0)
    tok = lax.broadcasted_iota(jnp.int32, (bt * tp, 1), 0) % tp
    real = tok < lt
    g_all = jnp.where(real, -jnp.exp(alog_ref[...]) * _softplus(ab + dt_ref[...]), 0.0)
    beta_all = jnp.where(real, _sigmoid(ab), 0.0)

    def heads_to_rows(t, off, width):
        return jnp.concatenate([t[:, off + h * width:off + (h + 1) * width] for h in range(DN_HEADS)], axis=0)

    realr = jnp.concatenate([real] * DN_HEADS, axis=0)
    q = jnp.where(realr, heads_to_rows(y, 0, DN_DK), 0.0)
    k = jnp.where(realr, heads_to_rows(y, QK_W, DN_DK), 0.0)
    v = jnp.where(realr, heads_to_rows(y, 2 * QK_W, DN_DV), 0.0)
    gcol = jnp.concatenate([g_all[:, h:h + 1] for h in range(DN_HEADS)], axis=0)
    beta = jnp.concatenate([beta_all[:, DN_HEADS + h:DN_HEADS + h + 1] for h in range(DN_HEADS)], axis=0)
    q = q * lax.rsqrt(jnp.sum(q * q, axis=-1, keepdims=True) + NORM_EPS) * (DN_DK ** -0.5)
    k = k * lax.rsqrt(jnp.sum(k * k, axis=-1, keepdims=True) + NORM_EPS)

    rows = lax.broadcasted_iota(jnp.int32, (nrow, nrow), 0)
    cols = lax.broadcasted_iota(jnp.int32, (nrow, nrow), 1)
    same = (rows // tp) == (cols // tp)
    causal = same & (rows >= cols)
    strict = same & (rows > cols)
    gcc, gtot = _mask_sums(jnp.concatenate([causal.astype(BF16), same.astype(BF16)], axis=0), gcol, nrow)
    gtot = jnp.broadcast_to(gtot, (nrow, LANES))
    grow = jnp.broadcast_to(gcc, (nrow, LANES)).T[0:1, :]
    decay = jnp.where(causal, jnp.exp(jnp.where(causal, gcc - grow, 0.0)), 0.0)
    egc = jnp.exp(gcc)
    kb = k * beta
    a_mat = jnp.where(strict, _dot_nt(kb, k) * decay, 0.0)
    a2 = _dot(a_mat, a_mat)
    tm1 = a2 - a_mat - _dot(a_mat, a2)
    u = v * beta + _dot(tm1, v * beta)
    w = kb * egc + _dot(tm1, kb * egc)
    attn = _dot_nt(q, k) * decay
    qg = q * egc
    for i in range(nblk):
        wq_scr[pl.ds(2 * tp * i, tp), :] = w[i * tp:(i + 1) * tp, :]
        wq_scr[pl.ds(2 * tp * i + tp, tp), :] = qg[i * tp:(i + 1) * tp, :]
    kdt_scr[...] = (k * jnp.exp(gtot[:, 0:1] - gcc)).T
    gl_scr[...] = jnp.exp(gtot)

    def read_state(i, carry):
        r0 = pl.multiple_of(i * 2 * tp, 2 * tp)
        r_scr[pl.ds(r0, 2 * tp), :] = _dot(wq_scr[pl.ds(r0, 2 * tp), :], s0_ref[i % bt, i // bt])
        return carry

    lax.fori_loop(0, nblk, read_state, 0, unroll=16)
    ws = jnp.concatenate([r_scr[pl.ds(2 * tp * i, tp), :] for i in range(nblk)], axis=0)
    qs = jnp.concatenate([r_scr[pl.ds(2 * tp * i + tp, tp), :] for i in range(nblk)], axis=0)
    v_new = u - ws
    vn_scr[...] = v_new
    o = qs + _dot(attn, v_new)
    blockid = lax.broadcasted_iota(jnp.int32, (nrow, 1), 0) // tp

    def write_state(i, carry):
        r0 = pl.multiple_of(i * tp, tp)
        upd = _dot(kdt_scr[...], jnp.where(blockid == i, vn_scr[...], 0.0))
        s_ref[i % bt, i // bt] = s0_ref[i % bt, i // bt] * gl_scr[pl.ds(r0, 1), :] + upd
        return carry

    lax.fori_loop(0, nblk, write_state, 0, unroll=16)

    og = og_ref[...]
    o = o * lax.rsqrt(jnp.mean(o * o, axis=-1, keepdims=True) + NORM_EPS) * og
    for b in range(bt):
        for h in range(DN_HEADS):
            zbh = z_ref[pl.ds(b * lt, lt), h * DN_DV:(h + 1) * DN_DV]
            blk = o[(h * bt + b) * tp:(h * bt + b) * tp + lt, :]
            o_ref[pl.ds(b * lt, lt), h * DN_DV:(h + 1) * DN_DV] = blk * _silu(zbh)


def _gdn_sample(proj, row0, hist, s0, wconv, alog, dtb, og, batch, col_z, col_ab):
    bt, tp, lt = SAMPLE_BT, SAMPLE_TP, DN_CONV
    rows = batch * lt
    nrow = DN_HEADS * bt * tp
    blk = bt * lt
    assert row0 % blk == 0 and batch % bt == 0
    b0 = row0 // blk
    return pl.pallas_call(
        _gdn_sample_kernel,
        grid=(batch // bt,),
        in_specs=[pl.BlockSpec((blk, QKV_W), lambda i: (b0 + i, 0)),
                  pl.BlockSpec((blk, V_W), lambda i: (b0 + i, col_z // V_W)),
                  pl.BlockSpec((blk, LANES), lambda i: (b0 + i, col_ab // LANES)),
                  pl.BlockSpec((bt, DN_CONV - 1, QKV_W), lambda i: (i, 0, 0)),
                  pl.BlockSpec((bt, DN_HEADS, DN_DK, DN_DV), lambda i: (i, 0, 0, 0)),
                  pl.BlockSpec((DN_CONV, QKV_W), lambda i: (0, 0)),
                  pl.BlockSpec((1, LANES), lambda i: (0, 0)),
                  pl.BlockSpec((1, LANES), lambda i: (0, 0)),
                  pl.BlockSpec((1, DN_DV), lambda i: (0, 0))],
        out_specs=[pl.BlockSpec((blk, V_W), lambda i: (i, 0)),
                   pl.BlockSpec((bt, DN_CONV - 1, QKV_W), lambda i: (i, 0, 0)),
                   pl.BlockSpec((bt, DN_HEADS, DN_DK, DN_DV), lambda i: (i, 0, 0, 0))],
        out_shape=[jax.ShapeDtypeStruct((rows, V_W), F32),
                   jax.ShapeDtypeStruct((batch, DN_CONV - 1, QKV_W), F32),
                   jax.ShapeDtypeStruct((batch, DN_HEADS, DN_DK, DN_DV), F32)],
        scratch_shapes=[pltpu.VMEM((bt, SUBLANES + tp, QKV_W), F32),
                        pltpu.VMEM((bt, tp, LANES), F32),
                        pltpu.VMEM((2 * nrow, DN_DK), F32),
                        pltpu.VMEM((2 * nrow, DN_DV), F32),
                        pltpu.VMEM((DN_DK, nrow), F32),
                        pltpu.VMEM((nrow, DN_DV), F32),
                        pltpu.VMEM((nrow, LANES), F32)],
        compiler_params=_params("arbitrary"),
        name="gdn_sample",
    )(proj, proj, proj, hist, s0, wconv, alog, dtb, og)


def _sgu_kernel(n_p, u_ref, v_ref, g_ref, b_ref, wsp_ref, wss_ref, bp_ref, bs_ref, o_ref, vn_ref):
    is_s = pl.program_id(0) >= n_p
    u = jax.nn.gelu(u_ref[...])
    vn = _layer_norm(jax.nn.gelu(v_ref[...]), g_ref[...], b_ref[...])
    vn_ref[...] = vn
    bias = _pick(is_s, bp_ref, bs_ref)
    for g in range(SG_GROUPS):
        sl = slice(g * SG_CH, (g + 1) * SG_CH)
        ws = jnp.where(is_s, wss_ref[g], wsp_ref[g])
        for c in range(u.shape[0] // SG_CHUNK):
            rows = slice(c * SG_CHUNK, (c + 1) * SG_CHUNK)
            mixed = _dot(ws, vn[rows, sl]) + bias[:, sl]
            o_ref[rows, sl] = u[rows, sl] * mixed


def _sgu(rt, proj, ln_g, ln_b, ws_p, ws_s, bias_p, bias_s, col_u, col_v):
    t = rt.tm
    const = lambda shape: pl.BlockSpec(shape, lambda i: (0,) * len(shape))
    return pl.pallas_call(
        functools.partial(_sgu_kernel, rt.n_p),
        grid=(rt.n,),
        in_specs=[rt.joint(SG_W, col_u // SG_W), rt.joint(SG_W, col_v // SG_W),
                  const((1, SG_W)), const((1, SG_W)),
                  const((SG_GROUPS, SG_CHUNK, SG_CHUNK)), const((SG_GROUPS, SG_CHUNK, SG_CHUNK)),
                  const((SG_CHUNK, SG_W)), const((SG_CHUNK, SG_W))],
        out_specs=[rt.joint(SG_W), rt.sample(SG_W, single=False)],
        out_shape=[jax.ShapeDtypeStruct((rt.n * t, SG_W), F32),
                   jax.ShapeDtypeStruct((rt.n_s * t, SG_W), F32)],
        compiler_params=_params("arbitrary"),
        name="sgu",
    )(proj, proj, ln_g, ln_b, ws_p, ws_s, bias_p, bias_s)


def _merge_kernel(alpha, n_experts, rt, oap_ref, oas_ref, ob_ref, ga_ref, gb_ref, xp_ref, xs_ref,
                  gtp_ref, gts_ref, scp_ref, scs_ref, shp_ref, shs_ref,
                  pa_ref, pb_ref, wo_ref, lg_ref, lb_ref, rw_ref, rb_ref,
                  x1_ref, h2_ref, ti_ref, tg_ref):
    i = pl.program_id(0)
    is_s = i >= rt.n_p
    oa = _pick(is_s, oap_ref, oas_ref)
    merged = (_sigmoid(ga_ref[...]) * _dot(oa, pa_ref[...])
              + _sigmoid(gb_ref[...]) * _dot(ob_ref[...], pb_ref[...]))
    y = _dot(merged, wo_ref[...])
    x = _pick(is_s, xp_ref, xs_ref)
    x1 = _layer_norm(alpha * x + rt.pick_vec(i, gtp_ref, gts_ref) * y, lg_ref[...], lb_ref[...])
    x1_ref[...] = x1
    h2 = x1 * (1.0 + rt.pick_vec(i, scp_ref, scs_ref)) + rt.pick_vec(i, shp_ref, shs_ref)
    h2_ref[...] = h2
    hs = _split(h2)
    logits = jnp.dot(jnp.concatenate([hs[0], hs[1], hs[0]], axis=1), rw_ref[...],
                     preferred_element_type=F32) + rb_ref[...]
    lane = lax.broadcasted_iota(jnp.int32, logits.shape, 1)
    logits = jnp.where(lane < n_experts, logits, -jnp.inf)
    ti = jnp.zeros(logits.shape, jnp.int32)
    tv = jnp.zeros(logits.shape, F32)
    top = None
    for kk in range(TOP_K):
        m = jnp.max(logits, axis=-1, keepdims=True)
        idx = jnp.min(jnp.where(logits == m, lane.astype(F32), float(LANES)), axis=-1,
                      keepdims=True).astype(jnp.int32)
        if kk == 0:
            top = m
        ti = jnp.where(lane == kk, idx, ti)
        tv = jnp.where(lane == kk, jnp.exp(m - top), tv)
        logits = jnp.where(lane == idx, -jnp.inf, logits)
    ti_ref[...] = ti
    tg_ref[...] = tv / jnp.sum(tv, axis=-1, keepdims=True)


def _merge(alpha, n_experts, rt, oa_p, oa_s, ob, proj, xp, xs, mod_seq, mod_row, pa, pb, wo, lg, lb, rw, rb,
           col_ga, col_gb):
    d = xp.shape[1]
    tm = rt.tm
    rows = rt.n * tm
    const = lambda shape: pl.BlockSpec(shape, lambda i: (0,) * len(shape), pipeline_mode=pl.Buffered(1))
    return pl.pallas_call(
        functools.partial(_merge_kernel, alpha, n_experts, rt),
        grid=(rt.n,),
        in_specs=[rt.prompt(V_W), rt.sample(V_W), rt.joint(SG_W),
                  rt.joint(d, col_ga // d), rt.joint(d, col_gb // d),
                  rt.prompt(d), rt.sample(d),
                  rt.seq_vec(d, MOD_GATE1), rt.row_vec(d, MOD_GATE1),
                  rt.seq_vec(d, MOD_SCALE2), rt.row_vec(d, MOD_SCALE2),
                  rt.seq_vec(d, MOD_SHIFT2), rt.row_vec(d, MOD_SHIFT2),
                  const((V_W, d)), const((SG_W, d)), const((d, d)),
                  const((1, d)), const((1, d)), const((3 * d, LANES)), const((1, LANES))],
        out_specs=[rt.joint(d), rt.joint(d), rt.joint(LANES), rt.joint(LANES)],
        out_shape=[jax.ShapeDtypeStruct((rows, d), F32),
                   jax.ShapeDtypeStruct((rows, d), F32),
                   jax.ShapeDtypeStruct((rows, LANES), jnp.int32),
                   jax.ShapeDtypeStruct((rows, LANES), F32)],
        compiler_params=_params("arbitrary"),
        name="merge",
    )(oa_p, oa_s, ob, proj, proj, xp, xs, mod_seq, mod_row, mod_seq, mod_row, mod_seq, mod_row,
      pa, pb, wo, lg, lb, rw, rb)


FILL_ROWS = (128, 64, 32, 16, 8)


def _dispatch_kernel(tm, n_experts, padlo_ref, padlen_ref, nblk_ref, dest_ref, h_ref, xb_hbm, zero_scr, sem, zsem):
    i = pl.program_id(0)
    tq = h_ref.shape[0]
    nb_max = xb_hbm.shape[0] // tm
    zrows = zero_scr.shape[0]

    def fill(act):
        def per_expert(e, carry):
            lo = padlo_ref[e]
            ln = padlen_ref[e]
            head = (SUBLANES - lo % SUBLANES) % SUBLANES
            for r in range(SUBLANES - 1):
                @pl.when(r < head)
                def _(r=r):
                    act(pltpu.make_async_copy(zero_scr.at[pl.ds(0, 1)], xb_hbm.at[pl.ds(lo + r, 1)], zsem))
            off = lo + head
            rem = ln - head
            for b in FILL_ROWS:
                @pl.when((rem & b) != 0)
                def _(off=off, b=b):
                    dst = xb_hbm.at[pl.ds(pl.multiple_of(off, SUBLANES), b)]
                    act(pltpu.make_async_copy(zero_scr.at[pl.ds(0, b)], dst, zsem))
                off = off + (rem & b)
            return carry

        lax.fori_loop(0, n_experts, per_expert, 0)

        def per_block(blk, carry):
            for part in range(tm // zrows):
                row0 = pl.multiple_of(blk * tm + part * zrows, zrows)
                act(pltpu.make_async_copy(zero_scr, xb_hbm.at[pl.ds(row0, zrows)], zsem))
            return carry

        lax.fori_loop(nblk_ref[0], nb_max, per_block, 0)

    @pl.when(i == 0)
    def _():
        zero_scr[...] = jnp.zeros(zero_scr.shape, F32)
        fill(lambda cp: cp.start())
        fill(lambda cp: cp.wait())

    def start(t, carry):
        for kk in range(TOP_K):
            pltpu.make_async_copy(h_ref.at[pl.ds(t, 1)], xb_hbm.at[pl.ds(dest_ref[0, t * TOP_K + kk], 1)],
                                  sem).start()
        return carry

    lax.fori_loop(0, tq, start, 0, unroll=8)
    for kk in range(TOP_K):
        pltpu.make_async_copy(h_ref, xb_hbm.at[pl.ds(0, tq)], sem).wait()


def _dispatch(padlo, padlen, nblk_used, dest, h2, tm, nb_max):
    t, d = h2.shape
    tq = math.gcd(DISPATCH_TQ, t)
    n_experts = padlo.shape[0]
    grid_spec = pltpu.PrefetchScalarGridSpec(
        num_scalar_prefetch=3,
        grid=(t // tq,),
        in_specs=[pl.BlockSpec((None, 1, tq * TOP_K), lambda i, *_: (i, 0, 0), memory_space=pltpu.SMEM),
                  pl.BlockSpec((tq, d), lambda i, *_: (i, 0))],
        out_specs=pl.BlockSpec(memory_space=pl.ANY),
        scratch_shapes=[pltpu.VMEM((FILL_ROWS[0], d), F32), pltpu.SemaphoreType.DMA(()),
                        pltpu.SemaphoreType.DMA(())],
    )
    return pl.pallas_call(
        functools.partial(_dispatch_kernel, tm, n_experts),
        grid_spec=grid_spec,
        out_shape=jax.ShapeDtypeStruct((nb_max * tm, d), F32),
        compiler_params=pltpu.CompilerParams(dimension_semantics=("arbitrary",), vmem_limit_bytes=VMEM_LIMIT,
                                             has_side_effects=True),
        name="dispatch",
    )(padlo, padlen, nblk_used.reshape(1), dest.reshape(t // tq, 1, tq * TOP_K), h2)


def _moe_kernel(nj, e_ref, j_ref, xb_ref, ob_ref, r_ref, flag_ref, wg_code_ref, wu_code_ref,
                x_ref, wg_ref, wu_ref, wd_ref, bg_ref, bu_ref, bd_ref, o_ref,
                wg_scr, wu_scr, wd_scr, acc_scr):
    s = pl.program_id(0)
    flags = flag_ref[s]
    j = j_ref[s]
    r = r_ref[s]

    @pl.when((flags & 2) != 0)
    def _():
        wg_scr[...] = wg_ref[...].astype(BF16)
        wu_scr[...] = wu_ref[...].astype(BF16)
        wd_scr[...] = wd_ref[...].astype(BF16)

    @pl.when(flags == 0)
    def _():
        o_ref[...] = jnp.zeros(o_ref.shape, F32)

    @pl.when((flags & 1) != 0)
    def _():
        x = x_ref[...].astype(BF16)
        gate = jnp.dot(x, wg_scr[...], preferred_element_type=F32) + bg_ref[...]
        up = jnp.dot(x, wu_scr[...], preferred_element_type=F32) + bu_ref[...]
        gate = jnp.minimum(gate, SWIGLU_LIMIT)
        up = jnp.clip(up, -SWIGLU_LIMIT, SWIGLU_LIMIT)
        act = (up + 1.0) * gate * _sigmoid(SWIGLU_ALPHA * gate)
        y = jnp.dot(act.astype(BF16), wd_scr[...], preferred_element_type=F32)

        @pl.when(j == 0)
        def _():
            acc_scr[r] = y

        @pl.when(jnp.logical_and(j > 0, j < nj - 1))
        def _():
            acc_scr[r] = acc_scr[r] + y

        @pl.when(j == nj - 1)
        def _():
            o_ref[...] = acc_scr[r] + y + bd_ref[...]


def _moe(items, xb, w_gu, b_gu, w_dn, b_dn, tm, tf, subs):
    item_e, item_j, item_xb, item_ob, item_r, item_flag, wg_code, wu_code = items
    n_items = item_e.shape[0]
    nslot, d = xb.shape
    n_exp, _, f2 = w_gu.shape
    f = f2 // 2
    nj = f // tf
    assert nj >= 2
    grid_spec = pltpu.PrefetchScalarGridSpec(
        num_scalar_prefetch=8,
        grid=(n_items,),
        in_specs=[pl.BlockSpec((tm, d), lambda s, e, j, xbk, obk, r, fl, cg, cu: (xbk[s], 0)),
                  pl.BlockSpec((None, d, tf), lambda s, e, j, xbk, obk, r, fl, cg, cu: (cg[s] // nj, 0, cg[s] % nj)),
                  pl.BlockSpec((None, d, tf),
                               lambda s, e, j, xbk, obk, r, fl, cg, cu: (cu[s] // nj, 0, nj + cu[s] % nj)),
                  pl.BlockSpec((None, tf, d), lambda s, e, j, xbk, obk, r, fl, cg, cu: (e[s], j[s], 0)),
                  pl.BlockSpec((None, 1, tf), lambda s, e, j, xbk, obk, r, fl, cg, cu: (e[s], 0, j[s])),
                  pl.BlockSpec((None, 1, tf), lambda s, e, j, xbk, obk, r, fl, cg, cu: (e[s], 0, nj + j[s])),
                  pl.BlockSpec((None, 1, d), lambda s, e, j, xbk, obk, r, fl, cg, cu: (e[s], 0, 0))],
        out_specs=pl.BlockSpec((tm, d), lambda s, e, j, xbk, obk, r, fl, cg, cu: (obk[s], 0)),
        scratch_shapes=[pltpu.VMEM((d, tf), BF16), pltpu.VMEM((d, tf), BF16), pltpu.VMEM((tf, d), BF16),
                        pltpu.VMEM((subs, tm, d), F32)],
    )
    return pl.pallas_call(
        functools.partial(_moe_kernel, nj),
        grid_spec=grid_spec,
        out_shape=jax.ShapeDtypeStruct((nslot, d), F32),
        compiler_params=_params("arbitrary"),
        name="moe",
    )(item_e, item_j, item_xb, item_ob, item_r, item_flag, wg_code, wu_code,
      xb, w_gu, w_gu, w_dn, b_gu.reshape(n_exp, 1, f2), b_gu.reshape(n_exp, 1, f2), b_dn.reshape(n_exp, 1, d))


def _combine_kernel(alpha, rt, dest_ref, destn_ref, yb_hbm, tg_ref, x1_ref, gtp_ref, gts_ref, lg_ref, lb_ref,
                    op_ref, os_ref, buf, sem):
    i = pl.program_id(0)
    last = pl.num_programs(0) - 1
    tq = x1_ref.shape[0]
    slot = i % 2

    def start(ids_ref, sl, t):
        for kk in range(TOP_K):
            pltpu.make_async_copy(yb_hbm.at[pl.ds(ids_ref[0, t * TOP_K + kk], 1)],
                                  buf.at[sl, kk, pl.ds(t, 1)], sem.at[sl]).start()

    def wait(sl):
        for kk in range(TOP_K):
            pltpu.make_async_copy(yb_hbm.at[pl.ds(0, tq)], buf.at[sl, kk], sem.at[sl]).wait()

    @pl.when(i == 0)
    def _():
        lax.fori_loop(0, tq, lambda t, c: (start(dest_ref, 0, t), c)[1], 0, unroll=4)

    for t in range(tq):
        start(destn_ref, 1 - slot, t)
    wait(slot)
    tg = tg_ref[...]
    y = jnp.zeros(x1_ref.shape, F32)
    for kk in range(TOP_K):
        y = y + buf[slot, kk] * tg[:, kk:kk + 1]
    is_s = i >= rt.n_p
    out = _layer_norm(alpha * x1_ref[...] + rt.pick_vec(i, gtp_ref, gts_ref) * y, lg_ref[...], lb_ref[...])

    @pl.when(jnp.logical_not(is_s))
    def _():
        op_ref[...] = out

    @pl.when(is_s)
    def _():
        os_ref[...] = out

    @pl.when(i == last)
    def _():
        wait(1 - slot)


def _combine(alpha, rt, dest, yb, tg, x1, mod_seq, mod_row, lg, lb):
    d = x1.shape[1]
    tq = rt.tm
    ids = dest.reshape(rt.n, 1, tq * TOP_K)
    return pl.pallas_call(
        functools.partial(_combine_kernel, alpha, rt),
        grid=(rt.n,),
        in_specs=[pl.BlockSpec((None, 1, tq * TOP_K), lambda i: (i, 0, 0), memory_space=pltpu.SMEM),
                  pl.BlockSpec((None, 1, tq * TOP_K), lambda i: (jnp.minimum(i + 1, rt.n - 1), 0, 0),
                               memory_space=pltpu.SMEM),
                  pl.BlockSpec(memory_space=pl.ANY),
                  rt.joint(LANES), rt.joint(d), rt.seq_vec(d, MOD_GATE2), rt.row_vec(d, MOD_GATE2),
                  pl.BlockSpec((1, d), lambda i: (0, 0)),
                  pl.BlockSpec((1, d), lambda i: (0, 0))],
        out_specs=[rt.prompt(d), rt.sample(d, single=False)],
        out_shape=[jax.ShapeDtypeStruct((rt.n_p * tq, d), F32), jax.ShapeDtypeStruct((rt.n_s * tq, d), F32)],
        scratch_shapes=[pltpu.VMEM((2, TOP_K, tq, d), F32), pltpu.SemaphoreType.DMA((2,))],
        compiler_params=_params("arbitrary"),
        name="combine",
    )(ids, ids, yb, tg, x1, mod_seq, mod_row, lg, lb)


def _take(table, idx):
    hit = idx[:, None] == jnp.arange(table.shape[0], dtype=jnp.int32)[None, :]
    return jnp.sum(jnp.where(hit, table[None, :], 0), axis=1)


def _routing(ti, n_experts, tm, nj, subs):
    t = ti.shape[0]
    n_assign = t * TOP_K
    nb_max = n_assign // tm + n_experts
    onehot = (ti[:, :, None] == jnp.arange(n_experts, dtype=jnp.int32)[None, None, :]).astype(jnp.int32)
    per_tok = jnp.sum(onehot, axis=1)
    cum = jnp.cumsum(per_tok, axis=0)
    counts = cum[-1]
    rank = jnp.sum(onehot * cum[:, None, :], axis=2) - 1
    nblk = (counts + tm - 1) // tm
    blk_end = jnp.cumsum(nblk)
    blk_start = blk_end - nblk
    dest = jnp.sum(onehot * blk_start[None, None, :], axis=2) * tm + rank
    pad_lo = (blk_start * tm + counts).astype(jnp.int32)
    pad_len = (nblk * tm - counts).astype(jnp.int32)
    total_blk = blk_end[-1]
    blocks = jnp.arange(nb_max, dtype=jnp.int32)
    blk_e = jnp.minimum(jnp.sum((blk_end[None, :] <= blocks[:, None]).astype(jnp.int32), axis=1), n_experts - 1)
    start_b, nblk_b = _take(blk_start, blk_e), _take(nblk, blk_e)
    r_in_e = blocks - start_b
    g0 = start_b + (r_in_e // subs) * subs
    nsub = jnp.minimum(subs, nblk_b - (r_in_e // subs) * subs)
    p = jnp.arange(nb_max * nj, dtype=jnp.int32)
    bp = p // nj
    valid = bp < total_blk
    last = jnp.maximum(total_blk - 1, 0)
    bq = jnp.where(valid, bp, last)
    gq, nq, eq = _take(g0, bq), jnp.maximum(_take(nsub, bq), 1), _take(blk_e, bq)
    local = p - nj * gq
    jq = jnp.where(valid, local // nq, nj - 1)
    rq = jnp.where(valid, local % nq, nq - 1)
    item_xb = gq + rq
    item_ob = jnp.where(valid, jnp.where(jq == nj - 1, gq + rq, gq), bp)
    flags = valid.astype(jnp.int32) + 2 * (valid & (rq == 0)).astype(jnp.int32)
    nxt_blk = gq + nq
    has_next = nxt_blk < total_blk
    last_j = jq == nj - 1
    e_n = jnp.where(last_j & has_next, _take(blk_e, jnp.minimum(nxt_blk, nb_max - 1)), eq)
    j_n = jnp.where(last_j, jnp.where(has_next, 0, jq), jq + 1)
    cur, nxt = eq * nj + jq, e_n * nj + j_n
    wg_code = jnp.where(valid & (rq >= 1), nxt, cur)
    wu_code = jnp.where(valid & (rq >= 2), nxt, cur)
    items = (eq.astype(jnp.int32), jq.astype(jnp.int32), item_xb.astype(jnp.int32),
             item_ob.astype(jnp.int32), rq.astype(jnp.int32), flags,
             wg_code.astype(jnp.int32), wu_code.astype(jnp.int32))
    return dest.astype(jnp.int32), pad_lo, pad_len, total_blk.astype(jnp.int32), nb_max, items


WPREP_TR = 256


def _win_prep_kernel(c_ab, w_hbm, o_ref, buf, sem):
    c = pl.program_id(0)
    tr = o_ref.shape[0]
    o_a = QKV_W + V_W
    c_shift = o_a // tr

    def copy(cc):
        row = jnp.where(cc < c_shift, cc * tr, jnp.where(cc < c_ab, cc * tr + 2 * DN_HEADS, o_a))
        src = w_hbm.at[pl.ds(pl.multiple_of(row, 2 * DN_HEADS), tr)]
        return pltpu.make_async_copy(src, buf.at[cc % 2], sem.at[cc % 2])

    @pl.when(c == 0)
    def _():
        copy(c).start()

    @pl.when(c + 1 <= c_ab)
    def _():
        copy(c + 1).start()

    @pl.when(c <= c_ab)
    def _():
        copy(c).wait()

    x = buf[c % 2]
    rows = lax.broadcasted_iota(jnp.int32, x.shape, 0)
    keep = jnp.logical_or(c < c_ab, jnp.logical_and(c == c_ab, rows < 2 * DN_HEADS))
    o_ref[...] = jnp.where(keep, x, 0.0).astype(BF16)


def _rearranged_w_in(w_in, d):
    dm, nw = w_in.shape
    tr = WPREP_TR
    o_a = QKV_W + V_W
    rest = nw - o_a - 2 * DN_HEADS
    assert o_a % tr == 0 and rest % tr == 0
    nw_out = -(-(nw - 2 * DN_HEADS + LANES) // PROJ_TN) * PROJ_TN
    assert nw_out % tr == 0
    w = pl.pallas_call(
        functools.partial(_win_prep_kernel, (o_a + rest) // tr),
        grid=(nw_out // tr,),
        in_specs=[pl.BlockSpec(memory_space=pl.ANY)],
        out_specs=pl.BlockSpec((tr, dm), lambda c: (c, 0)),
        out_shape=jax.ShapeDtypeStruct((nw_out, dm), BF16),
        scratch_shapes=[pltpu.VMEM((2, tr, dm), F32), pltpu.SemaphoreType.DMA((2,))],
        compiler_params=_params("arbitrary"),
        name="w_in_prep",
    )(w_in.T)
    cols = dict(z=QKV_W, u=QKV_W + V_W, v=QKV_W + V_W + SG_W, ga=QKV_W + V_W + 2 * SG_W,
                gb=QKV_W + V_W + 2 * SG_W + d, ab=QKV_W + V_W + 2 * SG_W + 2 * d)
    return w, cols


def kernel(x_prompt, x_sample, state_conv_qkv, state_delta, c_prompt, c_sample, w_ada, b_ada, w_in, w_conv, a_log, dt_bias, o_norm_g, sg_ln_g, sg_ln_b, w_s, b_s, p_a, p_b, w_out, ln1_g, ln1_b, router_w, router_b, w_gu, b_gu, w_dn, b_dn, ln2_g, ln2_b):
    depth = w_ada.shape[0]
    alpha = float((2 * depth) ** 0.25)
    bp, seq, d = x_prompt.shape
    bs, lt, _ = x_sample.shape
    assert lt == DN_CONV and seq % SG_CHUNK == 0
    n_experts = router_w.shape[2]
    rows_p, rows_s = bp * seq, bs * lt
    xp = x_prompt.reshape(rows_p, d)
    xs = x_sample.reshape(rows_s, d)
    c_all = jnp.concatenate([jnp.repeat(c_sample, lt, axis=0), c_prompt,
                             jnp.zeros((-(rows_s + bp) % SUBLANES, d), F32)], axis=0)
    tile = lambda cap: _Rows(math.gcd(math.gcd(cap, rows_s), seq), rows_p, rows_s, seq)
    rt_proj, rt_sgu, rt_merge, rt_comb = tile(PROJ_TM), tile(SGU_TM), tile(MERGE_TM), tile(COMB_TQ)
    outs = dict(conv_p=[], delta_p=[], conv_s=[], delta_s=[], vrows=[])

    for l in range(depth):
        mod = _adaln(c_all, w_ada[l], b_ada[l])

        w_r, col = _rearranged_w_in(w_in[l], d)
        proj = _inproj(rt_proj, xp, xs, mod, mod, w_r)

        alog = jnp.pad(a_log[l:l + 1], ((0, 0), (0, LANES - DN_HEADS)))
        dtb = jnp.pad(dt_bias[l:l + 1], ((0, 0), (0, LANES - DN_HEADS)))
        og = o_norm_g[l:l + 1]
        conv0 = jnp.zeros((bp, DN_CONV - 1, QKV_W), F32)
        s0 = jnp.zeros((bp, DN_HEADS, DN_DK, DN_DV), F32)
        oa_p, conv_p, delta_p = _gdn_prompt(proj, conv0, s0, w_conv[l], alog, dtb, og, bp, seq,
                                            col['z'], col['ab'])
        oa_s, conv_s, delta_s = _gdn_sample(proj, rows_p, state_conv_qkv[l], state_delta[l], w_conv[l],
                                            alog, dtb, og, bs, col['z'], col['ab'])

        assert rt_sgu.tm % SG_CHUNK == 0 and SG_CHUNK % lt == 0
        ws_p = jnp.tril(w_s[l]).astype(BF16)
        bias_p = jnp.repeat(b_s[l].T, SG_CH, axis=1)
        eye = jnp.eye(SG_CHUNK // lt, dtype=F32)
        ws_s = jnp.stack([jnp.kron(eye, jnp.tril(w_s[l, g, :lt, :lt])) for g in range(SG_GROUPS)]).astype(BF16)
        bias_s = jnp.tile(jnp.repeat(b_s[l, :, :lt].T, SG_CH, axis=1), (SG_CHUNK // lt, 1))
        ob, vn_s = _sgu(rt_sgu, proj, sg_ln_g[l:l + 1], sg_ln_b[l:l + 1],
                        ws_p, ws_s, bias_p, bias_s, col['u'], col['v'])

        pa, pb, wo = p_a[l].astype(BF16), p_b[l].astype(BF16), w_out[l].astype(BF16)
        rws = _split(jnp.pad(router_w[l], ((0, 0), (0, LANES - n_experts))))
        rw = jnp.concatenate([rws[0], rws[0], rws[1]], axis=0)
        rb = jnp.pad(router_b[l:l + 1], ((0, 0), (0, LANES - n_experts)))
        x1, h2, ti, tg = _merge(alpha, n_experts, rt_merge, oa_p, oa_s, ob, proj, xp, xs,
                                mod, mod,
                                pa, pb, wo, ln1_g[l:l + 1], ln1_b[l:l + 1], rw, rb,
                                col['ga'], col['gb'])

        nj = w_dn.shape[2] // MOE_TF
        dest, pad_lo, pad_len, nblk_used, nb_max, items = _routing(ti[:, :TOP_K], n_experts, MOE_TM, nj, MOE_SUBS)
        xb = _dispatch(pad_lo, pad_len, nblk_used, dest, h2, MOE_TM, nb_max)
        yb = _moe(items, xb, w_gu[l], b_gu[l], w_dn[l], b_dn[l], MOE_TM, MOE_TF, MOE_SUBS)
        xp, xs = _combine(alpha, rt_comb, dest, yb, tg, x1, mod, mod,
                          ln2_g[l:l + 1], ln2_b[l:l + 1])

        outs['conv_p'].append(conv_p)
        outs['delta_p'].append(delta_p)
        outs['conv_s'].append(conv_s)
        outs['delta_s'].append(delta_s)
        outs['vrows'].append(vn_s.reshape(bs, lt, SG_W))

    return (xp.reshape(bp, seq, d), xs.reshape(bs, lt, d),
            jnp.stack(outs['conv_p']), jnp.stack(outs['delta_p']),
            jnp.stack(outs['conv_s']), jnp.stack(outs['delta_s']), jnp.stack(outs['vrows']))
```
